```python
import math
import jax
import jax.numpy as jnp
from jax import lax
import numpy as np

D_MODEL = 1024
BATCH = 8
SEQ = 2048
DEPTH = 1
DEC_BATCH = 32
DEC_SEQ = 4
PAST_LEN = 16384
PAGE_SIZE = 128

D_MIX = D_MODEL
D_ATTN = D_MIX // 2
D_SSM = D_MIX - D_ATTN
HEAD_DIM = 64
N_HEADS = D_ATTN // HEAD_DIM
N_KV_HEADS = 2
GQA = N_HEADS // N_KV_HEADS
KV_W = N_KV_HEADS * HEAD_DIM
ROT_DIM = HEAD_DIM // 4
ROPE_THETA = 500000.0
ATTN_SCALE = HEAD_DIM ** -0.5
CMP_LEN = 32
CMP_STRIDE = 16
CMP_RATIO = CMP_LEN // CMP_STRIDE
CMP_HIDDEN = 4 * HEAD_DIM
SEL_BLOCK = 64
N_SELECT = 16
SEL_QBLOCK = 64
WINDOW = 512
WIN_BLOCK = 128
SSM_CH = 16
SSM_GROUPS = D_SSM // SSM_CH
SSM_STATE = 64
D_FF = 2816
D_IN = D_ATTN + 6 * KV_W + 3 * N_HEADS + D_SSM
RMS_EPS = 1e-6
NEG_INF = -1e30
FORCE_SCORE = 1e4
F32 = jnp.float32

kernel_name = 'hymba_nsa_s5_macaron_step'


def rmsnorm(x, g):
    xf = x.astype(F32)
    y = xf * lax.rsqrt(jnp.mean(xf * xf, axis=-1, keepdims=True) + RMS_EPS)
    return (y * g.astype(F32)).astype(x.dtype)


def macaron_ffn(x, g, w_gate, w_up, w_down):
    h = rmsnorm(x, g)
    return x + 0.5 * ((jax.nn.silu(h @ w_gate) * (h @ w_up)) @ w_down)


def partial_rope(x, pos):
    half = ROT_DIM // 2
    inv_freq = jnp.power(ROPE_THETA, -(jnp.arange(half, dtype=F32) * 2.0 / ROT_DIM))
    ang = pos.astype(F32)[:, None] * inv_freq[None, :]
    cos, sin = jnp.cos(ang)[:, None, :], jnp.sin(ang)[:, None, :]
    xf = x.astype(F32)
    x1, x2 = xf[..., :half], xf[..., half:ROT_DIM]
    out = jnp.concatenate([x1 * cos - x2 * sin, x2 * cos + x1 * sin, xf[..., ROT_DIM:]], axis=-1)
    return out.astype(x.dtype)


def split_projection(h, w_in):
    B, T = h.shape[:2]
    z = h @ w_in
    sizes = [D_ATTN] + [KV_W] * 6 + [3 * N_HEADS, D_SSM]
    cuts = [int(c) for c in np.cumsum(sizes)[:-1]]
    q, kc, vc, ks, vs, kw, vw, g, u = jnp.split(z, cuts, axis=-1)
    kv = lambda a: a.reshape(B, T, N_KV_HEADS, HEAD_DIM)
    gates = jax.nn.sigmoid(g.astype(F32)).reshape(B, T, N_HEADS, 3)
    return q.reshape(B, T, N_HEADS, HEAD_DIM), kv(kc), kv(vc), kv(ks), kv(vs), kv(kw), kv(vw), gates, u


def compress_blocks(k, pos_emb, w1, b1, w2):
    B, T = k.shape[:2]
    nh = T // CMP_STRIDE
    nc = nh - CMP_RATIO + 1
    halves = k[:, :nh * CMP_STRIDE].reshape(B, nh, CMP_STRIDE, N_KV_HEADS, HEAD_DIM)
    pe = pos_emb.reshape(CMP_RATIO, CMP_STRIDE, HEAD_DIM)
    w1r = w1.reshape(CMP_RATIO, CMP_STRIDE, HEAD_DIM, CMP_HIDDEN)
    pre = b1
    for r in range(CMP_RATIO):
        pre = pre + jnp.einsum('bnskd,sdh->bnkh', halves[:, r:r + nc] + pe[r][:, None, :], w1r[r])
    return jax.nn.gelu(pre) @ w2


def compressed_attention(q, kc, vc, q_pos):
    B, Tq = q.shape[:2]
    nc = kc.shape[1]
    qg = q.reshape(B, Tq, N_KV_HEADS, GQA, HEAD_DIM)
    s = jnp.einsum('bqkgd,bnkd->bqkgn', qg, kc).astype(F32) * ATTN_SCALE
    blk_end = jnp.arange(nc) * CMP_STRIDE + CMP_LEN - 1
    m = (blk_end[None, :] <= q_pos[:, None])[None, :, None, None, :]
    p = jnp.where(m, jax.nn.softmax(jnp.where(m, s, NEG_INF), axis=-1), 0.0)
    o = jnp.einsum('bqkgn,bnkd->bqkgd', p.astype(vc.dtype), vc)
    return o.reshape(B, Tq, N_HEADS, HEAD_DIM), p.sum(axis=3)


def select_blocks(p_cmp, q_pos, n_sel):
    per = SEL_BLOCK // CMP_STRIDE
    nc = p_cmp.shape[-1]
    total = n_sel * per + CMP_RATIO + per
    pp = jnp.pad(p_cmp, ((0, 0), (0, 0), (0, 0), (CMP_RATIO - 1, total - nc - (CMP_RATIO - 1))))
    imp = sum(pp[..., o:o + n_sel * per:per] for o in range(per + CMP_RATIO - 1))
    j = jnp.arange(n_sel)[None, None, None, :]
    jq = (q_pos // SEL_BLOCK)[None, :, None, None]
    forced = (j == 0) | (j == jq) | (j == jq - 1)
    score = jnp.where(j > jq, -1.0, jnp.where(forced, FORCE_SCORE, imp))
    _, idx = lax.top_k(score, min(N_SELECT, n_sel))
    return idx, idx <= jq


def to_blocks(k, n_sel):
    B, T = k.shape[:2]
    kp = jnp.pad(k, ((0, 0), (0, n_sel * SEL_BLOCK - T), (0, 0), (0, 0)))
    return kp.reshape(B, n_sel, SEL_BLOCK, N_KV_HEADS, HEAD_DIM).transpose(0, 3, 1, 2, 4)


def selected_attention(q_rot, kb, vb, idx, valid, q_pos):
    B, Tq = q_rot.shape[:2]
    qg = q_rot.reshape(B, Tq, N_KV_HEADS, GQA, HEAD_DIM)
    bi = jnp.arange(B)[:, None, None, None]
    hi = jnp.arange(N_KV_HEADS)[None, None, :, None]
    ksel, vsel = kb[bi, hi, idx], vb[bi, hi, idx]
    s = jnp.einsum('bqkgd,bqknsd->bqkgns', qg, ksel).astype(F32) * ATTN_SCALE
    kpos = idx[..., None] * SEL_BLOCK + jnp.arange(SEL_BLOCK)
    mask = valid[..., None] & (kpos <= q_pos[None, :, None, None, None])
    s = jnp.where(mask[:, :, :, None], s, NEG_INF)
    shp = s.shape
    p = jax.nn.softmax(s.reshape(shp[:-2] + (-1,)), axis=-1).reshape(shp)
    o = jnp.einsum('bqkgns,bqknsd->bqkgd', p.astype(vsel.dtype), vsel)
    return o.reshape(B, Tq, N_HEADS, HEAD_DIM)


def chunk_q(a, n):
    B, T = a.shape[:2]
    return jnp.moveaxis(a.reshape((B, T // n, n) + a.shape[2:]), 1, 0)


def window_attention(qb, kb, vb, q_pos, k_pos):
    B, NB, QB = qb.shape[:3]
    qg = qb.reshape(B, NB, QB, N_KV_HEADS, GQA, HEAD_DIM)
    s = jnp.einsum('bnqkgd,bnjkd->bnqkgj', qg, kb).astype(F32) * ATTN_SCALE
    kp, qp = k_pos[:, None, :], q_pos[:, :, None]
    m = ((kp >= 0) & (kp <= qp) & (kp > qp - WINDOW))[None, :, :, None, None, :]
    p = jax.nn.softmax(jnp.where(m, s, NEG_INF), axis=-1)
    o = jnp.einsum('bnqkgj,bnjkd->bnqkgd', p.astype(vb.dtype), vb)
    return o.reshape(B, NB * QB, N_HEADS, HEAD_DIM)


def combine_branches(gates, o_cmp, o_slc, o_win):
    B, T = o_cmp.shape[:2]
    o = (gates[..., 0:1] * o_cmp.astype(F32) + gates[..., 1:2] * o_slc.astype(F32)
         + gates[..., 2:3] * o_win.astype(F32))
    return o.reshape(B, T, D_ATTN).astype(o_cmp.dtype)


def complex_affine_combine(e1, e2):
    a1r, a1i, b1r, b1i = e1
    a2r, a2i, b2r, b2i = e2
    return (a2r * a1r - a2i * a1i, a2r * a1i + a2i * a1r,
            a2r * b1r - a2i * b1i + b2r, a2r * b1i + a2i * b1r + b2i)


def s5_mixer(u, h0_re, h0_im, lp):
    B, T = u.shape[:2]
    uf = u.astype(F32).reshape(B, T, SSM_GROUPS, SSM_CH)
    dt = jnp.exp(lp['s5_log_dt'].astype(F32))[:, None]
    ar, ai = lp['s5_a_re'].astype(F32), lp['s5_a_im'].astype(F32)
    mag = jnp.exp(ar * dt)
    lr, li = mag * jnp.cos(ai * dt), mag * jnp.sin(ai * dt)
    den = ar * ar + ai * ai
    inv_r, inv_i = ar / den, -ai / den
    nr, ni = lr - 1.0, li
    fr, fi = nr * inv_r - ni * inv_i, nr * inv_i + ni * inv_r
    br, bim = lp['s5_b_re'].astype(F32), lp['s5_b_im'].astype(F32)
    bbr = fr[..., None] * br - fi[..., None] * bim
    bbi = fr[..., None] * bim + fi[..., None] * br
    xr = jnp.einsum('btgc,gpc->btgp', uf, bbr)
    xi = jnp.einsum('btgc,gpc->btgp', uf, bbi)
    h0r, h0i = h0_re.astype(F32), h0_im.astype(F32)
    xr = xr.at[:, 0].add(lr * h0r - li * h0i)
    xi = xi.at[:, 0].add(lr * h0i + li * h0r)
    a_r, a_i = jnp.broadcast_to(lr, xr.shape), jnp.broadcast_to(li, xr.shape)
    _, _, hr, hi = lax.associative_scan(complex_affine_combine, (a_r, a_i, xr, xi), axis=1)
    y = (jnp.einsum('btgp,gcp->btgc', hr, lp['s5_c_re'].astype(F32))
         - jnp.einsum('btgp,gcp->btgc', hi, lp['s5_c_im'].astype(F32))
         + lp['s5_d'].astype(F32) * uf)
    z = jax.nn.gelu(y.reshape(B, T, D_SSM))
    out = z * jax.nn.sigmoid(z @ lp['s5_glu_w'].astype(F32) + lp['s5_glu_b'].astype(F32))
    return out.astype(u.dtype), hr[:, -1], hi[:, -1]


def compress_kv(kc, vc, lp):
    k_cmp = compress_blocks(kc, lp['cmp_k_pos'], lp['cmp_k_w1'], lp['cmp_k_b1'], lp['cmp_k_w2'])
    v_cmp = compress_blocks(vc, lp['cmp_v_pos'], lp['cmp_v_w1'], lp['cmp_v_b1'], lp['cmp_v_w2'])
    return k_cmp, v_cmp


def prompt_mixer(h, lp):
    B, T = h.shape[:2]
    q, kc, vc, ks, vs, kw, vw, gates, u = split_projection(h, lp['w_in'])
    pos = jnp.arange(T, dtype=jnp.int32)
    q_rot, ks, kw = partial_rope(q, pos), partial_rope(ks, pos), partial_rope(kw, pos)
    k_cmp, v_cmp = compress_kv(kc, vc, lp)
    o_cmp, imp = compressed_attention(q, k_cmp, v_cmp, pos)
    n_sel = -(-T // SEL_BLOCK)
    idx, valid = select_blocks(imp, pos, n_sel)
    kb, vb = to_blocks(ks, n_sel), to_blocks(vs, n_sel)
    n_ch = T // SEL_QBLOCK
    o_slc = lax.map(lambda a: selected_attention(a[0], kb, vb, a[1], a[2], a[3]),
                    (chunk_q(q_rot, SEL_QBLOCK), chunk_q(idx, SEL_QBLOCK), chunk_q(valid, SEL_QBLOCK),
                     pos.reshape(n_ch, SEL_QBLOCK)))
    o_slc = jnp.moveaxis(o_slc, 0, 1).reshape(B, T, N_HEADS, HEAD_DIM)
    n_wb = T // WIN_BLOCK
    kidx = (jnp.arange(n_wb) * WIN_BLOCK)[:, None] + jnp.arange(WIN_BLOCK + WINDOW)[None, :]
    pad = ((0, 0), (WINDOW, 0), (0, 0), (0, 0))
    o_win = window_attention(q_rot.reshape(B, n_wb, WIN_BLOCK, N_HEADS, HEAD_DIM),
                             jnp.pad(kw, pad)[:, kidx], jnp.pad(vw, pad)[:, kidx],
                             pos.reshape(n_wb, WIN_BLOCK), kidx - WINDOW)
    o_attn = combine_branches(gates, o_cmp, o_slc, o_win)
    zeros = jnp.zeros((B, SSM_GROUPS, SSM_STATE), F32)
    y_ssm, s_re, s_im = s5_mixer(u, zeros, zeros, lp)
    wb = min(WINDOW, T)
    return o_attn, y_ssm, (kc, vc, ks, vs, kw[:, T - wb:], vw[:, T - wb:], s_re, s_im)


def sample_mixer(h, ck, cv, sk, sv, wk, wv, st_re, st_im, page_table, lp):
    B, T = h.shape[:2]
    q, kc, vc, ks, vs, kw, vw, gates, u = split_projection(h, lp['w_in'])
    pos = PAST_LEN + jnp.arange(T, dtype=jnp.int32)
    q_rot, ks, kw = partial_rope(q, pos), partial_rope(ks, pos), partial_rope(kw, pos)
    past = lambda pool: pool[page_table].reshape(B, -1, N_KV_HEADS, HEAD_DIM)
    kc_all, vc_all = jnp.concatenate([past(ck), kc], axis=1), jnp.concatenate([past(cv), vc], axis=1)
    ks_all, vs_all = jnp.concatenate([past(sk), ks], axis=1), jnp.concatenate([past(sv), vs], axis=1)
    k_cmp, v_cmp = compress_kv(kc_all, vc_all, lp)
    o_cmp, imp = compressed_attention(q, k_cmp, v_cmp, pos)
    n_sel = -(-(PAST_LEN + T) // SEL_BLOCK)
    idx, valid = select_blocks(imp, pos, n_sel)
    o_slc = selected_attention(q_rot, to_blocks(ks_all, n_sel), to_blocks(vs_all, n_sel), idx, valid, pos)
    wb = wk.shape[1]
    kw_all, vw_all = jnp.concatenate([wk, kw], axis=1), jnp.concatenate([wv, vw], axis=1)
    k_pos = PAST_LEN - wb + jnp.arange(wb + T, dtype=jnp.int32)
    o_win = window_attention(q_rot[:, None], kw_all[:, None], vw_all[:, None], pos[None], k_pos[None])
    o_attn = combine_branches(gates, o_cmp, o_slc, o_win)
    y_ssm, s_re, s_im = s5_mixer(u, st_re, st_im, lp)
    return o_attn, y_ssm, (kc, vc, ks, vs, kw_all[:, T:], vw_all[:, T:], s_re, s_im)


def layer(x, lp, mixer):
    x = macaron_ffn(x, lp['norm_ffn1'], lp['ffn1_gate'], lp['ffn1_up'], lp['ffn1_down'])
    o_attn, y_ssm, state = mixer(rmsnorm(x, lp['norm_mix']))
    merged = jnp.concatenate([rmsnorm(o_attn, lp['norm_attn_out']), rmsnorm(y_ssm, lp['norm_ssm_out'])], axis=-1)
    x = x + merged @ lp['w_out']
    x = macaron_ffn(x, lp['norm_ffn2'], lp['ffn2_gate'], lp['ffn2_up'], lp['ffn2_down'])
    return x, state


def setup_inputs(seed: int = 0) -> dict:
    key = jax.random.key(seed)
    subkeys = jax.random.split(key, 64)
    ki = iter(range(64))

    def nrm(shape, scale=1.0):
        return scale * jax.random.normal(subkeys[next(ki)], shape, F32)

    def gain(d):
        return 1.0 + 0.01 * nrm((DEPTH, d))

    n_pages = PAST_LEN // PAGE_SIZE
    n_used = DEC_BATCH * n_pages
    n_pool = n_used + max(n_used // 4, 1)
    wb = min(WINDOW, PAST_LEN)
    pool_shape = (DEPTH, n_pool, PAGE_SIZE, N_KV_HEADS, HEAD_DIM)
    win_shape = (DEPTH, DEC_BATCH, wb, N_KV_HEADS, HEAD_DIM)
    st_shape = (DEPTH, DEC_BATCH, SSM_GROUPS, SSM_STATE)
    page_table = jax.random.permutation(subkeys[next(ki)], n_pool)[:n_used].reshape(DEC_BATCH, n_pages).astype(jnp.int32)
    a_im = math.pi * jnp.arange(SSM_STATE, dtype=F32)[None, None, :] + 0.01 * nrm((DEPTH, SSM_GROUPS, SSM_STATE))
    log_dt = jax.random.uniform(subkeys[next(ki)], (DEPTH, SSM_GROUPS), F32,
                                minval=math.log(1e-3), maxval=math.log(1e-1))
    return {
        'x_prompt': nrm((BATCH, SEQ, D_MODEL)),
        'x_sample': nrm((DEC_BATCH, DEC_SEQ, D_MODEL)),
        'cache_cmp_k': nrm(pool_shape),
        'cache_cmp_v': nrm(pool_shape),
        'cache_slc_k': nrm(pool_shape),
        'cache_slc_v': nrm(pool_shape),
        'state_win_k': nrm(win_shape),
        'state_win_v': nrm(win_shape),
        'state_s5_re': nrm(st_shape, 0.5),
        'state_s5_im': nrm(st_shape, 0.5),
        'page_table': page_table,
        'norm_ffn1': gain(D_MODEL),
        'ffn1_gate': nrm((DEPTH, D_MODEL, D_FF), D_MODEL ** -0.5),
        'ffn1_up': nrm((DEPTH, D_MODEL, D_FF), D_MODEL ** -0.5),
        'ffn1_down': nrm((DEPTH, D_FF, D_MODEL), D_FF ** -0.5),
        'norm_mix': gain(D_MODEL),
        'w_in': nrm((DEPTH, D_MODEL, D_IN), D_MODEL ** -0.5),
        'cmp_k_pos': nrm((DEPTH, CMP_LEN, HEAD_DIM), 0.1),
        'cmp_k_w1': nrm((DEPTH, CMP_LEN * HEAD_DIM, CMP_HIDDEN), (CMP_LEN * HEAD_DIM) ** -0.5),
        'cmp_k_b1': nrm((DEPTH, CMP_HIDDEN), 0.01),
        'cmp_k_w2': nrm((DEPTH, CMP_HIDDEN, HEAD_DIM), CMP_HIDDEN ** -0.5),
        'cmp_v_pos': nrm((DEPTH, CMP_LEN, HEAD_DIM), 0.1),
        'cmp_v_w1': nrm((DEPTH, CMP_LEN * HEAD_DIM, CMP_HIDDEN), (CMP_LEN * HEAD_DIM) ** -0.5),
        'cmp_v_b1': nrm((DEPTH, CMP_HIDDEN), 0.01),
        'cmp_v_w2': nrm((DEPTH, CMP_HIDDEN, HEAD_DIM), CMP_HIDDEN ** -0.5),
        's5_log_dt': log_dt,
        's5_a_re': -0.5 + 0.01 * nrm((DEPTH, SSM_GROUPS, SSM_STATE)),
        's5_a_im': a_im,
        's5_b_re': nrm((DEPTH, SSM_GROUPS, SSM_STATE, SSM_CH), (2 * SSM_CH) ** -0.5),
        's5_b_im': nrm((DEPTH, SSM_GROUPS, SSM_STATE, SSM_CH), (2 * SSM_CH) ** -0.5),
        's5_c_re': nrm((DEPTH, SSM_GROUPS, SSM_CH, SSM_STATE), (2 * SSM_STATE) ** -0.5),
        's5_c_im': nrm((DEPTH, SSM_GROUPS, SSM_CH, SSM_STATE), (2 * SSM_STATE) ** -0.5),
        's5_d': nrm((DEPTH, SSM_GROUPS, SSM_CH)),
        's5_glu_w': nrm((DEPTH, D_SSM, D_SSM), D_SSM ** -0.5),
        's5_glu_b': nrm((DEPTH, D_SSM), 0.01),
        'norm_attn_out': gain(D_ATTN),
        'norm_ssm_out': gain(D_SSM),
        'w_out': nrm((DEPTH, D_MIX, D_MODEL), D_MIX ** -0.5),
        'norm_ffn2': gain(D_MODEL),
        'ffn2_gate': nrm((DEPTH, D_MODEL, D_FF), D_MODEL ** -0.5),
        'ffn2_up': nrm((DEPTH, D_MODEL, D_FF), D_MODEL ** -0.5),
        'ffn2_down': nrm((DEPTH, D_FF, D_MODEL), D_FF ** -0.5),
        'norm_final': 1.0 + 0.01 * nrm((D_MODEL,)),
    }


def reference(x_prompt, x_sample, cache_cmp_k, cache_cmp_v, cache_slc_k, cache_slc_v,
              state_win_k, state_win_v, state_s5_re, state_s5_im, page_table,
              norm_ffn1, ffn1_gate, ffn1_up, ffn1_down, norm_mix, w_in,
              cmp_k_pos, cmp_k_w1, cmp_k_b1, cmp_k_w2, cmp_v_pos, cmp_v_w1, cmp_v_b1, cmp_v_w2,
              s5_log_dt, s5_a_re, s5_a_im, s5_b_re, s5_b_im, s5_c_re, s5_c_im, s5_d, s5_glu_w, s5_glu_b,
              norm_attn_out, norm_ssm_out, w_out, norm_ffn2, ffn2_gate, ffn2_up, ffn2_down, norm_final):
    xp, xs = x_prompt, x_sample
    acc_p = [[] for _ in range(8)]
    acc_s = [[] for _ in range(8)]
    for l in range(DEPTH):
        lp = dict(norm_ffn1=norm_ffn1[l], ffn1_gate=ffn1_gate[l], ffn1_up=ffn1_up[l], ffn1_down=ffn1_down[l],
                  norm_mix=norm_mix[l], w_in=w_in[l],
                  cmp_k_pos=cmp_k_pos[l], cmp_k_w1=cmp_k_w1[l], cmp_k_b1=cmp_k_b1[l], cmp_k_w2=cmp_k_w2[l],
                  cmp_v_pos=cmp_v_pos[l], cmp_v_w1=cmp_v_w1[l], cmp_v_b1=cmp_v_b1[l], cmp_v_w2=cmp_v_w2[l],
                  s5_log_dt=s5_log_dt[l], s5_a_re=s5_a_re[l], s5_a_im=s5_a_im[l], s5_b_re=s5_b_re[l],
                  s5_b_im=s5_b_im[l], s5_c_re=s5_c_re[l], s5_c_im=s5_c_im[l], s5_d=s5_d[l],
                  s5_glu_w=s5_glu_w[l], s5_glu_b=s5_glu_b[l],
                  norm_attn_out=norm_attn_out[l], norm_ssm_out=norm_ssm_out[l], w_out=w_out[l],
                  norm_ffn2=norm_ffn2[l], ffn2_gate=ffn2_gate[l], ffn2_up=ffn2_up[l], ffn2_down=ffn2_down[l])
        xp, st_p = layer(xp, lp, lambda h: prompt_mixer(h, lp))
        xs, st_s = layer(xs, lp, lambda h: sample_mixer(h, cache_cmp_k[l], cache_cmp_v[l], cache_slc_k[l],
                                                         cache_slc_v[l], state_win_k[l], state_win_v[l],
                                                         state_s5_re[l], state_s5_im[l], page_table, lp))
        for acc, a in zip(acc_p, st_p):
            acc.append(a)
        for acc, a in zip(acc_s, st_s):
            acc.append(a)
    y_prompt = rmsnorm(xp, norm_final)
    y_sample = rmsnorm(xs, norm_final)
    p_cmp_k, p_cmp_v, p_slc_k, p_slc_v, p_win_k, p_win_v, p_s5_re, p_s5_im = [jnp.stack(a) for a in acc_p]
    s_cmp_k, s_cmp_v, s_slc_k, s_slc_v, s_win_k, s_win_v, s_s5_re, s_s5_im = [jnp.stack(a) for a in acc_s]
    return (y_prompt, y_sample,
            p_cmp_k, p_cmp_v, p_slc_k, p_slc_v, p_win_k, p_win_v, p_s5_re, p_s5_im,
            s_cmp_k, s_cmp_v, s_slc_k, s_slc_v, s_win_k, s_win_v, s_s5_re, s_s5_im)
```

```python
import functools
import math

import numpy as np
import jax
import jax.numpy as jnp
from jax import lax
from jax.experimental import pallas as pl
from jax.experimental.pallas import tpu as pltpu

F32 = jnp.float32
BF16 = jnp.bfloat16

D_MODEL = 1024
PAST_LEN = 16384
PAGE_SIZE = 128
D_ATTN = 512
D_SSM = 512
HEAD_DIM = 64
N_HEADS = 8
N_KV_HEADS = 2
GQA = 4
KV_W = 128
ROT_DIM = 16
ROPE_THETA = 500000.0
ATTN_SCALE = HEAD_DIM ** -0.5
CMP_LEN = 32
CMP_STRIDE = 16
CMP_RATIO = 2
CMP_HIDDEN = 256
SEL_BLOCK = 64
N_SELECT = 16
WINDOW = 512
SSM_CH = 16
SSM_GROUPS = 32
SSM_STATE = 64
D_FF = 2816
RMS_EPS = 1e-6
NEG_INF = -1e30
FORCE_SCORE = 1e4

LANES = 128
VMEM_LIMIT = 56 * 1024 * 1024

Z_Q = 0
Z_KV = D_ATTN
Z_U = D_ATTN + 6 * KV_W
Z_G = Z_U + D_SSM
Z_W = Z_G + LANES


def _cparams(sem):
    return pltpu.CompilerParams(dimension_semantics=sem, vmem_limit_bytes=VMEM_LIMIT)


def _rms(x, g):
    return x * lax.rsqrt(jnp.mean(x * x, axis=-1, keepdims=True) + RMS_EPS) * g


def _dot(a, b):
    return jnp.dot(a, b, preferred_element_type=F32)


def _dot_nt(a, b):
    return lax.dot_general(a, b, (((1,), (1,)), ((), ())), preferred_element_type=F32)


def _ffn_body(x_ref, g_ref, wg_ref, wu_ref, wd_ref, gf_ref, o_ref, h_ref, acc_ref, *, n_f, final_norm):
    j = pl.program_id(1)

    @pl.when(j == 0)
    def _():
        h_ref[...] = _rms(x_ref[...], g_ref[...]).astype(BF16)
        acc_ref[...] = jnp.zeros_like(acc_ref)

    h = h_ref[...]
    a = _dot(h, wg_ref[...])
    b = _dot(h, wu_ref[...])
    t = (a * jax.nn.sigmoid(a) * b).astype(BF16)
    acc_ref[...] += _dot(t, wd_ref[...])

    @pl.when(j == n_f - 1)
    def _():
        y = x_ref[...] + 0.5 * acc_ref[...]
        if final_norm:
            y = _rms(y, gf_ref[...])
        o_ref[...] = y


def _ffn(x, g, wg, wu, wd, gf, final_norm):
    m = x.shape[0]
    tm = min(m, 1024)
    tf = D_FF // 2
    n_f = D_FF // tf
    return pl.pallas_call(
        functools.partial(_ffn_body, n_f=n_f, final_norm=final_norm),
        grid=(m // tm, n_f),
        in_specs=[
            pl.BlockSpec((tm, D_MODEL), lambda i, j: (i, 0)),
            pl.BlockSpec((1, D_MODEL), lambda i, j: (0, 0)),
            pl.BlockSpec((D_MODEL, tf), lambda i, j: (0, j)),
            pl.BlockSpec((D_MODEL, tf), lambda i, j: (0, j)),
            pl.BlockSpec((tf, D_MODEL), lambda i, j: (j, 0)),
            pl.BlockSpec((1, D_MODEL), lambda i, j: (0, 0)),
        ],
        out_specs=pl.BlockSpec((tm, D_MODEL), lambda i, j: (i, 0)),
        out_shape=jax.ShapeDtypeStruct((m, D_MODEL), F32),
        scratch_shapes=[pltpu.VMEM((tm, D_MODEL), BF16), pltpu.VMEM((tm, D_MODEL), F32)],
        compiler_params=_cparams(("parallel", "arbitrary")),
        name="ffn",
    )(x, g, wg, wu, wd, gf)


def _proj_body(x_ref, g_ref, w_ref, c_ref, sa_ref, sb_ref,
               q_ref, qr_ref, kc_ref, vc_ref, ks_ref, vs_ref, kw_ref, vw_ref, gt_ref, u_ref):
    h = _rms(x_ref[...], g_ref[...]).astype(BF16)
    z = _dot(h, w_ref[...])
    c, sa, sb = c_ref[...], sa_ref[...], sb_ref[...]

    def rope(v):
        return v * c + pltpu.roll(v, LANES - ROT_DIM // 2, 1) * sa + pltpu.roll(v, ROT_DIM // 2, 1) * sb

    q_ref[...] = z[:, Z_Q:Z_Q + D_ATTN]
    for i in range(D_ATTN // LANES):
        qr_ref[:, i * LANES:(i + 1) * LANES] = rope(z[:, Z_Q + i * LANES:Z_Q + (i + 1) * LANES])
    kv = [z[:, Z_KV + i * KV_W:Z_KV + (i + 1) * KV_W] for i in range(6)]
    kc_ref[...] = kv[0]
    vc_ref[...] = kv[1]
    ks_ref[...] = rope(kv[2])
    vs_ref[...] = kv[3]
    kw_ref[...] = rope(kv[4])
    vw_ref[...] = kv[5]
    u_ref[...] = z[:, Z_U:Z_U + D_SSM]
    gt_ref[...] = jax.nn.sigmoid(z[:, Z_G:Z_G + LANES])


def _proj(x, g, w, rope_c, rope_sa, rope_sb):
    m = x.shape[0]
    tm = min(m, 512)
    n_rep = rope_c.shape[0] // tm
    row = lambda i: (i, 0)
    const = lambda i: (0, 0)
    tab = lambda i: (i % n_rep, 0)
    widths = [D_ATTN, D_ATTN] + [KV_W] * 6 + [LANES, D_SSM]
    return pl.pallas_call(
        _proj_body,
        grid=(m // tm,),
        in_specs=[pl.BlockSpec((tm, D_MODEL), row), pl.BlockSpec((1, D_MODEL), const),
                  pl.BlockSpec((D_MODEL, Z_W), const),
                  pl.BlockSpec((tm, LANES), tab), pl.BlockSpec((tm, LANES), tab), pl.BlockSpec((tm, LANES), tab)],
        out_specs=[pl.BlockSpec((tm, wd), row) for wd in widths],
        out_shape=[jax.ShapeDtypeStruct((m, wd), F32) for wd in widths],
        compiler_params=_cparams(("parallel",)),
        name="proj",
    )(x, g, w, rope_c, rope_sa, rope_sb)


def _rope_tables(pos):
    half = ROT_DIM // 2
    inv_freq = jnp.power(ROPE_THETA, -(jnp.arange(half, dtype=F32) * 2.0 / ROT_DIM))
    ang = pos.astype(F32)[:, None] * inv_freq[None, :]
    cos, sin = jnp.cos(ang), jnp.sin(ang)
    n = pos.shape[0]
    one = jnp.ones((n, HEAD_DIM - ROT_DIM), F32)
    zero = jnp.zeros((n, HEAD_DIM - ROT_DIM), F32)
    zh = jnp.zeros((n, half), F32)
    c = jnp.concatenate([cos, cos, one], axis=1)
    sa = jnp.concatenate([-sin, zh, zero], axis=1)
    sb = jnp.concatenate([zh, sin, zero], axis=1)
    t2 = lambda a: jnp.concatenate([a, a], axis=1)
    return t2(c), t2(sa), t2(sb)


def _compress_rows(x_ref, n_half, w1_ref, b_ref, w2_ref):
    xs = [x_ref[0, pl.ds(s, n_half, stride=CMP_STRIDE), :] for s in range(CMP_STRIDE)]
    xcat = jnp.concatenate(xs, axis=1).astype(BF16)
    p = _dot(xcat, w1_ref[...])
    nh = N_KV_HEADS * CMP_HIDDEN
    p1_next = pltpu.roll(p[:, nh:], n_half - 1, 0)
    pre = p[:, :nh] + p1_next + b_ref[...]
    return _dot(jax.nn.gelu(pre).astype(BF16), w2_ref[...])


def _cmp_prompt_body(kc_ref, vc_ref, wk1_ref, bk_ref, wk2_ref, wv1_ref, bv_ref, wv2_ref, ko_ref, vo_ref, *, n_half):
    ko_ref[0] = _compress_rows(kc_ref, n_half, wk1_ref, bk_ref, wk2_ref)
    vo_ref[0] = _compress_rows(vc_ref, n_half, wv1_ref, bv_ref, wv2_ref)


def _cmp_prompt(kc, vc, wk, wv):
    b, t, _ = kc.shape
    n_half = t // CMP_STRIDE
    seq = pl.BlockSpec((1, t, KV_W), lambda i: (i, 0, 0))
    const = lambda shp: pl.BlockSpec(shp, lambda i: (0,) * len(shp))
    wspecs = [const(wk[0].shape), const(wk[1].shape), const(wk[2].shape)]
    out = pl.BlockSpec((1, n_half, KV_W), lambda i: (i, 0, 0))
    return pl.pallas_call(
        functools.partial(_cmp_prompt_body, n_half=n_half),
        grid=(b,),
        in_specs=[seq, seq] + wspecs + wspecs,
        out_specs=[out, out],
        out_shape=[jax.ShapeDtypeStruct((b, n_half, KV_W), F32)] * 2,
        compiler_params=_cparams(("parallel",)),
        name="cmp_prompt",
    )(kc, vc, *wk, *wv)


def _cmp_weights(pos_emb, w1, b1, w2):
    w1r = w1.reshape(CMP_RATIO, CMP_STRIDE, HEAD_DIM, CMP_HIDDEN)
    eye = jnp.eye(N_KV_HEADS, dtype=F32)
    w1p = jnp.einsum('rsdh,kj->skdrjh', w1r, eye).reshape(CMP_STRIDE * KV_W, CMP_RATIO * N_KV_HEADS * CMP_HIDDEN)
    pe = pos_emb.reshape(CMP_RATIO, CMP_STRIDE, HEAD_DIM)
    bias = b1 + jnp.einsum('rsd,rsdh->h', pe, w1r, precision=lax.Precision.HIGHEST)
    bias = jnp.tile(bias, N_KV_HEADS)[None, :]
    w2p = jnp.einsum('hd,kj->khjd', w2, eye).reshape(N_KV_HEADS * CMP_HIDDEN, KV_W)
    return w1p.astype(BF16), bias, w2p.astype(BF16)


TQ = 128
KCHUNK = 512
N_SEL_PROMPT = 32


def _softmax_rows(s):
    mx = jnp.max(s, axis=-1, keepdims=True)
    e = jnp.exp(s - mx)
    return e / jnp.sum(e, axis=-1, keepdims=True)


def _attn_prompt_body(q_ref, qr_ref, kcmp_ref, vcmp_ref, ks_ref, vs_ref, kw_ref, vw_ref, gt_ref,
                      amat_ref, emat_ref, gn_ref, o_ref, *, seq):
    t0 = pl.program_id(1) * TQ
    lane = lax.broadcasted_iota(jnp.int32, (TQ, LANES), 1)
    low = lane < HEAD_DIM
    tq = t0 + lax.broadcasted_iota(jnp.int32, (TQ, 1), 0)
    gates = gt_ref[0]

    def head_rows(src_ref, k):
        rows = []
        for g in range(GQA):
            h = GQA * k + g
            v = src_ref[0, :, (h // 2) * LANES:(h // 2 + 1) * LANES]
            if h % 2 != k:
                v = pltpu.roll(v, HEAD_DIM, 1)
            rows.append(jnp.where(low if k == 0 else jnp.logical_not(low), v, 0.0))
        return jnp.concatenate(rows, axis=0).astype(BF16)

    jj = lax.broadcasted_iota(jnp.int32, (N_SEL_PROMPT, TQ), 0)
    jq = (t0 + lax.broadcasted_iota(jnp.int32, (N_SEL_PROMPT, TQ), 1)) // SEL_BLOCK
    forced = (jj == 0) | (jj == jq) | (jj == jq - 1)

    n_cmp = kcmp_ref.shape[1]
    blk_end = lax.broadcasted_iota(jnp.int32, (1, n_cmp), 1) * CMP_STRIDE + (CMP_LEN - 1)
    m_cmp = (blk_end <= tq)[None]

    n_chunks = t0 // KCHUNK + 1
    w_keys = WINDOW + TQ
    w_start = pl.multiple_of(jnp.clip(t0 - WINDOW, 0, seq - w_keys), TQ)
    kp_w = w_start + lax.broadcasted_iota(jnp.int32, (1, w_keys), 1)
    m_win = ((kp_w <= tq) & (kp_w > tq - WINDOW))[None]

    comb = []
    for k in range(N_KV_HEADS):
        qc = head_rows(q_ref, k)
        qr = head_rows(qr_ref, k)

        s = _dot_nt(qc, kcmp_ref[0].astype(BF16)).reshape(GQA, TQ, n_cmp) * ATTN_SCALE
        p = jnp.where(m_cmp, _softmax_rows(jnp.where(m_cmp, s, NEG_INF)), 0.0)
        o_c = _dot(p.reshape(GQA * TQ, n_cmp).astype(BF16), vcmp_ref[0].astype(BF16))
        imp = jnp.sum(p, axis=0)

        imp_sel = jnp.dot(amat_ref[...], imp.T, precision=lax.Precision.HIGHEST, preferred_element_type=F32)
        score = jnp.where(jj > jq, -1.0, jnp.where(forced, FORCE_SCORE, imp_sel))
        rank = jnp.zeros((N_SEL_PROMPT, TQ), F32)
        for i in range(N_SEL_PROMPT):
            si = score[i:i + 1, :]
            beats = (si > score) | ((si == score) & (jj > i))
            rank = rank + jnp.where(beats, 1.0, 0.0)
        sel_t = jnp.where((rank < N_SELECT) & (jj <= jq), 1.0, 0.0)
        sel_t = jnp.concatenate([sel_t, jnp.zeros((LANES - N_SEL_PROMPT, TQ), F32)], axis=0)
        sel = sel_t.T.astype(BF16)

        def chunk(c, carry):
            m_i, l_i, acc = carry
            k0 = pl.multiple_of(c * KCHUNK, KCHUNK)
            kk = ks_ref[0, pl.ds(k0, KCHUNK), :].astype(BF16)
            vv = vs_ref[0, pl.ds(k0, KCHUNK), :].astype(BF16)
            sc = _dot_nt(qr, kk).reshape(GQA, TQ, KCHUNK) * ATTN_SCALE
            sel_e = _dot(sel, emat_ref[:, pl.ds(k0, KCHUNK)])
            kpos = k0 + lax.broadcasted_iota(jnp.int32, (1, KCHUNK), 1)
            msk = ((sel_e > 0.5) & (kpos <= tq))[None]
            sc = jnp.where(msk, sc, NEG_INF)
            m_n = jnp.maximum(m_i, jnp.max(sc, axis=-1, keepdims=True))
            alpha = jnp.exp(m_i - m_n)
            pe = jnp.exp(sc - m_n)
            l_n = alpha * l_i + jnp.sum(pe, axis=-1, keepdims=True)
            pv = _dot(pe.reshape(GQA * TQ, KCHUNK).astype(BF16), vv).reshape(GQA, TQ, LANES)
            return m_n, l_n, alpha * acc + pv

        init = (jnp.full((GQA, TQ, 1), NEG_INF, F32), jnp.zeros((GQA, TQ, 1), F32), jnp.zeros((GQA, TQ, LANES), F32))
        _, l_f, acc_f = lax.fori_loop(0, n_chunks, chunk, init)
        o_s = acc_f / l_f

        kk = kw_ref[0, pl.ds(w_start, w_keys), :].astype(BF16)
        vv = vw_ref[0, pl.ds(w_start, w_keys), :].astype(BF16)
        sw = _dot_nt(qr, kk).reshape(GQA, TQ, w_keys) * ATTN_SCALE
        pw = _softmax_rows(jnp.where(m_win, sw, NEG_INF))
        o_w = _dot(pw.reshape(GQA * TQ, w_keys).astype(BF16), vv).reshape(GQA, TQ, LANES)

        o_c = o_c.reshape(GQA, TQ, LANES)
        for g in range(GQA):
            h = GQA * k + g
            comb.append(gates[:, 3 * h:3 * h + 1] * o_c[g] + gates[:, 3 * h + 1:3 * h + 2] * o_s[g]
                        + gates[:, 3 * h + 2:3 * h + 3] * o_w[g])

    cols = []
    for pr in range(N_HEADS // 2):
        k = (2 * pr) // GQA
        a, b = comb[2 * pr], comb[2 * pr + 1]
        a = pltpu.roll(a, HEAD_DIM, 1) if k == 1 else a
        b = pltpu.roll(b, HEAD_DIM, 1) if k == 0 else b
        cols.append(jnp.where(low, a, b))
    o = jnp.concatenate(cols, axis=1)
    o_ref[0] = _rms(o, gn_ref[...]).astype(o_ref.dtype)


def _sel_matrices(n_cmp_pad, n_sel, n_keys):
    j = np.arange(n_sel)[:, None]
    n = np.arange(n_cmp_pad)[None, :]
    per = SEL_BLOCK // CMP_STRIDE
    amat = ((n >= per * j - (CMP_RATIO - 1)) & (n <= per * j + per - 1)).astype(np.float32)
    jrow = np.arange(LANES)[:, None]
    key = np.arange(n_keys)[None, :]
    emat = (key // SEL_BLOCK == jrow).astype(np.float32)
    return jnp.asarray(amat), jnp.asarray(emat, dtype=BF16)


def _attn_prompt(q, qr, kcmp, vcmp, ks, vs, kw, vw, gt, gn):
    b, t, _ = q.shape
    amat, emat = _sel_matrices(kcmp.shape[1], N_SEL_PROMPT, t)
    qt = lambda w: pl.BlockSpec((1, TQ, w), lambda i, j: (i, j, 0))
    full = lambda a: pl.BlockSpec((1,) + a.shape[1:], lambda i, j: (i, 0, 0))
    const = lambda a: pl.BlockSpec(a.shape, lambda i, j: (0, 0))
    return pl.pallas_call(
        functools.partial(_attn_prompt_body, seq=t),
        grid=(b, t // TQ),
        in_specs=[qt(D_ATTN), qt(D_ATTN), full(kcmp), full(vcmp), full(ks), full(vs), full(kw), full(vw),
                  qt(LANES), const(amat), const(emat), const(gn)],
        out_specs=qt(D_ATTN),
        out_shape=jax.ShapeDtypeStruct((b, t, D_ATTN), BF16),
        compiler_params=_cparams(("parallel", "arbitrary")),
        name="attn_prompt",
    )(q, qr, kcmp, vcmp, ks, vs, kw, vw, gt, amat, emat, gn)


S5_LB = D_SSM // LANES
S5_SW = LANES // SSM_CH * SSM_STATE


def _s5_body(u_ref, h0r_ref, h0i_ref, lr_ref, li_ref, bre_ref, bim_ref, c_ref, d_ref, gw_ref, gb_ref, gn_ref,
             y_ref, hr_ref, hi_ref, xr_s, xi_s, y_s, *, tc, bb):
    step = pl.program_id(0)
    m = tc * bb

    @pl.when(step == 0)
    def _():
        hr_ref[...] = h0r_ref[...]
        hi_ref[...] = h0i_ref[...]

    u = u_ref[...].reshape(m, D_SSM)
    ub = u.astype(BF16)
    for j in range(S5_LB):
        uj = ub[:, j * LANES:(j + 1) * LANES]
        sl = slice(j * S5_SW, (j + 1) * S5_SW)
        xr_s[...] = _dot(uj, bre_ref[j]).reshape(tc, bb, S5_SW)
        xi_s[...] = _dot(uj, bim_ref[j]).reshape(tc, bb, S5_SW)
        lr = jnp.broadcast_to(lr_ref[:, sl], (bb, S5_SW))
        li = jnp.broadcast_to(li_ref[:, sl], (bb, S5_SW))

        def scan(t, carry):
            hr, hi = carry
            nr = lr * hr - li * hi + xr_s[t]
            ni = lr * hi + li * hr + xi_s[t]
            xr_s[t] = nr
            xi_s[t] = ni
            return nr, ni

        hr, hi = lax.fori_loop(0, tc, scan, (hr_ref[:, sl], hi_ref[:, sl]))
        hr_ref[:, sl] = hr
        hi_ref[:, sl] = hi
        hcat = jnp.concatenate([xr_s[...].reshape(m, S5_SW), xi_s[...].reshape(m, S5_SW)], axis=1).astype(BF16)
        y_s[:, j * LANES:(j + 1) * LANES] = _dot(hcat, c_ref[j])
    z = jax.nn.gelu(y_s[...] + d_ref[...] * u)
    out = z * jax.nn.sigmoid(_dot(z.astype(BF16), gw_ref[...]) + gb_ref[...])
    y_ref[...] = _rms(out, gn_ref[...]).astype(y_ref.dtype).reshape(tc, bb, D_SSM)


def _s5(u_tm, h0r, h0i, sp, tc):
    t, bb, _ = u_tm.shape
    const = lambda a: pl.BlockSpec(a.shape, lambda i: (0,) * a.ndim)
    args = (h0r, h0i, sp['lr'], sp['li'], sp['bre'], sp['bim'], sp['c'], sp['d'], sp['glu_w'], sp['glu_b'], sp['gn'])
    st = jax.ShapeDtypeStruct(h0r.shape, F32)
    return pl.pallas_call(
        functools.partial(_s5_body, tc=tc, bb=bb),
        grid=(t // tc,),
        in_specs=[pl.BlockSpec((tc, bb, D_SSM), lambda i: (i, 0, 0))] + [const(a) for a in args],
        out_specs=[pl.BlockSpec((tc, bb, D_SSM), lambda i: (i, 0, 0)), const(h0r), const(h0i)],
        out_shape=[jax.ShapeDtypeStruct((t, bb, D_SSM), F32), st, st],
        scratch_shapes=[pltpu.VMEM((tc, bb, S5_SW), F32), pltpu.VMEM((tc, bb, S5_SW), F32),
                        pltpu.VMEM((tc * bb, D_SSM), F32)],
        compiler_params=_cparams(("arbitrary",)),
        name="s5",
    )(u_tm, *args)


def _s5_params(log_dt, a_re, a_im, b_re, b_im, c_re, c_im, d, glu_w, glu_b, gn):
    dt = jnp.exp(log_dt)[:, None]
    mag = jnp.exp(a_re * dt)
    lr, li = mag * jnp.cos(a_im * dt), mag * jnp.sin(a_im * dt)
    den = a_re * a_re + a_im * a_im
    inv_r, inv_i = a_re / den, -a_im / den
    nr, ni = lr - 1.0, li
    fr, fi = nr * inv_r - ni * inv_i, nr * inv_i + ni * inv_r
    bbr = fr[..., None] * b_re - fi[..., None] * b_im
    bbi = fr[..., None] * b_im + fi[..., None] * b_re
    gl = LANES // SSM_CH
    eye = jnp.eye(gl, dtype=F32)

    def bmat(bb):
        x = bb.reshape(S5_LB, gl, SSM_STATE, SSM_CH)
        return jnp.einsum('jgpc,gh->jgchp', x, eye).reshape(S5_LB, LANES, S5_SW).astype(BF16)

    def cmat(cc):
        x = cc.reshape(S5_LB, gl, SSM_CH, SSM_STATE)
        return jnp.einsum('jgcp,gh->jgphc', x, eye).reshape(S5_LB, S5_SW, LANES)

    cm = jnp.concatenate([cmat(c_re), -cmat(c_im)], axis=1).astype(BF16)
    return dict(lr=lr.reshape(1, -1), li=li.reshape(1, -1), bre=bmat(bbr), bim=bmat(bbi), c=cm,
                d=d.reshape(1, -1), glu_w=glu_w.astype(BF16), glu_b=glu_b[None, :], gn=gn[None, :])


def _outproj_body(x_ref, a_ref, s_ref, wa_ref, ws_ref, o_ref):
    o_ref[...] = x_ref[...] + _dot(a_ref[...], wa_ref[...]) + _dot(s_ref[...].astype(BF16), ws_ref[...])


def _outproj(x, a, s, wa, ws):
    m = x.shape[0]
    tm = min(m, 1024)
    row = lambda w: pl.BlockSpec((tm, w), lambda i: (i, 0))
    const = lambda a_: pl.BlockSpec(a_.shape, lambda i: (0, 0))
    return pl.pallas_call(
        _outproj_body,
        grid=(m // tm,),
        in_specs=[row(D_MODEL), row(D_ATTN), row(D_SSM), const(wa), const(ws)],
        out_specs=row(D_MODEL),
        out_shape=jax.ShapeDtypeStruct((m, D_MODEL), F32),
        compiler_params=_cparams(("parallel",)),
        name="outproj",
    )(x, a, s, wa, ws)


def _sample_attention_jnp(q, q_rot, kc, vc, ks, vs, kw, vw, gates, ck, cv, sk, sv, wk, wv, page_table, cw, gn):
    B, T = q.shape[:2]
    pos = PAST_LEN + jnp.arange(T, dtype=jnp.int32)
    q = q.reshape(B, T, N_HEADS, HEAD_DIM)
    q_rot = q_rot.reshape(B, T, N_HEADS, HEAD_DIM)
    kvr = lambda a: a.reshape(B, T, N_KV_HEADS, HEAD_DIM)
    kc, vc, ks, vs, kw, vw = map(kvr, (kc, vc, ks, vs, kw, vw))
    gates = gates[..., :3 * N_HEADS].reshape(B, T, N_HEADS, 3)
    past = lambda pool: pool[page_table].reshape(B, -1, N_KV_HEADS, HEAD_DIM)
    kc_all, vc_all = jnp.concatenate([past(ck), kc], axis=1), jnp.concatenate([past(cv), vc], axis=1)
    ks_all, vs_all = jnp.concatenate([past(sk), ks], axis=1), jnp.concatenate([past(sv), vs], axis=1)

    def compress_blocks(k, pos_emb, w1, b1, w2):
        Bk, Tk = k.shape[:2]
        nh = Tk // CMP_STRIDE
        nc = nh - CMP_RATIO + 1
        halves = k[:, :nh * CMP_STRIDE].reshape(Bk, nh, CMP_STRIDE, N_KV_HEADS, HEAD_DIM)
        pe = pos_emb.reshape(CMP_RATIO, CMP_STRIDE, HEAD_DIM)
        w1r = w1.reshape(CMP_RATIO, CMP_STRIDE, HEAD_DIM, CMP_HIDDEN)
        pre = b1
        for r in range(CMP_RATIO):
            pre = pre + jnp.einsum('bnskd,sdh->bnkh', halves[:, r:r + nc] + pe[r][:, None, :], w1r[r])
        return jax.nn.gelu(pre) @ w2

    k_cmp = compress_blocks(kc_all, *cw[0])
    v_cmp = compress_blocks(vc_all, *cw[1])
    nc = k_cmp.shape[1]
    qg = q.reshape(B, T, N_KV_HEADS, GQA, HEAD_DIM)
    s = jnp.einsum('bqkgd,bnkd->bqkgn', qg, k_cmp).astype(F32) * ATTN_SCALE
    blk_end = jnp.arange(nc) * CMP_STRIDE + CMP_LEN - 1
    m = (blk_end[None, :] <= pos[:, None])[None, :, None, None, :]
    p = jnp.where(m, jax.nn.softmax(jnp.where(m, s, NEG_INF), axis=-1), 0.0)
    o_cmp = jnp.einsum('bqkgn,bnkd->bqkgd', p, v_cmp).reshape(B, T, N_HEADS, HEAD_DIM)
    imp = p.sum(axis=3)
    n_sel = -(-(PAST_LEN + T) // SEL_BLOCK)
    per = SEL_BLOCK // CMP_STRIDE
    total = n_sel * per + CMP_RATIO + per
    pp = jnp.pad(imp, ((0, 0), (0, 0), (0, 0), (CMP_RATIO - 1, total - nc - (CMP_RATIO - 1))))
    impb = sum(pp[..., o:o + n_sel * per:per] for o in range(per + CMP_RATIO - 1))
    j = jnp.arange(n_sel)[None, None, None, :]
    jq = (pos // SEL_BLOCK)[None, :, None, None]
    forced = (j == 0) | (j == jq) | (j == jq - 1)
    score = jnp.where(j > jq, -1.0, jnp.where(forced, FORCE_SCORE, impb))
    _, idx = lax.top_k(score, min(N_SELECT, n_sel))
    valid = idx <= jq

    def to_blocks(k):
        kp = jnp.pad(k, ((0, 0), (0, n_sel * SEL_BLOCK - k.shape[1]), (0, 0), (0, 0)))
        return kp.reshape(B, n_sel, SEL_BLOCK, N_KV_HEADS, HEAD_DIM).transpose(0, 3, 1, 2, 4)

    kb, vb = to_blocks(ks_all), to_blocks(vs_all)
    qg = q_rot.reshape(B, T, N_KV_HEADS, GQA, HEAD_DIM)
    bi = jnp.arange(B)[:, None, None, None]
    hi = jnp.arange(N_KV_HEADS)[None, None, :, None]
    ksel, vsel = kb[bi, hi, idx], vb[bi, hi, idx]
    s = jnp.einsum('bqkgd,bqknsd->bqkgns', qg, ksel).astype(F32) * ATTN_SCALE
    kpos = idx[..., None] * SEL_BLOCK + jnp.arange(SEL_BLOCK)
    mask = valid[..., None] & (kpos <= pos[None, :, None, None, None])
    s = jnp.where(mask[:, :, :, None], s, NEG_INF)
    shp = s.shape
    p = jax.nn.softmax(s.reshape(shp[:-2] + (-1,)), axis=-1).reshape(shp)
    o_slc = jnp.einsum('bqkgns,bqknsd->bqkgd', p, vsel).reshape(B, T, N_HEADS, HEAD_DIM)
    wb = wk.shape[1]
    kw_all, vw_all = jnp.concatenate([wk, kw], axis=1), jnp.concatenate([wv, vw], axis=1)
    k_pos = PAST_LEN - wb + jnp.arange(wb + T, dtype=jnp.int32)
    s = jnp.einsum('bqkgd,bjkd->bqkgj', qg, kw_all).astype(F32) * ATTN_SCALE
    kp, qp = k_pos[None, :], pos[:, None]
    m = ((kp >= 0) & (kp <= qp) & (kp > qp - WINDOW))[None, :, None, None, :]
    p = jax.nn.softmax(jnp.where(m, s, NEG_INF), axis=-1)
    o_win = jnp.einsum('bqkgj,bjkd->bqkgd', p, vw_all).reshape(B, T, N_HEADS, HEAD_DIM)
    o = (gates[..., 0:1] * o_cmp + gates[..., 1:2] * o_slc + gates[..., 2:3] * o_win).reshape(B, T, D_ATTN)
    return _rms(o, gn).astype(BF16)


def kernel(x_prompt, x_sample, cache_cmp_k, cache_cmp_v, cache_slc_k, cache_slc_v, state_win_k, state_win_v, state_s5_re, state_s5_im, page_table, norm_ffn1, ffn1_gate, ffn1_up, ffn1_down, norm_mix, w_in, cmp_k_pos, cmp_k_w1, cmp_k_b1, cmp_k_w2, cmp_v_pos, cmp_v_w1, cmp_v_b1, cmp_v_w2, s5_log_dt, s5_a_re, s5_a_im, s5_b_re, s5_b_im, s5_c_re, s5_c_im, s5_d, s5_glu_w, s5_glu_b, norm_attn_out, norm_ssm_out, w_out, norm_ffn2, ffn2_gate, ffn2_up, ffn2_down, norm_final):
    depth = w_in.shape[0]
    assert depth == 1
    l = 0
    B, T, _ = x_prompt.shape
    BS, TS, _ = x_sample.shape
    row = lambda a: a[None, :]

    w = w_in[l]
    cut = D_ATTN + 6 * KV_W
    w_p = jnp.concatenate([w[:, :cut], w[:, cut + 3 * N_HEADS:], w[:, cut:cut + 3 * N_HEADS],
                           jnp.zeros((D_MODEL, LANES - 3 * N_HEADS), F32)], axis=1).astype(BF16)
    f1 = (row(norm_ffn1[l]), ffn1_gate[l].astype(BF16), ffn1_up[l].astype(BF16), ffn1_down[l].astype(BF16))
    f2 = (row(norm_ffn2[l]), ffn2_gate[l].astype(BF16), ffn2_up[l].astype(BF16), ffn2_down[l].astype(BF16))
    gfin = row(norm_final)
    wo = w_out[l].astype(BF16)
    wo_a, wo_s = wo[:D_ATTN], wo[D_ATTN:]
    cwk = _cmp_weights(cmp_k_pos[l], cmp_k_w1[l], cmp_k_b1[l], cmp_k_w2[l])
    cwv = _cmp_weights(cmp_v_pos[l], cmp_v_w1[l], cmp_v_b1[l], cmp_v_w2[l])
    sp = _s5_params(s5_log_dt[l], s5_a_re[l], s5_a_im[l], s5_b_re[l], s5_b_im[l], s5_c_re[l], s5_c_im[l],
                    s5_d[l], s5_glu_w[l], s5_glu_b[l], norm_ssm_out[l])
    gn_attn = row(norm_attn_out[l])
    g_mix = row(norm_mix[l])

    xp = x_prompt.reshape(B * T, D_MODEL)
    x1 = _ffn(xp, *f1, gfin, False)
    tabs = _rope_tables(jnp.arange(T, dtype=jnp.int32))
    q, qr, kc, vc, ks, vs, kw, vw, gt, u = _proj(x1, g_mix, w_p, *tabs)
    b3 = lambda a: a.reshape(B, T, a.shape[-1])
    kcmp, vcmp = _cmp_prompt(b3(kc), b3(vc), cwk, cwv)
    attn = _attn_prompt(b3(q), b3(qr), kcmp, vcmp, b3(ks), b3(vs), b3(kw), b3(vw), b3(gt), gn_attn)
    zeros = jnp.zeros((B, SSM_GROUPS * SSM_STATE), F32)
    y_tm, p_re, p_im = _s5(b3(u).transpose(1, 0, 2), zeros, zeros, sp, 64)
    ssm = y_tm.transpose(1, 0, 2).reshape(B * T, D_SSM)
    x2 = _outproj(x1, attn.reshape(B * T, D_ATTN), ssm, wo_a, wo_s)
    y_prompt = _ffn(x2, *f2, gfin, True).reshape(B, T, D_MODEL)
    kv5 = lambda a, n: a.reshape(1, n, -1, N_KV_HEADS, HEAD_DIM)
    wb = min(WINDOW, T)
    p_state = (kv5(kc, B), kv5(vc, B), kv5(ks, B), kv5(vs, B), kv5(kw, B)[:, :, T - wb:], kv5(vw, B)[:, :, T - wb:],
               p_re.reshape(1, B, SSM_GROUPS, SSM_STATE), p_im.reshape(1, B, SSM_GROUPS, SSM_STATE))

    xs = x_sample.reshape(BS * TS, D_MODEL)
    s1 = _ffn(xs, *f1, gfin, False)
    pos_s = PAST_LEN + jnp.arange(TS, dtype=jnp.int32)
    tabs_s = [jnp.tile(a, (BS, 1)) for a in _rope_tables(pos_s)]
    q, qr, kc, vc, ks, vs, kw, vw, gt, u = _proj(s1, g_mix, w_p, *tabs_s)
    s3 = lambda a: a.reshape(BS, TS, a.shape[-1])
    attn_s = _sample_attention_jnp(
        s3(q), s3(qr), s3(kc), s3(vc), s3(ks), s3(vs), s3(kw), s3(vw), s3(gt),
        cache_cmp_k[l], cache_cmp_v[l], cache_slc_k[l], cache_slc_v[l], state_win_k[l], state_win_v[l], page_table,
        ((cmp_k_pos[l], cmp_k_w1[l], cmp_k_b1[l], cmp_k_w2[l]), (cmp_v_pos[l], cmp_v_w1[l], cmp_v_b1[l], cmp_v_w2[l])),
        norm_attn_out[l])
    y_tm, s_re, s_im = _s5(s3(u).transpose(1, 0, 2), state_s5_re[l].reshape(BS, -1), state_s5_im[l].reshape(BS, -1), sp, TS)
    ssm_s = y_tm.transpose(1, 0, 2).reshape(BS * TS, D_SSM)
    s2 = _outproj(s1, attn_s.reshape(BS * TS, D_ATTN), ssm_s, wo_a, wo_s)
    y_sample = _ffn(s2, *f2, gfin, True).reshape(BS, TS, D_MODEL)
    win_k = jnp.concatenate([state_win_k[l], kv5(kw, BS)[0]], axis=1)[:, TS:][None]
    win_v = jnp.concatenate([state_win_v[l], kv5(vw, BS)[0]], axis=1)[:, TS:][None]
    s_state = (kv5(kc, BS), kv5(vc, BS), kv5(ks, BS), kv5(vs, BS), win_k, win_v,
               s_re.reshape(1, BS, SSM_GROUPS, SSM_STATE), s_im.reshape(1, BS, SSM_GROUPS, SSM_STATE))

    return (y_prompt, y_sample) + p_state + s_state
```

```python
import functools
import math

import numpy as np
import jax
import jax.numpy as jnp
from jax import lax
from jax.experimental import pallas as pl
from jax.experimental.pallas import tpu as pltpu

F32 = jnp.float32
BF16 = jnp.bfloat16

D_MODEL = 1024
PAST_LEN = 16384
PAGE_SIZE = 128
D_ATTN = 512
D_SSM = 512
HEAD_DIM = 64
N_HEADS = 8
N_KV_HEADS = 2
GQA = 4
KV_W = 128
ROT_DIM = 16
ROPE_THETA = 500000.0
ATTN_SCALE = HEAD_DIM ** -0.5
CMP_LEN = 32
CMP_STRIDE = 16
CMP_RATIO = 2
CMP_HIDDEN = 256
SEL_BLOCK = 64
N_SELECT = 16
WINDOW = 512
SSM_CH = 16
SSM_GROUPS = 32
SSM_STATE = 64
D_FF = 2816
RMS_EPS = 1e-6
NEG_INF = -1e30
FORCE_SCORE = 1e4

LANES = 128
VMEM_LIMIT = 56 * 1024 * 1024

Z_Q = 0
Z_KV = D_ATTN
Z_U = D_ATTN + 6 * KV_W
Z_G = Z_U + D_SSM
Z_W = Z_G + LANES


def _cparams(sem):
    return pltpu.CompilerParams(dimension_semantics=sem, vmem_limit_bytes=VMEM_LIMIT)


def _rms(x, g):
    return x * lax.rsqrt(jnp.mean(x * x, axis=-1, keepdims=True) + RMS_EPS) * g


def _dot(a, b):
    return jnp.dot(a, b, preferred_element_type=F32)


def _dot_nt(a, b):
    return lax.dot_general(a, b, (((1,), (1,)), ((), ())), preferred_element_type=F32)


def _ffn_body(x_ref, g_ref, wg_ref, wu_ref, wd_ref, gf_ref, o_ref, h_ref, acc_ref, *, n_f, final_norm):
    j = pl.program_id(1)

    @pl.when(j == 0)
    def _():
        h_ref[...] = _rms(x_ref[...], g_ref[...]).astype(BF16)
        acc_ref[...] = jnp.zeros_like(acc_ref)

    h = h_ref[...]
    a = _dot(h, wg_ref[...])
    b = _dot(h, wu_ref[...])
    t = (a * jax.nn.sigmoid(a) * b).astype(BF16)
    acc_ref[...] += _dot(t, wd_ref[...])

    @pl.when(j == n_f - 1)
    def _():
        y = x_ref[...] + 0.5 * acc_ref[...]
        if final_norm:
            y = _rms(y, gf_ref[...])
        o_ref[...] = y


def _ffn(x, g, wg, wu, wd, gf, final_norm):
    m = x.shape[0]
    tm = min(m, 1024)
    tf = D_FF // 2
    n_f = D_FF // tf
    return pl.pallas_call(
        functools.partial(_ffn_body, n_f=n_f, final_norm=final_norm),
        grid=(m // tm, n_f),
        in_specs=[
            pl.BlockSpec((tm, D_MODEL), lambda i, j: (i, 0)),
            pl.BlockSpec((1, D_MODEL), lambda i, j: (0, 0)),
            pl.BlockSpec((D_MODEL, tf), lambda i, j: (0, j)),
            pl.BlockSpec((D_MODEL, tf), lambda i, j: (0, j)),
            pl.BlockSpec((tf, D_MODEL), lambda i, j: (j, 0)),
            pl.BlockSpec((1, D_MODEL), lambda i, j: (0, 0)),
        ],
        out_specs=pl.BlockSpec((tm, D_MODEL), lambda i, j: (i, 0)),
        out_shape=jax.ShapeDtypeStruct((m, D_MODEL), F32),
        scratch_shapes=[pltpu.VMEM((tm, D_MODEL), BF16), pltpu.VMEM((tm, D_MODEL), F32)],
        compiler_params=_cparams(("parallel", "arbitrary")),
        name="ffn",
    )(x, g, wg, wu, wd, gf)


def _proj_body(x_ref, g_ref, w_ref, c_ref, sa_ref, sb_ref,
               q_ref, qr_ref, kc_ref, vc_ref, ks_ref, vs_ref, kw_ref, vw_ref, gt_ref, u_ref, *t_refs):
    h = _rms(x_ref[...], g_ref[...]).astype(BF16)
    z = _dot(h, w_ref[...])
    c, sa, sb = c_ref[...], sa_ref[...], sb_ref[...]

    def rope(v):
        return v * c + pltpu.roll(v, LANES - ROT_DIM // 2, 1) * sa + pltpu.roll(v, ROT_DIM // 2, 1) * sb

    q_ref[...] = z[:, Z_Q:Z_Q + D_ATTN]
    for i in range(D_ATTN // LANES):
        qr_ref[:, i * LANES:(i + 1) * LANES] = rope(z[:, Z_Q + i * LANES:Z_Q + (i + 1) * LANES])
    kv = [z[:, Z_KV + i * KV_W:Z_KV + (i + 1) * KV_W] for i in range(6)]
    kv[2] = rope(kv[2])
    kv[4] = rope(kv[4])
    for ref, v in zip((kc_ref, vc_ref, ks_ref, vs_ref, kw_ref, vw_ref), kv):
        ref[...] = v
    for ref, v in zip(t_refs, kv):
        ref[0] = v.T
    u_ref[...] = z[:, Z_U:Z_U + D_SSM]
    gt_ref[...] = jax.nn.sigmoid(z[:, Z_G:Z_G + LANES])


def _proj(x, g, w, rope_c, rope_sa, rope_sb, seq_t=None):
    m = x.shape[0]
    tm = min(m, 512)
    n_rep = rope_c.shape[0] // tm
    row = lambda i: (i, 0)
    const = lambda i: (0, 0)
    tab = lambda i: (i % n_rep, 0)
    widths = [D_ATTN, D_ATTN] + [KV_W] * 6 + [LANES, D_SSM]
    out_specs = [pl.BlockSpec((tm, wd), row) for wd in widths]
    out_shape = [jax.ShapeDtypeStruct((m, wd), F32) for wd in widths]
    if seq_t is not None:
        assert rope_c.shape[0] == seq_t
        out_specs += [pl.BlockSpec((1, KV_W, tm), lambda i: (i // n_rep, 0, i % n_rep))] * 6
        out_shape += [jax.ShapeDtypeStruct((m // seq_t, KV_W, seq_t), F32)] * 6
    return pl.pallas_call(
        _proj_body,
        grid=(m // tm,),
        in_specs=[pl.BlockSpec((tm, D_MODEL), row), pl.BlockSpec((1, D_MODEL), const),
                  pl.BlockSpec((D_MODEL, Z_W), const),
                  pl.BlockSpec((tm, LANES), tab), pl.BlockSpec((tm, LANES), tab), pl.BlockSpec((tm, LANES), tab)],
        out_specs=out_specs,
        out_shape=out_shape,
        compiler_params=_cparams(("parallel",)),
        name="proj",
    )(x, g, w, rope_c, rope_sa, rope_sb)


def _rope_tables(pos):
    half = ROT_DIM // 2
    inv_freq = jnp.power(ROPE_THETA, -(jnp.arange(half, dtype=F32) * 2.0 / ROT_DIM))
    ang = pos.astype(F32)[:, None] * inv_freq[None, :]
    cos, sin = jnp.cos(ang), jnp.sin(ang)
    n = pos.shape[0]
    one = jnp.ones((n, HEAD_DIM - ROT_DIM), F32)
    zero = jnp.zeros((n, HEAD_DIM - ROT_DIM), F32)
    zh = jnp.zeros((n, half), F32)
    c = jnp.concatenate([cos, cos, one], axis=1)
    sa = jnp.concatenate([-sin, zh, zero], axis=1)
    sb = jnp.concatenate([zh, sin, zero], axis=1)
    t2 = lambda a: jnp.concatenate([a, a], axis=1)
    return t2(c), t2(sa), t2(sb)


def _compress_rows(x_ref, n_half, w1_ref, b_ref, w2_ref):
    xs = [x_ref[0, pl.ds(s, n_half, stride=CMP_STRIDE), :] for s in range(CMP_STRIDE)]
    xcat = jnp.concatenate(xs, axis=1).astype(BF16)
    p = _dot(xcat, w1_ref[...])
    nh = N_KV_HEADS * CMP_HIDDEN
    p1_next = pltpu.roll(p[:, nh:], n_half - 1, 0)
    pre = p[:, :nh] + p1_next + b_ref[...]
    return _dot(jax.nn.gelu(pre).astype(BF16), w2_ref[...])


def _cmp_prompt_body(kc_ref, vc_ref, wk1_ref, bk_ref, wk2_ref, wv1_ref, bv_ref, wv2_ref, ko_ref, vo_ref, *, n_half):
    ko_ref[0] = _compress_rows(kc_ref, n_half, wk1_ref, bk_ref, wk2_ref)
    vo_ref[0] = _compress_rows(vc_ref, n_half, wv1_ref, bv_ref, wv2_ref)


def _cmp_prompt(kc, vc, wk, wv):
    b, t, _ = kc.shape
    n_half = t // CMP_STRIDE
    seq = pl.BlockSpec((1, t, KV_W), lambda i: (i, 0, 0))
    const = lambda shp: pl.BlockSpec(shp, lambda i: (0,) * len(shp))
    wspecs = [const(wk[0].shape), const(wk[1].shape), const(wk[2].shape)]
    out = pl.BlockSpec((1, n_half, KV_W), lambda i: (i, 0, 0))
    return pl.pallas_call(
        functools.partial(_cmp_prompt_body, n_half=n_half),
        grid=(b,),
        in_specs=[seq, seq] + wspecs + wspecs,
        out_specs=[out, out],
        out_shape=[jax.ShapeDtypeStruct((b, n_half, KV_W), F32)] * 2,
        compiler_params=_cparams(("parallel",)),
        name="cmp_prompt",
    )(kc, vc, *wk, *wv)


def _cmp_weights(pos_emb, w1, b1, w2):
    w1r = w1.reshape(CMP_RATIO, CMP_STRIDE, HEAD_DIM, CMP_HIDDEN)
    eye = jnp.eye(N_KV_HEADS, dtype=F32)
    w1p = jnp.einsum('rsdh,kj->skdrjh', w1r, eye).reshape(CMP_STRIDE * KV_W, CMP_RATIO * N_KV_HEADS * CMP_HIDDEN)
    pe = pos_emb.reshape(CMP_RATIO, CMP_STRIDE, HEAD_DIM)
    bias = b1 + jnp.einsum('rsd,rsdh->h', pe, w1r, precision=lax.Precision.HIGHEST)
    bias = jnp.tile(bias, N_KV_HEADS)[None, :]
    w2p = jnp.einsum('hd,kj->khjd', w2, eye).reshape(N_KV_HEADS * CMP_HIDDEN, KV_W)
    return w1p.astype(BF16), bias, w2p.astype(BF16)


TQ = 128
KCHUNK = 512
N_SEL_PROMPT = 32


def _softmax_rows(s):
    mx = jnp.max(s, axis=-1, keepdims=True)
    e = jnp.exp(s - mx)
    return e / jnp.sum(e, axis=-1, keepdims=True)


def _attn_prompt_body(q_ref, qr_ref, kcmp_ref, vcmp_ref, ks_ref, vs_ref, kw_ref, vw_ref, gt_ref,
                      amat_ref, emat_ref, gn_ref, o_ref, *, seq):
    t0 = pl.program_id(1) * TQ
    lane = lax.broadcasted_iota(jnp.int32, (TQ, LANES), 1)
    low = lane < HEAD_DIM
    tq = t0 + lax.broadcasted_iota(jnp.int32, (TQ, 1), 0)
    gates = gt_ref[0]

    def head_rows(src_ref, k):
        rows = []
        for g in range(GQA):
            h = GQA * k + g
            v = src_ref[0, :, (h // 2) * LANES:(h // 2 + 1) * LANES]
            if h % 2 != k:
                v = pltpu.roll(v, HEAD_DIM, 1)
            rows.append(jnp.where(low if k == 0 else jnp.logical_not(low), v, 0.0))
        return jnp.concatenate(rows, axis=0).astype(BF16)

    jj = lax.broadcasted_iota(jnp.int32, (N_SEL_PROMPT, TQ), 0)
    jq = (t0 + lax.broadcasted_iota(jnp.int32, (N_SEL_PROMPT, TQ), 1)) // SEL_BLOCK
    forced = (jj == 0) | (jj == jq) | (jj == jq - 1)

    n_cmp = kcmp_ref.shape[1]
    blk_end = lax.broadcasted_iota(jnp.int32, (1, n_cmp), 1) * CMP_STRIDE + (CMP_LEN - 1)
    m_cmp = (blk_end <= tq)[None]

    n_chunks = t0 // KCHUNK + 1
    w_keys = WINDOW + TQ
    w_start = pl.multiple_of(jnp.clip(t0 - WINDOW, 0, seq - w_keys), TQ)
    kp_w = w_start + lax.broadcasted_iota(jnp.int32, (1, w_keys), 1)
    m_win = ((kp_w <= tq) & (kp_w > tq - WINDOW))[None]

    comb = []
    for k in range(N_KV_HEADS):
        qc = head_rows(q_ref, k)
        qr = head_rows(qr_ref, k)

        s = _dot_nt(qc, kcmp_ref[0].astype(BF16)).reshape(GQA, TQ, n_cmp) * ATTN_SCALE
        p = jnp.where(m_cmp, _softmax_rows(jnp.where(m_cmp, s, NEG_INF)), 0.0)
        o_c = _dot(p.reshape(GQA * TQ, n_cmp).astype(BF16), vcmp_ref[0].astype(BF16))
        imp = jnp.sum(p, axis=0)

        imp_sel = jnp.dot(amat_ref[...], imp.T, precision=lax.Precision.HIGHEST, preferred_element_type=F32)
        score = jnp.where(jj > jq, -1.0, jnp.where(forced, FORCE_SCORE, imp_sel))
        rank = jnp.zeros((N_SEL_PROMPT, TQ), F32)
        for i in range(N_SEL_PROMPT):
            si = score[i:i + 1, :]
            beats = (si > score) | ((si == score) & (jj > i))
            rank = rank + jnp.where(beats, 1.0, 0.0)
        sel_t = jnp.where((rank < N_SELECT) & (jj <= jq), 1.0, 0.0)
        sel_t = jnp.concatenate([sel_t, jnp.zeros((LANES - N_SEL_PROMPT, TQ), F32)], axis=0)
        sel = sel_t.T.astype(BF16)

        def chunk(c, carry):
            m_i, l_i, acc = carry
            k0 = pl.multiple_of(c * KCHUNK, KCHUNK)
            kk = ks_ref[0, :, pl.ds(k0, KCHUNK)].astype(BF16)
            vv = vs_ref[0, :, pl.ds(k0, KCHUNK)].astype(BF16)
            sc = _dot(qr, kk).reshape(GQA, TQ, KCHUNK) * ATTN_SCALE
            sel_e = _dot(sel, emat_ref[:, pl.ds(k0, KCHUNK)])
            kpos = k0 + lax.broadcasted_iota(jnp.int32, (1, KCHUNK), 1)
            msk = ((sel_e > 0.5) & (kpos <= tq))[None]
            sc = jnp.where(msk, sc, NEG_INF)
            m_n = jnp.maximum(m_i, jnp.max(sc, axis=-1, keepdims=True))
            alpha = jnp.exp(m_i - m_n)
            pe = jnp.exp(sc - m_n)
            l_n = alpha * l_i + jnp.sum(pe, axis=-1, keepdims=True)
            pv = _dot_nt(pe.reshape(GQA * TQ, KCHUNK).astype(BF16), vv).reshape(GQA, TQ, LANES)
            return m_n, l_n, alpha * acc + pv

        init = (jnp.full((GQA, TQ, 1), NEG_INF, F32), jnp.zeros((GQA, TQ, 1), F32), jnp.zeros((GQA, TQ, LANES), F32))
        _, l_f, acc_f = lax.fori_loop(0, n_chunks, chunk, init)
        o_s = acc_f / l_f

        kk = kw_ref[0, :, pl.ds(w_start, w_keys)].astype(BF16)
        vv = vw_ref[0, :, pl.ds(w_start, w_keys)].astype(BF16)
        sw = _dot(qr, kk).reshape(GQA, TQ, w_keys) * ATTN_SCALE
        pw = _softmax_rows(jnp.where(m_win, sw, NEG_INF))
        o_w = _dot_nt(pw.reshape(GQA * TQ, w_keys).astype(BF16), vv).reshape(GQA, TQ, LANES)

        o_c = o_c.reshape(GQA, TQ, LANES)
        for g in range(GQA):
            h = GQA * k + g
            comb.append(gates[:, 3 * h:3 * h + 1] * o_c[g] + gates[:, 3 * h + 1:3 * h + 2] * o_s[g]
                        + gates[:, 3 * h + 2:3 * h + 3] * o_w[g])

    cols = []
    for pr in range(N_HEADS // 2):
        k = (2 * pr) // GQA
        a, b = comb[2 * pr], comb[2 * pr + 1]
        a = pltpu.roll(a, HEAD_DIM, 1) if k == 1 else a
        b = pltpu.roll(b, HEAD_DIM, 1) if k == 0 else b
        cols.append(jnp.where(low, a, b))
    o = jnp.concatenate(cols, axis=1)
    o_ref[0] = _rms(o, gn_ref[...]).astype(o_ref.dtype)


def _sel_matrices(n_cmp_pad, n_sel, n_keys):
    j = np.arange(n_sel)[:, None]
    n = np.arange(n_cmp_pad)[None, :]
    per = SEL_BLOCK // CMP_STRIDE
    amat = ((n >= per * j - (CMP_RATIO - 1)) & (n <= per * j + per - 1)).astype(np.float32)
    jrow = np.arange(LANES)[:, None]
    key = np.arange(n_keys)[None, :]
    emat = (key // SEL_BLOCK == jrow).astype(np.float32)
    return jnp.asarray(amat), jnp.asarray(emat, dtype=BF16)


def _attn_prompt(q, qr, kcmp, vcmp, ks, vs, kw, vw, gt, gn):
    b, t, _ = q.shape
    amat, emat = _sel_matrices(kcmp.shape[1], N_SEL_PROMPT, t)
    qt = lambda w: pl.BlockSpec((1, TQ, w), lambda i, j: (i, j, 0))
    full = lambda a: pl.BlockSpec((1,) + a.shape[1:], lambda i, j: (i, 0, 0))
    const = lambda a: pl.BlockSpec(a.shape, lambda i, j: (0, 0))
    return pl.pallas_call(
        functools.partial(_attn_prompt_body, seq=t),
        grid=(b, t // TQ),
        in_specs=[qt(D_ATTN), qt(D_ATTN), full(kcmp), full(vcmp), full(ks), full(vs), full(kw), full(vw),
                  qt(LANES), const(amat), const(emat), const(gn)],
        out_specs=qt(D_ATTN),
        out_shape=jax.ShapeDtypeStruct((b, t, D_ATTN), BF16),
        compiler_params=_cparams(("parallel", "arbitrary")),
        name="attn_prompt",
    )(q, qr, kcmp, vcmp, ks, vs, kw, vw, gt, amat, emat, gn)


S5_LB = D_SSM // LANES
S5_SW = LANES // SSM_CH * SSM_STATE


def _s5_body(u_ref, h0r_ref, h0i_ref, lr_ref, li_ref, bre_ref, bim_ref, c_ref, d_ref, gw_ref, gb_ref, gn_ref,
             y_ref, hr_ref, hi_ref, xr_s, xi_s, y_s, *, tc, bb):
    step = pl.program_id(0)
    m = tc * bb

    @pl.when(step == 0)
    def _():
        hr_ref[...] = h0r_ref[...]
        hi_ref[...] = h0i_ref[...]

    u = u_ref[...].reshape(m, D_SSM)
    ub = u.astype(BF16)
    for j in range(S5_LB):
        uj = ub[:, j * LANES:(j + 1) * LANES]
        sl = slice(j * S5_SW, (j + 1) * S5_SW)
        xr_s[...] = _dot(uj, bre_ref[j]).reshape(tc, bb, S5_SW)
        xi_s[...] = _dot(uj, bim_ref[j]).reshape(tc, bb, S5_SW)
        lr = jnp.broadcast_to(lr_ref[:, sl], (bb, S5_SW))
        li = jnp.broadcast_to(li_ref[:, sl], (bb, S5_SW))

        def scan(t, carry):
            hr, hi = carry
            nr = lr * hr - li * hi + xr_s[t]
            ni = lr * hi + li * hr + xi_s[t]
            xr_s[t] = nr
            xi_s[t] = ni
            return nr, ni

        hr, hi = lax.fori_loop(0, tc, scan, (hr_ref[:, sl], hi_ref[:, sl]))
        hr_ref[:, sl] = hr
        hi_ref[:, sl] = hi
        hcat = jnp.concatenate([xr_s[...].reshape(m, S5_SW), xi_s[...].reshape(m, S5_SW)], axis=1).astype(BF16)
        y_s[:, j * LANES:(j + 1) * LANES] = _dot(hcat, c_ref[j])
    z = jax.nn.gelu(y_s[...] + d_ref[...] * u)
    out = z * jax.nn.sigmoid(_dot(z.astype(BF16), gw_ref[...]) + gb_ref[...])
    y_ref[...] = _rms(out, gn_ref[...]).astype(y_ref.dtype).reshape(tc, bb, D_SSM)


def _s5(u_tm, h0r, h0i, sp, tc):
    t, bb, _ = u_tm.shape
    const = lambda a: pl.BlockSpec(a.shape, lambda i: (0,) * a.ndim)
    args = (h0r, h0i, sp['lr'], sp['li'], sp['bre'], sp['bim'], sp['c'], sp['d'], sp['glu_w'], sp['glu_b'], sp['gn'])
    st = jax.ShapeDtypeStruct(h0r.shape, F32)
    return pl.pallas_call(
        functools.partial(_s5_body, tc=tc, bb=bb),
        grid=(t // tc,),
        in_specs=[pl.BlockSpec((tc, bb, D_SSM), lambda i: (i, 0, 0))] + [const(a) for a in args],
        out_specs=[pl.BlockSpec((tc, bb, D_SSM), lambda i: (i, 0, 0)), const(h0r), const(h0i)],
        out_shape=[jax.ShapeDtypeStruct((t, bb, D_SSM), F32), st, st],
        scratch_shapes=[pltpu.VMEM((tc, bb, S5_SW), F32), pltpu.VMEM((tc, bb, S5_SW), F32),
                        pltpu.VMEM((tc * bb, D_SSM), F32)],
        compiler_params=_cparams(("arbitrary",)),
        name="s5",
    )(u_tm, *args)


def _s5_params(log_dt, a_re, a_im, b_re, b_im, c_re, c_im, d, glu_w, glu_b, gn):
    dt = jnp.exp(log_dt)[:, None]
    mag = jnp.exp(a_re * dt)
    lr, li = mag * jnp.cos(a_im * dt), mag * jnp.sin(a_im * dt)
    den = a_re * a_re + a_im * a_im
    inv_r, inv_i = a_re / den, -a_im / den
    nr, ni = lr - 1.0, li
    fr, fi = nr * inv_r - ni * inv_i, nr * inv_i + ni * inv_r
    bbr = fr[..., None] * b_re - fi[..., None] * b_im
    bbi = fr[..., None] * b_im + fi[..., None] * b_re
    gl = LANES // SSM_CH
    eye = jnp.eye(gl, dtype=F32)

    def bmat(bb):
        x = bb.reshape(S5_LB, gl, SSM_STATE, SSM_CH)
        return jnp.einsum('jgpc,gh->jgchp', x, eye).reshape(S5_LB, LANES, S5_SW).astype(BF16)

    def cmat(cc):
        x = cc.reshape(S5_LB, gl, SSM_CH, SSM_STATE)
        return jnp.einsum('jgcp,gh->jgphc', x, eye).reshape(S5_LB, S5_SW, LANES)

    cm = jnp.concatenate([cmat(c_re), -cmat(c_im)], axis=1).astype(BF16)
    return dict(lr=lr.reshape(1, -1), li=li.reshape(1, -1), bre=bmat(bbr), bim=bmat(bbi), c=cm,
                d=d.reshape(1, -1), glu_w=glu_w.astype(BF16), glu_b=glu_b[None, :], gn=gn[None, :])


def _outproj_body(x_ref, a_ref, s_ref, wa_ref, ws_ref, gn_ref, o_ref, *, norm_attn):
    a = a_ref[...]
    if norm_attn:
        a = _rms(a, gn_ref[...])
    o_ref[...] = x_ref[...] + _dot(a.astype(BF16), wa_ref[...]) + _dot(s_ref[...].astype(BF16), ws_ref[...])


def _outproj(x, a, s, wa, ws, gn, norm_attn):
    m = x.shape[0]
    tm = min(m, 1024)
    row = lambda w: pl.BlockSpec((tm, w), lambda i: (i, 0))
    const = lambda a_: pl.BlockSpec(a_.shape, lambda i: (0, 0))
    return pl.pallas_call(
        functools.partial(_outproj_body, norm_attn=norm_attn),
        grid=(m // tm,),
        in_specs=[row(D_MODEL), row(D_ATTN), row(D_SSM), const(wa), const(ws), const(gn)],
        out_specs=row(D_MODEL),
        out_shape=jax.ShapeDtypeStruct((m, D_MODEL), F32),
        compiler_params=_cparams(("parallel",)),
        name="outproj",
    )(x, a, s, wa, ws, gn)


N_PAGES = PAST_LEN // PAGE_SIZE
CMP_CHUNK = 32
CHUNK_HALVES = CMP_CHUNK * PAGE_SIZE // CMP_STRIDE


def _page_copy(pool_ref, buf_ref, sem_ref, pt_ref, b, i, slot):
    return pltpu.make_async_copy(pool_ref.at[pt_ref[b * N_PAGES + i]], buf_ref.at[slot, i], sem_ref.at[slot])


def _cmp_paged_body(pt_ref, pool_ref, w1_ref, b_ref, w2_ref, o_ref, buf_ref, rows_ref, p_ref, sem_ref):
    b = pl.program_id(0)
    slot = b % 2

    def start_all(bb, sl):
        def body(i, c):
            _page_copy(pool_ref, buf_ref, sem_ref, pt_ref, bb, i, sl).start()
            return c
        lax.fori_loop(0, N_PAGES, body, 0)

    @pl.when(b == 0)
    def _():
        start_all(0, 0)

    @pl.when(b + 1 < pl.num_programs(0))
    def _():
        start_all(b + 1, 1 - slot)

    def wait_one(i, c):
        _page_copy(pool_ref, buf_ref, sem_ref, pt_ref, b, i, slot).wait()
        return c
    lax.fori_loop(0, N_PAGES, wait_one, 0)

    def chunk(c, carry):
        for i in range(CMP_CHUNK):
            rows_ref[i * PAGE_SIZE:(i + 1) * PAGE_SIZE, :] = buf_ref[slot, c * CMP_CHUNK + i].T
        xs = [rows_ref[pl.ds(s, CHUNK_HALVES, stride=CMP_STRIDE), :] for s in range(CMP_STRIDE)]
        xcat = jnp.concatenate(xs, axis=1).astype(BF16)
        p_ref[pl.ds(pl.multiple_of(c * CHUNK_HALVES, CHUNK_HALVES), CHUNK_HALVES), :] = _dot(xcat, w1_ref[...])
        return carry
    lax.fori_loop(0, N_PAGES // CMP_CHUNK, chunk, 0)

    nh = N_KV_HEADS * CMP_HIDDEN
    pre = p_ref[:, :nh] + pltpu.roll(p_ref[:, nh:], p_ref.shape[0] - 1, 0) + b_ref[...]
    o_ref[0] = _dot(jax.nn.gelu(pre).astype(BF16), w2_ref[...])


def _cmp_paged(page_table_flat, pool_t, w1, bias, w2, n_seq):
    n_half = PAST_LEN // CMP_STRIDE
    const = lambda a: pl.BlockSpec(a.shape, lambda i, pt: (0,) * a.ndim)
    return pl.pallas_call(
        _cmp_paged_body,
        grid_spec=pltpu.PrefetchScalarGridSpec(
            num_scalar_prefetch=1,
            grid=(n_seq,),
            in_specs=[pl.BlockSpec(memory_space=pl.ANY), const(w1), const(bias), const(w2)],
            out_specs=pl.BlockSpec((1, n_half, KV_W), lambda i, pt: (i, 0, 0)),
            scratch_shapes=[pltpu.VMEM((2, N_PAGES, PAGE_SIZE, KV_W), F32),
                            pltpu.VMEM((CMP_CHUNK * PAGE_SIZE, KV_W), F32),
                            pltpu.VMEM((n_half, CMP_RATIO * N_KV_HEADS * CMP_HIDDEN), F32),
                            pltpu.SemaphoreType.DMA((2,))]),
        out_shape=jax.ShapeDtypeStruct((n_seq, n_half, KV_W), F32),
        compiler_params=_cparams(("arbitrary",)),
        name="cmp_paged",
    )(page_table_flat, pool_t, w1, bias, w2)


def _cmpattn_body(q_ref, kc_ref, vc_ref, amat_ref, o_ref, idx_ref, *, ts, n_cmp, n_sel):
    rows = GQA * ts
    qpos = PAST_LEN + lax.broadcasted_iota(jnp.int32, (rows, 1), 0) % ts
    n_pad = kc_ref.shape[1]
    n_io = lax.broadcasted_iota(jnp.int32, (1, n_pad), 1)
    m = (n_io * CMP_STRIDE + (CMP_LEN - 1) <= qpos) & (n_io < n_cmp)
    kc = kc_ref[0].astype(BF16)
    vc = vc_ref[0].astype(BF16)
    sel_w = amat_ref.shape[1]
    j = lax.broadcasted_iota(jnp.int32, (ts, sel_w), 1)
    jq = (PAST_LEN + lax.broadcasted_iota(jnp.int32, (ts, sel_w), 0)) // SEL_BLOCK
    forced = (j == 0) | (j == jq) | (j == jq - 1)
    jf = j.astype(F32)
    lane_k = lax.broadcasted_iota(jnp.int32, (ts, LANES), 1)
    picked = []
    for k in range(N_KV_HEADS):
        s = _dot_nt(q_ref[0, k].astype(BF16), kc) * ATTN_SCALE
        p = jnp.where(m, _softmax_rows(jnp.where(m, s, NEG_INF)), 0.0)
        o_ref[0, k] = _dot(p.astype(BF16), vc)
        imp = p[0:ts]
        for g in range(1, GQA):
            imp = imp + p[g * ts:(g + 1) * ts]
        imp_sel = jnp.dot(imp, amat_ref[...], precision=lax.Precision.HIGHEST, preferred_element_type=F32)
        score = jnp.where(j > jq, -1.0, jnp.where(forced, FORCE_SCORE, imp_sel))
        score = jnp.where(j < n_sel, score, -jnp.inf)
        res = jnp.zeros((ts, LANES), F32)
        for i in range(N_SELECT):
            mx = jnp.max(score, axis=-1, keepdims=True)
            ix = jnp.min(jnp.where(score == mx, jf, 1e9), axis=-1, keepdims=True)
            res = jnp.where(lane_k == i, ix, res)
            score = jnp.where(jf == ix, -jnp.inf, score)
        picked.append(res)
    idx_ref[0] = jnp.concatenate(picked, axis=0).astype(jnp.int32)


def _cmpattn(q_pad, kcmp, vcmp, ts):
    n_seq = q_pad.shape[0]
    n_cmp = (PAST_LEN + ts) // CMP_STRIDE - CMP_RATIO + 1
    n_sel = -(-(PAST_LEN + ts) // SEL_BLOCK)
    sel_w = -(-n_sel // LANES) * LANES
    per = SEL_BLOCK // CMP_STRIDE
    n = np.arange(kcmp.shape[1])[:, None]
    j = np.arange(sel_w)[None, :]
    amat = jnp.asarray(((n >= per * j - (CMP_RATIO - 1)) & (n <= per * j + per - 1) & (j < n_sel)).astype(np.float32))
    rows = GQA * ts
    return pl.pallas_call(
        functools.partial(_cmpattn_body, ts=ts, n_cmp=n_cmp, n_sel=n_sel),
        grid=(n_seq,),
        in_specs=[pl.BlockSpec((1, N_KV_HEADS, rows, LANES), lambda i: (i, 0, 0, 0)),
                  pl.BlockSpec((1,) + kcmp.shape[1:], lambda i: (i, 0, 0)),
                  pl.BlockSpec((1,) + vcmp.shape[1:], lambda i: (i, 0, 0)),
                  pl.BlockSpec(amat.shape, lambda i: (0, 0))],
        out_specs=[pl.BlockSpec((1, N_KV_HEADS, rows, LANES), lambda i: (i, 0, 0, 0)),
                   pl.BlockSpec((1, N_KV_HEADS * ts, LANES), lambda i: (i, 0, 0))],
        out_shape=[jax.ShapeDtypeStruct((n_seq, N_KV_HEADS, rows, LANES), F32),
                   jax.ShapeDtypeStruct((n_seq, N_KV_HEADS * ts, LANES), jnp.int32)],
        compiler_params=_cparams(("parallel",)),
        name="cmpattn_sample",
    )(q_pad, kcmp, vcmp, amat)


def _slc_copy(pool_ref, buf_ref, sem_ref, pt_ref, idx_ref, step, r, slot, ts):
    blk = idx_ref[step * (ts * N_SELECT) + r]
    page = jnp.minimum(blk // (PAGE_SIZE // SEL_BLOCK), N_PAGES - 1)
    b = step // N_KV_HEADS
    return pltpu.make_async_copy(pool_ref.at[pt_ref[b * N_PAGES + page]], buf_ref.at[slot, r], sem_ref.at[slot])


def _tail_body(pt_ref, idx_ref, kpool_ref, vpool_ref, q_ref, ksn_ref, vsn_ref, kwn_ref, vwn_ref, wk_ref, wv_ref,
               oc_ref, gt_ref, o_ref, kbuf_ref, vbuf_ref, ksem_ref, vsem_ref, *, ts):
    step = pl.program_id(0) * N_KV_HEADS + pl.program_id(1)
    n_steps = pl.num_programs(0) * N_KV_HEADS
    slot = step % 2
    n_fetch = ts * N_SELECT

    def start_all(st, sl):
        def body(r, c):
            _slc_copy(kpool_ref, kbuf_ref, ksem_ref, pt_ref, idx_ref, st, r, sl, ts).start()
            _slc_copy(vpool_ref, vbuf_ref, vsem_ref, pt_ref, idx_ref, st, r, sl, ts).start()
            return c
        lax.fori_loop(0, n_fetch, body, 0)

    @pl.when(step == 0)
    def _():
        start_all(0, 0)

    @pl.when(step + 1 < n_steps)
    def _():
        start_all(step + 1, 1 - slot)

    def wait_one(r, c):
        _slc_copy(kpool_ref, kbuf_ref, ksem_ref, pt_ref, idx_ref, step, r, slot, ts).wait()
        _slc_copy(vpool_ref, vbuf_ref, vsem_ref, pt_ref, idx_ref, step, r, slot, ts).wait()
        return c
    lax.fori_loop(0, n_fetch, wait_one, 0)

    rows = GQA * ts
    q = q_ref[0, 0].astype(BF16)
    row_t = lax.broadcasted_iota(jnp.int32, (rows, 1), 0) % ts
    qpos = PAST_LEN + row_t
    lane = lax.broadcasted_iota(jnp.int32, (1, LANES), 1)
    lane_half, lane_off = lane // SEL_BLOCK, lane % SEL_BLOCK
    pad = jnp.zeros((LANES - ksn_ref.shape[1], LANES), F32)
    new_pos = PAST_LEN + lane
    new_ok = lane < ts
    blk_new = PAST_LEN // SEL_BLOCK

    def new_rows(ref):
        return jnp.concatenate([ref[0], pad], axis=0).astype(BF16)

    ksn, vsn = new_rows(ksn_ref), new_rows(vsn_ref)
    s_new = _dot_nt(q, ksn) * ATTN_SCALE
    o_s = jnp.zeros((rows, LANES), F32)
    for t in range(ts):
        pieces = []
        has_new = jnp.int32(0)
        for i in range(N_SELECT):
            r = t * N_SELECT + i
            blk = idx_ref[step * n_fetch + r]
            has_new = has_new | (blk == blk_new).astype(jnp.int32)
            sc = _dot(q, kbuf_ref[slot, r].astype(BF16)) * ATTN_SCALE
            want_half = jnp.where(blk < blk_new, blk % (PAGE_SIZE // SEL_BLOCK), -1)
            ok = (lane_half == want_half) & (blk * SEL_BLOCK + lane_off <= qpos)
            pieces.append(jnp.where(ok, sc, NEG_INF))
        ok_new = (lane < jnp.where(has_new > 0, ts, 0)) & (new_pos <= qpos)
        pieces.append(jnp.where(ok_new, s_new, NEG_INF))
        p = _softmax_rows(jnp.concatenate(pieces, axis=1)).astype(BF16)
        o_t = _dot(p[:, N_SELECT * LANES:], vsn)
        for i in range(N_SELECT):
            o_t = o_t + _dot_nt(p[:, i * LANES:(i + 1) * LANES], vbuf_ref[slot, t * N_SELECT + i].astype(BF16))
        o_s = jnp.where(row_t == t, o_t, o_s)

    wb = wk_ref.shape[2]
    kp = PAST_LEN - wb + lax.broadcasted_iota(jnp.int32, (1, wb), 1)
    ok_w = (kp >= 0) & (kp <= qpos) & (kp > qpos - WINDOW)
    ok_wn = new_ok & (new_pos <= qpos) & (new_pos > qpos - WINDOW)
    s_w = jnp.where(ok_w, _dot(q, wk_ref[0].astype(BF16)) * ATTN_SCALE, NEG_INF)
    s_wn = jnp.where(ok_wn, _dot_nt(q, new_rows(kwn_ref)) * ATTN_SCALE, NEG_INF)
    p = _softmax_rows(jnp.concatenate([s_w, s_wn], axis=1)).astype(BF16)
    o_w = _dot_nt(p[:, :wb], wv_ref[0].astype(BF16)) + _dot(p[:, wb:], new_rows(vwn_ref))

    gt = gt_ref[0, 0]
    o_ref[0, 0] = gt[:, 0:1] * oc_ref[0, 0] + gt[:, 1:2] * o_s + gt[:, 2:3] * o_w


def _tail(page_table_flat, idx_flat, kpool_t, vpool_t, q_pad, ksn, vsn, kwn, vwn, wk_t, wv_t, o_cmp, gates_pad, ts):
    n_seq = q_pad.shape[0]
    rows = GQA * ts
    n_fetch = ts * N_SELECT
    per_head = pl.BlockSpec((1, 1, rows, LANES), lambda i, k, pt, ix: (i, k, 0, 0))
    per_seq = lambda a: pl.BlockSpec((1,) + a.shape[1:], lambda i, k, pt, ix: (i, 0, 0))
    hbm = pl.BlockSpec(memory_space=pl.ANY)
    return pl.pallas_call(
        functools.partial(_tail_body, ts=ts),
        grid_spec=pltpu.PrefetchScalarGridSpec(
            num_scalar_prefetch=2,
            grid=(n_seq, N_KV_HEADS),
            in_specs=[hbm, hbm, per_head, per_seq(ksn), per_seq(vsn), per_seq(kwn), per_seq(vwn),
                      per_seq(wk_t), per_seq(wv_t), per_head, per_head],
            out_specs=per_head,
            scratch_shapes=[pltpu.VMEM((2, n_fetch, PAGE_SIZE, KV_W), F32), pltpu.VMEM((2, n_fetch, PAGE_SIZE, KV_W), F32),
                            pltpu.SemaphoreType.DMA((2,)), pltpu.SemaphoreType.DMA((2,))]),
        out_shape=jax.ShapeDtypeStruct((n_seq, N_KV_HEADS, rows, LANES), F32),
        compiler_params=_cparams(("arbitrary", "arbitrary")),
        name="slc_win_sample",
    )(page_table_flat, idx_flat, kpool_t, vpool_t, q_pad, ksn, vsn, kwn, vwn, wk_t, wv_t, o_cmp, gates_pad)


def _pad_heads(x, ts):
    n = x.shape[0] // ts
    w = x.shape[1] // N_HEADS
    return x.reshape(n, ts, N_KV_HEADS, GQA, w).transpose(0, 2, 3, 1, 4).reshape(n, N_KV_HEADS, GQA * ts, w)


def _q_on_kv_lanes(q, ts):
    x = _pad_heads(q, ts)
    z = jnp.zeros_like(x[:, 0])
    return jnp.stack([jnp.concatenate([x[:, 0], z], axis=-1), jnp.concatenate([z, x[:, 1]], axis=-1)], axis=1)


def _sample_attention_jnp(q, q_rot, kc, vc, ks, vs, kw, vw, gates, ck, cv, sk, sv, wk, wv, page_table, cw, gn):
    B, T = q.shape[:2]
    pos = PAST_LEN + jnp.arange(T, dtype=jnp.int32)
    q = q.reshape(B, T, N_HEADS, HEAD_DIM)
    q_rot = q_rot.reshape(B, T, N_HEADS, HEAD_DIM)
    kvr = lambda a: a.reshape(B, T, N_KV_HEADS, HEAD_DIM)
    kc, vc, ks, vs, kw, vw = map(kvr, (kc, vc, ks, vs, kw, vw))
    gates = gates[..., :3 * N_HEADS].reshape(B, T, N_HEADS, 3)
    past = lambda pool: pool[page_table].reshape(B, -1, N_KV_HEADS, HEAD_DIM)
    kc_all, vc_all = jnp.concatenate([past(ck), kc], axis=1), jnp.concatenate([past(cv), vc], axis=1)
    ks_all, vs_all = jnp.concatenate([past(sk), ks], axis=1), jnp.concatenate([past(sv), vs], axis=1)

    def compress_blocks(k, pos_emb, w1, b1, w2):
        Bk, Tk = k.shape[:2]
        nh = Tk // CMP_STRIDE
        nc = nh - CMP_RATIO + 1
        halves = k[:, :nh * CMP_STRIDE].reshape(Bk, nh, CMP_STRIDE, N_KV_HEADS, HEAD_DIM)
        pe = pos_emb.reshape(CMP_RATIO, CMP_STRIDE, HEAD_DIM)
        w1r = w1.reshape(CMP_RATIO, CMP_STRIDE, HEAD_DIM, CMP_HIDDEN)
        pre = b1
        for r in range(CMP_RATIO):
            pre = pre + jnp.einsum('bnskd,sdh->bnkh', halves[:, r:r + nc] + pe[r][:, None, :], w1r[r])
        return jax.nn.gelu(pre) @ w2

    k_cmp = compress_blocks(kc_all, *cw[0])
    v_cmp = compress_blocks(vc_all, *cw[1])
    nc = k_cmp.shape[1]
    qg = q.reshape(B, T, N_KV_HEADS, GQA, HEAD_DIM)
    s = jnp.einsum('bqkgd,bnkd->bqkgn', qg, k_cmp).astype(F32) * ATTN_SCALE
    blk_end = jnp.arange(nc) * CMP_STRIDE + CMP_LEN - 1
    m = (blk_end[None, :] <= pos[:, None])[None, :, None, None, :]
    p = jnp.where(m, jax.nn.softmax(jnp.where(m, s, NEG_INF), axis=-1), 0.0)
    o_cmp = jnp.einsum('bqkgn,bnkd->bqkgd', p, v_cmp).reshape(B, T, N_HEADS, HEAD_DIM)
    imp = p.sum(axis=3)
    n_sel = -(-(PAST_LEN + T) // SEL_BLOCK)
    per = SEL_BLOCK // CMP_STRIDE
    total = n_sel * per + CMP_RATIO + per
    pp = jnp.pad(imp, ((0, 0), (0, 0), (0, 0), (CMP_RATIO - 1, total - nc - (CMP_RATIO - 1))))
    impb = sum(pp[..., o:o + n_sel * per:per] for o in range(per + CMP_RATIO - 1))
    j = jnp.arange(n_sel)[None, None, None, :]
    jq = (pos // SEL_BLOCK)[None, :, None, None]
    forced = (j == 0) | (j == jq) | (j == jq - 1)
    score = jnp.where(j > jq, -1.0, jnp.where(forced, FORCE_SCORE, impb))
    _, idx = lax.top_k(score, min(N_SELECT, n_sel))
    valid = idx <= jq

    def to_blocks(k):
        kp = jnp.pad(k, ((0, 0), (0, n_sel * SEL_BLOCK - k.shape[1]), (0, 0), (0, 0)))
        return kp.reshape(B, n_sel, SEL_BLOCK, N_KV_HEADS, HEAD_DIM).transpose(0, 3, 1, 2, 4)

    kb, vb = to_blocks(ks_all), to_blocks(vs_all)
    qg = q_rot.reshape(B, T, N_KV_HEADS, GQA, HEAD_DIM)
    bi = jnp.arange(B)[:, None, None, None]
    hi = jnp.arange(N_KV_HEADS)[None, None, :, None]
    ksel, vsel = kb[bi, hi, idx], vb[bi, hi, idx]
    s = jnp.einsum('bqkgd,bqknsd->bqkgns', qg, ksel).astype(F32) * ATTN_SCALE
    kpos = idx[..., None] * SEL_BLOCK + jnp.arange(SEL_BLOCK)
    mask = valid[..., None] & (kpos <= pos[None, :, None, None, None])
    s = jnp.where(mask[:, :, :, None], s, NEG_INF)
    shp = s.shape
    p = jax.nn.softmax(s.reshape(shp[:-2] + (-1,)), axis=-1).reshape(shp)
    o_slc = jnp.einsum('bqkgns,bqknsd->bqkgd', p, vsel).reshape(B, T, N_HEADS, HEAD_DIM)
    wb = wk.shape[1]
    kw_all, vw_all = jnp.concatenate([wk, kw], axis=1), jnp.concatenate([wv, vw], axis=1)
    k_pos = PAST_LEN - wb + jnp.arange(wb + T, dtype=jnp.int32)
    s = jnp.einsum('bqkgd,bjkd->bqkgj', qg, kw_all).astype(F32) * ATTN_SCALE
    kp, qp = k_pos[None, :], pos[:, None]
    m = ((kp >= 0) & (kp <= qp) & (kp > qp - WINDOW))[None, :, None, None, :]
    p = jax.nn.softmax(jnp.where(m, s, NEG_INF), axis=-1)
    o_win = jnp.einsum('bqkgj,bjkd->bqkgd', p, vw_all).reshape(B, T, N_HEADS, HEAD_DIM)
    o = (gates[..., 0:1] * o_cmp + gates[..., 1:2] * o_slc + gates[..., 2:3] * o_win).reshape(B, T, D_ATTN)
    return _rms(o, gn).astype(BF16)


def kernel(x_prompt, x_sample, cache_cmp_k, cache_cmp_v, cache_slc_k, cache_slc_v, state_win_k, state_win_v, state_s5_re, state_s5_im, page_table, norm_ffn1, ffn1_gate, ffn1_up, ffn1_down, norm_mix, w_in, cmp_k_pos, cmp_k_w1, cmp_k_b1, cmp_k_w2, cmp_v_pos, cmp_v_w1, cmp_v_b1, cmp_v_w2, s5_log_dt, s5_a_re, s5_a_im, s5_b_re, s5_b_im, s5_c_re, s5_c_im, s5_d, s5_glu_w, s5_glu_b, norm_attn_out, norm_ssm_out, w_out, norm_ffn2, ffn2_gate, ffn2_up, ffn2_down, norm_final):
    depth = w_in.shape[0]
    assert depth == 1
    l = 0
    B, T, _ = x_prompt.shape
    BS, TS, _ = x_sample.shape
    row = lambda a: a[None, :]

    w = w_in[l]
    cut = D_ATTN + 6 * KV_W
    w_p = jnp.concatenate([w[:, :cut], w[:, cut + 3 * N_HEADS:], w[:, cut:cut + 3 * N_HEADS],
                           jnp.zeros((D_MODEL, LANES - 3 * N_HEADS), F32)], axis=1).astype(BF16)
    f1 = (row(norm_ffn1[l]), ffn1_gate[l].astype(BF16), ffn1_up[l].astype(BF16), ffn1_down[l].astype(BF16))
    f2 = (row(norm_ffn2[l]), ffn2_gate[l].astype(BF16), ffn2_up[l].astype(BF16), ffn2_down[l].astype(BF16))
    gfin = row(norm_final)
    wo = w_out[l].astype(BF16)
    wo_a, wo_s = wo[:D_ATTN], wo[D_ATTN:]
    cwk = _cmp_weights(cmp_k_pos[l], cmp_k_w1[l], cmp_k_b1[l], cmp_k_w2[l])
    cwv = _cmp_weights(cmp_v_pos[l], cmp_v_w1[l], cmp_v_b1[l], cmp_v_w2[l])
    sp = _s5_params(s5_log_dt[l], s5_a_re[l], s5_a_im[l], s5_b_re[l], s5_b_im[l], s5_c_re[l], s5_c_im[l],
                    s5_d[l], s5_glu_w[l], s5_glu_b[l], norm_ssm_out[l])
    gn_attn = row(norm_attn_out[l])
    g_mix = row(norm_mix[l])

    xp = x_prompt.reshape(B * T, D_MODEL)
    x1 = _ffn(xp, *f1, gfin, False)
    tabs = _rope_tables(jnp.arange(T, dtype=jnp.int32))
    (q, qr, kc, vc, _, _, _, _, gt, u,
     kc_t, vc_t, ks_t, vs_t, kw_t, vw_t) = _proj(x1, g_mix, w_p, *tabs, seq_t=T)
    b3 = lambda a: a.reshape(B, T, a.shape[-1])
    kcmp, vcmp = _cmp_prompt(b3(kc), b3(vc), cwk, cwv)
    attn = _attn_prompt(b3(q), b3(qr), kcmp, vcmp, ks_t, vs_t, kw_t, vw_t, b3(gt), gn_attn)
    zeros = jnp.zeros((B, SSM_GROUPS * SSM_STATE), F32)
    y_tm, p_re, p_im = _s5(b3(u).transpose(1, 0, 2), zeros, zeros, sp, 64)
    ssm = y_tm.transpose(1, 0, 2).reshape(B * T, D_SSM)
    x2 = _outproj(x1, attn.reshape(B * T, D_ATTN), ssm, wo_a, wo_s, gn_attn, False)
    y_prompt = _ffn(x2, *f2, gfin, True).reshape(B, T, D_MODEL)
    from_t = lambda a: a.reshape(a.shape[0], N_KV_HEADS, HEAD_DIM, a.shape[2]).transpose(0, 3, 1, 2)[None]
    wb = min(WINDOW, T)
    p_state = (from_t(kc_t), from_t(vc_t), from_t(ks_t), from_t(vs_t),
               from_t(kw_t[:, :, T - wb:]), from_t(vw_t[:, :, T - wb:]),
               p_re.reshape(1, B, SSM_GROUPS, SSM_STATE), p_im.reshape(1, B, SSM_GROUPS, SSM_STATE))

    xs = x_sample.reshape(BS * TS, D_MODEL)
    s1 = _ffn(xs, *f1, gfin, False)
    pos_s = PAST_LEN + jnp.arange(TS, dtype=jnp.int32)
    tabs_s = [jnp.tile(a, (BS, 1)) for a in _rope_tables(pos_s)]
    q, qr, kc, vc, ks, vs, kw, vw, gt, u = _proj(s1, g_mix, w_p, *tabs_s)
    s3 = lambda a: a.reshape(BS, TS, a.shape[-1])
    to_t = lambda a: a.transpose(0, 2, 3, 1).reshape(a.shape[0], KV_W, a.shape[1])
    pt_flat = page_table.reshape(-1)
    kcmp_s = _cmp_paged(pt_flat, to_t(cache_cmp_k[l]), *cwk, BS)
    vcmp_s = _cmp_paged(pt_flat, to_t(cache_cmp_v[l]), *cwv, BS)
    o_cmp_s, idx = _cmpattn(_q_on_kv_lanes(q, TS), kcmp_s, vcmp_s, TS)
    pad8 = lambda a: jnp.pad(s3(a), ((0, 0), (0, 8 - TS), (0, 0)))
    gates_s = _pad_heads(gt[:, :3 * N_HEADS], TS)
    gates_s = jnp.pad(gates_s, ((0, 0), (0, 0), (0, 0), (0, LANES - 3)))
    wk_t, wv_t = to_t(state_win_k[l]), to_t(state_win_v[l])
    comb = _tail(pt_flat, idx[:, :, :N_SELECT].reshape(-1), to_t(cache_slc_k[l]), to_t(cache_slc_v[l]),
                 _q_on_kv_lanes(qr, TS), pad8(ks), pad8(vs), pad8(kw), pad8(vw), wk_t, wv_t, o_cmp_s, gates_s, TS)
    comb = comb.reshape(BS, N_KV_HEADS, GQA, TS, N_KV_HEADS, HEAD_DIM)
    comb = jnp.stack([comb[:, k, :, :, k] for k in range(N_KV_HEADS)], axis=1)
    attn_s = comb.transpose(0, 3, 1, 2, 4).reshape(BS * TS, D_ATTN)
    y_tm, s_re, s_im = _s5(s3(u).transpose(1, 0, 2), state_s5_re[l].reshape(BS, -1), state_s5_im[l].reshape(BS, -1), sp, TS)
    ssm_s = y_tm.transpose(1, 0, 2).reshape(BS * TS, D_SSM)
    s2 = _outproj(s1, attn_s, ssm_s, wo_a, wo_s, gn_attn, True)
    y_sample = _ffn(s2, *f2, gfin, True).reshape(BS, TS, D_MODEL)
    kv5 = lambda a, n: a.reshape(1, n, -1, N_KV_HEADS, HEAD_DIM)
    win_k = from_t(jnp.concatenate([wk_t[:, :, TS:], s3(kw).transpose(0, 2, 1)], axis=2))
    win_v = from_t(jnp.concatenate([wv_t[:, :, TS:], s3(vw).transpose(0, 2, 1)], axis=2))
    s_state = (kv5(kc, BS), kv5(vc, BS), kv5(ks, BS), kv5(vs, BS), win_k, win_v,
               s_re.reshape(1, BS, SSM_GROUPS, SSM_STATE), s_im.reshape(1, BS, SSM_GROUPS, SSM_STATE))

    return (y_prompt, y_sample) + p_state + s_state
```

```python
import functools
import math

import numpy as np
import jax
import jax.numpy as jnp
from jax import lax
from jax.experimental import pallas as pl
from jax.experimental.pallas import tpu as pltpu

F32 = jnp.float32
BF16 = jnp.bfloat16

D_MODEL = 1024
PAST_LEN = 16384
PAGE_SIZE = 128
D_ATTN = 512
D_SSM = 512
HEAD_DIM = 64
N_HEADS = 8
N_KV_HEADS = 2
GQA = 4
KV_W = 128
ROT_DIM = 16
ROPE_THETA = 500000.0
ATTN_SCALE = HEAD_DIM ** -0.5
CMP_LEN = 32
CMP_STRIDE = 16
CMP_RATIO = 2
CMP_HIDDEN = 256
SEL_BLOCK = 64
N_SELECT = 16
WINDOW = 512
SSM_CH = 16
SSM_GROUPS = 32
SSM_STATE = 64
D_FF = 2816
RMS_EPS = 1e-6
NEG_INF = -1e30
FORCE_SCORE = 1e4

LANES = 128
VMEM_LIMIT = 56 * 1024 * 1024

Z_Q = 0
Z_KV = D_ATTN
Z_U = D_ATTN + 6 * KV_W
Z_G = Z_U + D_SSM
Z_W = Z_G + LANES


def _cparams(sem):
    return pltpu.CompilerParams(dimension_semantics=sem, vmem_limit_bytes=VMEM_LIMIT)


def _rms(x, g):
    return x * lax.rsqrt(jnp.mean(x * x, axis=-1, keepdims=True) + RMS_EPS) * g


def _dot(a, b):
    return jnp.dot(a, b, preferred_element_type=F32)


def _dot_nt(a, b):
    return lax.dot_general(a, b, (((1,), (1,)), ((), ())), preferred_element_type=F32)


def _ffn_body(x_ref, g_ref, wg_ref, wu_ref, wd_ref, gf_ref, o_ref, h_ref, acc_ref, *, n_f, final_norm):
    j = pl.program_id(1)

    @pl.when(j == 0)
    def _():
        h_ref[...] = _rms(x_ref[...], g_ref[...]).astype(BF16)
        acc_ref[...] = jnp.zeros_like(acc_ref)

    h = h_ref[...]
    a = _dot(h, wg_ref[...])
    b = _dot(h, wu_ref[...])
    t = (a * jax.nn.sigmoid(a) * b).astype(BF16)
    acc_ref[...] += _dot(t, wd_ref[...])

    @pl.when(j == n_f - 1)
    def _():
        y = x_ref[...] + 0.5 * acc_ref[...]
        if final_norm:
            y = _rms(y, gf_ref[...])
        o_ref[...] = y


def _ffn(x, g, wg, wu, wd, gf, final_norm):
    m = x.shape[0]
    tm = min(m, 1024)
    tf = D_FF // 2
    n_f = D_FF // tf
    return pl.pallas_call(
        functools.partial(_ffn_body, n_f=n_f, final_norm=final_norm),
        grid=(m // tm, n_f),
        in_specs=[
            pl.BlockSpec((tm, D_MODEL), lambda i, j: (i, 0)),
            pl.BlockSpec((1, D_MODEL), lambda i, j: (0, 0)),
            pl.BlockSpec((D_MODEL, tf), lambda i, j: (0, j)),
            pl.BlockSpec((D_MODEL, tf), lambda i, j: (0, j)),
            pl.BlockSpec((tf, D_MODEL), lambda i, j: (j, 0)),
            pl.BlockSpec((1, D_MODEL), lambda i, j: (0, 0)),
        ],
        out_specs=pl.BlockSpec((tm, D_MODEL), lambda i, j: (i, 0)),
        out_shape=jax.ShapeDtypeStruct((m, D_MODEL), F32),
        scratch_shapes=[pltpu.VMEM((tm, D_MODEL), BF16), pltpu.VMEM((tm, D_MODEL), F32)],
        compiler_params=_cparams(("parallel", "arbitrary")),
        name="ffn",
    )(x, g, wg, wu, wd, gf)


def _proj_body(x_ref, g_ref, w_ref, c_ref, sa_ref, sb_ref,
               q_ref, qr_ref, kc_ref, vc_ref, ks_ref, vs_ref, kw_ref, vw_ref, gt_ref, u_ref, *t_refs):
    h = _rms(x_ref[...], g_ref[...]).astype(BF16)
    z = _dot(h, w_ref[...])
    c, sa, sb = c_ref[...], sa_ref[...], sb_ref[...]

    def rope(v):
        return v * c + pltpu.roll(v, LANES - ROT_DIM // 2, 1) * sa + pltpu.roll(v, ROT_DIM // 2, 1) * sb

    q_ref[...] = z[:, Z_Q:Z_Q + D_ATTN]
    for i in range(D_ATTN // LANES):
        qr_ref[:, i * LANES:(i + 1) * LANES] = rope(z[:, Z_Q + i * LANES:Z_Q + (i + 1) * LANES])
    kv = [z[:, Z_KV + i * KV_W:Z_KV + (i + 1) * KV_W] for i in range(6)]
    kv[2] = rope(kv[2])
    kv[4] = rope(kv[4])
    for ref, v in zip((kc_ref, vc_ref, ks_ref, vs_ref, kw_ref, vw_ref), kv):
        ref[...] = v
    for ref, v in zip(t_refs, kv):
        ref[0] = v.T
    u_ref[...] = z[:, Z_U:Z_U + D_SSM]
    gt_ref[...] = jax.nn.sigmoid(z[:, Z_G:Z_G + LANES])


def _proj(x, g, w, rope_c, rope_sa, rope_sb, seq_t=None):
    m = x.shape[0]
    tm = min(m, 512)
    n_rep = rope_c.shape[0] // tm
    row = lambda i: (i, 0)
    const = lambda i: (0, 0)
    tab = lambda i: (i % n_rep, 0)
    widths = [D_ATTN, D_ATTN] + [KV_W] * 6 + [LANES, D_SSM]
    out_specs = [pl.BlockSpec((tm, wd), row) for wd in widths]
    out_shape = [jax.ShapeDtypeStruct((m, wd), F32) for wd in widths]
    if seq_t is not None:
        assert rope_c.shape[0] == seq_t
        out_specs += [pl.BlockSpec((1, KV_W, tm), lambda i: (i // n_rep, 0, i % n_rep))] * 6
        out_shape += [jax.ShapeDtypeStruct((m // seq_t, KV_W, seq_t), F32)] * 6
    return pl.pallas_call(
        _proj_body,
        grid=(m // tm,),
        in_specs=[pl.BlockSpec((tm, D_MODEL), row), pl.BlockSpec((1, D_MODEL), const),
                  pl.BlockSpec((D_MODEL, Z_W), const),
                  pl.BlockSpec((tm, LANES), tab), pl.BlockSpec((tm, LANES), tab), pl.BlockSpec((tm, LANES), tab)],
        out_specs=out_specs,
        out_shape=out_shape,
        compiler_params=_cparams(("parallel",)),
        name="proj",
    )(x, g, w, rope_c, rope_sa, rope_sb)


def _rope_tables(pos):
    half = ROT_DIM // 2
    inv_freq = jnp.power(ROPE_THETA, -(jnp.arange(half, dtype=F32) * 2.0 / ROT_DIM))
    ang = pos.astype(F32)[:, None] * inv_freq[None, :]
    cos, sin = jnp.cos(ang), jnp.sin(ang)
    n = pos.shape[0]
    one = jnp.ones((n, HEAD_DIM - ROT_DIM), F32)
    zero = jnp.zeros((n, HEAD_DIM - ROT_DIM), F32)
    zh = jnp.zeros((n, half), F32)
    c = jnp.concatenate([cos, cos, one], axis=1)
    sa = jnp.concatenate([-sin, zh, zero], axis=1)
    sb = jnp.concatenate([zh, sin, zero], axis=1)
    t2 = lambda a: jnp.concatenate([a, a], axis=1)
    return t2(c), t2(sa), t2(sb)


def _compress_rows(x_ref, n_half, w1_ref, b_ref, w2_ref):
    xs = [x_ref[0, pl.ds(s, n_half, stride=CMP_STRIDE), :] for s in range(CMP_STRIDE)]
    xcat = jnp.concatenate(xs, axis=1).astype(BF16)
    p = _dot(xcat, w1_ref[...])
    nh = N_KV_HEADS * CMP_HIDDEN
    p1_next = pltpu.roll(p[:, nh:], n_half - 1, 0)
    pre = p[:, :nh] + p1_next + b_ref[...]
    return _dot(jax.nn.gelu(pre).astype(BF16), w2_ref[...])


def _cmp_prompt_body(kc_ref, vc_ref, wk1_ref, bk_ref, wk2_ref, wv1_ref, bv_ref, wv2_ref, ko_ref, vo_ref, *, n_half):
    ko_ref[0] = _compress_rows(kc_ref, n_half, wk1_ref, bk_ref, wk2_ref)
    vo_ref[0] = _compress_rows(vc_ref, n_half, wv1_ref, bv_ref, wv2_ref)


def _cmp_prompt(kc, vc, wk, wv):
    b, t, _ = kc.shape
    n_half = t // CMP_STRIDE
    seq = pl.BlockSpec((1, t, KV_W), lambda i: (i, 0, 0))
    const = lambda shp: pl.BlockSpec(shp, lambda i: (0,) * len(shp))
    wspecs = [const(wk[0].shape), const(wk[1].shape), const(wk[2].shape)]
    out = pl.BlockSpec((1, n_half, KV_W), lambda i: (i, 0, 0))
    return pl.pallas_call(
        functools.partial(_cmp_prompt_body, n_half=n_half),
        grid=(b,),
        in_specs=[seq, seq] + wspecs + wspecs,
        out_specs=[out, out],
        out_shape=[jax.ShapeDtypeStruct((b, n_half, KV_W), F32)] * 2,
        compiler_params=_cparams(("parallel",)),
        name="cmp_prompt",
    )(kc, vc, *wk, *wv)


def _cmp_weights(pos_emb, w1, b1, w2):
    w1r = w1.reshape(CMP_RATIO, CMP_STRIDE, HEAD_DIM, CMP_HIDDEN)
    eye = jnp.eye(N_KV_HEADS, dtype=F32)
    w1p = jnp.einsum('rsdh,kj->skdrjh', w1r, eye).reshape(CMP_STRIDE * KV_W, CMP_RATIO * N_KV_HEADS * CMP_HIDDEN)
    pe = pos_emb.reshape(CMP_RATIO, CMP_STRIDE, HEAD_DIM)
    bias = b1 + jnp.einsum('rsd,rsdh->h', pe, w1r, precision=lax.Precision.HIGHEST)
    bias = jnp.tile(bias, N_KV_HEADS)[None, :]
    w2p = jnp.einsum('hd,kj->khjd', w2, eye).reshape(N_KV_HEADS * CMP_HIDDEN, KV_W)
    return w1p.astype(BF16), bias, w2p.astype(BF16)


TQ = 128
KCHUNK = 512
N_SEL_PROMPT = 32


def _softmax_rows(s):
    mx = jnp.max(s, axis=-1, keepdims=True)
    e = jnp.exp(s - mx)
    return e / jnp.sum(e, axis=-1, keepdims=True)


def _attn_prompt_body(q_ref, qr_ref, kcmp_ref, vcmp_ref, ks_ref, vs_ref, kw_ref, vw_ref, gt_ref,
                      amat_ref, emat_ref, gn_ref, o_ref, *, seq):
    t0 = pl.program_id(1) * TQ
    lane = lax.broadcasted_iota(jnp.int32, (TQ, LANES), 1)
    low = lane < HEAD_DIM
    tq = t0 + lax.broadcasted_iota(jnp.int32, (TQ, 1), 0)
    gates = gt_ref[0]

    def head_rows(src_ref, k):
        rows = []
        for g in range(GQA):
            h = GQA * k + g
            v = src_ref[0, :, (h // 2) * LANES:(h // 2 + 1) * LANES]
            if h % 2 != k:
                v = pltpu.roll(v, HEAD_DIM, 1)
            rows.append(jnp.where(low if k == 0 else jnp.logical_not(low), v * ATTN_SCALE, 0.0))
        return jnp.concatenate(rows, axis=0).astype(BF16)

    jj = lax.broadcasted_iota(jnp.int32, (N_SEL_PROMPT, TQ), 0)
    jq = (t0 + lax.broadcasted_iota(jnp.int32, (N_SEL_PROMPT, TQ), 1)) // SEL_BLOCK
    forced = (jj == 0) | (jj == jq) | (jj == jq - 1)

    n_cmp = kcmp_ref.shape[1]
    blk_end = lax.broadcasted_iota(jnp.int32, (1, n_cmp), 1) * CMP_STRIDE + (CMP_LEN - 1)
    m_cmp = (blk_end <= tq)[None]

    n_chunks = t0 // KCHUNK + 1
    w_keys = WINDOW + TQ
    w_start = pl.multiple_of(jnp.clip(t0 - WINDOW, 0, seq - w_keys), TQ)
    kp_w = w_start + lax.broadcasted_iota(jnp.int32, (1, w_keys), 1)
    m_win = ((kp_w <= tq) & (kp_w > tq - WINDOW))[None]

    comb = []
    for k in range(N_KV_HEADS):
        qc = head_rows(q_ref, k)
        qr = head_rows(qr_ref, k)

        s = _dot_nt(qc, kcmp_ref[0].astype(BF16)).reshape(GQA, TQ, n_cmp)
        p = jnp.where(m_cmp, _softmax_rows(jnp.where(m_cmp, s, NEG_INF)), 0.0)
        o_c = _dot(p.reshape(GQA * TQ, n_cmp).astype(BF16), vcmp_ref[0].astype(BF16))
        imp = jnp.sum(p, axis=0)

        imp_sel = jnp.dot(amat_ref[...], imp.T, precision=lax.Precision.HIGHEST, preferred_element_type=F32)
        score = jnp.where(jj > jq, -1.0, jnp.where(forced, FORCE_SCORE, imp_sel))
        rank = jnp.zeros((N_SEL_PROMPT, TQ), F32)
        for i in range(N_SEL_PROMPT):
            si = score[i:i + 1, :]
            beats = (si > score) | ((si == score) & (jj > i))
            rank = rank + jnp.where(beats, 1.0, 0.0)
        sel_t = jnp.where((rank < N_SELECT) & (jj <= jq), 1.0, 0.0)
        sel_t = jnp.concatenate([sel_t, jnp.zeros((LANES - N_SEL_PROMPT, TQ), F32)], axis=0)
        sel = sel_t.T

        bias = ((sel - 1.0) * (-NEG_INF)).astype(BF16)
        q_aug = jnp.concatenate([qr, jnp.concatenate([bias] * GQA, axis=0)], axis=1)

        def chunk(c, carry, causal):
            m_i, l_i, acc = carry
            k0 = pl.multiple_of(c * KCHUNK, KCHUNK)
            kk = jnp.concatenate([ks_ref[0, :, pl.ds(k0, KCHUNK)].astype(BF16),
                                  emat_ref[:, pl.ds(k0, KCHUNK)]], axis=0)
            vv = vs_ref[0, :, pl.ds(k0, KCHUNK)].astype(BF16)
            sc = _dot(q_aug, kk).reshape(GQA, TQ, KCHUNK)
            if causal:
                kpos = k0 + lax.broadcasted_iota(jnp.int32, (1, KCHUNK), 1)
                sc = jnp.where((kpos <= tq)[None], sc, NEG_INF)
            m_n = jnp.maximum(m_i, jnp.max(sc, axis=-1, keepdims=True))
            alpha = jnp.exp(m_i - m_n)
            pe = jnp.exp(sc - m_n)
            l_n = alpha * l_i + jnp.sum(pe, axis=-1, keepdims=True)
            pv = _dot_nt(pe.reshape(GQA * TQ, KCHUNK).astype(BF16), vv).reshape(GQA, TQ, LANES)
            return m_n, l_n, alpha * acc + pv

        init = (jnp.full((GQA, TQ, 1), NEG_INF, F32), jnp.zeros((GQA, TQ, 1), F32), jnp.zeros((GQA, TQ, LANES), F32))
        carry = lax.fori_loop(0, n_chunks - 1, functools.partial(chunk, causal=False), init)
        _, l_f, acc_f = chunk(n_chunks - 1, carry, True)
        o_s = acc_f / l_f

        kk = kw_ref[0, :, pl.ds(w_start, w_keys)].astype(BF16)
        vv = vw_ref[0, :, pl.ds(w_start, w_keys)].astype(BF16)
        sw = jnp.where(m_win, _dot(qr, kk).reshape(GQA, TQ, w_keys), NEG_INF)
        ew = jnp.exp(sw - jnp.max(sw, axis=-1, keepdims=True))
        o_w = (_dot_nt(ew.reshape(GQA * TQ, w_keys).astype(BF16), vv).reshape(GQA, TQ, LANES)
               / jnp.sum(ew, axis=-1, keepdims=True))

        o_c = o_c.reshape(GQA, TQ, LANES)
        for g in range(GQA):
            h = GQA * k + g
            comb.append(gates[:, 3 * h:3 * h + 1] * o_c[g] + gates[:, 3 * h + 1:3 * h + 2] * o_s[g]
                        + gates[:, 3 * h + 2:3 * h + 3] * o_w[g])

    cols = []
    for pr in range(N_HEADS // 2):
        k = (2 * pr) // GQA
        a, b = comb[2 * pr], comb[2 * pr + 1]
        a = pltpu.roll(a, HEAD_DIM, 1) if k == 1 else a
        b = pltpu.roll(b, HEAD_DIM, 1) if k == 0 else b
        cols.append(jnp.where(low, a, b))
    o = jnp.concatenate(cols, axis=1)
    o_ref[0] = _rms(o, gn_ref[...]).astype(o_ref.dtype)


def _sel_matrices(n_cmp_pad, n_sel, n_keys):
    j = np.arange(n_sel)[:, None]
    n = np.arange(n_cmp_pad)[None, :]
    per = SEL_BLOCK // CMP_STRIDE
    amat = ((n >= per * j - (CMP_RATIO - 1)) & (n <= per * j + per - 1)).astype(np.float32)
    jrow = np.arange(LANES)[:, None]
    key = np.arange(n_keys)[None, :]
    emat = (key // SEL_BLOCK == jrow).astype(np.float32)
    return jnp.asarray(amat), jnp.asarray(emat, dtype=BF16)


def _attn_prompt(q, qr, kcmp, vcmp, ks, vs, kw, vw, gt, gn):
    b, t, _ = q.shape
    amat, emat = _sel_matrices(kcmp.shape[1], N_SEL_PROMPT, t)
    qt = lambda w: pl.BlockSpec((1, TQ, w), lambda i, j: (i, j, 0))
    full = lambda a: pl.BlockSpec((1,) + a.shape[1:], lambda i, j: (i, 0, 0))
    const = lambda a: pl.BlockSpec(a.shape, lambda i, j: (0, 0))
    return pl.pallas_call(
        functools.partial(_attn_prompt_body, seq=t),
        grid=(b, t // TQ),
        in_specs=[qt(D_ATTN), qt(D_ATTN), full(kcmp), full(vcmp), full(ks), full(vs), full(kw), full(vw),
                  qt(LANES), const(amat), const(emat), const(gn)],
        out_specs=qt(D_ATTN),
        out_shape=jax.ShapeDtypeStruct((b, t, D_ATTN), BF16),
        compiler_params=_cparams(("parallel", "arbitrary")),
        name="attn_prompt",
    )(q, qr, kcmp, vcmp, ks, vs, kw, vw, gt, amat, emat, gn)


S5_LB = D_SSM // LANES
S5_SW = LANES // SSM_CH * SSM_STATE


def _s5_body(u_ref, h0r_ref, h0i_ref, lr_ref, li_ref, bre_ref, bim_ref, c_ref, d_ref, gw_ref, gb_ref, gn_ref,
             y_ref, hr_ref, hi_ref, xr_s, xi_s, y_s, *, tc, bb):
    step = pl.program_id(0)
    m = tc * bb

    @pl.when(step == 0)
    def _():
        hr_ref[...] = h0r_ref[...]
        hi_ref[...] = h0i_ref[...]

    u = u_ref[...].reshape(m, D_SSM)
    ub = u.astype(BF16)
    for j in range(S5_LB):
        uj = ub[:, j * LANES:(j + 1) * LANES]
        sl = slice(j * S5_SW, (j + 1) * S5_SW)
        xr_s[...] = _dot(uj, bre_ref[j]).reshape(tc, bb, S5_SW)
        xi_s[...] = _dot(uj, bim_ref[j]).reshape(tc, bb, S5_SW)
        lr = jnp.broadcast_to(lr_ref[:, sl], (bb, S5_SW))
        li = jnp.broadcast_to(li_ref[:, sl], (bb, S5_SW))

        def scan(t, carry):
            hr, hi = carry
            nr = lr * hr - li * hi + xr_s[t]
            ni = lr * hi + li * hr + xi_s[t]
            xr_s[t] = nr
            xi_s[t] = ni
            return nr, ni

        hr, hi = lax.fori_loop(0, tc, scan, (hr_ref[:, sl], hi_ref[:, sl]))
        hr_ref[:, sl] = hr
        hi_ref[:, sl] = hi
        hcat = jnp.concatenate([xr_s[...].reshape(m, S5_SW), xi_s[...].reshape(m, S5_SW)], axis=1).astype(BF16)
        y_s[:, j * LANES:(j + 1) * LANES] = _dot(hcat, c_ref[j])
    z = jax.nn.gelu(y_s[...] + d_ref[...] * u)
    out = z * jax.nn.sigmoid(_dot(z.astype(BF16), gw_ref[...]) + gb_ref[...])
    y_ref[...] = _rms(out, gn_ref[...]).astype(y_ref.dtype).reshape(tc, bb, D_SSM)


def _s5(u_tm, h0r, h0i, sp, tc):
    t, bb, _ = u_tm.shape
    const = lambda a: pl.BlockSpec(a.shape, lambda i: (0,) * a.ndim)
    args = (h0r, h0i, sp['lr'], sp['li'], sp['bre'], sp['bim'], sp['c'], sp['d'], sp['glu_w'], sp['glu_b'], sp['gn'])
    st = jax.ShapeDtypeStruct(h0r.shape, F32)
    return pl.pallas_call(
        functools.partial(_s5_body, tc=tc, bb=bb),
        grid=(t // tc,),
        in_specs=[pl.BlockSpec((tc, bb, D_SSM), lambda i: (i, 0, 0))] + [const(a) for a in args],
        out_specs=[pl.BlockSpec((tc, bb, D_SSM), lambda i: (i, 0, 0)), const(h0r), const(h0i)],
        out_shape=[jax.ShapeDtypeStruct((t, bb, D_SSM), F32), st, st],
        scratch_shapes=[pltpu.VMEM((tc, bb, S5_SW), F32), pltpu.VMEM((tc, bb, S5_SW), F32),
                        pltpu.VMEM((tc * bb, D_SSM), F32)],
        compiler_params=_cparams(("arbitrary",)),
        name="s5",
    )(u_tm, *args)


def _s5_params(log_dt, a_re, a_im, b_re, b_im, c_re, c_im, d, glu_w, glu_b, gn):
    dt = jnp.exp(log_dt)[:, None]
    mag = jnp.exp(a_re * dt)
    lr, li = mag * jnp.cos(a_im * dt), mag * jnp.sin(a_im * dt)
    den = a_re * a_re + a_im * a_im
    inv_r, inv_i = a_re / den, -a_im / den
    nr, ni = lr - 1.0, li
    fr, fi = nr * inv_r - ni * inv_i, nr * inv_i + ni * inv_r
    bbr = fr[..., None] * b_re - fi[..., None] * b_im
    bbi = fr[..., None] * b_im + fi[..., None] * b_re
    gl = LANES // SSM_CH
    eye = jnp.eye(gl, dtype=F32)

    def bmat(bb):
        x = bb.reshape(S5_LB, gl, SSM_STATE, SSM_CH)
        return jnp.einsum('jgpc,gh->jgchp', x, eye).reshape(S5_LB, LANES, S5_SW).astype(BF16)

    def cmat(cc):
        x = cc.reshape(S5_LB, gl, SSM_CH, SSM_STATE)
        return jnp.einsum('jgcp,gh->jgphc', x, eye).reshape(S5_LB, S5_SW, LANES)

    cm = jnp.concatenate([cmat(c_re), -cmat(c_im)], axis=1).astype(BF16)
    return dict(lr=lr.reshape(1, -1), li=li.reshape(1, -1), bre=bmat(bbr), bim=bmat(bbi), c=cm,
                d=d.reshape(1, -1), glu_w=glu_w.astype(BF16), glu_b=glu_b[None, :], gn=gn[None, :])


def _outproj_body(x_ref, a_ref, s_ref, wa_ref, ws_ref, gn_ref, o_ref, *, norm_attn):
    a = a_ref[...]
    if norm_attn:
        a = _rms(a, gn_ref[...])
    o_ref[...] = x_ref[...] + _dot(a.astype(BF16), wa_ref[...]) + _dot(s_ref[...].astype(BF16), ws_ref[...])


def _outproj(x, a, s, wa, ws, gn, norm_attn):
    m = x.shape[0]
    tm = min(m, 1024)
    row = lambda w: pl.BlockSpec((tm, w), lambda i: (i, 0))
    const = lambda a_: pl.BlockSpec(a_.shape, lambda i: (0, 0))
    return pl.pallas_call(
        functools.partial(_outproj_body, norm_attn=norm_attn),
        grid=(m // tm,),
        in_specs=[row(D_MODEL), row(D_ATTN), row(D_SSM), const(wa), const(ws), const(gn)],
        out_specs=row(D_MODEL),
        out_shape=jax.ShapeDtypeStruct((m, D_MODEL), F32),
        compiler_params=_cparams(("parallel",)),
        name="outproj",
    )(x, a, s, wa, ws, gn)


N_PAGES = PAST_LEN // PAGE_SIZE
CMP_CHUNK = 8


def _page_copy(pool_ref, buf_ref, sem_ref, pt_ref, b, i, slot):
    return pltpu.make_async_copy(pool_ref.at[pt_ref[b * N_PAGES + i]], buf_ref.at[slot, i], sem_ref.at[slot])


def _cmp_paged_body(pt_ref, pool_ref, perm_ref, w1_ref, b_ref, w2_ref, o_ref, buf_ref, x_ref, sem_ref):
    b = pl.program_id(0)
    slot = b % 2

    def start_all(bb, sl):
        def body(i, c):
            _page_copy(pool_ref, buf_ref, sem_ref, pt_ref, bb, i, sl).start()
            return c
        lax.fori_loop(0, N_PAGES, body, 0)

    @pl.when(b == 0)
    def _():
        start_all(0, 0)

    @pl.when(b + 1 < pl.num_programs(0))
    def _():
        start_all(b + 1, 1 - slot)

    def wait_one(i, c):
        _page_copy(pool_ref, buf_ref, sem_ref, pt_ref, b, i, slot).wait()
        return c
    lax.fori_loop(0, N_PAGES, wait_one, 0)

    hp = PAGE_SIZE // CMP_STRIDE
    zero = jnp.zeros((HEAD_DIM, PAGE_SIZE), BF16)

    def regroup(c, carry):
        for u in range(CMP_CHUNK):
            i = c * CMP_CHUNK + u
            pg = buf_ref[slot, i].astype(BF16)
            r0 = pl.multiple_of(i * hp, hp)
            for k in range(N_KV_HEADS):
                pk = pg[k * HEAD_DIM:(k + 1) * HEAD_DIM]
                both = jnp.concatenate([jnp.concatenate([pk, zero], axis=1),
                                        jnp.concatenate([zero, pk], axis=1)], axis=0)
                zt = _dot_nt(perm_ref[...], both)
                for j in range(CMP_STRIDE // 2):
                    x_ref[k, pl.ds(r0, hp), j * LANES:(j + 1) * LANES] = zt[j * hp:(j + 1) * hp]
        return carry
    lax.fori_loop(0, N_PAGES // CMP_CHUNK, regroup, 0)

    n_half = N_PAGES * hp
    x = x_ref[...].reshape(N_KV_HEADS * n_half, CMP_STRIDE * HEAD_DIM).astype(BF16)
    p = _dot(x, w1_ref[...])
    out = None
    for k in range(N_KV_HEADS):
        pk = p[k * n_half:(k + 1) * n_half]
        pre = pk[:, :CMP_HIDDEN] + pltpu.roll(pk[:, CMP_HIDDEN:], n_half - 1, 0) + b_ref[...]
        ok = _dot(jax.nn.gelu(pre).astype(BF16), w2_ref[k])
        out = ok if out is None else out + ok
    o_ref[0] = out


def _cmp_paged(page_table_flat, pool_t, w1, bias, w2, n_seq):
    n_half = PAST_LEN // CMP_STRIDE
    hp = PAGE_SIZE // CMP_STRIDE
    perm = np.zeros((CMP_STRIDE // 2 * hp, 2 * PAGE_SIZE), np.float32)
    for j in range(CMP_STRIDE // 2):
        for h in range(hp):
            for p in range(2):
                perm[j * hp + h, p * PAGE_SIZE + CMP_STRIDE * h + 2 * j + p] = 1.0
    perm = jnp.asarray(perm, dtype=BF16)
    const = lambda a: pl.BlockSpec(a.shape, lambda i, pt: (0,) * a.ndim)
    return pl.pallas_call(
        _cmp_paged_body,
        grid_spec=pltpu.PrefetchScalarGridSpec(
            num_scalar_prefetch=1,
            grid=(n_seq,),
            in_specs=[pl.BlockSpec(memory_space=pl.ANY), const(perm), const(w1), const(bias), const(w2)],
            out_specs=pl.BlockSpec((1, n_half, KV_W), lambda i, pt: (i, 0, 0)),
            scratch_shapes=[pltpu.VMEM((2, N_PAGES, PAGE_SIZE, KV_W), F32),
                            pltpu.VMEM((N_KV_HEADS, n_half, CMP_STRIDE * HEAD_DIM), F32),
                            pltpu.SemaphoreType.DMA((2,))]),
        out_shape=jax.ShapeDtypeStruct((n_seq, n_half, KV_W), F32),
        compiler_params=_cparams(("arbitrary",)),
        name="cmp_paged",
    )(page_table_flat, pool_t, perm, w1, bias, w2)


def _cmp_weights_split(pos_emb, w1, b1, w2):
    w1r = w1.reshape(CMP_RATIO, CMP_STRIDE * HEAD_DIM, CMP_HIDDEN)
    w1s = w1r.transpose(1, 0, 2).reshape(CMP_STRIDE * HEAD_DIM, CMP_RATIO * CMP_HIDDEN)
    pe = pos_emb.reshape(CMP_RATIO, CMP_STRIDE * HEAD_DIM)
    bias = b1 + jnp.einsum('rx,rxh->h', pe, w1r, precision=lax.Precision.HIGHEST)
    z = jnp.zeros_like(w2)
    w2k = jnp.stack([jnp.concatenate([w2, z], axis=1), jnp.concatenate([z, w2], axis=1)])
    return w1s.astype(BF16), bias[None, :], w2k.astype(BF16)


CMPATTN_SEQS = 8


def _cmpattn_body(q_ref, kc_ref, vc_ref, amat_ref, o_ref, idx_ref, *, ts, n_cmp, n_sel):
    rows = GQA * ts
    qpos = PAST_LEN + lax.broadcasted_iota(jnp.int32, (rows, 1), 0) % ts
    n_pad = kc_ref.shape[1]
    n_io = lax.broadcasted_iota(jnp.int32, (1, n_pad), 1)
    m = (n_io * CMP_STRIDE + (CMP_LEN - 1) <= qpos) & (n_io < n_cmp)
    n_sb = q_ref.shape[0]
    imps = []
    for b in range(n_sb):
        kc = kc_ref[b].astype(BF16)
        vc = vc_ref[b].astype(BF16)
        for k in range(N_KV_HEADS):
            s = _dot_nt((q_ref[b, k] * ATTN_SCALE).astype(BF16), kc)
            p = jnp.where(m, _softmax_rows(jnp.where(m, s, NEG_INF)), 0.0)
            o_ref[b, k] = _dot(p.astype(BF16), vc)
            imp = p[0:ts]
            for g in range(1, GQA):
                imp = imp + p[g * ts:(g + 1) * ts]
            imps.append(imp)
    imp_all = jnp.concatenate(imps, axis=0)
    n_rows = imp_all.shape[0]
    sel_w = amat_ref.shape[1]
    imp_sel = jnp.dot(imp_all, amat_ref[...], precision=lax.Precision.HIGHEST, preferred_element_type=F32)
    j = lax.broadcasted_iota(jnp.int32, (n_rows, sel_w), 1)
    jq = (PAST_LEN + lax.broadcasted_iota(jnp.int32, (n_rows, sel_w), 0) % ts) // SEL_BLOCK
    forced = (j == 0) | (j == jq) | (j == jq - 1)
    jf = j.astype(F32)
    lane_k = lax.broadcasted_iota(jnp.int32, (n_rows, LANES), 1)
    score = jnp.where(j > jq, -1.0, jnp.where(forced, FORCE_SCORE, imp_sel))
    score = jnp.where(j < n_sel, score, -jnp.inf)
    res = jnp.zeros((n_rows, LANES), F32)
    for i in range(N_SELECT):
        mx = jnp.max(score, axis=-1, keepdims=True)
        ix = jnp.min(jnp.where(score == mx, jf, 1e9), axis=-1, keepdims=True)
        res = jnp.where(lane_k == i, ix, res)
        score = jnp.where(jf == ix, -jnp.inf, score)
    idx_ref[...] = res.astype(jnp.int32).reshape(idx_ref.shape)


def _cmpattn(q_pad, kcmp, vcmp, ts):
    n_seq = q_pad.shape[0]
    n_cmp = (PAST_LEN + ts) // CMP_STRIDE - CMP_RATIO + 1
    n_sel = -(-(PAST_LEN + ts) // SEL_BLOCK)
    sel_w = -(-n_sel // LANES) * LANES
    per = SEL_BLOCK // CMP_STRIDE
    n = np.arange(kcmp.shape[1])[:, None]
    j = np.arange(sel_w)[None, :]
    amat = jnp.asarray(((n >= per * j - (CMP_RATIO - 1)) & (n <= per * j + per - 1) & (j < n_sel)).astype(np.float32))
    rows = GQA * ts
    sb = CMPATTN_SEQS
    return pl.pallas_call(
        functools.partial(_cmpattn_body, ts=ts, n_cmp=n_cmp, n_sel=n_sel),
        grid=(n_seq // sb,),
        in_specs=[pl.BlockSpec((sb, N_KV_HEADS, rows, LANES), lambda i: (i, 0, 0, 0)),
                  pl.BlockSpec((sb,) + kcmp.shape[1:], lambda i: (i, 0, 0)),
                  pl.BlockSpec((sb,) + vcmp.shape[1:], lambda i: (i, 0, 0)),
                  pl.BlockSpec(amat.shape, lambda i: (0, 0))],
        out_specs=[pl.BlockSpec((sb, N_KV_HEADS, rows, LANES), lambda i: (i, 0, 0, 0)),
                   pl.BlockSpec((sb, N_KV_HEADS * ts, LANES), lambda i: (i, 0, 0))],
        out_shape=[jax.ShapeDtypeStruct((n_seq, N_KV_HEADS, rows, LANES), F32),
                   jax.ShapeDtypeStruct((n_seq, N_KV_HEADS * ts, LANES), jnp.int32)],
        compiler_params=_cparams(("parallel",)),
        name="cmpattn_sample",
    )(q_pad, kcmp, vcmp, amat)


def _slc_copy(pool_ref, buf_ref, sem_ref, pt_ref, idx_ref, step, r, slot, ts):
    blk = idx_ref[step * (ts * N_SELECT) + r]
    page = jnp.minimum(blk // (PAGE_SIZE // SEL_BLOCK), N_PAGES - 1)
    b = step // N_KV_HEADS
    return pltpu.make_async_copy(pool_ref.at[pt_ref[b * N_PAGES + page]], buf_ref.at[slot, r], sem_ref.at[slot])


def _tail_body(pt_ref, idx_ref, kpool_ref, vpool_ref, q_ref, ksn_ref, vsn_ref, kwn_ref, vwn_ref, wk_ref, wv_ref,
               oc_ref, gt_ref, o_ref, kbuf_ref, vbuf_ref, ksem_ref, vsem_ref, *, ts):
    step = pl.program_id(0) * N_KV_HEADS + pl.program_id(1)
    n_steps = pl.num_programs(0) * N_KV_HEADS
    slot = step % 2
    n_fetch = ts * N_SELECT

    def start_all(st, sl):
        def body(r, c):
            _slc_copy(kpool_ref, kbuf_ref, ksem_ref, pt_ref, idx_ref, st, r, sl, ts).start()
            _slc_copy(vpool_ref, vbuf_ref, vsem_ref, pt_ref, idx_ref, st, r, sl, ts).start()
            return c
        lax.fori_loop(0, n_fetch, body, 0)

    @pl.when(step == 0)
    def _():
        start_all(0, 0)

    @pl.when(step + 1 < n_steps)
    def _():
        start_all(step + 1, 1 - slot)

    def wait_one(r, c):
        _slc_copy(kpool_ref, kbuf_ref, ksem_ref, pt_ref, idx_ref, step, r, slot, ts).wait()
        _slc_copy(vpool_ref, vbuf_ref, vsem_ref, pt_ref, idx_ref, step, r, slot, ts).wait()
        return c
    lax.fori_loop(0, n_fetch, wait_one, 0)

    rows = GQA * ts
    q = q_ref[0, 0].astype(BF16)
    row_t = lax.broadcasted_iota(jnp.int32, (rows, 1), 0) % ts
    qpos = PAST_LEN + row_t
    lane = lax.broadcasted_iota(jnp.int32, (1, LANES), 1)
    lane_half, lane_off = lane // SEL_BLOCK, lane % SEL_BLOCK
    pad = jnp.zeros((LANES - ksn_ref.shape[1], LANES), F32)
    new_pos = PAST_LEN + lane
    new_ok = lane < ts
    blk_new = PAST_LEN // SEL_BLOCK

    def new_rows(ref):
        return jnp.concatenate([ref[0], pad], axis=0).astype(BF16)

    ksn, vsn = new_rows(ksn_ref), new_rows(vsn_ref)
    s_new = _dot_nt(q, ksn) * ATTN_SCALE
    o_s = jnp.zeros((rows, LANES), F32)
    for t in range(ts):
        pieces = []
        has_new = jnp.int32(0)
        for i in range(N_SELECT):
            r = t * N_SELECT + i
            blk = idx_ref[step * n_fetch + r]
            has_new = has_new | (blk == blk_new).astype(jnp.int32)
            sc = _dot(q, kbuf_ref[slot, r].astype(BF16)) * ATTN_SCALE
            want_half = jnp.where(blk < blk_new, blk % (PAGE_SIZE // SEL_BLOCK), -1)
            ok = (lane_half == want_half) & (blk * SEL_BLOCK + lane_off <= qpos)
            pieces.append(jnp.where(ok, sc, NEG_INF))
        ok_new = (lane < jnp.where(has_new > 0, ts, 0)) & (new_pos <= qpos)
        pieces.append(jnp.where(ok_new, s_new, NEG_INF))
        p = _softmax_rows(jnp.concatenate(pieces, axis=1)).astype(BF16)
        o_t = _dot(p[:, N_SELECT * LANES:], vsn)
        for i in range(N_SELECT):
            o_t = o_t + _dot_nt(p[:, i * LANES:(i + 1) * LANES], vbuf_ref[slot, t * N_SELECT + i].astype(BF16))
        o_s = jnp.where(row_t == t, o_t, o_s)

    wb = wk_ref.shape[2]
    kp = PAST_LEN - wb + lax.broadcasted_iota(jnp.int32, (1, wb), 1)
    ok_w = (kp >= 0) & (kp <= qpos) & (kp > qpos - WINDOW)
    ok_wn = new_ok & (new_pos <= qpos) & (new_pos > qpos - WINDOW)
    s_w = jnp.where(ok_w, _dot(q, wk_ref[0].astype(BF16)) * ATTN_SCALE, NEG_INF)
    s_wn = jnp.where(ok_wn, _dot_nt(q, new_rows(kwn_ref)) * ATTN_SCALE, NEG_INF)
    p = _softmax_rows(jnp.concatenate([s_w, s_wn], axis=1)).astype(BF16)
    o_w = _dot_nt(p[:, :wb], wv_ref[0].astype(BF16)) + _dot(p[:, wb:], new_rows(vwn_ref))

    gt = gt_ref[0, 0]
    o_ref[0, 0] = gt[:, 0:1] * oc_ref[0, 0] + gt[:, 1:2] * o_s + gt[:, 2:3] * o_w


def _tail(page_table_flat, idx_flat, kpool_t, vpool_t, q_pad, ksn, vsn, kwn, vwn, wk_t, wv_t, o_cmp, gates_pad, ts):
    n_seq = q_pad.shape[0]
    rows = GQA * ts
    n_fetch = ts * N_SELECT
    per_head = pl.BlockSpec((1, 1, rows, LANES), lambda i, k, pt, ix: (i, k, 0, 0))
    per_seq = lambda a: pl.BlockSpec((1,) + a.shape[1:], lambda i, k, pt, ix: (i, 0, 0))
    hbm = pl.BlockSpec(memory_space=pl.ANY)
    return pl.pallas_call(
        functools.partial(_tail_body, ts=ts),
        grid_spec=pltpu.PrefetchScalarGridSpec(
            num_scalar_prefetch=2,
            grid=(n_seq, N_KV_HEADS),
            in_specs=[hbm, hbm, per_head, per_seq(ksn), per_seq(vsn), per_seq(kwn), per_seq(vwn),
                      per_seq(wk_t), per_seq(wv_t), per_head, per_head],
            out_specs=per_head,
            scratch_shapes=[pltpu.VMEM((2, n_fetch, PAGE_SIZE, KV_W), F32), pltpu.VMEM((2, n_fetch, PAGE_SIZE, KV_W), F32),
                            pltpu.SemaphoreType.DMA((2,)), pltpu.SemaphoreType.DMA((2,))]),
        out_shape=jax.ShapeDtypeStruct((n_seq, N_KV_HEADS, rows, LANES), F32),
        compiler_params=_cparams(("arbitrary", "arbitrary")),
        name="slc_win_sample",
    )(page_table_flat, idx_flat, kpool_t, vpool_t, q_pad, ksn, vsn, kwn, vwn, wk_t, wv_t, o_cmp, gates_pad)


def _pad_heads(x, ts):
    n = x.shape[0] // ts
    w = x.shape[1] // N_HEADS
    return x.reshape(n, ts, N_KV_HEADS, GQA, w).transpose(0, 2, 3, 1, 4).reshape(n, N_KV_HEADS, GQA * ts, w)


def _q_on_kv_lanes(q, ts):
    x = _pad_heads(q, ts)
    z = jnp.zeros_like(x[:, 0])
    return jnp.stack([jnp.concatenate([x[:, 0], z], axis=-1), jnp.concatenate([z, x[:, 1]], axis=-1)], axis=1)


def _sample_attention_jnp(q, q_rot, kc, vc, ks, vs, kw, vw, gates, ck, cv, sk, sv, wk, wv, page_table, cw, gn):
    B, T = q.shape[:2]
    pos = PAST_LEN + jnp.arange(T, dtype=jnp.int32)
    q = q.reshape(B, T, N_HEADS, HEAD_DIM)
    q_rot = q_rot.reshape(B, T, N_HEADS, HEAD_DIM)
    kvr = lambda a: a.reshape(B, T, N_KV_HEADS, HEAD_DIM)
    kc, vc, ks, vs, kw, vw = map(kvr, (kc, vc, ks, vs, kw, vw))
    gates = gates[..., :3 * N_HEADS].reshape(B, T, N_HEADS, 3)
    past = lambda pool: pool[page_table].reshape(B, -1, N_KV_HEADS, HEAD_DIM)
    kc_all, vc_all = jnp.concatenate([past(ck), kc], axis=1), jnp.concatenate([past(cv), vc], axis=1)
    ks_all, vs_all = jnp.concatenate([past(sk), ks], axis=1), jnp.concatenate([past(sv), vs], axis=1)

    def compress_blocks(k, pos_emb, w1, b1, w2):
        Bk, Tk = k.shape[:2]
        nh = Tk // CMP_STRIDE
        nc = nh - CMP_RATIO + 1
        halves = k[:, :nh * CMP_STRIDE].reshape(Bk, nh, CMP_STRIDE, N_KV_HEADS, HEAD_DIM)
        pe = pos_emb.reshape(CMP_RATIO, CMP_STRIDE, HEAD_DIM)
        w1r = w1.reshape(CMP_RATIO, CMP_STRIDE, HEAD_DIM, CMP_HIDDEN)
        pre = b1
        for r in range(CMP_RATIO):
            pre = pre + jnp.einsum('bnskd,sdh->bnkh', halves[:, r:r + nc] + pe[r][:, None, :], w1r[r])
        return jax.nn.gelu(pre) @ w2

    k_cmp = compress_blocks(kc_all, *cw[0])
    v_cmp = compress_blocks(vc_all, *cw[1])
    nc = k_cmp.shape[1]
    qg = q.reshape(B, T, N_KV_HEADS, GQA, HEAD_DIM)
    s = jnp.einsum('bqkgd,bnkd->bqkgn', qg, k_cmp).astype(F32) * ATTN_SCALE
    blk_end = jnp.arange(nc) * CMP_STRIDE + CMP_LEN - 1
    m = (blk_end[None, :] <= pos[:, None])[None, :, None, None, :]
    p = jnp.where(m, jax.nn.softmax(jnp.where(m, s, NEG_INF), axis=-1), 0.0)
    o_cmp = jnp.einsum('bqkgn,bnkd->bqkgd', p, v_cmp).reshape(B, T, N_HEADS, HEAD_DIM)
    imp = p.sum(axis=3)
    n_sel = -(-(PAST_LEN + T) // SEL_BLOCK)
    per = SEL_BLOCK // CMP_STRIDE
    total = n_sel * per + CMP_RATIO + per
    pp = jnp.pad(imp, ((0, 0), (0, 0), (0, 0), (CMP_RATIO - 1, total - nc - (CMP_RATIO - 1))))
    impb = sum(pp[..., o:o + n_sel * per:per] for o in range(per + CMP_RATIO - 1))
    j = jnp.arange(n_sel)[None, None, None, :]
    jq = (pos // SEL_BLOCK)[None, :, None, None]
    forced = (j == 0) | (j == jq) | (j == jq - 1)
    score = jnp.where(j > jq, -1.0, jnp.where(forced, FORCE_SCORE, impb))
    _, idx = lax.top_k(score, min(N_SELECT, n_sel))
    valid = idx <= jq

    def to_blocks(k):
        kp = jnp.pad(k, ((0, 0), (0, n_sel * SEL_BLOCK - k.shape[1]), (0, 0), (0, 0)))
        return kp.reshape(B, n_sel, SEL_BLOCK, N_KV_HEADS, HEAD_DIM).transpose(0, 3, 1, 2, 4)

    kb, vb = to_blocks(ks_all), to_blocks(vs_all)
    qg = q_rot.reshape(B, T, N_KV_HEADS, GQA, HEAD_DIM)
    bi = jnp.arange(B)[:, None, None, None]
    hi = jnp.arange(N_KV_HEADS)[None, None, :, None]
    ksel, vsel = kb[bi, hi, idx], vb[bi, hi, idx]
    s = jnp.einsum('bqkgd,bqknsd->bqkgns', qg, ksel).astype(F32) * ATTN_SCALE
    kpos = idx[..., None] * SEL_BLOCK + jnp.arange(SEL_BLOCK)
    mask = valid[..., None] & (kpos <= pos[None, :, None, None, None])
    s = jnp.where(mask[:, :, :, None], s, NEG_INF)
    shp = s.shape
    p = jax.nn.softmax(s.reshape(shp[:-2] + (-1,)), axis=-1).reshape(shp)
    o_slc = jnp.einsum('bqkgns,bqknsd->bqkgd', p, vsel).reshape(B, T, N_HEADS, HEAD_DIM)
    wb = wk.shape[1]
    kw_all, vw_all = jnp.concatenate([wk, kw], axis=1), jnp.concatenate([wv, vw], axis=1)
    k_pos = PAST_LEN - wb + jnp.arange(wb + T, dtype=jnp.int32)
    s = jnp.einsum('bqkgd,bjkd->bqkgj', qg, kw_all).astype(F32) * ATTN_SCALE
    kp, qp = k_pos[None, :], pos[:, None]
    m = ((kp >= 0) & (kp <= qp) & (kp > qp - WINDOW))[None, :, None, None, :]
    p = jax.nn.softmax(jnp.where(m, s, NEG_INF), axis=-1)
    o_win = jnp.einsum('bqkgj,bjkd->bqkgd', p, vw_all).reshape(B, T, N_HEADS, HEAD_DIM)
    o = (gates[..., 0:1] * o_cmp + gates[..., 1:2] * o_slc + gates[..., 2:3] * o_win).reshape(B, T, D_ATTN)
    return _rms(o, gn).astype(BF16)


def kernel(x_prompt, x_sample, cache_cmp_k, cache_cmp_v, cache_slc_k, cache_slc_v, state_win_k, state_win_v, state_s5_re, state_s5_im, page_table, norm_ffn1, ffn1_gate, ffn1_up, ffn1_down, norm_mix, w_in, cmp_k_pos, cmp_k_w1, cmp_k_b1, cmp_k_w2, cmp_v_pos, cmp_v_w1, cmp_v_b1, cmp_v_w2, s5_log_dt, s5_a_re, s5_a_im, s5_b_re, s5_b_im, s5_c_re, s5_c_im, s5_d, s5_glu_w, s5_glu_b, norm_attn_out, norm_ssm_out, w_out, norm_ffn2, ffn2_gate, ffn2_up, ffn2_down, norm_final):
    depth = w_in.shape[0]
    assert depth == 1
    l = 0
    B, T, _ = x_prompt.shape
    BS, TS, _ = x_sample.shape
    row = lambda a: a[None, :]

    w = w_in[l]
    cut = D_ATTN + 6 * KV_W
    w_p = jnp.concatenate([w[:, :cut], w[:, cut + 3 * N_HEADS:], w[:, cut:cut + 3 * N_HEADS],
                           jnp.zeros((D_MODEL, LANES - 3 * N_HEADS), F32)], axis=1).astype(BF16)
    f1 = (row(norm_ffn1[l]), ffn1_gate[l].astype(BF16), ffn1_up[l].astype(BF16), ffn1_down[l].astype(BF16))
    f2 = (row(norm_ffn2[l]), ffn2_gate[l].astype(BF16), ffn2_up[l].astype(BF16), ffn2_down[l].astype(BF16))
    gfin = row(norm_final)
    wo = w_out[l].astype(BF16)
    wo_a, wo_s = wo[:D_ATTN], wo[D_ATTN:]
    cwk = _cmp_weights(cmp_k_pos[l], cmp_k_w1[l], cmp_k_b1[l], cmp_k_w2[l])
    cwv = _cmp_weights(cmp_v_pos[l], cmp_v_w1[l], cmp_v_b1[l], cmp_v_w2[l])
    sp = _s5_params(s5_log_dt[l], s5_a_re[l], s5_a_im[l], s5_b_re[l], s5_b_im[l], s5_c_re[l], s5_c_im[l],
                    s5_d[l], s5_glu_w[l], s5_glu_b[l], norm_ssm_out[l])
    gn_attn = row(norm_attn_out[l])
    g_mix = row(norm_mix[l])

    xp = x_prompt.reshape(B * T, D_MODEL)
    x1 = _ffn(xp, *f1, gfin, False)
    tabs = _rope_tables(jnp.arange(T, dtype=jnp.int32))
    (q, qr, kc, vc, _, _, _, _, gt, u,
     kc_t, vc_t, ks_t, vs_t, kw_t, vw_t) = _proj(x1, g_mix, w_p, *tabs, seq_t=T)
    b3 = lambda a: a.reshape(B, T, a.shape[-1])
    kcmp, vcmp = _cmp_prompt(b3(kc), b3(vc), cwk, cwv)
    attn = _attn_prompt(b3(q), b3(qr), kcmp, vcmp, ks_t, vs_t, kw_t, vw_t, b3(gt), gn_attn)
    zeros = jnp.zeros((B, SSM_GROUPS * SSM_STATE), F32)
    y_tm, p_re, p_im = _s5(b3(u).transpose(1, 0, 2), zeros, zeros, sp, 64)
    ssm = y_tm.transpose(1, 0, 2).reshape(B * T, D_SSM)
    x2 = _outproj(x1, attn.reshape(B * T, D_ATTN), ssm, wo_a, wo_s, gn_attn, False)
    y_prompt = _ffn(x2, *f2, gfin, True).reshape(B, T, D_MODEL)
    from_t = lambda a: a.reshape(a.shape[0], N_KV_HEADS, HEAD_DIM, a.shape[2]).transpose(0, 3, 1, 2)[None]
    wb = min(WINDOW, T)
    p_state = (from_t(kc_t), from_t(vc_t), from_t(ks_t), from_t(vs_t),
               from_t(kw_t[:, :, T - wb:]), from_t(vw_t[:, :, T - wb:]),
               p_re.reshape(1, B, SSM_GROUPS, SSM_STATE), p_im.reshape(1, B, SSM_GROUPS, SSM_STATE))

    xs = x_sample.reshape(BS * TS, D_MODEL)
    s1 = _ffn(xs, *f1, gfin, False)
    pos_s = PAST_LEN + jnp.arange(TS, dtype=jnp.int32)
    tabs_s = [jnp.tile(a, (BS, 1)) for a in _rope_tables(pos_s)]
    q, qr, kc, vc, ks, vs, kw, vw, gt, u = _proj(s1, g_mix, w_p, *tabs_s)
    s3 = lambda a: a.reshape(BS, TS, a.shape[-1])
    to_t = lambda a: a.transpose(0, 2, 3, 1).reshape(a.shape[0], KV_W, a.shape[1])
    pt_flat = page_table.reshape(-1)
    kcmp_s = _cmp_paged(pt_flat, to_t(cache_cmp_k[l]),
                        *_cmp_weights_split(cmp_k_pos[l], cmp_k_w1[l], cmp_k_b1[l], cmp_k_w2[l]), BS)
    vcmp_s = _cmp_paged(pt_flat, to_t(cache_cmp_v[l]),
                        *_cmp_weights_split(cmp_v_pos[l], cmp_v_w1[l], cmp_v_b1[l], cmp_v_w2[l]), BS)
    o_cmp_s, idx = _cmpattn(_q_on_kv_lanes(q, TS), kcmp_s, vcmp_s, TS)
    pad8 = lambda a: jnp.pad(s3(a), ((0, 0), (0, 8 - TS), (0, 0)))
    gates_s = _pad_heads(gt[:, :3 * N_HEADS], TS)
    gates_s = jnp.pad(gates_s, ((0, 0), (0, 0), (0, 0), (0, LANES - 3)))
    wk_t, wv_t = to_t(state_win_k[l]), to_t(state_win_v[l])
    comb = _tail(pt_flat, idx[:, :, :N_SELECT].reshape(-1), to_t(cache_slc_k[l]), to_t(cache_slc_v[l]),
                 _q_on_kv_lanes(qr, TS), pad8(ks), pad8(vs), pad8(kw), pad8(vw), wk_t, wv_t, o_cmp_s, gates_s, TS)
    comb = comb.reshape(BS, N_KV_HEADS, GQA, TS, N_KV_HEADS, HEAD_DIM)
    comb = jnp.stack([comb[:, k, :, :, k] for k in range(N_KV_HEADS)], axis=1)
    attn_s = comb.transpose(0, 3, 1, 2, 4).reshape(BS * TS, D_ATTN)
    y_tm, s_re, s_im = _s5(s3(u).transpose(1, 0, 2), state_s5_re[l].reshape(BS, -1), state_s5_im[l].reshape(BS, -1), sp, TS)
    ssm_s = y_tm.transpose(1, 0, 2).reshape(BS * TS, D_SSM)
    s2 = _outproj(s1, attn_s, ssm_s, wo_a, wo_s, gn_attn, True)
    y_sample = _ffn(s2, *f2, gfin, True).reshape(BS, TS, D_MODEL)
    kv5 = lambda a, n: a.reshape(1, n, -1, N_KV_HEADS, HEAD_DIM)
    win_k = from_t(jnp.concatenate([wk_t[:, :, TS:], s3(kw).transpose(0, 2, 1)], axis=2))
    win_v = from_t(jnp.concatenate([wv_t[:, :, TS:], s3(vw).transpose(0, 2, 1)], axis=2))
    s_state = (kv5(kc, BS), kv5(vc, BS), kv5(ks, BS), kv5(vs, BS), win_k, win_v,
               s_re.reshape(1, BS, SSM_GROUPS, SSM_STATE), s_im.reshape(1, BS, SSM_GROUPS, SSM_STATE))

    return (y_prompt, y_sample) + p_state + s_state
```

```python
import functools
import math

import numpy as np
import jax
import jax.numpy as jnp
from jax import lax
from jax.experimental import pallas as pl
from jax.experimental.pallas import tpu as pltpu

F32 = jnp.float32
BF16 = jnp.bfloat16

D_MODEL = 1024
PAST_LEN = 16384
PAGE_SIZE = 128
D_ATTN = 512
D_SSM = 512
HEAD_DIM = 64
N_HEADS = 8
N_KV_HEADS = 2
GQA = 4
KV_W = 128
ROT_DIM = 16
ROPE_THETA = 500000.0
ATTN_SCALE = HEAD_DIM ** -0.5
CMP_LEN = 32
CMP_STRIDE = 16
CMP_RATIO = 2
CMP_HIDDEN = 256
SEL_BLOCK = 64
N_SELECT = 16
WINDOW = 512
SSM_CH = 16
SSM_GROUPS = 32
SSM_STATE = 64
D_FF = 2816
RMS_EPS = 1e-6
NEG_INF = -1e30
FORCE_SCORE = 1e4

LANES = 128
VMEM_LIMIT = 56 * 1024 * 1024

Z_Q = 0
Z_KV = D_ATTN
Z_U = D_ATTN + 6 * KV_W
Z_G = Z_U + D_SSM
Z_W = Z_G + LANES


def _cparams(sem):
    return pltpu.CompilerParams(dimension_semantics=sem, vmem_limit_bytes=VMEM_LIMIT)


def _rms(x, g):
    return x * lax.rsqrt(jnp.mean(x * x, axis=-1, keepdims=True) + RMS_EPS) * g


def _dot(a, b):
    return jnp.dot(a, b, preferred_element_type=F32)


def _dot_nt(a, b):
    return lax.dot_general(a, b, (((1,), (1,)), ((), ())), preferred_element_type=F32)


def _ffn_body(x_ref, g_ref, wg_ref, wu_ref, wd_ref, gf_ref, o_ref, h_ref, acc_ref, *, n_f, final_norm):
    j = pl.program_id(1)

    @pl.when(j == 0)
    def _():
        h_ref[...] = _rms(x_ref[...], g_ref[...]).astype(BF16)
        acc_ref[...] = jnp.zeros_like(acc_ref)

    h = h_ref[...]
    a = _dot(h, wg_ref[...])
    b = _dot(h, wu_ref[...])
    t = (a * jax.nn.sigmoid(a) * b).astype(BF16)
    acc_ref[...] += _dot(t, wd_ref[...])

    @pl.when(j == n_f - 1)
    def _():
        y = x_ref[...] + 0.5 * acc_ref[...]
        if final_norm:
            y = _rms(y, gf_ref[...])
        o_ref[...] = y


def _ffn(x, g, wg, wu, wd, gf, final_norm):
    m = x.shape[0]
    tm = min(m, 1024)
    tf = D_FF // 2
    n_f = D_FF // tf
    return pl.pallas_call(
        functools.partial(_ffn_body, n_f=n_f, final_norm=final_norm),
        grid=(m // tm, n_f),
        in_specs=[
            pl.BlockSpec((tm, D_MODEL), lambda i, j: (i, 0)),
            pl.BlockSpec((1, D_MODEL), lambda i, j: (0, 0)),
            pl.BlockSpec((D_MODEL, tf), lambda i, j: (0, j)),
            pl.BlockSpec((D_MODEL, tf), lambda i, j: (0, j)),
            pl.BlockSpec((tf, D_MODEL), lambda i, j: (j, 0)),
            pl.BlockSpec((1, D_MODEL), lambda i, j: (0, 0)),
        ],
        out_specs=pl.BlockSpec((tm, D_MODEL), lambda i, j: (i, 0)),
        out_shape=jax.ShapeDtypeStruct((m, D_MODEL), F32),
        scratch_shapes=[pltpu.VMEM((tm, D_MODEL), BF16), pltpu.VMEM((tm, D_MODEL), F32)],
        compiler_params=_cparams(("parallel", "arbitrary")),
        name="ffn",
    )(x, g, wg, wu, wd, gf)


def _proj_body(x_ref, g_ref, w_ref, c_ref, sa_ref, sb_ref,
               q_ref, qr_ref, kc_ref, vc_ref, ks_ref, vs_ref, kw_ref, vw_ref, gt_ref, u_ref, *t_refs):
    h = _rms(x_ref[...], g_ref[...]).astype(BF16)
    z = _dot(h, w_ref[...])
    c, sa, sb = c_ref[...], sa_ref[...], sb_ref[...]

    def rope(v):
        return v * c + pltpu.roll(v, LANES - ROT_DIM // 2, 1) * sa + pltpu.roll(v, ROT_DIM // 2, 1) * sb

    q_ref[...] = z[:, Z_Q:Z_Q + D_ATTN]
    for i in range(D_ATTN // LANES):
        qr_ref[:, i * LANES:(i + 1) * LANES] = rope(z[:, Z_Q + i * LANES:Z_Q + (i + 1) * LANES])
    kv = [z[:, Z_KV + i * KV_W:Z_KV + (i + 1) * KV_W] for i in range(6)]
    kv[2] = rope(kv[2])
    kv[4] = rope(kv[4])
    for ref, v in zip((kc_ref, vc_ref, ks_ref, vs_ref, kw_ref, vw_ref), kv):
        ref[...] = v
    for ref, v in zip(t_refs, kv):
        ref[0] = v.T
    u_ref[...] = z[:, Z_U:Z_U + D_SSM]
    gt_ref[...] = jax.nn.sigmoid(z[:, Z_G:Z_G + LANES])


def _proj(x, g, w, rope_c, rope_sa, rope_sb, seq_t=None):
    m = x.shape[0]
    tm = min(m, 512)
    n_rep = rope_c.shape[0] // tm
    row = lambda i: (i, 0)
    const = lambda i: (0, 0)
    tab = lambda i: (i % n_rep, 0)
    widths = [D_ATTN, D_ATTN] + [KV_W] * 6 + [LANES, D_SSM]
    out_specs = [pl.BlockSpec((tm, wd), row) for wd in widths]
    out_shape = [jax.ShapeDtypeStruct((m, wd), F32) for wd in widths]
    if seq_t is not None:
        assert rope_c.shape[0] == seq_t
        out_specs += [pl.BlockSpec((1, KV_W, tm), lambda i: (i // n_rep, 0, i % n_rep))] * 6
        out_shape += [jax.ShapeDtypeStruct((m // seq_t, KV_W, seq_t), F32)] * 6
    return pl.pallas_call(
        _proj_body,
        grid=(m // tm,),
        in_specs=[pl.BlockSpec((tm, D_MODEL), row), pl.BlockSpec((1, D_MODEL), const),
                  pl.BlockSpec((D_MODEL, Z_W), const),
                  pl.BlockSpec((tm, LANES), tab), pl.BlockSpec((tm, LANES), tab), pl.BlockSpec((tm, LANES), tab)],
        out_specs=out_specs,
        out_shape=out_shape,
        compiler_params=_cparams(("parallel",)),
        name="proj",
    )(x, g, w, rope_c, rope_sa, rope_sb)


def _rope_tables(pos):
    half = ROT_DIM // 2
    inv_freq = jnp.power(ROPE_THETA, -(jnp.arange(half, dtype=F32) * 2.0 / ROT_DIM))
    ang = pos.astype(F32)[:, None] * inv_freq[None, :]
    cos, sin = jnp.cos(ang), jnp.sin(ang)
    n = pos.shape[0]
    one = jnp.ones((n, HEAD_DIM - ROT_DIM), F32)
    zero = jnp.zeros((n, HEAD_DIM - ROT_DIM), F32)
    zh = jnp.zeros((n, half), F32)
    c = jnp.concatenate([cos, cos, one], axis=1)
    sa = jnp.concatenate([-sin, zh, zero], axis=1)
    sb = jnp.concatenate([zh, sin, zero], axis=1)
    t2 = lambda a: jnp.concatenate([a, a], axis=1)
    return t2(c), t2(sa), t2(sb)


def _compress_rows(x_ref, n_half, w1_ref, b_ref, w2_ref):
    xs = [x_ref[0, pl.ds(s, n_half, stride=CMP_STRIDE), :] for s in range(CMP_STRIDE)]
    xcat = jnp.concatenate(xs, axis=1).astype(BF16)
    p = _dot(xcat, w1_ref[...])
    nh = N_KV_HEADS * CMP_HIDDEN
    p1_next = pltpu.roll(p[:, nh:], n_half - 1, 0)
    pre = p[:, :nh] + p1_next + b_ref[...]
    return _dot(jax.nn.gelu(pre).astype(BF16), w2_ref[...])


def _cmp_prompt_body(kc_ref, vc_ref, wk1_ref, bk_ref, wk2_ref, wv1_ref, bv_ref, wv2_ref, ko_ref, vo_ref, *, n_half):
    ko_ref[0] = _compress_rows(kc_ref, n_half, wk1_ref, bk_ref, wk2_ref)
    vo_ref[0] = _compress_rows(vc_ref, n_half, wv1_ref, bv_ref, wv2_ref)


def _cmp_prompt(kc, vc, wk, wv):
    b, t, _ = kc.shape
    n_half = t // CMP_STRIDE
    seq = pl.BlockSpec((1, t, KV_W), lambda i: (i, 0, 0))
    const = lambda shp: pl.BlockSpec(shp, lambda i: (0,) * len(shp))
    wspecs = [const(wk[0].shape), const(wk[1].shape), const(wk[2].shape)]
    out = pl.BlockSpec((1, n_half, KV_W), lambda i: (i, 0, 0))
    return pl.pallas_call(
        functools.partial(_cmp_prompt_body, n_half=n_half),
        grid=(b,),
        in_specs=[seq, seq] + wspecs + wspecs,
        out_specs=[out, out],
        out_shape=[jax.ShapeDtypeStruct((b, n_half, KV_W), F32)] * 2,
        compiler_params=_cparams(("parallel",)),
        name="cmp_prompt",
    )(kc, vc, *wk, *wv)


def _cmp_weights(pos_emb, w1, b1, w2):
    w1r = w1.reshape(CMP_RATIO, CMP_STRIDE, HEAD_DIM, CMP_HIDDEN)
    eye = jnp.eye(N_KV_HEADS, dtype=F32)
    w1p = jnp.einsum('rsdh,kj->skdrjh', w1r, eye).reshape(CMP_STRIDE * KV_W, CMP_RATIO * N_KV_HEADS * CMP_HIDDEN)
    pe = pos_emb.reshape(CMP_RATIO, CMP_STRIDE, HEAD_DIM)
    bias = b1 + jnp.einsum('rsd,rsdh->h', pe, w1r, precision=lax.Precision.HIGHEST)
    bias = jnp.tile(bias, N_KV_HEADS)[None, :]
    w2p = jnp.einsum('hd,kj->khjd', w2, eye).reshape(N_KV_HEADS * CMP_HIDDEN, KV_W)
    return w1p.astype(BF16), bias, w2p.astype(BF16)


TQ = 128
KCHUNK = 512
N_SEL_PROMPT = 32


def _softmax_rows(s):
    mx = jnp.max(s, axis=-1, keepdims=True)
    e = jnp.exp(s - mx)
    return e / jnp.sum(e, axis=-1, keepdims=True)


def _attn_prompt_body(q_ref, qr_ref, kcmp_ref, vcmp_ref, ks_ref, vs_ref, kw_ref, vw_ref, gt_ref,
                      amat_ref, emat_ref, gn_ref, o_ref, *, seq):
    t0 = pl.program_id(1) * TQ
    lane = lax.broadcasted_iota(jnp.int32, (TQ, LANES), 1)
    low = lane < HEAD_DIM
    tq = t0 + lax.broadcasted_iota(jnp.int32, (TQ, 1), 0)
    gates = gt_ref[0]

    def head_rows(src_ref, k):
        rows = []
        for g in range(GQA):
            h = GQA * k + g
            v = src_ref[0, :, (h // 2) * LANES:(h // 2 + 1) * LANES]
            if h % 2 != k:
                v = pltpu.roll(v, HEAD_DIM, 1)
            rows.append(jnp.where(low if k == 0 else jnp.logical_not(low), v * ATTN_SCALE, 0.0))
        return jnp.concatenate(rows, axis=0).astype(BF16)

    jj = lax.broadcasted_iota(jnp.int32, (N_SEL_PROMPT, TQ), 0)
    jq = (t0 + lax.broadcasted_iota(jnp.int32, (N_SEL_PROMPT, TQ), 1)) // SEL_BLOCK
    forced = (jj == 0) | (jj == jq) | (jj == jq - 1)

    n_cmp = kcmp_ref.shape[1]
    blk_end = lax.broadcasted_iota(jnp.int32, (1, n_cmp), 1) * CMP_STRIDE + (CMP_LEN - 1)
    m_cmp = (blk_end <= tq)[None]

    n_chunks = t0 // KCHUNK + 1
    w_keys = WINDOW + TQ
    w_start = pl.multiple_of(jnp.clip(t0 - WINDOW, 0, seq - w_keys), TQ)
    kp_w = w_start + lax.broadcasted_iota(jnp.int32, (1, w_keys), 1)
    m_win = ((kp_w <= tq) & (kp_w > tq - WINDOW))[None]

    comb = []
    for k in range(N_KV_HEADS):
        qc = head_rows(q_ref, k)
        qr = head_rows(qr_ref, k)

        s = _dot_nt(qc, kcmp_ref[0].astype(BF16)).reshape(GQA, TQ, n_cmp)
        p = jnp.where(m_cmp, _softmax_rows(jnp.where(m_cmp, s, NEG_INF)), 0.0)
        o_c = _dot(p.reshape(GQA * TQ, n_cmp).astype(BF16), vcmp_ref[0].astype(BF16))
        imp = jnp.sum(p, axis=0)

        imp_sel = jnp.dot(amat_ref[...], imp.T, precision=lax.Precision.HIGHEST, preferred_element_type=F32)
        score = jnp.where(jj > jq, -1.0, jnp.where(forced, FORCE_SCORE, imp_sel))
        rank = jnp.zeros((N_SEL_PROMPT, TQ), F32)
        for i in range(N_SEL_PROMPT):
            si = score[i:i + 1, :]
            beats = (si > score) | ((si == score) & (jj > i))
            rank = rank + jnp.where(beats, 1.0, 0.0)
        sel_t = jnp.where((rank < N_SELECT) & (jj <= jq), 1.0, 0.0)
        sel_t = jnp.concatenate([sel_t, jnp.zeros((LANES - N_SEL_PROMPT, TQ), F32)], axis=0)
        sel = sel_t.T

        bias = ((sel - 1.0) * (-NEG_INF)).astype(BF16)
        q_aug = jnp.concatenate([qr, jnp.concatenate([bias] * GQA, axis=0)], axis=1)

        def chunk(c, carry, causal):
            m_i, l_i, acc = carry
            k0 = pl.multiple_of(c * KCHUNK, KCHUNK)
            kk = jnp.concatenate([ks_ref[0, :, pl.ds(k0, KCHUNK)].astype(BF16),
                                  emat_ref[:, pl.ds(k0, KCHUNK)]], axis=0)
            vv = vs_ref[0, :, pl.ds(k0, KCHUNK)].astype(BF16)
            sc = _dot(q_aug, kk).reshape(GQA, TQ, KCHUNK)
            if causal:
                kpos = k0 + lax.broadcasted_iota(jnp.int32, (1, KCHUNK), 1)
                sc = jnp.where((kpos <= tq)[None], sc, NEG_INF)
            m_n = jnp.maximum(m_i, jnp.max(sc, axis=-1, keepdims=True))
            alpha = jnp.exp(m_i - m_n)
            pe = jnp.exp(sc - m_n)
            l_n = alpha * l_i + jnp.sum(pe, axis=-1, keepdims=True)
            pv = _dot_nt(pe.reshape(GQA * TQ, KCHUNK).astype(BF16), vv).reshape(GQA, TQ, LANES)
            return m_n, l_n, alpha * acc + pv

        init = (jnp.full((GQA, TQ, 1), NEG_INF, F32), jnp.zeros((GQA, TQ, 1), F32), jnp.zeros((GQA, TQ, LANES), F32))
        carry = lax.fori_loop(0, n_chunks - 1, functools.partial(chunk, causal=False), init)
        _, l_f, acc_f = chunk(n_chunks - 1, carry, True)
        o_s = acc_f / l_f

        kk = kw_ref[0, :, pl.ds(w_start, w_keys)].astype(BF16)
        vv = vw_ref[0, :, pl.ds(w_start, w_keys)].astype(BF16)
        sw = jnp.where(m_win, _dot(qr, kk).reshape(GQA, TQ, w_keys), NEG_INF)
        ew = jnp.exp(sw - jnp.max(sw, axis=-1, keepdims=True))
        o_w = (_dot_nt(ew.reshape(GQA * TQ, w_keys).astype(BF16), vv).reshape(GQA, TQ, LANES)
               / jnp.sum(ew, axis=-1, keepdims=True))

        o_c = o_c.reshape(GQA, TQ, LANES)
        for g in range(GQA):
            h = GQA * k + g
            comb.append(gates[:, 3 * h:3 * h + 1] * o_c[g] + gates[:, 3 * h + 1:3 * h + 2] * o_s[g]
                        + gates[:, 3 * h + 2:3 * h + 3] * o_w[g])

    cols = []
    for pr in range(N_HEADS // 2):
        k = (2 * pr) // GQA
        a, b = comb[2 * pr], comb[2 * pr + 1]
        a = pltpu.roll(a, HEAD_DIM, 1) if k == 1 else a
        b = pltpu.roll(b, HEAD_DIM, 1) if k == 0 else b
        cols.append(jnp.where(low, a, b))
    o = jnp.concatenate(cols, axis=1)
    o_ref[0] = _rms(o, gn_ref[...]).astype(o_ref.dtype)


def _sel_matrices(n_cmp_pad, n_sel, n_keys):
    j = np.arange(n_sel)[:, None]
    n = np.arange(n_cmp_pad)[None, :]
    per = SEL_BLOCK // CMP_STRIDE
    amat = ((n >= per * j - (CMP_RATIO - 1)) & (n <= per * j + per - 1)).astype(np.float32)
    jrow = np.arange(LANES)[:, None]
    key = np.arange(n_keys)[None, :]
    emat = (key // SEL_BLOCK == jrow).astype(np.float32)
    return jnp.asarray(amat), jnp.asarray(emat, dtype=BF16)


def _attn_prompt(q, qr, kcmp, vcmp, ks, vs, kw, vw, gt, gn):
    b, t, _ = q.shape
    amat, emat = _sel_matrices(kcmp.shape[1], N_SEL_PROMPT, t)
    qt = lambda w: pl.BlockSpec((1, TQ, w), lambda i, j: (i, j, 0))
    full = lambda a: pl.BlockSpec((1,) + a.shape[1:], lambda i, j: (i, 0, 0))
    const = lambda a: pl.BlockSpec(a.shape, lambda i, j: (0, 0))
    return pl.pallas_call(
        functools.partial(_attn_prompt_body, seq=t),
        grid=(b, t // TQ),
        in_specs=[qt(D_ATTN), qt(D_ATTN), full(kcmp), full(vcmp), full(ks), full(vs), full(kw), full(vw),
                  qt(LANES), const(amat), const(emat), const(gn)],
        out_specs=qt(D_ATTN),
        out_shape=jax.ShapeDtypeStruct((b, t, D_ATTN), BF16),
        compiler_params=_cparams(("parallel", "arbitrary")),
        name="attn_prompt",
    )(q, qr, kcmp, vcmp, ks, vs, kw, vw, gt, amat, emat, gn)


S5_LB = D_SSM // LANES
S5_SW = LANES // SSM_CH * SSM_STATE


def _s5_body(u_ref, h0r_ref, h0i_ref, lr_ref, li_ref, bre_ref, bim_ref, c_ref, d_ref, gw_ref, gb_ref, gn_ref,
             y_ref, hr_ref, hi_ref, xr_s, xi_s, y_s, *, tc, bb):
    step = pl.program_id(0)
    m = tc * bb

    @pl.when(step == 0)
    def _():
        hr_ref[...] = h0r_ref[...]
        hi_ref[...] = h0i_ref[...]

    u = u_ref[...].reshape(m, D_SSM)
    ub = u.astype(BF16)
    for j in range(S5_LB):
        uj = ub[:, j * LANES:(j + 1) * LANES]
        sl = slice(j * S5_SW, (j + 1) * S5_SW)
        xr_s[...] = _dot(uj, bre_ref[j]).reshape(tc, bb, S5_SW)
        xi_s[...] = _dot(uj, bim_ref[j]).reshape(tc, bb, S5_SW)
        lr = jnp.broadcast_to(lr_ref[:, sl], (bb, S5_SW))
        li = jnp.broadcast_to(li_ref[:, sl], (bb, S5_SW))

        def scan(t, carry):
            hr, hi = carry
            nr = lr * hr - li * hi + xr_s[t]
            ni = lr * hi + li * hr + xi_s[t]
            xr_s[t] = nr
            xi_s[t] = ni
            return nr, ni

        hr, hi = lax.fori_loop(0, tc, scan, (hr_ref[:, sl], hi_ref[:, sl]))
        hr_ref[:, sl] = hr
        hi_ref[:, sl] = hi
        hcat = jnp.concatenate([xr_s[...].reshape(m, S5_SW), xi_s[...].reshape(m, S5_SW)], axis=1).astype(BF16)
        y_s[:, j * LANES:(j + 1) * LANES] = _dot(hcat, c_ref[j])
    z = jax.nn.gelu(y_s[...] + d_ref[...] * u)
    out = z * jax.nn.sigmoid(_dot(z.astype(BF16), gw_ref[...]) + gb_ref[...])
    y_ref[...] = _rms(out, gn_ref[...]).astype(y_ref.dtype).reshape(tc, bb, D_SSM)


def _s5(u_tm, h0r, h0i, sp, tc):
    t, bb, _ = u_tm.shape
    const = lambda a: pl.BlockSpec(a.shape, lambda i: (0,) * a.ndim)
    args = (h0r, h0i, sp['lr'], sp['li'], sp['bre'], sp['bim'], sp['c'], sp['d'], sp['glu_w'], sp['glu_b'], sp['gn'])
    st = jax.ShapeDtypeStruct(h0r.shape, F32)
    return pl.pallas_call(
        functools.partial(_s5_body, tc=tc, bb=bb),
        grid=(t // tc,),
        in_specs=[pl.BlockSpec((tc, bb, D_SSM), lambda i: (i, 0, 0))] + [const(a) for a in args],
        out_specs=[pl.BlockSpec((tc, bb, D_SSM), lambda i: (i, 0, 0)), const(h0r), const(h0i)],
        out_shape=[jax.ShapeDtypeStruct((t, bb, D_SSM), F32), st, st],
        scratch_shapes=[pltpu.VMEM((tc, bb, S5_SW), F32), pltpu.VMEM((tc, bb, S5_SW), F32),
                        pltpu.VMEM((tc * bb, D_SSM), F32)],
        compiler_params=_cparams(("arbitrary",)),
        name="s5",
    )(u_tm, *args)


def _s5_params(log_dt, a_re, a_im, b_re, b_im, c_re, c_im, d, glu_w, glu_b, gn):
    dt = jnp.exp(log_dt)[:, None]
    mag = jnp.exp(a_re * dt)
    lr, li = mag * jnp.cos(a_im * dt), mag * jnp.sin(a_im * dt)
    den = a_re * a_re + a_im * a_im
    inv_r, inv_i = a_re / den, -a_im / den
    nr, ni = lr - 1.0, li
    fr, fi = nr * inv_r - ni * inv_i, nr * inv_i + ni * inv_r
    bbr = fr[..., None] * b_re - fi[..., None] * b_im
    bbi = fr[..., None] * b_im + fi[..., None] * b_re
    gl = LANES // SSM_CH
    eye = jnp.eye(gl, dtype=F32)

    def bmat(bb):
        x = bb.reshape(S5_LB, gl, SSM_STATE, SSM_CH)
        return jnp.einsum('jgpc,gh->jgchp', x, eye).reshape(S5_LB, LANES, S5_SW).astype(BF16)

    def cmat(cc):
        x = cc.reshape(S5_LB, gl, SSM_CH, SSM_STATE)
        return jnp.einsum('jgcp,gh->jgphc', x, eye).reshape(S5_LB, S5_SW, LANES)

    cm = jnp.concatenate([cmat(c_re), -cmat(c_im)], axis=1).astype(BF16)
    return dict(lr=lr.reshape(1, -1), li=li.reshape(1, -1), bre=bmat(bbr), bim=bmat(bbi), c=cm,
                d=d.reshape(1, -1), glu_w=glu_w.astype(BF16), glu_b=glu_b[None, :], gn=gn[None, :])


def _outproj_body(x_ref, a_ref, s_ref, wa_ref, ws_ref, gn_ref, o_ref, *, norm_attn):
    a = a_ref[...]
    if norm_attn:
        a = _rms(a, gn_ref[...])
    o_ref[...] = x_ref[...] + _dot(a.astype(BF16), wa_ref[...]) + _dot(s_ref[...].astype(BF16), ws_ref[...])


def _outproj(x, a, s, wa, ws, gn, norm_attn):
    m = x.shape[0]
    tm = min(m, 1024)
    row = lambda w: pl.BlockSpec((tm, w), lambda i: (i, 0))
    const = lambda a_: pl.BlockSpec(a_.shape, lambda i: (0, 0))
    return pl.pallas_call(
        functools.partial(_outproj_body, norm_attn=norm_attn),
        grid=(m // tm,),
        in_specs=[row(D_MODEL), row(D_ATTN), row(D_SSM), const(wa), const(ws), const(gn)],
        out_specs=row(D_MODEL),
        out_shape=jax.ShapeDtypeStruct((m, D_MODEL), F32),
        compiler_params=_cparams(("parallel",)),
        name="outproj",
    )(x, a, s, wa, ws, gn)


N_PAGES = PAST_LEN // PAGE_SIZE
CMP_CHUNK = 32
CMP_PITCH = 24


def _page_copy(pool_ref, buf_ref, sem_ref, pt_ref, b, i, slot):
    return pltpu.make_async_copy(pool_ref.at[pt_ref[b * N_PAGES + i]], buf_ref.at[slot, i], sem_ref.at[slot])


def _cmp_paged_body(pt_ref, pool_ref, w1_ref, b_ref, w2_ref, o_ref, buf_ref, rows_ref, p_ref, sem_ref):
    b = pl.program_id(0)
    slot = b % 2

    def start_all(bb, sl):
        def body(i, c):
            _page_copy(pool_ref, buf_ref, sem_ref, pt_ref, bb, i, sl).start()
            return c
        lax.fori_loop(0, N_PAGES, body, 0, unroll=8)

    @pl.when(b == 0)
    def _():
        start_all(0, 0)

    @pl.when(b + 1 < pl.num_programs(0))
    def _():
        start_all(b + 1, 1 - slot)

    def wait_one(i, c):
        _page_copy(pool_ref, buf_ref, sem_ref, pt_ref, b, i, slot).wait()
        return c
    lax.fori_loop(0, N_PAGES, wait_one, 0, unroll=8)

    hp = PAGE_SIZE // CMP_STRIDE
    hc = CMP_CHUNK * hp
    lane = lax.broadcasted_iota(jnp.int32, (hc, LANES), 1)
    low = lane < HEAD_DIM

    def stage_a(c):
        for i in range(CMP_CHUNK):
            t = buf_ref[slot, c * CMP_CHUNK + i].T
            for h in range(hp):
                rows_ref[c % 2, (i * hp + h) * CMP_PITCH:(i * hp + h) * CMP_PITCH + CMP_STRIDE, :] = (
                    t[h * CMP_STRIDE:(h + 1) * CMP_STRIDE])

    def stage_b(c):
        xs = [rows_ref[c % 2, pl.ds(s, hc, stride=CMP_PITCH), :] for s in range(CMP_STRIDE)]
        heads = [[], []]
        for j in range(CMP_STRIDE // 2):
            a, b = xs[2 * j], xs[2 * j + 1]
            heads[0].append(jnp.where(low, a, pltpu.roll(b, HEAD_DIM, 1)))
            heads[1].append(jnp.where(low, pltpu.roll(a, HEAD_DIM, 1), b))
        x = jnp.concatenate([jnp.concatenate(hd, axis=1) for hd in heads], axis=0).astype(BF16)
        p = _dot(x, w1_ref[...])
        for k in range(N_KV_HEADS):
            p_ref[k, c * hc:(c + 1) * hc, :] = p[k * hc:(k + 1) * hc]

    n_chunks = N_PAGES // CMP_CHUNK
    stage_a(0)
    for c in range(n_chunks):
        if c + 1 < n_chunks:
            stage_a(c + 1)
        stage_b(c)

    n_half = N_PAGES * hp
    out = None
    for k in range(N_KV_HEADS):
        pre = p_ref[k, :, :CMP_HIDDEN] + pltpu.roll(p_ref[k, :, CMP_HIDDEN:], n_half - 1, 0) + b_ref[...]
        ok = _dot(jax.nn.gelu(pre).astype(BF16), w2_ref[k])
        out = ok if out is None else out + ok
    o_ref[0] = out


def _cmp_paged(page_table_flat, pool_t, w1, bias, w2, n_seq):
    n_half = PAST_LEN // CMP_STRIDE
    hc = CMP_CHUNK * PAGE_SIZE // CMP_STRIDE
    const = lambda a: pl.BlockSpec(a.shape, lambda i, pt: (0,) * a.ndim)
    return pl.pallas_call(
        _cmp_paged_body,
        grid_spec=pltpu.PrefetchScalarGridSpec(
            num_scalar_prefetch=1,
            grid=(n_seq,),
            in_specs=[pl.BlockSpec(memory_space=pl.ANY), const(w1), const(bias), const(w2)],
            out_specs=pl.BlockSpec((1, n_half, KV_W), lambda i, pt: (i, 0, 0)),
            scratch_shapes=[pltpu.VMEM((2, N_PAGES, PAGE_SIZE, KV_W), F32),
                            pltpu.VMEM((2, hc * CMP_PITCH, KV_W), F32),
                            pltpu.VMEM((N_KV_HEADS, n_half, CMP_RATIO * CMP_HIDDEN), F32),
                            pltpu.SemaphoreType.DMA((2,))]),
        out_shape=jax.ShapeDtypeStruct((n_seq, n_half, KV_W), F32),
        compiler_params=_cparams(("arbitrary",)),
        name="cmp_paged",
    )(page_table_flat, pool_t, w1, bias, w2)


def _cmp_weights_split(pos_emb, w1, b1, w2):
    w1r = w1.reshape(CMP_RATIO, CMP_STRIDE * HEAD_DIM, CMP_HIDDEN)
    w1s = w1r.transpose(1, 0, 2).reshape(CMP_STRIDE * HEAD_DIM, CMP_RATIO * CMP_HIDDEN)
    pe = pos_emb.reshape(CMP_RATIO, CMP_STRIDE * HEAD_DIM)
    bias = b1 + jnp.einsum('rx,rxh->h', pe, w1r, precision=lax.Precision.HIGHEST)
    z = jnp.zeros_like(w2)
    w2k = jnp.stack([jnp.concatenate([w2, z], axis=1), jnp.concatenate([z, w2], axis=1)])
    return w1s.astype(BF16), bias[None, :], w2k.astype(BF16)


CMPATTN_SEQS = 8


def _cmpattn_body(q_ref, kc_ref, vc_ref, amat_ref, o_ref, idx_ref, *, ts, n_cmp, n_sel):
    rows = GQA * ts
    qpos = PAST_LEN + lax.broadcasted_iota(jnp.int32, (rows, 1), 0) % ts
    n_pad = kc_ref.shape[1]
    n_io = lax.broadcasted_iota(jnp.int32, (1, n_pad), 1)
    m = (n_io * CMP_STRIDE + (CMP_LEN - 1) <= qpos) & (n_io < n_cmp)
    n_sb = q_ref.shape[0]
    imps = []
    for b in range(n_sb):
        kc = kc_ref[b].astype(BF16)
        vc = vc_ref[b].astype(BF16)
        for k in range(N_KV_HEADS):
            s = _dot_nt((q_ref[b, k] * ATTN_SCALE).astype(BF16), kc)
            p = jnp.where(m, _softmax_rows(jnp.where(m, s, NEG_INF)), 0.0)
            o_ref[b, k] = _dot(p.astype(BF16), vc)
            imp = p[0:ts]
            for g in range(1, GQA):
                imp = imp + p[g * ts:(g + 1) * ts]
            imps.append(imp)
    imp_all = jnp.concatenate(imps, axis=0)
    n_rows = imp_all.shape[0]
    sel_w = amat_ref.shape[1]
    imp_sel = jnp.dot(imp_all, amat_ref[...], precision=lax.Precision.HIGHEST, preferred_element_type=F32)
    j = lax.broadcasted_iota(jnp.int32, (n_rows, sel_w), 1)
    jq = (PAST_LEN + lax.broadcasted_iota(jnp.int32, (n_rows, sel_w), 0) % ts) // SEL_BLOCK
    forced = (j == 0) | (j == jq) | (j == jq - 1)
    jf = j.astype(F32)
    lane_k = lax.broadcasted_iota(jnp.int32, (n_rows, LANES), 1)
    score = jnp.where(j > jq, -1.0, jnp.where(forced, FORCE_SCORE, imp_sel))
    score = jnp.where(j < n_sel, score, -jnp.inf)
    res = jnp.zeros((n_rows, LANES), F32)
    for i in range(N_SELECT):
        mx = jnp.max(score, axis=-1, keepdims=True)
        ix = jnp.min(jnp.where(score == mx, jf, 1e9), axis=-1, keepdims=True)
        res = jnp.where(lane_k == i, ix, res)
        score = jnp.where(jf == ix, -jnp.inf, score)
    idx_ref[...] = res.astype(jnp.int32).reshape(idx_ref.shape)


def _cmpattn(q_pad, kcmp, vcmp, ts):
    n_seq = q_pad.shape[0]
    n_cmp = (PAST_LEN + ts) // CMP_STRIDE - CMP_RATIO + 1
    n_sel = -(-(PAST_LEN + ts) // SEL_BLOCK)
    sel_w = -(-n_sel // LANES) * LANES
    per = SEL_BLOCK // CMP_STRIDE
    n = np.arange(kcmp.shape[1])[:, None]
    j = np.arange(sel_w)[None, :]
    amat = jnp.asarray(((n >= per * j - (CMP_RATIO - 1)) & (n <= per * j + per - 1) & (j < n_sel)).astype(np.float32))
    rows = GQA * ts
    sb = CMPATTN_SEQS
    return pl.pallas_call(
        functools.partial(_cmpattn_body, ts=ts, n_cmp=n_cmp, n_sel=n_sel),
        grid=(n_seq // sb,),
        in_specs=[pl.BlockSpec((sb, N_KV_HEADS, rows, LANES), lambda i: (i, 0, 0, 0)),
                  pl.BlockSpec((sb,) + kcmp.shape[1:], lambda i: (i, 0, 0)),
                  pl.BlockSpec((sb,) + vcmp.shape[1:], lambda i: (i, 0, 0)),
                  pl.BlockSpec(amat.shape, lambda i: (0, 0))],
        out_specs=[pl.BlockSpec((sb, N_KV_HEADS, rows, LANES), lambda i: (i, 0, 0, 0)),
                   pl.BlockSpec((sb, N_KV_HEADS * ts, LANES), lambda i: (i, 0, 0))],
        out_shape=[jax.ShapeDtypeStruct((n_seq, N_KV_HEADS, rows, LANES), F32),
                   jax.ShapeDtypeStruct((n_seq, N_KV_HEADS * ts, LANES), jnp.int32)],
        compiler_params=_cparams(("parallel",)),
        name="cmpattn_sample",
    )(q_pad, kcmp, vcmp, amat)


def _slc_copies(pools, bufs, sems, pt_ref, idx_ref, step, r, slot, ts, for_wait):
    if for_wait:
        pg = 0
    else:
        blk = idx_ref[step * (ts * N_SELECT) + r]
        page = jnp.minimum(blk // (PAGE_SIZE // SEL_BLOCK), N_PAGES - 1)
        pg = pt_ref[(step // N_KV_HEADS) * N_PAGES + page]
    return [pltpu.make_async_copy(pool.at[pg], buf.at[slot, r], sem.at[slot])
            for pool, buf, sem in zip(pools, bufs, sems)]


def _tail_body(pt_ref, idx_ref, kpool_ref, vpool_ref, q_ref, ksn_ref, vsn_ref, kwn_ref, vwn_ref, wk_ref, wv_ref,
               oc_ref, gt_ref, o_ref, kbuf_ref, vbuf_ref, ksem_ref, vsem_ref, *, ts):
    step = pl.program_id(0) * N_KV_HEADS + pl.program_id(1)
    n_steps = pl.num_programs(0) * N_KV_HEADS
    slot = step % 2
    n_fetch = ts * N_SELECT

    pools, bufs, sems = (kpool_ref, vpool_ref), (kbuf_ref, vbuf_ref), (ksem_ref, vsem_ref)

    def start_all(st, sl):
        def body(r, c):
            for cp in _slc_copies(pools, bufs, sems, pt_ref, idx_ref, st, r, sl, ts, False):
                cp.start()
            return c
        lax.fori_loop(0, n_fetch, body, 0, unroll=8)

    @pl.when(step == 0)
    def _():
        start_all(0, 0)

    @pl.when(step + 1 < n_steps)
    def _():
        start_all(step + 1, 1 - slot)

    def wait_one(r, c):
        for cp in _slc_copies(pools, bufs, sems, pt_ref, idx_ref, step, r, slot, ts, True):
            cp.wait()
        return c
    lax.fori_loop(0, n_fetch, wait_one, 0, unroll=8)

    rows = GQA * ts
    q = (q_ref[0, 0] * ATTN_SCALE).astype(BF16)
    row_t = lax.broadcasted_iota(jnp.int32, (rows, 1), 0) % ts
    qpos = PAST_LEN + row_t
    lane = lax.broadcasted_iota(jnp.int32, (1, LANES), 1)
    lane_half, lane_off = lane // SEL_BLOCK, lane % SEL_BLOCK
    pad = jnp.zeros((LANES - ksn_ref.shape[1], LANES), F32)
    new_pos = PAST_LEN + lane
    new_ok = lane < ts
    blk_new = PAST_LEN // SEL_BLOCK

    def new_rows(ref):
        return jnp.concatenate([ref[0], pad], axis=0).astype(BF16)

    ksn, vsn = new_rows(ksn_ref), new_rows(vsn_ref)
    s_new = _dot_nt(q, ksn)
    o_s = jnp.zeros((rows, LANES), F32)
    grp = 4

    def pages(buf_ref, r0):
        return jnp.concatenate([buf_ref[slot, r0 + u].astype(BF16) for u in range(grp)], axis=1)

    for t in range(ts):
        pieces = []
        has_new = jnp.int32(0)
        for i0 in range(0, N_SELECT, grp):
            sc = _dot(q, pages(kbuf_ref, t * N_SELECT + i0))
            for u in range(grp):
                blk = idx_ref[step * n_fetch + t * N_SELECT + i0 + u]
                has_new = has_new | (blk == blk_new).astype(jnp.int32)
                want_half = jnp.where(blk < blk_new, blk % (PAGE_SIZE // SEL_BLOCK), -1)
                ok = (lane_half == want_half) & (blk * SEL_BLOCK + lane_off <= qpos)
                pieces.append(jnp.where(ok, sc[:, u * LANES:(u + 1) * LANES], NEG_INF))
        ok_new = (lane < jnp.where(has_new > 0, ts, 0)) & (new_pos <= qpos)
        pieces.append(jnp.where(ok_new, s_new, NEG_INF))
        p = _softmax_rows(jnp.concatenate(pieces, axis=1)).astype(BF16)
        o_t = _dot(p[:, N_SELECT * LANES:], vsn)
        for i0 in range(0, N_SELECT, grp):
            o_t = o_t + _dot_nt(p[:, i0 * LANES:(i0 + grp) * LANES], pages(vbuf_ref, t * N_SELECT + i0))
        o_s = jnp.where(row_t == t, o_t, o_s)

    wb = wk_ref.shape[2]
    kp = PAST_LEN - wb + lax.broadcasted_iota(jnp.int32, (1, wb), 1)
    ok_w = (kp >= 0) & (kp <= qpos) & (kp > qpos - WINDOW)
    ok_wn = new_ok & (new_pos <= qpos) & (new_pos > qpos - WINDOW)
    s_w = jnp.where(ok_w, _dot(q, wk_ref[0].astype(BF16)), NEG_INF)
    s_wn = jnp.where(ok_wn, _dot_nt(q, new_rows(kwn_ref)), NEG_INF)
    p = _softmax_rows(jnp.concatenate([s_w, s_wn], axis=1)).astype(BF16)
    o_w = _dot_nt(p[:, :wb], wv_ref[0].astype(BF16)) + _dot(p[:, wb:], new_rows(vwn_ref))

    gt = gt_ref[0, 0]
    o_ref[0, 0] = gt[:, 0:1] * oc_ref[0, 0] + gt[:, 1:2] * o_s + gt[:, 2:3] * o_w


def _tail(page_table_flat, idx_flat, kpool_t, vpool_t, q_pad, ksn, vsn, kwn, vwn, wk_t, wv_t, o_cmp, gates_pad, ts):
    n_seq = q_pad.shape[0]
    rows = GQA * ts
    n_fetch = ts * N_SELECT
    per_head = pl.BlockSpec((1, 1, rows, LANES), lambda i, k, pt, ix: (i, k, 0, 0))
    per_seq = lambda a: pl.BlockSpec((1,) + a.shape[1:], lambda i, k, pt, ix: (i, 0, 0))
    hbm = pl.BlockSpec(memory_space=pl.ANY)
    return pl.pallas_call(
        functools.partial(_tail_body, ts=ts),
        grid_spec=pltpu.PrefetchScalarGridSpec(
            num_scalar_prefetch=2,
            grid=(n_seq, N_KV_HEADS),
            in_specs=[hbm, hbm, per_head, per_seq(ksn), per_seq(vsn), per_seq(kwn), per_seq(vwn),
                      per_seq(wk_t), per_seq(wv_t), per_head, per_head],
            out_specs=per_head,
            scratch_shapes=[pltpu.VMEM((2, n_fetch, PAGE_SIZE, KV_W), F32), pltpu.VMEM((2, n_fetch, PAGE_SIZE, KV_W), F32),
                            pltpu.SemaphoreType.DMA((2,)), pltpu.SemaphoreType.DMA((2,))]),
        out_shape=jax.ShapeDtypeStruct((n_seq, N_KV_HEADS, rows, LANES), F32),
        compiler_params=_cparams(("arbitrary", "arbitrary")),
        name="slc_win_sample",
    )(page_table_flat, idx_flat, kpool_t, vpool_t, q_pad, ksn, vsn, kwn, vwn, wk_t, wv_t, o_cmp, gates_pad)


def _pad_heads(x, ts):
    n = x.shape[0] // ts
    w = x.shape[1] // N_HEADS
    return x.reshape(n, ts, N_KV_HEADS, GQA, w).transpose(0, 2, 3, 1, 4).reshape(n, N_KV_HEADS, GQA * ts, w)


def _q_on_kv_lanes(q, ts):
    x = _pad_heads(q, ts)
    z = jnp.zeros_like(x[:, 0])
    return jnp.stack([jnp.concatenate([x[:, 0], z], axis=-1), jnp.concatenate([z, x[:, 1]], axis=-1)], axis=1)


def _sample_attention_jnp(q, q_rot, kc, vc, ks, vs, kw, vw, gates, ck, cv, sk, sv, wk, wv, page_table, cw, gn):
    B, T = q.shape[:2]
    pos = PAST_LEN + jnp.arange(T, dtype=jnp.int32)
    q = q.reshape(B, T, N_HEADS, HEAD_DIM)
    q_rot = q_rot.reshape(B, T, N_HEADS, HEAD_DIM)
    kvr = lambda a: a.reshape(B, T, N_KV_HEADS, HEAD_DIM)
    kc, vc, ks, vs, kw, vw = map(kvr, (kc, vc, ks, vs, kw, vw))
    gates = gates[..., :3 * N_HEADS].reshape(B, T, N_HEADS, 3)
    past = lambda pool: pool[page_table].reshape(B, -1, N_KV_HEADS, HEAD_DIM)
    kc_all, vc_all = jnp.concatenate([past(ck), kc], axis=1), jnp.concatenate([past(cv), vc], axis=1)
    ks_all, vs_all = jnp.concatenate([past(sk), ks], axis=1), jnp.concatenate([past(sv), vs], axis=1)

    def compress_blocks(k, pos_emb, w1, b1, w2):
        Bk, Tk = k.shape[:2]
        nh = Tk // CMP_STRIDE
        nc = nh - CMP_RATIO + 1
        halves = k[:, :nh * CMP_STRIDE].reshape(Bk, nh, CMP_STRIDE, N_KV_HEADS, HEAD_DIM)
        pe = pos_emb.reshape(CMP_RATIO, CMP_STRIDE, HEAD_DIM)
        w1r = w1.reshape(CMP_RATIO, CMP_STRIDE, HEAD_DIM, CMP_HIDDEN)
        pre = b1
        for r in range(CMP_RATIO):
            pre = pre + jnp.einsum('bnskd,sdh->bnkh', halves[:, r:r + nc] + pe[r][:, None, :], w1r[r])
        return jax.nn.gelu(pre) @ w2

    k_cmp = compress_blocks(kc_all, *cw[0])
    v_cmp = compress_blocks(vc_all, *cw[1])
    nc = k_cmp.shape[1]
    qg = q.reshape(B, T, N_KV_HEADS, GQA, HEAD_DIM)
    s = jnp.einsum('bqkgd,bnkd->bqkgn', qg, k_cmp).astype(F32) * ATTN_SCALE
    blk_end = jnp.arange(nc) * CMP_STRIDE + CMP_LEN - 1
    m = (blk_end[None, :] <= pos[:, None])[None, :, None, None, :]
    p = jnp.where(m, jax.nn.softmax(jnp.where(m, s, NEG_INF), axis=-1), 0.0)
    o_cmp = jnp.einsum('bqkgn,bnkd->bqkgd', p, v_cmp).reshape(B, T, N_HEADS, HEAD_DIM)
    imp = p.sum(axis=3)
    n_sel = -(-(PAST_LEN + T) // SEL_BLOCK)
    per = SEL_BLOCK // CMP_STRIDE
    total = n_sel * per + CMP_RATIO + per
    pp = jnp.pad(imp, ((0, 0), (0, 0), (0, 0), (CMP_RATIO - 1, total - nc - (CMP_RATIO - 1))))
    impb = sum(pp[..., o:o + n_sel * per:per] for o in range(per + CMP_RATIO - 1))
    j = jnp.arange(n_sel)[None, None, None, :]
    jq = (pos // SEL_BLOCK)[None, :, None, None]
    forced = (j == 0) | (j == jq) | (j == jq - 1)
    score = jnp.where(j > jq, -1.0, jnp.where(forced, FORCE_SCORE, impb))
    _, idx = lax.top_k(score, min(N_SELECT, n_sel))
    valid = idx <= jq

    def to_blocks(k):
        kp = jnp.pad(k, ((0, 0), (0, n_sel * SEL_BLOCK - k.shape[1]), (0, 0), (0, 0)))
        return kp.reshape(B, n_sel, SEL_BLOCK, N_KV_HEADS, HEAD_DIM).transpose(0, 3, 1, 2, 4)

    kb, vb = to_blocks(ks_all), to_blocks(vs_all)
    qg = q_rot.reshape(B, T, N_KV_HEADS, GQA, HEAD_DIM)
    bi = jnp.arange(B)[:, None, None, None]
    hi = jnp.arange(N_KV_HEADS)[None, None, :, None]
    ksel, vsel = kb[bi, hi, idx], vb[bi, hi, idx]
    s = jnp.einsum('bqkgd,bqknsd->bqkgns', qg, ksel).astype(F32) * ATTN_SCALE
    kpos = idx[..., None] * SEL_BLOCK + jnp.arange(SEL_BLOCK)
    mask = valid[..., None] & (kpos <= pos[None, :, None, None, None])
    s = jnp.where(mask[:, :, :, None], s, NEG_INF)
    shp = s.shape
    p = jax.nn.softmax(s.reshape(shp[:-2] + (-1,)), axis=-1).reshape(shp)
    o_slc = jnp.einsum('bqkgns,bqknsd->bqkgd', p, vsel).reshape(B, T, N_HEADS, HEAD_DIM)
    wb = wk.shape[1]
    kw_all, vw_all = jnp.concatenate([wk, kw], axis=1), jnp.concatenate([wv, vw], axis=1)
    k_pos = PAST_LEN - wb + jnp.arange(wb + T, dtype=jnp.int32)
    s = jnp.einsum('bqkgd,bjkd->bqkgj', qg, kw_all).astype(F32) * ATTN_SCALE
    kp, qp = k_pos[None, :], pos[:, None]
    m = ((kp >= 0) & (kp <= qp) & (kp > qp - WINDOW))[None, :, None, None, :]
    p = jax.nn.softmax(jnp.where(m, s, NEG_INF), axis=-1)
    o_win = jnp.einsum('bqkgj,bjkd->bqkgd', p, vw_all).reshape(B, T, N_HEADS, HEAD_DIM)
    o = (gates[..., 0:1] * o_cmp + gates[..., 1:2] * o_slc + gates[..., 2:3] * o_win).reshape(B, T, D_ATTN)
    return _rms(o, gn).astype(BF16)


def kernel(x_prompt, x_sample, cache_cmp_k, cache_cmp_v, cache_slc_k, cache_slc_v, state_win_k, state_win_v, state_s5_re, state_s5_im, page_table, norm_ffn1, ffn1_gate, ffn1_up, ffn1_down, norm_mix, w_in, cmp_k_pos, cmp_k_w1, cmp_k_b1, cmp_k_w2, cmp_v_pos, cmp_v_w1, cmp_v_b1, cmp_v_w2, s5_log_dt, s5_a_re, s5_a_im, s5_b_re, s5_b_im, s5_c_re, s5_c_im, s5_d, s5_glu_w, s5_glu_b, norm_attn_out, norm_ssm_out, w_out, norm_ffn2, ffn2_gate, ffn2_up, ffn2_down, norm_final):
    depth = w_in.shape[0]
    assert depth == 1
    l = 0
    B, T, _ = x_prompt.shape
    BS, TS, _ = x_sample.shape
    row = lambda a: a[None, :]

    w = w_in[l]
    cut = D_ATTN + 6 * KV_W
    w_p = jnp.concatenate([w[:, :cut], w[:, cut + 3 * N_HEADS:], w[:, cut:cut + 3 * N_HEADS],
                           jnp.zeros((D_MODEL, LANES - 3 * N_HEADS), F32)], axis=1).astype(BF16)
    f1 = (row(norm_ffn1[l]), ffn1_gate[l].astype(BF16), ffn1_up[l].astype(BF16), ffn1_down[l].astype(BF16))
    f2 = (row(norm_ffn2[l]), ffn2_gate[l].astype(BF16), ffn2_up[l].astype(BF16), ffn2_down[l].astype(BF16))
    gfin = row(norm_final)
    wo = w_out[l].astype(BF16)
    wo_a, wo_s = wo[:D_ATTN], wo[D_ATTN:]
    cwk = _cmp_weights(cmp_k_pos[l], cmp_k_w1[l], cmp_k_b1[l], cmp_k_w2[l])
    cwv = _cmp_weights(cmp_v_pos[l], cmp_v_w1[l], cmp_v_b1[l], cmp_v_w2[l])
    sp = _s5_params(s5_log_dt[l], s5_a_re[l], s5_a_im[l], s5_b_re[l], s5_b_im[l], s5_c_re[l], s5_c_im[l],
                    s5_d[l], s5_glu_w[l], s5_glu_b[l], norm_ssm_out[l])
    gn_attn = row(norm_attn_out[l])
    g_mix = row(norm_mix[l])

    xp = x_prompt.reshape(B * T, D_MODEL)
    x1 = _ffn(xp, *f1, gfin, False)
    tabs = _rope_tables(jnp.arange(T, dtype=jnp.int32))
    (q, qr, kc, vc, _, _, _, _, gt, u,
     kc_t, vc_t, ks_t, vs_t, kw_t, vw_t) = _proj(x1, g_mix, w_p, *tabs, seq_t=T)
    b3 = lambda a: a.reshape(B, T, a.shape[-1])
    kcmp, vcmp = _cmp_prompt(b3(kc), b3(vc), cwk, cwv)
    attn = _attn_prompt(b3(q), b3(qr), kcmp, vcmp, ks_t, vs_t, kw_t, vw_t, b3(gt), gn_attn)
    zeros = jnp.zeros((B, SSM_GROUPS * SSM_STATE), F32)
    y_tm, p_re, p_im = _s5(b3(u).transpose(1, 0, 2), zeros, zeros, sp, 64)
    ssm = y_tm.transpose(1, 0, 2).reshape(B * T, D_SSM)
    x2 = _outproj(x1, attn.reshape(B * T, D_ATTN), ssm, wo_a, wo_s, gn_attn, False)
    y_prompt = _ffn(x2, *f2, gfin, True).reshape(B, T, D_MODEL)
    from_t = lambda a: a.reshape(a.shape[0], N_KV_HEADS, HEAD_DIM, a.shape[2]).transpose(0, 3, 1, 2)[None]
    wb = min(WINDOW, T)
    p_state = (from_t(kc_t), from_t(vc_t), from_t(ks_t), from_t(vs_t),
               from_t(kw_t[:, :, T - wb:]), from_t(vw_t[:, :, T - wb:]),
               p_re.reshape(1, B, SSM_GROUPS, SSM_STATE), p_im.reshape(1, B, SSM_GROUPS, SSM_STATE))

    xs = x_sample.reshape(BS * TS, D_MODEL)
    s1 = _ffn(xs, *f1, gfin, False)
    pos_s = PAST_LEN + jnp.arange(TS, dtype=jnp.int32)
    tabs_s = [jnp.tile(a, (BS, 1)) for a in _rope_tables(pos_s)]
    q, qr, kc, vc, ks, vs, kw, vw, gt, u = _proj(s1, g_mix, w_p, *tabs_s)
    s3 = lambda a: a.reshape(BS, TS, a.shape[-1])
    to_t = lambda a: a.transpose(0, 2, 3, 1).reshape(a.shape[0], KV_W, a.shape[1])
    pt_flat = page_table.reshape(-1)
    kcmp_s = _cmp_paged(pt_flat, to_t(cache_cmp_k[l]),
                        *_cmp_weights_split(cmp_k_pos[l], cmp_k_w1[l], cmp_k_b1[l], cmp_k_w2[l]), BS)
    vcmp_s = _cmp_paged(pt_flat, to_t(cache_cmp_v[l]),
                        *_cmp_weights_split(cmp_v_pos[l], cmp_v_w1[l], cmp_v_b1[l], cmp_v_w2[l]), BS)
    o_cmp_s, idx = _cmpattn(_q_on_kv_lanes(q, TS), kcmp_s, vcmp_s, TS)
    pad8 = lambda a: jnp.pad(s3(a), ((0, 0), (0, 8 - TS), (0, 0)))
    gates_s = _pad_heads(gt[:, :3 * N_HEADS], TS)
    gates_s = jnp.pad(gates_s, ((0, 0), (0, 0), (0, 0), (0, LANES - 3)))
    wk_t, wv_t = to_t(state_win_k[l]), to_t(state_win_v[l])
    comb = _tail(pt_flat, idx[:, :, :N_SELECT].reshape(-1), to_t(cache_slc_k[l]), to_t(cache_slc_v[l]),
                 _q_on_kv_lanes(qr, TS), pad8(ks), pad8(vs), pad8(kw), pad8(vw), wk_t, wv_t, o_cmp_s, gates_s, TS)
    comb = comb.reshape(BS, N_KV_HEADS, GQA, TS, N_KV_HEADS, HEAD_DIM)
    comb = jnp.stack([comb[:, k, :, :, k] for k in range(N_KV_HEADS)], axis=1)
    attn_s = comb.transpose(0, 3, 1, 2, 4).reshape(BS * TS, D_ATTN)
    y_tm, s_re, s_im = _s5(s3(u).transpose(1, 0, 2), state_s5_re[l].reshape(BS, -1), state_s5_im[l].reshape(BS, -1), sp, TS)
    ssm_s = y_tm.transpose(1, 0, 2).reshape(BS * TS, D_SSM)
    s2 = _outproj(s1, attn_s, ssm_s, wo_a, wo_s, gn_attn, True)
    y_sample = _ffn(s2, *f2, gfin, True).reshape(BS, TS, D_MODEL)
    kv5 = lambda a, n: a.reshape(1, n, -1, N_KV_HEADS, HEAD_DIM)
    win_k = from_t(jnp.concatenate([wk_t[:, :, TS:], s3(kw).transpose(0, 2, 1)], axis=2))
    win_v = from_t(jnp.concatenate([wv_t[:, :, TS:], s3(vw).transpose(0, 2, 1)], axis=2))
    s_state = (kv5(kc, BS), kv5(vc, BS), kv5(ks, BS), kv5(vs, BS), win_k, win_v,
               s_re.reshape(1, BS, SSM_GROUPS, SSM_STATE), s_im.reshape(1, BS, SSM_GROUPS, SSM_STATE))

    return (y_prompt, y_sample) + p_state + s_state
```

```python
import functools
import math

import numpy as np
import jax
import jax.numpy as jnp
from jax import lax
from jax.experimental import pallas as pl
from jax.experimental.pallas import tpu as pltpu

F32 = jnp.float32
BF16 = jnp.bfloat16

D_MODEL = 1024
PAST_LEN = 16384
PAGE_SIZE = 128
D_ATTN = 512
D_SSM = 512
HEAD_DIM = 64
N_HEADS = 8
N_KV_HEADS = 2
GQA = 4
KV_W = 128
ROT_DIM = 16
ROPE_THETA = 500000.0
ATTN_SCALE = HEAD_DIM ** -0.5
CMP_LEN = 32
CMP_STRIDE = 16
CMP_RATIO = 2
CMP_HIDDEN = 256
SEL_BLOCK = 64
N_SELECT = 16
WINDOW = 512
SSM_CH = 16
SSM_GROUPS = 32
SSM_STATE = 64
D_FF = 2816
RMS_EPS = 1e-6
NEG_INF = -1e30
FORCE_SCORE = 1e4

LANES = 128
VMEM_LIMIT = 56 * 1024 * 1024

Z_Q = 0
Z_KV = D_ATTN
Z_U = D_ATTN + 6 * KV_W
Z_G = Z_U + D_SSM
Z_W = Z_G + LANES


def _cparams(sem):
    return pltpu.CompilerParams(dimension_semantics=sem, vmem_limit_bytes=VMEM_LIMIT)


def _rms(x, g):
    return x * lax.rsqrt(jnp.mean(x * x, axis=-1, keepdims=True) + RMS_EPS) * g


def _dot(a, b):
    return jnp.dot(a, b, preferred_element_type=F32)


def _dot_nt(a, b):
    return lax.dot_general(a, b, (((1,), (1,)), ((), ())), preferred_element_type=F32)


def _ffn_body(x_ref, g_ref, wg_ref, wu_ref, wd_ref, gf_ref, o_ref, h_ref, acc_ref, *, n_f, final_norm):
    j = pl.program_id(1)

    @pl.when(j == 0)
    def _():
        h_ref[...] = _rms(x_ref[...], g_ref[...]).astype(BF16)
        acc_ref[...] = jnp.zeros_like(acc_ref)

    h = h_ref[...]
    a = _dot(h, wg_ref[...])
    b = _dot(h, wu_ref[...])
    t = (a * jax.nn.sigmoid(a) * b).astype(BF16)
    acc_ref[...] += _dot(t, wd_ref[...])

    @pl.when(j == n_f - 1)
    def _():
        y = x_ref[...] + 0.5 * acc_ref[...]
        if final_norm:
            y = _rms(y, gf_ref[...])
        o_ref[...] = y


def _ffn(x, g, wg, wu, wd, gf, final_norm):
    m = x.shape[0]
    tm = min(m, 1024)
    tf = D_FF // 2
    n_f = D_FF // tf
    return pl.pallas_call(
        functools.partial(_ffn_body, n_f=n_f, final_norm=final_norm),
        grid=(m // tm, n_f),
        in_specs=[
            pl.BlockSpec((tm, D_MODEL), lambda i, j: (i, 0)),
            pl.BlockSpec((1, D_MODEL), lambda i, j: (0, 0)),
            pl.BlockSpec((D_MODEL, tf), lambda i, j: (0, j)),
            pl.BlockSpec((D_MODEL, tf), lambda i, j: (0, j)),
            pl.BlockSpec((tf, D_MODEL), lambda i, j: (j, 0)),
            pl.BlockSpec((1, D_MODEL), lambda i, j: (0, 0)),
        ],
        out_specs=pl.BlockSpec((tm, D_MODEL), lambda i, j: (i, 0)),
        out_shape=jax.ShapeDtypeStruct((m, D_MODEL), F32),
        scratch_shapes=[pltpu.VMEM((tm, D_MODEL), BF16), pltpu.VMEM((tm, D_MODEL), F32)],
        compiler_params=_cparams(("parallel", "arbitrary")),
        name="ffn",
    )(x, g, wg, wu, wd, gf)


def _proj_body(x_ref, g_ref, w_ref, c_ref, sa_ref, sb_ref,
               q_ref, qr_ref, kc_ref, vc_ref, ks_ref, vs_ref, kw_ref, vw_ref, gt_ref, u_ref, *t_refs):
    h = _rms(x_ref[...], g_ref[...]).astype(BF16)
    z = _dot(h, w_ref[...])
    c, sa, sb = c_ref[...], sa_ref[...], sb_ref[...]

    def rope(v):
        return v * c + pltpu.roll(v, LANES - ROT_DIM // 2, 1) * sa + pltpu.roll(v, ROT_DIM // 2, 1) * sb

    q_ref[...] = z[:, Z_Q:Z_Q + D_ATTN]
    for i in range(D_ATTN // LANES):
        qr_ref[:, i * LANES:(i + 1) * LANES] = rope(z[:, Z_Q + i * LANES:Z_Q + (i + 1) * LANES])
    kv = [z[:, Z_KV + i * KV_W:Z_KV + (i + 1) * KV_W] for i in range(6)]
    kv[2] = rope(kv[2])
    kv[4] = rope(kv[4])
    for ref, v in zip((kc_ref, vc_ref, ks_ref, vs_ref, kw_ref, vw_ref), kv):
        ref[...] = v
    for ref, v in zip(t_refs, kv):
        ref[0] = v.T
    u_ref[...] = z[:, Z_U:Z_U + D_SSM]
    gt_ref[...] = jax.nn.sigmoid(z[:, Z_G:Z_G + LANES])


def _proj(x, g, w, rope_c, rope_sa, rope_sb, seq_t=None):
    m = x.shape[0]
    tm = min(m, 512)
    n_rep = rope_c.shape[0] // tm
    row = lambda i: (i, 0)
    const = lambda i: (0, 0)
    tab = lambda i: (i % n_rep, 0)
    widths = [D_ATTN, D_ATTN] + [KV_W] * 6 + [LANES, D_SSM]
    out_specs = [pl.BlockSpec((tm, wd), row) for wd in widths]
    out_shape = [jax.ShapeDtypeStruct((m, wd), F32) for wd in widths]
    if seq_t is not None:
        assert rope_c.shape[0] == seq_t
        out_specs += [pl.BlockSpec((1, KV_W, tm), lambda i: (i // n_rep, 0, i % n_rep))] * 6
        out_shape += [jax.ShapeDtypeStruct((m // seq_t, KV_W, seq_t), F32)] * 6
    return pl.pallas_call(
        _proj_body,
        grid=(m // tm,),
        in_specs=[pl.BlockSpec((tm, D_MODEL), row), pl.BlockSpec((1, D_MODEL), const),
                  pl.BlockSpec((D_MODEL, Z_W), const),
                  pl.BlockSpec((tm, LANES), tab), pl.BlockSpec((tm, LANES), tab), pl.BlockSpec((tm, LANES), tab)],
        out_specs=out_specs,
        out_shape=out_shape,
        compiler_params=_cparams(("parallel",)),
        name="proj",
    )(x, g, w, rope_c, rope_sa, rope_sb)


def _rope_tables(pos):
    half = ROT_DIM // 2
    inv_freq = jnp.power(ROPE_THETA, -(jnp.arange(half, dtype=F32) * 2.0 / ROT_DIM))
    ang = pos.astype(F32)[:, None] * inv_freq[None, :]
    cos, sin = jnp.cos(ang), jnp.sin(ang)
    n = pos.shape[0]
    one = jnp.ones((n, HEAD_DIM - ROT_DIM), F32)
    zero = jnp.zeros((n, HEAD_DIM - ROT_DIM), F32)
    zh = jnp.zeros((n, half), F32)
    c = jnp.concatenate([cos, cos, one], axis=1)
    sa = jnp.concatenate([-sin, zh, zero], axis=1)
    sb = jnp.concatenate([zh, sin, zero], axis=1)
    t2 = lambda a: jnp.concatenate([a, a], axis=1)
    return t2(c), t2(sa), t2(sb)


def _compress_rows(x_ref, n_half, w1_ref, b_ref, w2_ref):
    xs = [x_ref[0, pl.ds(s, n_half, stride=CMP_STRIDE), :] for s in range(CMP_STRIDE)]
    xcat = jnp.concatenate(xs, axis=1).astype(BF16)
    p = _dot(xcat, w1_ref[...])
    nh = N_KV_HEADS * CMP_HIDDEN
    p1_next = pltpu.roll(p[:, nh:], n_half - 1, 0)
    pre = p[:, :nh] + p1_next + b_ref[...]
    return _dot(jax.nn.gelu(pre).astype(BF16), w2_ref[...])


def _cmp_prompt_body(kc_ref, vc_ref, wk1_ref, bk_ref, wk2_ref, wv1_ref, bv_ref, wv2_ref, ko_ref, vo_ref, *, n_half):
    ko_ref[0] = _compress_rows(kc_ref, n_half, wk1_ref, bk_ref, wk2_ref)
    vo_ref[0] = _compress_rows(vc_ref, n_half, wv1_ref, bv_ref, wv2_ref)


def _cmp_prompt(kc, vc, wk, wv):
    b, t, _ = kc.shape
    n_half = t // CMP_STRIDE
    seq = pl.BlockSpec((1, t, KV_W), lambda i: (i, 0, 0))
    const = lambda shp: pl.BlockSpec(shp, lambda i: (0,) * len(shp))
    wspecs = [const(wk[0].shape), const(wk[1].shape), const(wk[2].shape)]
    out = pl.BlockSpec((1, n_half, KV_W), lambda i: (i, 0, 0))
    return pl.pallas_call(
        functools.partial(_cmp_prompt_body, n_half=n_half),
        grid=(b,),
        in_specs=[seq, seq] + wspecs + wspecs,
        out_specs=[out, out],
        out_shape=[jax.ShapeDtypeStruct((b, n_half, KV_W), F32)] * 2,
        compiler_params=_cparams(("parallel",)),
        name="cmp_prompt",
    )(kc, vc, *wk, *wv)


def _cmp_weights(pos_emb, w1, b1, w2):
    w1r = w1.reshape(CMP_RATIO, CMP_STRIDE, HEAD_DIM, CMP_HIDDEN)
    eye = jnp.eye(N_KV_HEADS, dtype=F32)
    w1p = jnp.einsum('rsdh,kj->skdrjh', w1r, eye).reshape(CMP_STRIDE * KV_W, CMP_RATIO * N_KV_HEADS * CMP_HIDDEN)
    pe = pos_emb.reshape(CMP_RATIO, CMP_STRIDE, HEAD_DIM)
    bias = b1 + jnp.einsum('rsd,rsdh->h', pe, w1r, precision=lax.Precision.HIGHEST)
    bias = jnp.tile(bias, N_KV_HEADS)[None, :]
    w2p = jnp.einsum('hd,kj->khjd', w2, eye).reshape(N_KV_HEADS * CMP_HIDDEN, KV_W)
    return w1p.astype(BF16), bias, w2p.astype(BF16)


TQ = 128
KCHUNK = 512
N_SEL_PROMPT = 32


def _softmax_rows(s):
    mx = jnp.max(s, axis=-1, keepdims=True)
    e = jnp.exp(s - mx)
    return e / jnp.sum(e, axis=-1, keepdims=True)


def _attn_prompt_body(q_ref, qr_ref, kcmp_ref, vcmp_ref, ks_ref, vs_ref, kw_ref, vw_ref, gt_ref,
                      amat_ref, emat_ref, gn_ref, o_ref, *, seq):
    t0 = pl.program_id(1) * TQ
    lane = lax.broadcasted_iota(jnp.int32, (TQ, LANES), 1)
    low = lane < HEAD_DIM
    tq = t0 + lax.broadcasted_iota(jnp.int32, (TQ, 1), 0)
    gates = gt_ref[0]

    def head_rows(src_ref, k):
        rows = []
        for g in range(GQA):
            h = GQA * k + g
            v = src_ref[0, :, (h // 2) * LANES:(h // 2 + 1) * LANES]
            if h % 2 != k:
                v = pltpu.roll(v, HEAD_DIM, 1)
            rows.append(jnp.where(low if k == 0 else jnp.logical_not(low), v * ATTN_SCALE, 0.0))
        return jnp.concatenate(rows, axis=0).astype(BF16)

    jj = lax.broadcasted_iota(jnp.int32, (N_SEL_PROMPT, TQ), 0)
    jq = (t0 + lax.broadcasted_iota(jnp.int32, (N_SEL_PROMPT, TQ), 1)) // SEL_BLOCK
    forced = (jj == 0) | (jj == jq) | (jj == jq - 1)

    n_cmp = kcmp_ref.shape[1]
    blk_end = lax.broadcasted_iota(jnp.int32, (1, n_cmp), 1) * CMP_STRIDE + (CMP_LEN - 1)
    m_cmp = (blk_end <= tq)[None]

    n_chunks = t0 // KCHUNK + 1
    w_keys = WINDOW + TQ
    w_start = pl.multiple_of(jnp.clip(t0 - WINDOW, 0, seq - w_keys), TQ)
    kp_w = w_start + lax.broadcasted_iota(jnp.int32, (1, w_keys), 1)
    m_win = ((kp_w <= tq) & (kp_w > tq - WINDOW))[None]

    comb, q_augs, q_rots, o_cs = [], [], [], []
    for k in range(N_KV_HEADS):
        qc = head_rows(q_ref, k)
        qr = head_rows(qr_ref, k)

        s = _dot_nt(qc, kcmp_ref[0].astype(BF16)).reshape(GQA, TQ, n_cmp)
        p = jnp.where(m_cmp, _softmax_rows(jnp.where(m_cmp, s, NEG_INF)), 0.0)
        o_c = _dot(p.reshape(GQA * TQ, n_cmp).astype(BF16), vcmp_ref[0].astype(BF16))
        imp = jnp.sum(p, axis=0)

        imp_sel = jnp.dot(amat_ref[...], imp.T, precision=lax.Precision.HIGHEST, preferred_element_type=F32)
        score = jnp.where(jj > jq, -1.0, jnp.where(forced, FORCE_SCORE, imp_sel))
        rank = jnp.zeros((N_SEL_PROMPT, TQ), F32)
        for i in range(N_SEL_PROMPT):
            si = score[i:i + 1, :]
            beats = (si > score) | ((si == score) & (jj > i))
            rank = rank + jnp.where(beats, 1.0, 0.0)
        sel_t = jnp.where((rank < N_SELECT) & (jj <= jq), 1.0, 0.0)
        sel_t = jnp.concatenate([sel_t, jnp.zeros((LANES - N_SEL_PROMPT, TQ), F32)], axis=0)
        sel = sel_t.T

        bias = ((sel - 1.0) * (-NEG_INF)).astype(BF16)
        q_augs.append(jnp.concatenate([qr, jnp.concatenate([bias] * GQA, axis=0)], axis=1))
        q_rots.append(qr)
        o_cs.append(o_c.reshape(GQA, TQ, LANES))

    sum_lanes = [HEAD_DIM * (1 - k) for k in range(N_KV_HEADS)]
    row_id = lax.broadcasted_iota(jnp.int32, (LANES, 1), 0)

    def chunk(c, carry, causal):
        k0 = pl.multiple_of(c * KCHUNK, KCHUNK)
        kk = jnp.concatenate([ks_ref[0, :, pl.ds(k0, KCHUNK)].astype(BF16),
                              emat_ref[:, pl.ds(k0, KCHUNK)]], axis=0)
        vv = vs_ref[0, :, pl.ds(k0, KCHUNK)].astype(BF16)
        out = []
        for k in range(N_KV_HEADS):
            m_i, acc = carry[k]
            sc = _dot(q_augs[k], kk).reshape(GQA, TQ, KCHUNK)
            if causal:
                kpos = k0 + lax.broadcasted_iota(jnp.int32, (1, KCHUNK), 1)
                sc = jnp.where((kpos <= tq)[None], sc, NEG_INF)
            m_n = jnp.maximum(m_i, jnp.max(sc, axis=-1, keepdims=True))
            alpha = jnp.exp(m_i - m_n)
            pe = jnp.exp((sc - m_n).astype(BF16))
            vk = jnp.where(row_id == sum_lanes[k], 1.0, vv).astype(BF16)
            pv = _dot_nt(pe.reshape(GQA * TQ, KCHUNK), vk).reshape(GQA, TQ, LANES)
            out.append((m_n, alpha * acc + pv))
        return tuple(out)

    init = tuple((jnp.full((GQA, TQ, 1), NEG_INF, F32), jnp.zeros((GQA, TQ, LANES), F32)) for _ in range(N_KV_HEADS))
    carry = lax.fori_loop(0, n_chunks - 1, functools.partial(chunk, causal=False), init)
    acc_ss = [a for _, a in chunk(n_chunks - 1, carry, True)]

    kk = kw_ref[0, :, pl.ds(w_start, w_keys)].astype(BF16)
    vv = vw_ref[0, :, pl.ds(w_start, w_keys)].astype(BF16)
    for k in range(N_KV_HEADS):
        sw = jnp.where(m_win, _dot(q_rots[k], kk).reshape(GQA, TQ, w_keys), NEG_INF)
        ew = jnp.exp((sw - jnp.max(sw, axis=-1, keepdims=True)).astype(BF16))
        vk = jnp.where(row_id == sum_lanes[k], 1.0, vv).astype(BF16)
        acc_w = _dot_nt(ew.reshape(GQA * TQ, w_keys), vk).reshape(GQA, TQ, LANES)
        sl = slice(sum_lanes[k], sum_lanes[k] + 1)
        for g in range(GQA):
            h = GQA * k + g
            g_s = gates[:, 3 * h + 1:3 * h + 2] / acc_ss[k][g][:, sl]
            g_w = gates[:, 3 * h + 2:3 * h + 3] / acc_w[g][:, sl]
            comb.append(gates[:, 3 * h:3 * h + 1] * o_cs[k][g] + g_s * acc_ss[k][g] + g_w * acc_w[g])

    cols = []
    for pr in range(N_HEADS // 2):
        k = (2 * pr) // GQA
        a, b = comb[2 * pr], comb[2 * pr + 1]
        a = pltpu.roll(a, HEAD_DIM, 1) if k == 1 else a
        b = pltpu.roll(b, HEAD_DIM, 1) if k == 0 else b
        cols.append(jnp.where(low, a, b))
    o = jnp.concatenate(cols, axis=1)
    o_ref[0] = _rms(o, gn_ref[...]).astype(o_ref.dtype)


def _sel_matrices(n_cmp_pad, n_sel, n_keys):
    j = np.arange(n_sel)[:, None]
    n = np.arange(n_cmp_pad)[None, :]
    per = SEL_BLOCK // CMP_STRIDE
    amat = ((n >= per * j - (CMP_RATIO - 1)) & (n <= per * j + per - 1)).astype(np.float32)
    jrow = np.arange(LANES)[:, None]
    key = np.arange(n_keys)[None, :]
    emat = (key // SEL_BLOCK == jrow).astype(np.float32)
    return jnp.asarray(amat), jnp.asarray(emat, dtype=BF16)


def _attn_prompt(q, qr, kcmp, vcmp, ks, vs, kw, vw, gt, gn):
    b, t, _ = q.shape
    amat, emat = _sel_matrices(kcmp.shape[1], N_SEL_PROMPT, t)
    qt = lambda w: pl.BlockSpec((1, TQ, w), lambda i, j: (i, j, 0))
    full = lambda a: pl.BlockSpec((1,) + a.shape[1:], lambda i, j: (i, 0, 0))
    const = lambda a: pl.BlockSpec(a.shape, lambda i, j: (0, 0))
    return pl.pallas_call(
        functools.partial(_attn_prompt_body, seq=t),
        grid=(b, t // TQ),
        in_specs=[qt(D_ATTN), qt(D_ATTN), full(kcmp), full(vcmp), full(ks), full(vs), full(kw), full(vw),
                  qt(LANES), const(amat), const(emat), const(gn)],
        out_specs=qt(D_ATTN),
        out_shape=jax.ShapeDtypeStruct((b, t, D_ATTN), BF16),
        compiler_params=_cparams(("parallel", "arbitrary")),
        name="attn_prompt",
    )(q, qr, kcmp, vcmp, ks, vs, kw, vw, gt, amat, emat, gn)


S5_LB = D_SSM // LANES
S5_SW = LANES // SSM_CH * SSM_STATE


def _s5_body(u_ref, h0r_ref, h0i_ref, lr_ref, li_ref, bre_ref, bim_ref, c_ref, d_ref, gw_ref, gb_ref, gn_ref,
             y_ref, hr_ref, hi_ref, xr_s, xi_s, y_s, *, tc, bb):
    step = pl.program_id(0)
    m = tc * bb

    @pl.when(step == 0)
    def _():
        hr_ref[...] = h0r_ref[...]
        hi_ref[...] = h0i_ref[...]

    u = u_ref[...].reshape(m, D_SSM)
    ub = u.astype(BF16)
    for j in range(S5_LB):
        uj = ub[:, j * LANES:(j + 1) * LANES]
        sl = slice(j * S5_SW, (j + 1) * S5_SW)
        xr_s[...] = _dot(uj, bre_ref[j]).reshape(tc, bb, S5_SW)
        xi_s[...] = _dot(uj, bim_ref[j]).reshape(tc, bb, S5_SW)
        lr = jnp.broadcast_to(lr_ref[:, sl], (bb, S5_SW))
        li = jnp.broadcast_to(li_ref[:, sl], (bb, S5_SW))

        def scan(t, carry):
            hr, hi = carry
            nr = lr * hr - li * hi + xr_s[t]
            ni = lr * hi + li * hr + xi_s[t]
            xr_s[t] = nr
            xi_s[t] = ni
            return nr, ni

        hr, hi = lax.fori_loop(0, tc, scan, (hr_ref[:, sl], hi_ref[:, sl]))
        hr_ref[:, sl] = hr
        hi_ref[:, sl] = hi
        hcat = jnp.concatenate([xr_s[...].reshape(m, S5_SW), xi_s[...].reshape(m, S5_SW)], axis=1).astype(BF16)
        y_s[:, j * LANES:(j + 1) * LANES] = _dot(hcat, c_ref[j])
    z = jax.nn.gelu(y_s[...] + d_ref[...] * u)
    out = z * jax.nn.sigmoid(_dot(z.astype(BF16), gw_ref[...]) + gb_ref[...])
    y_ref[...] = _rms(out, gn_ref[...]).astype(y_ref.dtype).reshape(tc, bb, D_SSM)


def _s5(u_tm, h0r, h0i, sp, tc):
    t, bb, _ = u_tm.shape
    const = lambda a: pl.BlockSpec(a.shape, lambda i: (0,) * a.ndim)
    args = (h0r, h0i, sp['lr'], sp['li'], sp['bre'], sp['bim'], sp['c'], sp['d'], sp['glu_w'], sp['glu_b'], sp['gn'])
    st = jax.ShapeDtypeStruct(h0r.shape, F32)
    return pl.pallas_call(
        functools.partial(_s5_body, tc=tc, bb=bb),
        grid=(t // tc,),
        in_specs=[pl.BlockSpec((tc, bb, D_SSM), lambda i: (i, 0, 0))] + [const(a) for a in args],
        out_specs=[pl.BlockSpec((tc, bb, D_SSM), lambda i: (i, 0, 0)), const(h0r), const(h0i)],
        out_shape=[jax.ShapeDtypeStruct((t, bb, D_SSM), F32), st, st],
        scratch_shapes=[pltpu.VMEM((tc, bb, S5_SW), F32), pltpu.VMEM((tc, bb, S5_SW), F32),
                        pltpu.VMEM((tc * bb, D_SSM), F32)],
        compiler_params=_cparams(("arbitrary",)),
        name="s5",
    )(u_tm, *args)


def _s5_params(log_dt, a_re, a_im, b_re, b_im, c_re, c_im, d, glu_w, glu_b, gn):
    dt = jnp.exp(log_dt)[:, None]
    mag = jnp.exp(a_re * dt)
    lr, li = mag * jnp.cos(a_im * dt), mag * jnp.sin(a_im * dt)
    den = a_re * a_re + a_im * a_im
    inv_r, inv_i = a_re / den, -a_im / den
    nr, ni = lr - 1.0, li
    fr, fi = nr * inv_r - ni * inv_i, nr * inv_i + ni * inv_r
    bbr = fr[..., None] * b_re - fi[..., None] * b_im
    bbi = fr[..., None] * b_im + fi[..., None] * b_re
    gl = LANES // SSM_CH
    eye = jnp.eye(gl, dtype=F32)

    def bmat(bb):
        x = bb.reshape(S5_LB, gl, SSM_STATE, SSM_CH)
        return jnp.einsum('jgpc,gh->jgchp', x, eye).reshape(S5_LB, LANES, S5_SW).astype(BF16)

    def cmat(cc):
        x = cc.reshape(S5_LB, gl, SSM_CH, SSM_STATE)
        return jnp.einsum('jgcp,gh->jgphc', x, eye).reshape(S5_LB, S5_SW, LANES)

    cm = jnp.concatenate([cmat(c_re), -cmat(c_im)], axis=1).astype(BF16)
    return dict(lr=lr.reshape(1, -1), li=li.reshape(1, -1), bre=bmat(bbr), bim=bmat(bbi), c=cm,
                d=d.reshape(1, -1), glu_w=glu_w.astype(BF16), glu_b=glu_b[None, :], gn=gn[None, :])


def _outproj_body(x_ref, a_ref, s_ref, wa_ref, ws_ref, gn_ref, o_ref, *, norm_attn):
    a = a_ref[...]
    if norm_attn:
        a = _rms(a, gn_ref[...])
    o_ref[...] = x_ref[...] + _dot(a.astype(BF16), wa_ref[...]) + _dot(s_ref[...].astype(BF16), ws_ref[...])


def _outproj(x, a, s, wa, ws, gn, norm_attn):
    m = x.shape[0]
    tm = min(m, 1024)
    row = lambda w: pl.BlockSpec((tm, w), lambda i: (i, 0))
    const = lambda a_: pl.BlockSpec(a_.shape, lambda i: (0, 0))
    return pl.pallas_call(
        functools.partial(_outproj_body, norm_attn=norm_attn),
        grid=(m // tm,),
        in_specs=[row(D_MODEL), row(D_ATTN), row(D_SSM), const(wa), const(ws), const(gn)],
        out_specs=row(D_MODEL),
        out_shape=jax.ShapeDtypeStruct((m, D_MODEL), F32),
        compiler_params=_cparams(("parallel",)),
        name="outproj",
    )(x, a, s, wa, ws, gn)


N_PAGES = PAST_LEN // PAGE_SIZE
CMP_CHUNK = 32
CMP_PITCH = 24


def _page_copy(pool_ref, buf_ref, sem_ref, pt_ref, b, i, slot):
    return pltpu.make_async_copy(pool_ref.at[pt_ref[b * N_PAGES + i]], buf_ref.at[slot, i], sem_ref.at[slot])


def _cmp_paged_body(pt_ref, pool_ref, w1_ref, b_ref, w2_ref, o_ref, buf_ref, rows_ref, p_ref, sem_ref):
    b = pl.program_id(0)
    slot = b % 2

    def start_all(bb, sl):
        def body(i, c):
            _page_copy(pool_ref, buf_ref, sem_ref, pt_ref, bb, i, sl).start()
            return c
        lax.fori_loop(0, N_PAGES, body, 0, unroll=8)

    @pl.when(b == 0)
    def _():
        start_all(0, 0)

    @pl.when(b + 1 < pl.num_programs(0))
    def _():
        start_all(b + 1, 1 - slot)

    def wait_one(i, c):
        _page_copy(pool_ref, buf_ref, sem_ref, pt_ref, b, i, slot).wait()
        return c
    lax.fori_loop(0, N_PAGES, wait_one, 0, unroll=8)

    hp = PAGE_SIZE // CMP_STRIDE
    hc = CMP_CHUNK * hp
    lane = lax.broadcasted_iota(jnp.int32, (hc, LANES), 1)
    low = lane < HEAD_DIM

    def stage_a(c):
        for i in range(CMP_CHUNK):
            t = buf_ref[slot, c * CMP_CHUNK + i].T
            for h in range(hp):
                rows_ref[c % 2, (i * hp + h) * CMP_PITCH:(i * hp + h) * CMP_PITCH + CMP_STRIDE, :] = (
                    t[h * CMP_STRIDE:(h + 1) * CMP_STRIDE])

    def stage_b(c):
        xs = [rows_ref[c % 2, pl.ds(s, hc, stride=CMP_PITCH), :] for s in range(CMP_STRIDE)]
        heads = [[], []]
        for j in range(CMP_STRIDE // 2):
            a, b = xs[2 * j], xs[2 * j + 1]
            heads[0].append(jnp.where(low, a, pltpu.roll(b, HEAD_DIM, 1)))
            heads[1].append(jnp.where(low, pltpu.roll(a, HEAD_DIM, 1), b))
        x = jnp.concatenate([jnp.concatenate(hd, axis=1) for hd in heads], axis=0).astype(BF16)
        p = _dot(x, w1_ref[...])
        for k in range(N_KV_HEADS):
            p_ref[k, c * hc:(c + 1) * hc, :] = p[k * hc:(k + 1) * hc]

    n_chunks = N_PAGES // CMP_CHUNK
    stage_a(0)
    for c in range(n_chunks):
        if c + 1 < n_chunks:
            stage_a(c + 1)
        stage_b(c)

    n_half = N_PAGES * hp
    out = None
    for k in range(N_KV_HEADS):
        pre = p_ref[k, :, :CMP_HIDDEN] + pltpu.roll(p_ref[k, :, CMP_HIDDEN:], n_half - 1, 0) + b_ref[...]
        ok = _dot(jax.nn.gelu(pre).astype(BF16), w2_ref[k])
        out = ok if out is None else out + ok
    o_ref[0] = out


def _cmp_paged(page_table_flat, pool_t, w1, bias, w2, n_seq):
    n_half = PAST_LEN // CMP_STRIDE
    hc = CMP_CHUNK * PAGE_SIZE // CMP_STRIDE
    const = lambda a: pl.BlockSpec(a.shape, lambda i, pt: (0,) * a.ndim)
    return pl.pallas_call(
        _cmp_paged_body,
        grid_spec=pltpu.PrefetchScalarGridSpec(
            num_scalar_prefetch=1,
            grid=(n_seq,),
            in_specs=[pl.BlockSpec(memory_space=pl.ANY), const(w1), const(bias), const(w2)],
            out_specs=pl.BlockSpec((1, n_half, KV_W), lambda i, pt: (i, 0, 0)),
            scratch_shapes=[pltpu.VMEM((2, N_PAGES, PAGE_SIZE, KV_W), F32),
                            pltpu.VMEM((2, hc * CMP_PITCH, KV_W), F32),
                            pltpu.VMEM((N_KV_HEADS, n_half, CMP_RATIO * CMP_HIDDEN), F32),
                            pltpu.SemaphoreType.DMA((2,))]),
        out_shape=jax.ShapeDtypeStruct((n_seq, n_half, KV_W), F32),
        compiler_params=_cparams(("arbitrary",)),
        name="cmp_paged",
    )(page_table_flat, pool_t, w1, bias, w2)


def _cmp_weights_split(pos_emb, w1, b1, w2):
    w1r = w1.reshape(CMP_RATIO, CMP_STRIDE * HEAD_DIM, CMP_HIDDEN)
    w1s = w1r.transpose(1, 0, 2).reshape(CMP_STRIDE * HEAD_DIM, CMP_RATIO * CMP_HIDDEN)
    pe = pos_emb.reshape(CMP_RATIO, CMP_STRIDE * HEAD_DIM)
    bias = b1 + jnp.einsum('rx,rxh->h', pe, w1r, precision=lax.Precision.HIGHEST)
    z = jnp.zeros_like(w2)
    w2k = jnp.stack([jnp.concatenate([w2, z], axis=1), jnp.concatenate([z, w2], axis=1)])
    return w1s.astype(BF16), bias[None, :], w2k.astype(BF16)


CMPATTN_SEQS = 8


def _cmpattn_body(q_ref, kc_ref, vc_ref, amat_ref, o_ref, idx_ref, *, ts, n_cmp, n_sel):
    rows = GQA * ts
    qpos = PAST_LEN + lax.broadcasted_iota(jnp.int32, (rows, 1), 0) % ts
    n_pad = kc_ref.shape[1]
    n_io = lax.broadcasted_iota(jnp.int32, (1, n_pad), 1)
    m = (n_io * CMP_STRIDE + (CMP_LEN - 1) <= qpos) & (n_io < n_cmp)
    n_sb = q_ref.shape[0]
    imps = []
    for b in range(n_sb):
        kc = kc_ref[b].astype(BF16)
        vc = vc_ref[b].astype(BF16)
        for k in range(N_KV_HEADS):
            s = _dot_nt((q_ref[b, k] * ATTN_SCALE).astype(BF16), kc)
            p = jnp.where(m, _softmax_rows(jnp.where(m, s, NEG_INF)), 0.0)
            o_ref[b, k] = _dot(p.astype(BF16), vc)
            imp = p[0:ts]
            for g in range(1, GQA):
                imp = imp + p[g * ts:(g + 1) * ts]
            imps.append(imp)
    imp_all = jnp.concatenate(imps, axis=0)
    n_rows = imp_all.shape[0]
    sel_w = amat_ref.shape[1]
    imp_sel = jnp.dot(imp_all, amat_ref[...], precision=lax.Precision.HIGHEST, preferred_element_type=F32)
    j = lax.broadcasted_iota(jnp.int32, (n_rows, sel_w), 1)
    jq = (PAST_LEN + lax.broadcasted_iota(jnp.int32, (n_rows, sel_w), 0) % ts) // SEL_BLOCK
    forced = (j == 0) | (j == jq) | (j == jq - 1)
    jf = j.astype(F32)
    lane_k = lax.broadcasted_iota(jnp.int32, (n_rows, LANES), 1)
    score = jnp.where(j > jq, -1.0, jnp.where(forced, FORCE_SCORE, imp_sel))
    score = jnp.where(j < n_sel, score, -jnp.inf)
    res = jnp.zeros((n_rows, LANES), F32)
    for i in range(N_SELECT):
        mx = jnp.max(score, axis=-1, keepdims=True)
        ix = jnp.min(jnp.where(score == mx, jf, 1e9), axis=-1, keepdims=True)
        res = jnp.where(lane_k == i, ix, res)
        score = jnp.where(jf == ix, -jnp.inf, score)
    idx_ref[...] = res.astype(jnp.int32).reshape(idx_ref.shape)


def _cmpattn(q_pad, kcmp, vcmp, ts):
    n_seq = q_pad.shape[0]
    n_cmp = (PAST_LEN + ts) // CMP_STRIDE - CMP_RATIO + 1
    n_sel = -(-(PAST_LEN + ts) // SEL_BLOCK)
    sel_w = -(-n_sel // LANES) * LANES
    per = SEL_BLOCK // CMP_STRIDE
    n = np.arange(kcmp.shape[1])[:, None]
    j = np.arange(sel_w)[None, :]
    amat = jnp.asarray(((n >= per * j - (CMP_RATIO - 1)) & (n <= per * j + per - 1) & (j < n_sel)).astype(np.float32))
    rows = GQA * ts
    sb = CMPATTN_SEQS
    return pl.pallas_call(
        functools.partial(_cmpattn_body, ts=ts, n_cmp=n_cmp, n_sel=n_sel),
        grid=(n_seq // sb,),
        in_specs=[pl.BlockSpec((sb, N_KV_HEADS, rows, LANES), lambda i: (i, 0, 0, 0)),
                  pl.BlockSpec((sb,) + kcmp.shape[1:], lambda i: (i, 0, 0)),
                  pl.BlockSpec((sb,) + vcmp.shape[1:], lambda i: (i, 0, 0)),
                  pl.BlockSpec(amat.shape, lambda i: (0, 0))],
        out_specs=[pl.BlockSpec((sb, N_KV_HEADS, rows, LANES), lambda i: (i, 0, 0, 0)),
                   pl.BlockSpec((sb, N_KV_HEADS * ts, LANES), lambda i: (i, 0, 0))],
        out_shape=[jax.ShapeDtypeStruct((n_seq, N_KV_HEADS, rows, LANES), F32),
                   jax.ShapeDtypeStruct((n_seq, N_KV_HEADS * ts, LANES), jnp.int32)],
        compiler_params=_cparams(("parallel",)),
        name="cmpattn_sample",
    )(q_pad, kcmp, vcmp, amat)


def _slc_copies(pools, bufs, sems, pt_ref, idx_ref, step, r, slot, ts, for_wait):
    if for_wait:
        pg = 0
    else:
        blk = idx_ref[step * (ts * N_SELECT) + r]
        page = jnp.minimum(blk // (PAGE_SIZE // SEL_BLOCK), N_PAGES - 1)
        pg = pt_ref[(step // N_KV_HEADS) * N_PAGES + page]
    return [pltpu.make_async_copy(pool.at[pg], buf.at[slot, r], sem.at[slot])
            for pool, buf, sem in zip(pools, bufs, sems)]


def _tail_body(pt_ref, idx_ref, kpool_ref, vpool_ref, q_ref, ksn_ref, vsn_ref, kwn_ref, vwn_ref, wk_ref, wv_ref,
               oc_ref, gt_ref, o_ref, kbuf_ref, vbuf_ref, ksem_ref, vsem_ref, *, ts):
    step = pl.program_id(0) * N_KV_HEADS + pl.program_id(1)
    n_steps = pl.num_programs(0) * N_KV_HEADS
    slot = step % 2
    n_fetch = ts * N_SELECT

    pools, bufs, sems = (kpool_ref, vpool_ref), (kbuf_ref, vbuf_ref), (ksem_ref, vsem_ref)

    def start_all(st, sl):
        def body(r, c):
            for cp in _slc_copies(pools, bufs, sems, pt_ref, idx_ref, st, r, sl, ts, False):
                cp.start()
            return c
        lax.fori_loop(0, n_fetch, body, 0, unroll=8)

    @pl.when(step == 0)
    def _():
        start_all(0, 0)

    @pl.when(step + 1 < n_steps)
    def _():
        start_all(step + 1, 1 - slot)

    def wait_one(r, c):
        for cp in _slc_copies(pools, bufs, sems, pt_ref, idx_ref, step, r, slot, ts, True):
            cp.wait()
        return c
    lax.fori_loop(0, n_fetch, wait_one, 0, unroll=8)

    rows = GQA * ts
    q = (q_ref[0, 0] * ATTN_SCALE).astype(BF16)
    row_t = lax.broadcasted_iota(jnp.int32, (rows, 1), 0) % ts
    qpos = PAST_LEN + row_t
    lane = lax.broadcasted_iota(jnp.int32, (1, LANES), 1)
    lane_half, lane_off = lane // SEL_BLOCK, lane % SEL_BLOCK
    pad = jnp.zeros((LANES - ksn_ref.shape[1], LANES), F32)
    new_pos = PAST_LEN + lane
    new_ok = lane < ts
    blk_new = PAST_LEN // SEL_BLOCK

    def new_rows(ref):
        return jnp.concatenate([ref[0], pad], axis=0).astype(BF16)

    ksn, vsn = new_rows(ksn_ref), new_rows(vsn_ref)
    s_new = _dot_nt(q, ksn)
    o_s = jnp.zeros((rows, LANES), F32)
    grp = 4

    def pages(buf_ref, r0):
        return jnp.concatenate([buf_ref[slot, r0 + u].astype(BF16) for u in range(grp)], axis=1)

    for t in range(ts):
        pieces = []
        has_new = jnp.int32(0)
        for i0 in range(0, N_SELECT, grp):
            sc = _dot(q, pages(kbuf_ref, t * N_SELECT + i0))
            for u in range(grp):
                blk = idx_ref[step * n_fetch + t * N_SELECT + i0 + u]
                has_new = has_new | (blk == blk_new).astype(jnp.int32)
                want_half = jnp.where(blk < blk_new, blk % (PAGE_SIZE // SEL_BLOCK), -1)
                ok = (lane_half == want_half) & (blk * SEL_BLOCK + lane_off <= qpos)
                pieces.append(jnp.where(ok, sc[:, u * LANES:(u + 1) * LANES], NEG_INF))
        ok_new = (lane < jnp.where(has_new > 0, ts, 0)) & (new_pos <= qpos)
        pieces.append(jnp.where(ok_new, s_new, NEG_INF))
        p = _softmax_rows(jnp.concatenate(pieces, axis=1)).astype(BF16)
        o_t = _dot(p[:, N_SELECT * LANES:], vsn)
        for i0 in range(0, N_SELECT, grp):
            o_t = o_t + _dot_nt(p[:, i0 * LANES:(i0 + grp) * LANES], pages(vbuf_ref, t * N_SELECT + i0))
        o_s = jnp.where(row_t == t, o_t, o_s)

    wb = wk_ref.shape[2]
    kp = PAST_LEN - wb + lax.broadcasted_iota(jnp.int32, (1, wb), 1)
    ok_w = (kp >= 0) & (kp <= qpos) & (kp > qpos - WINDOW)
    ok_wn = new_ok & (new_pos <= qpos) & (new_pos > qpos - WINDOW)
    s_w = jnp.where(ok_w, _dot(q, wk_ref[0].astype(BF16)), NEG_INF)
    s_wn = jnp.where(ok_wn, _dot_nt(q, new_rows(kwn_ref)), NEG_INF)
    p = _softmax_rows(jnp.concatenate([s_w, s_wn], axis=1)).astype(BF16)
    o_w = _dot_nt(p[:, :wb], wv_ref[0].astype(BF16)) + _dot(p[:, wb:], new_rows(vwn_ref))

    gt = gt_ref[0, 0]
    o_ref[0, 0] = gt[:, 0:1] * oc_ref[0, 0] + gt[:, 1:2] * o_s + gt[:, 2:3] * o_w


def _tail(page_table_flat, idx_flat, kpool_t, vpool_t, q_pad, ksn, vsn, kwn, vwn, wk_t, wv_t, o_cmp, gates_pad, ts):
    n_seq = q_pad.shape[0]
    rows = GQA * ts
    n_fetch = ts * N_SELECT
    per_head = pl.BlockSpec((1, 1, rows, LANES), lambda i, k, pt, ix: (i, k, 0, 0))
    per_seq = lambda a: pl.BlockSpec((1,) + a.shape[1:], lambda i, k, pt, ix: (i, 0, 0))
    hbm = pl.BlockSpec(memory_space=pl.ANY)
    return pl.pallas_call(
        functools.partial(_tail_body, ts=ts),
        grid_spec=pltpu.PrefetchScalarGridSpec(
            num_scalar_prefetch=2,
            grid=(n_seq, N_KV_HEADS),
            in_specs=[hbm, hbm, per_head, per_seq(ksn), per_seq(vsn), per_seq(kwn), per_seq(vwn),
                      per_seq(wk_t), per_seq(wv_t), per_head, per_head],
            out_specs=per_head,
            scratch_shapes=[pltpu.VMEM((2, n_fetch, PAGE_SIZE, KV_W), F32), pltpu.VMEM((2, n_fetch, PAGE_SIZE, KV_W), F32),
                            pltpu.SemaphoreType.DMA((2,)), pltpu.SemaphoreType.DMA((2,))]),
        out_shape=jax.ShapeDtypeStruct((n_seq, N_KV_HEADS, rows, LANES), F32),
        compiler_params=_cparams(("arbitrary", "arbitrary")),
        name="slc_win_sample",
    )(page_table_flat, idx_flat, kpool_t, vpool_t, q_pad, ksn, vsn, kwn, vwn, wk_t, wv_t, o_cmp, gates_pad)


def _pad_heads(x, ts):
    n = x.shape[0] // ts
    w = x.shape[1] // N_HEADS
    return x.reshape(n, ts, N_KV_HEADS, GQA, w).transpose(0, 2, 3, 1, 4).reshape(n, N_KV_HEADS, GQA * ts, w)


def _q_on_kv_lanes(q, ts):
    x = _pad_heads(q, ts)
    z = jnp.zeros_like(x[:, 0])
    return jnp.stack([jnp.concatenate([x[:, 0], z], axis=-1), jnp.concatenate([z, x[:, 1]], axis=-1)], axis=1)


def _sample_attention_jnp(q, q_rot, kc, vc, ks, vs, kw, vw, gates, ck, cv, sk, sv, wk, wv, page_table, cw, gn):
    B, T = q.shape[:2]
    pos = PAST_LEN + jnp.arange(T, dtype=jnp.int32)
    q = q.reshape(B, T, N_HEADS, HEAD_DIM)
    q_rot = q_rot.reshape(B, T, N_HEADS, HEAD_DIM)
    kvr = lambda a: a.reshape(B, T, N_KV_HEADS, HEAD_DIM)
    kc, vc, ks, vs, kw, vw = map(kvr, (kc, vc, ks, vs, kw, vw))
    gates = gates[..., :3 * N_HEADS].reshape(B, T, N_HEADS, 3)
    past = lambda pool: pool[page_table].reshape(B, -1, N_KV_HEADS, HEAD_DIM)
    kc_all, vc_all = jnp.concatenate([past(ck), kc], axis=1), jnp.concatenate([past(cv), vc], axis=1)
    ks_all, vs_all = jnp.concatenate([past(sk), ks], axis=1), jnp.concatenate([past(sv), vs], axis=1)

    def compress_blocks(k, pos_emb, w1, b1, w2):
        Bk, Tk = k.shape[:2]
        nh = Tk // CMP_STRIDE
        nc = nh - CMP_RATIO + 1
        halves = k[:, :nh * CMP_STRIDE].reshape(Bk, nh, CMP_STRIDE, N_KV_HEADS, HEAD_DIM)
        pe = pos_emb.reshape(CMP_RATIO, CMP_STRIDE, HEAD_DIM)
        w1r = w1.reshape(CMP_RATIO, CMP_STRIDE, HEAD_DIM, CMP_HIDDEN)
        pre = b1
        for r in range(CMP_RATIO):
            pre = pre + jnp.einsum('bnskd,sdh->bnkh', halves[:, r:r + nc] + pe[r][:, None, :], w1r[r])
        return jax.nn.gelu(pre) @ w2

    k_cmp = compress_blocks(kc_all, *cw[0])
    v_cmp = compress_blocks(vc_all, *cw[1])
    nc = k_cmp.shape[1]
    qg = q.reshape(B, T, N_KV_HEADS, GQA, HEAD_DIM)
    s = jnp.einsum('bqkgd,bnkd->bqkgn', qg, k_cmp).astype(F32) * ATTN_SCALE
    blk_end = jnp.arange(nc) * CMP_STRIDE + CMP_LEN - 1
    m = (blk_end[None, :] <= pos[:, None])[None, :, None, None, :]
    p = jnp.where(m, jax.nn.softmax(jnp.where(m, s, NEG_INF), axis=-1), 0.0)
    o_cmp = jnp.einsum('bqkgn,bnkd->bqkgd', p, v_cmp).reshape(B, T, N_HEADS, HEAD_DIM)
    imp = p.sum(axis=3)
    n_sel = -(-(PAST_LEN + T) // SEL_BLOCK)
    per = SEL_BLOCK // CMP_STRIDE
    total = n_sel * per + CMP_RATIO + per
    pp = jnp.pad(imp, ((0, 0), (0, 0), (0, 0), (CMP_RATIO - 1, total - nc - (CMP_RATIO - 1))))
    impb = sum(pp[..., o:o + n_sel * per:per] for o in range(per + CMP_RATIO - 1))
    j = jnp.arange(n_sel)[None, None, None, :]
    jq = (pos // SEL_BLOCK)[None, :, None, None]
    forced = (j == 0) | (j == jq) | (j == jq - 1)
    score = jnp.where(j > jq, -1.0, jnp.where(forced, FORCE_SCORE, impb))
    _, idx = lax.top_k(score, min(N_SELECT, n_sel))
    valid = idx <= jq

    def to_blocks(k):
        kp = jnp.pad(k, ((0, 0), (0, n_sel * SEL_BLOCK - k.shape[1]), (0, 0), (0, 0)))
        return kp.reshape(B, n_sel, SEL_BLOCK, N_KV_HEADS, HEAD_DIM).transpose(0, 3, 1, 2, 4)

    kb, vb = to_blocks(ks_all), to_blocks(vs_all)
    qg = q_rot.reshape(B, T, N_KV_HEADS, GQA, HEAD_DIM)
    bi = jnp.arange(B)[:, None, None, None]
    hi = jnp.arange(N_KV_HEADS)[None, None, :, None]
    ksel, vsel = kb[bi, hi, idx], vb[bi, hi, idx]
    s = jnp.einsum('bqkgd,bqknsd->bqkgns', qg, ksel).astype(F32) * ATTN_SCALE
    kpos = idx[..., None] * SEL_BLOCK + jnp.arange(SEL_BLOCK)
    mask = valid[..., None] & (kpos <= pos[None, :, None, None, None])
    s = jnp.where(mask[:, :, :, None], s, NEG_INF)
    shp = s.shape
    p = jax.nn.softmax(s.reshape(shp[:-2] + (-1,)), axis=-1).reshape(shp)
    o_slc = jnp.einsum('bqkgns,bqknsd->bqkgd', p, vsel).reshape(B, T, N_HEADS, HEAD_DIM)
    wb = wk.shape[1]
    kw_all, vw_all = jnp.concatenate([wk, kw], axis=1), jnp.concatenate([wv, vw], axis=1)
    k_pos = PAST_LEN - wb + jnp.arange(wb + T, dtype=jnp.int32)
    s = jnp.einsum('bqkgd,bjkd->bqkgj', qg, kw_all).astype(F32) * ATTN_SCALE
    kp, qp = k_pos[None, :], pos[:, None]
    m = ((kp >= 0) & (kp <= qp) & (kp > qp - WINDOW))[None, :, None, None, :]
    p = jax.nn.softmax(jnp.where(m, s, NEG_INF), axis=-1)
    o_win = jnp.einsum('bqkgj,bjkd->bqkgd', p, vw_all).reshape(B, T, N_HEADS, HEAD_DIM)
    o = (gates[..., 0:1] * o_cmp + gates[..., 1:2] * o_slc + gates[..., 2:3] * o_win).reshape(B, T, D_ATTN)
    return _rms(o, gn).astype(BF16)


def kernel(x_prompt, x_sample, cache_cmp_k, cache_cmp_v, cache_slc_k, cache_slc_v, state_win_k, state_win_v, state_s5_re, state_s5_im, page_table, norm_ffn1, ffn1_gate, ffn1_up, ffn1_down, norm_mix, w_in, cmp_k_pos, cmp_k_w1, cmp_k_b1, cmp_k_w2, cmp_v_pos, cmp_v_w1, cmp_v_b1, cmp_v_w2, s5_log_dt, s5_a_re, s5_a_im, s5_b_re, s5_b_im, s5_c_re, s5_c_im, s5_d, s5_glu_w, s5_glu_b, norm_attn_out, norm_ssm_out, w_out, norm_ffn2, ffn2_gate, ffn2_up, ffn2_down, norm_final):
    depth = w_in.shape[0]
    assert depth == 1
    l = 0
    B, T, _ = x_prompt.shape
    BS, TS, _ = x_sample.shape
    row = lambda a: a[None, :]

    w = w_in[l]
    cut = D_ATTN + 6 * KV_W
    w_p = jnp.concatenate([w[:, :cut], w[:, cut + 3 * N_HEADS:], w[:, cut:cut + 3 * N_HEADS],
                           jnp.zeros((D_MODEL, LANES - 3 * N_HEADS), F32)], axis=1).astype(BF16)
    f1 = (row(norm_ffn1[l]), ffn1_gate[l].astype(BF16), ffn1_up[l].astype(BF16), ffn1_down[l].astype(BF16))
    f2 = (row(norm_ffn2[l]), ffn2_gate[l].astype(BF16), ffn2_up[l].astype(BF16), ffn2_down[l].astype(BF16))
    gfin = row(norm_final)
    wo = w_out[l].astype(BF16)
    wo_a, wo_s = wo[:D_ATTN], wo[D_ATTN:]
    cwk = _cmp_weights(cmp_k_pos[l], cmp_k_w1[l], cmp_k_b1[l], cmp_k_w2[l])
    cwv = _cmp_weights(cmp_v_pos[l], cmp_v_w1[l], cmp_v_b1[l], cmp_v_w2[l])
    sp = _s5_params(s5_log_dt[l], s5_a_re[l], s5_a_im[l], s5_b_re[l], s5_b_im[l], s5_c_re[l], s5_c_im[l],
                    s5_d[l], s5_glu_w[l], s5_glu_b[l], norm_ssm_out[l])
    gn_attn = row(norm_attn_out[l])
    g_mix = row(norm_mix[l])

    xp = x_prompt.reshape(B * T, D_MODEL)
    x1 = _ffn(xp, *f1, gfin, False)
    tabs = _rope_tables(jnp.arange(T, dtype=jnp.int32))
    (q, qr, kc, vc, _, _, _, _, gt, u,
     kc_t, vc_t, ks_t, vs_t, kw_t, vw_t) = _proj(x1, g_mix, w_p, *tabs, seq_t=T)
    b3 = lambda a: a.reshape(B, T, a.shape[-1])
    kcmp, vcmp = _cmp_prompt(b3(kc), b3(vc), cwk, cwv)
    attn = _attn_prompt(b3(q), b3(qr), kcmp, vcmp, ks_t, vs_t, kw_t, vw_t, b3(gt), gn_attn)
    zeros = jnp.zeros((B, SSM_GROUPS * SSM_STATE), F32)
    y_tm, p_re, p_im = _s5(b3(u).transpose(1, 0, 2), zeros, zeros, sp, 64)
    ssm = y_tm.transpose(1, 0, 2).reshape(B * T, D_SSM)
    x2 = _outproj(x1, attn.reshape(B * T, D_ATTN), ssm, wo_a, wo_s, gn_attn, False)
    y_prompt = _ffn(x2, *f2, gfin, True).reshape(B, T, D_MODEL)
    from_t = lambda a: a.reshape(a.shape[0], N_KV_HEADS, HEAD_DIM, a.shape[2]).transpose(0, 3, 1, 2)[None]
    wb = min(WINDOW, T)
    p_state = (from_t(kc_t), from_t(vc_t), from_t(ks_t), from_t(vs_t),
               from_t(kw_t[:, :, T - wb:]), from_t(vw_t[:, :, T - wb:]),
               p_re.reshape(1, B, SSM_GROUPS, SSM_STATE), p_im.reshape(1, B, SSM_GROUPS, SSM_STATE))

    xs = x_sample.reshape(BS * TS, D_MODEL)
    s1 = _ffn(xs, *f1, gfin, False)
    pos_s = PAST_LEN + jnp.arange(TS, dtype=jnp.int32)
    tabs_s = [jnp.tile(a, (BS, 1)) for a in _rope_tables(pos_s)]
    q, qr, kc, vc, ks, vs, kw, vw, gt, u = _proj(s1, g_mix, w_p, *tabs_s)
    s3 = lambda a: a.reshape(BS, TS, a.shape[-1])
    to_t = lambda a: a.transpose(0, 2, 3, 1).reshape(a.shape[0], KV_W, a.shape[1])
    pt_flat = page_table.reshape(-1)
    kcmp_s = _cmp_paged(pt_flat, to_t(cache_cmp_k[l]),
                        *_cmp_weights_split(cmp_k_pos[l], cmp_k_w1[l], cmp_k_b1[l], cmp_k_w2[l]), BS)
    vcmp_s = _cmp_paged(pt_flat, to_t(cache_cmp_v[l]),
                        *_cmp_weights_split(cmp_v_pos[l], cmp_v_w1[l], cmp_v_b1[l], cmp_v_w2[l]), BS)
    o_cmp_s, idx = _cmpattn(_q_on_kv_lanes(q, TS), kcmp_s, vcmp_s, TS)
    pad8 = lambda a: jnp.pad(s3(a), ((0, 0), (0, 8 - TS), (0, 0)))
    gates_s = _pad_heads(gt[:, :3 * N_HEADS], TS)
    gates_s = jnp.pad(gates_s, ((0, 0), (0, 0), (0, 0), (0, LANES - 3)))
    wk_t, wv_t = to_t(state_win_k[l]), to_t(state_win_v[l])
    comb = _tail(pt_flat, idx[:, :, :N_SELECT].reshape(-1), to_t(cache_slc_k[l]), to_t(cache_slc_v[l]),
                 _q_on_kv_lanes(qr, TS), pad8(ks), pad8(vs), pad8(kw), pad8(vw), wk_t, wv_t, o_cmp_s, gates_s, TS)
    comb = comb.reshape(BS, N_KV_HEADS, GQA, TS, N_KV_HEADS, HEAD_DIM)
    comb = jnp.stack([comb[:, k, :, :, k] for k in range(N_KV_HEADS)], axis=1)
    attn_s = comb.transpose(0, 3, 1, 2, 4).reshape(BS * TS, D_ATTN)
    y_tm, s_re, s_im = _s5(s3(u).transpose(1, 0, 2), state_s5_re[l].reshape(BS, -1), state_s5_im[l].reshape(BS, -1), sp, TS)
    ssm_s = y_tm.transpose(1, 0, 2).reshape(BS * TS, D_SSM)
    s2 = _outproj(s1, attn_s, ssm_s, wo_a, wo_s, gn_attn, True)
    y_sample = _ffn(s2, *f2, gfin, True).reshape(BS, TS, D_MODEL)
    kv5 = lambda a, n: a.reshape(1, n, -1, N_KV_HEADS, HEAD_DIM)
    win_k = from_t(jnp.concatenate([wk_t[:, :, TS:], s3(kw).transpose(0, 2, 1)], axis=2))
    win_v = from_t(jnp.concatenate([wv_t[:, :, TS:], s3(vw).transpose(0, 2, 1)], axis=2))
    s_state = (kv5(kc, BS), kv5(vc, BS), kv5(ks, BS), kv5(vs, BS), win_k, win_v,
               s_re.reshape(1, BS, SSM_GROUPS, SSM_STATE), s_im.reshape(1, BS, SSM_GROUPS, SSM_STATE))

    return (y_prompt, y_sample) + p_state + s_state
```

```python
import functools
import math

import numpy as np
import jax
import jax.numpy as jnp
from jax import lax
from jax.experimental import pallas as pl
from jax.experimental.pallas import tpu as pltpu

F32 = jnp.float32
BF16 = jnp.bfloat16

D_MODEL = 1024
PAST_LEN = 16384
PAGE_SIZE = 128
D_ATTN = 512
D_SSM = 512
HEAD_DIM = 64
N_HEADS = 8
N_KV_HEADS = 2
GQA = 4
KV_W = 128
ROT_DIM = 16
ROPE_THETA = 500000.0
ATTN_SCALE = HEAD_DIM ** -0.5
CMP_LEN = 32
CMP_STRIDE = 16
CMP_RATIO = 2
CMP_HIDDEN = 256
SEL_BLOCK = 64
N_SELECT = 16
WINDOW = 512
SSM_CH = 16
SSM_GROUPS = 32
SSM_STATE = 64
D_FF = 2816
RMS_EPS = 1e-6
NEG_INF = -1e30
FORCE_SCORE = 1e4

LANES = 128
VMEM_LIMIT = 56 * 1024 * 1024

Z_Q = 0
Z_KV = D_ATTN
Z_U = D_ATTN + 6 * KV_W
Z_G = Z_U + D_SSM
Z_W = Z_G + LANES


def _cparams(sem):
    return pltpu.CompilerParams(dimension_semantics=sem, vmem_limit_bytes=VMEM_LIMIT)


def _rms(x, g):
    return x * lax.rsqrt(jnp.mean(x * x, axis=-1, keepdims=True) + RMS_EPS) * g


def _dot(a, b):
    return jnp.dot(a, b, preferred_element_type=F32)


def _dot_nt(a, b):
    return lax.dot_general(a, b, (((1,), (1,)), ((), ())), preferred_element_type=F32)


def _ffn_body(x_ref, g_ref, wg_ref, wu_ref, wd_ref, gf_ref, o_ref, h_ref, acc_ref, *, n_f, final_norm):
    j = pl.program_id(1)

    @pl.when(j == 0)
    def _():
        h_ref[...] = _rms(x_ref[...], g_ref[...]).astype(BF16)
        acc_ref[...] = jnp.zeros_like(acc_ref)

    h = h_ref[...]
    a = _dot(h, wg_ref[...])
    b = _dot(h, wu_ref[...])
    t = (a * jax.nn.sigmoid(a) * b).astype(BF16)
    acc_ref[...] += _dot(t, wd_ref[...])

    @pl.when(j == n_f - 1)
    def _():
        y = x_ref[...] + 0.5 * acc_ref[...]
        if final_norm:
            y = _rms(y, gf_ref[...])
        o_ref[...] = y


def _ffn(x, g, wg, wu, wd, gf, final_norm):
    m = x.shape[0]
    tm = min(m, 1024)
    tf = D_FF // 2
    n_f = D_FF // tf
    return pl.pallas_call(
        functools.partial(_ffn_body, n_f=n_f, final_norm=final_norm),
        grid=(m // tm, n_f),
        in_specs=[
            pl.BlockSpec((tm, D_MODEL), lambda i, j: (i, 0)),
            pl.BlockSpec((1, D_MODEL), lambda i, j: (0, 0)),
            pl.BlockSpec((D_MODEL, tf), lambda i, j: (0, j)),
            pl.BlockSpec((D_MODEL, tf), lambda i, j: (0, j)),
            pl.BlockSpec((tf, D_MODEL), lambda i, j: (j, 0)),
            pl.BlockSpec((1, D_MODEL), lambda i, j: (0, 0)),
        ],
        out_specs=pl.BlockSpec((tm, D_MODEL), lambda i, j: (i, 0)),
        out_shape=jax.ShapeDtypeStruct((m, D_MODEL), F32),
        scratch_shapes=[pltpu.VMEM((tm, D_MODEL), BF16), pltpu.VMEM((tm, D_MODEL), F32)],
        compiler_params=_cparams(("parallel", "arbitrary")),
        name="ffn",
    )(x, g, wg, wu, wd, gf)


def _proj_body(x_ref, g_ref, w_ref, c_ref, sa_ref, sb_ref,
               q_ref, qr_ref, kc_ref, vc_ref, ks_ref, vs_ref, kw_ref, vw_ref, gt_ref, u_ref, *t_refs):
    h = _rms(x_ref[...], g_ref[...]).astype(BF16)
    z = _dot(h, w_ref[...])
    c, sa, sb = c_ref[...], sa_ref[...], sb_ref[...]

    def rope(v):
        return v * c + pltpu.roll(v, LANES - ROT_DIM // 2, 1) * sa + pltpu.roll(v, ROT_DIM // 2, 1) * sb

    q_ref[...] = z[:, Z_Q:Z_Q + D_ATTN]
    for i in range(D_ATTN // LANES):
        qr_ref[:, i * LANES:(i + 1) * LANES] = rope(z[:, Z_Q + i * LANES:Z_Q + (i + 1) * LANES])
    kv = [z[:, Z_KV + i * KV_W:Z_KV + (i + 1) * KV_W] for i in range(6)]
    kv[2] = rope(kv[2])
    kv[4] = rope(kv[4])
    for ref, v in zip((kc_ref, vc_ref, ks_ref, vs_ref, kw_ref, vw_ref), kv):
        ref[...] = v
    for ref, v in zip(t_refs, kv):
        ref[0] = v.T
    u_ref[...] = z[:, Z_U:Z_U + D_SSM]
    gt_ref[...] = jax.nn.sigmoid(z[:, Z_G:Z_G + LANES])


def _proj(x, g, w, rope_c, rope_sa, rope_sb, seq_t=None):
    m = x.shape[0]
    tm = min(m, 512)
    n_rep = rope_c.shape[0] // tm
    row = lambda i: (i, 0)
    const = lambda i: (0, 0)
    tab = lambda i: (i % n_rep, 0)
    widths = [D_ATTN, D_ATTN] + [KV_W] * 6 + [LANES, D_SSM]
    out_specs = [pl.BlockSpec((tm, wd), row) for wd in widths]
    out_shape = [jax.ShapeDtypeStruct((m, wd), F32) for wd in widths]
    if seq_t is not None:
        assert rope_c.shape[0] == seq_t
        out_specs += [pl.BlockSpec((1, KV_W, tm), lambda i: (i // n_rep, 0, i % n_rep))] * 6
        out_shape += [jax.ShapeDtypeStruct((m // seq_t, KV_W, seq_t), F32)] * 6
    return pl.pallas_call(
        _proj_body,
        grid=(m // tm,),
        in_specs=[pl.BlockSpec((tm, D_MODEL), row), pl.BlockSpec((1, D_MODEL), const),
                  pl.BlockSpec((D_MODEL, Z_W), const),
                  pl.BlockSpec((tm, LANES), tab), pl.BlockSpec((tm, LANES), tab), pl.BlockSpec((tm, LANES), tab)],
        out_specs=out_specs,
        out_shape=out_shape,
        compiler_params=_cparams(("parallel",)),
        name="proj",
    )(x, g, w, rope_c, rope_sa, rope_sb)


def _rope_tables(pos):
    half = ROT_DIM // 2
    inv_freq = jnp.power(ROPE_THETA, -(jnp.arange(half, dtype=F32) * 2.0 / ROT_DIM))
    ang = pos.astype(F32)[:, None] * inv_freq[None, :]
    cos, sin = jnp.cos(ang), jnp.sin(ang)
    n = pos.shape[0]
    one = jnp.ones((n, HEAD_DIM - ROT_DIM), F32)
    zero = jnp.zeros((n, HEAD_DIM - ROT_DIM), F32)
    zh = jnp.zeros((n, half), F32)
    c = jnp.concatenate([cos, cos, one], axis=1)
    sa = jnp.concatenate([-sin, zh, zero], axis=1)
    sb = jnp.concatenate([zh, sin, zero], axis=1)
    t2 = lambda a: jnp.concatenate([a, a], axis=1)
    return t2(c), t2(sa), t2(sb)


def _compress_rows(x_ref, n_half, w1_ref, b_ref, w2_ref):
    xs = [x_ref[0, pl.ds(s, n_half, stride=CMP_STRIDE), :] for s in range(CMP_STRIDE)]
    xcat = jnp.concatenate(xs, axis=1).astype(BF16)
    p = _dot(xcat, w1_ref[...])
    nh = N_KV_HEADS * CMP_HIDDEN
    p1_next = pltpu.roll(p[:, nh:], n_half - 1, 0)
    pre = p[:, :nh] + p1_next + b_ref[...]
    return _dot(jax.nn.gelu(pre).astype(BF16), w2_ref[...])


def _cmp_prompt_body(kc_ref, vc_ref, wk1_ref, bk_ref, wk2_ref, wv1_ref, bv_ref, wv2_ref, ko_ref, vo_ref, *, n_half):
    ko_ref[0] = _compress_rows(kc_ref, n_half, wk1_ref, bk_ref, wk2_ref)
    vo_ref[0] = _compress_rows(vc_ref, n_half, wv1_ref, bv_ref, wv2_ref)


def _cmp_prompt(kc, vc, wk, wv):
    b, t, _ = kc.shape
    n_half = t // CMP_STRIDE
    seq = pl.BlockSpec((1, t, KV_W), lambda i: (i, 0, 0))
    const = lambda shp: pl.BlockSpec(shp, lambda i: (0,) * len(shp))
    wspecs = [const(wk[0].shape), const(wk[1].shape), const(wk[2].shape)]
    out = pl.BlockSpec((1, n_half, KV_W), lambda i: (i, 0, 0))
    return pl.pallas_call(
        functools.partial(_cmp_prompt_body, n_half=n_half),
        grid=(b,),
        in_specs=[seq, seq] + wspecs + wspecs,
        out_specs=[out, out],
        out_shape=[jax.ShapeDtypeStruct((b, n_half, KV_W), F32)] * 2,
        compiler_params=_cparams(("parallel",)),
        name="cmp_prompt",
    )(kc, vc, *wk, *wv)


def _cmp_weights(pos_emb, w1, b1, w2):
    w1r = w1.reshape(CMP_RATIO, CMP_STRIDE, HEAD_DIM, CMP_HIDDEN)
    eye = jnp.eye(N_KV_HEADS, dtype=F32)
    w1p = jnp.einsum('rsdh,kj->skdrjh', w1r, eye).reshape(CMP_STRIDE * KV_W, CMP_RATIO * N_KV_HEADS * CMP_HIDDEN)
    pe = pos_emb.reshape(CMP_RATIO, CMP_STRIDE, HEAD_DIM)
    bias = b1 + jnp.einsum('rsd,rsdh->h', pe, w1r, precision=lax.Precision.HIGHEST)
    bias = jnp.tile(bias, N_KV_HEADS)[None, :]
    w2p = jnp.einsum('hd,kj->khjd', w2, eye).reshape(N_KV_HEADS * CMP_HIDDEN, KV_W)
    return w1p.astype(BF16), bias, w2p.astype(BF16)


TQ = 256
KCHUNK = 512
N_SEL_PROMPT = 32


def _softmax_rows(s):
    mx = jnp.max(s, axis=-1, keepdims=True)
    e = jnp.exp(s - mx)
    return e / jnp.sum(e, axis=-1, keepdims=True)


def _attn_prompt_body(q_ref, qr_ref, kcmp_ref, vcmp_ref, ks_ref, vs_ref, kw_ref, vw_ref, gt_ref,
                      amat_ref, emat_ref, gn_ref, o_ref, *, seq):
    t0 = pl.program_id(1) * TQ
    lane = lax.broadcasted_iota(jnp.int32, (TQ, LANES), 1)
    low = lane < HEAD_DIM
    tq = t0 + lax.broadcasted_iota(jnp.int32, (TQ, 1), 0)
    gates = gt_ref[0]

    def head_rows(src_ref, k):
        rows = []
        for g in range(GQA):
            h = GQA * k + g
            v = src_ref[0, :, (h // 2) * LANES:(h // 2 + 1) * LANES]
            if h % 2 != k:
                v = pltpu.roll(v, HEAD_DIM, 1)
            rows.append(jnp.where(low if k == 0 else jnp.logical_not(low), v * ATTN_SCALE, 0.0))
        return jnp.concatenate(rows, axis=0).astype(BF16)

    jj = lax.broadcasted_iota(jnp.int32, (N_SEL_PROMPT, TQ), 0)
    jq = (t0 + lax.broadcasted_iota(jnp.int32, (N_SEL_PROMPT, TQ), 1)) // SEL_BLOCK
    forced = (jj == 0) | (jj == jq) | (jj == jq - 1)

    n_cmp = kcmp_ref.shape[1]
    blk_end = lax.broadcasted_iota(jnp.int32, (1, n_cmp), 1) * CMP_STRIDE + (CMP_LEN - 1)
    m_cmp = (blk_end <= tq)[None]

    n_chunks = t0 // KCHUNK + 1
    w_keys = WINDOW + TQ
    w_start = pl.multiple_of(jnp.clip(t0 - WINDOW, 0, seq - w_keys), TQ)
    kp_w = w_start + lax.broadcasted_iota(jnp.int32, (1, w_keys), 1)
    m_win = ((kp_w <= tq) & (kp_w > tq - WINDOW))[None]

    comb, q_augs, q_rots, o_cs = [], [], [], []
    for k in range(N_KV_HEADS):
        qc = head_rows(q_ref, k)
        qr = head_rows(qr_ref, k)

        s = _dot_nt(qc, kcmp_ref[0].astype(BF16)).reshape(GQA, TQ, n_cmp)
        p = jnp.where(m_cmp, _softmax_rows(jnp.where(m_cmp, s, NEG_INF)), 0.0)
        o_c = _dot(p.reshape(GQA * TQ, n_cmp).astype(BF16), vcmp_ref[0].astype(BF16))
        imp = jnp.sum(p, axis=0)

        imp_sel = jnp.dot(amat_ref[...], imp.T, precision=lax.Precision.HIGHEST, preferred_element_type=F32)
        score = jnp.where(jj > jq, -1.0, jnp.where(forced, FORCE_SCORE, imp_sel))
        rank = jnp.zeros((N_SEL_PROMPT, TQ), F32)
        for i in range(N_SEL_PROMPT):
            si = score[i:i + 1, :]
            beats = (si > score) | ((si == score) & (jj > i))
            rank = rank + jnp.where(beats, 1.0, 0.0)
        sel_t = jnp.where((rank < N_SELECT) & (jj <= jq), 1.0, 0.0)
        sel_t = jnp.concatenate([sel_t, jnp.zeros((LANES - N_SEL_PROMPT, TQ), F32)], axis=0)
        sel = sel_t.T

        bias = ((sel - 1.0) * (-NEG_INF)).astype(BF16)
        q_augs.append(jnp.concatenate([qr, jnp.concatenate([bias] * GQA, axis=0)], axis=1))
        q_rots.append(qr)
        o_cs.append(o_c.reshape(GQA, TQ, LANES))

    sum_lanes = [HEAD_DIM * (1 - k) for k in range(N_KV_HEADS)]
    row_id = lax.broadcasted_iota(jnp.int32, (LANES, 1), 0)

    def chunk(c, carry, causal):
        k0 = pl.multiple_of(c * KCHUNK, KCHUNK)
        kk = jnp.concatenate([ks_ref[0, :, pl.ds(k0, KCHUNK)].astype(BF16),
                              emat_ref[:, pl.ds(k0, KCHUNK)]], axis=0)
        vv = vs_ref[0, :, pl.ds(k0, KCHUNK)].astype(BF16)
        out = []
        for k in range(N_KV_HEADS):
            m_i, acc = carry[k]
            sc = _dot(q_augs[k], kk).reshape(GQA, TQ, KCHUNK)
            if causal:
                kpos = k0 + lax.broadcasted_iota(jnp.int32, (1, KCHUNK), 1)
                sc = jnp.where((kpos <= tq)[None], sc, NEG_INF)
            m_n = jnp.maximum(m_i, jnp.max(sc, axis=-1, keepdims=True))
            alpha = jnp.exp(m_i - m_n)
            pe = jnp.exp((sc - m_n).astype(BF16))
            vk = jnp.where(row_id == sum_lanes[k], 1.0, vv).astype(BF16)
            pv = _dot_nt(pe.reshape(GQA * TQ, KCHUNK), vk).reshape(GQA, TQ, LANES)
            out.append((m_n, alpha * acc + pv))
        return tuple(out)

    init = tuple((jnp.full((GQA, TQ, 1), NEG_INF, F32), jnp.zeros((GQA, TQ, LANES), F32)) for _ in range(N_KV_HEADS))
    carry = lax.fori_loop(0, n_chunks - 1, functools.partial(chunk, causal=False), init)
    acc_ss = [a for _, a in chunk(n_chunks - 1, carry, True)]

    kk = kw_ref[0, :, pl.ds(w_start, w_keys)].astype(BF16)
    vv = vw_ref[0, :, pl.ds(w_start, w_keys)].astype(BF16)
    for k in range(N_KV_HEADS):
        sw = jnp.where(m_win, _dot(q_rots[k], kk).reshape(GQA, TQ, w_keys), NEG_INF)
        ew = jnp.exp((sw - jnp.max(sw, axis=-1, keepdims=True)).astype(BF16))
        vk = jnp.where(row_id == sum_lanes[k], 1.0, vv).astype(BF16)
        acc_w = _dot_nt(ew.reshape(GQA * TQ, w_keys), vk).reshape(GQA, TQ, LANES)
        sl = slice(sum_lanes[k], sum_lanes[k] + 1)
        for g in range(GQA):
            h = GQA * k + g
            g_s = gates[:, 3 * h + 1:3 * h + 2] / acc_ss[k][g][:, sl]
            g_w = gates[:, 3 * h + 2:3 * h + 3] / acc_w[g][:, sl]
            comb.append(gates[:, 3 * h:3 * h + 1] * o_cs[k][g] + g_s * acc_ss[k][g] + g_w * acc_w[g])

    cols = []
    for pr in range(N_HEADS // 2):
        k = (2 * pr) // GQA
        a, b = comb[2 * pr], comb[2 * pr + 1]
        a = pltpu.roll(a, HEAD_DIM, 1) if k == 1 else a
        b = pltpu.roll(b, HEAD_DIM, 1) if k == 0 else b
        cols.append(jnp.where(low, a, b))
    o = jnp.concatenate(cols, axis=1)
    o_ref[0] = _rms(o, gn_ref[...]).astype(o_ref.dtype)


def _sel_matrices(n_cmp_pad, n_sel, n_keys):
    j = np.arange(n_sel)[:, None]
    n = np.arange(n_cmp_pad)[None, :]
    per = SEL_BLOCK // CMP_STRIDE
    amat = ((n >= per * j - (CMP_RATIO - 1)) & (n <= per * j + per - 1)).astype(np.float32)
    jrow = np.arange(LANES)[:, None]
    key = np.arange(n_keys)[None, :]
    emat = (key // SEL_BLOCK == jrow).astype(np.float32)
    return jnp.asarray(amat), jnp.asarray(emat, dtype=BF16)


def _attn_prompt(q, qr, kcmp, vcmp, ks, vs, kw, vw, gt, gn):
    b, t, _ = q.shape
    amat, emat = _sel_matrices(kcmp.shape[1], N_SEL_PROMPT, t)
    qt = lambda w: pl.BlockSpec((1, TQ, w), lambda i, j: (i, j, 0))
    full = lambda a: pl.BlockSpec((1,) + a.shape[1:], lambda i, j: (i, 0, 0))
    const = lambda a: pl.BlockSpec(a.shape, lambda i, j: (0, 0))
    return pl.pallas_call(
        functools.partial(_attn_prompt_body, seq=t),
        grid=(b, t // TQ),
        in_specs=[qt(D_ATTN), qt(D_ATTN), full(kcmp), full(vcmp), full(ks), full(vs), full(kw), full(vw),
                  qt(LANES), const(amat), const(emat), const(gn)],
        out_specs=qt(D_ATTN),
        out_shape=jax.ShapeDtypeStruct((b, t, D_ATTN), BF16),
        compiler_params=_cparams(("parallel", "arbitrary")),
        name="attn_prompt",
    )(q, qr, kcmp, vcmp, ks, vs, kw, vw, gt, amat, emat, gn)


S5_LB = D_SSM // LANES
S5_SW = LANES // SSM_CH * SSM_STATE


def _s5_body(u_ref, h0r_ref, h0i_ref, lr_ref, li_ref, bre_ref, bim_ref, c_ref, d_ref, gw_ref, gb_ref, gn_ref,
             y_ref, hr_ref, hi_ref, y_s, *, tc, bb):
    step = pl.program_id(0)
    m = tc * bb

    @pl.when(step == 0)
    def _():
        hr_ref[...] = h0r_ref[...]
        hi_ref[...] = h0i_ref[...]

    u = u_ref[...].reshape(m, D_SSM)
    ub = u.astype(BF16)
    for j in range(S5_LB):
        uj = ub[:, j * LANES:(j + 1) * LANES]
        sl = slice(j * S5_SW, (j + 1) * S5_SW)
        xr = _dot(uj, bre_ref[j]).reshape(tc, bb, S5_SW)
        xi = _dot(uj, bim_ref[j]).reshape(tc, bb, S5_SW)
        lr = jnp.broadcast_to(lr_ref[:, sl], (bb, S5_SW))
        li = jnp.broadcast_to(li_ref[:, sl], (bb, S5_SW))
        hr, hi = hr_ref[:, sl], hi_ref[:, sl]
        hrs, his = [], []
        for t in range(tc):
            hr, hi = lr * hr - li * hi + xr[t], lr * hi + li * hr + xi[t]
            hrs.append(hr)
            his.append(hi)
        hr_ref[:, sl] = hr
        hi_ref[:, sl] = hi
        hcat = jnp.concatenate([jnp.concatenate(hrs, axis=0), jnp.concatenate(his, axis=0)], axis=1).astype(BF16)
        y_s[:, j * LANES:(j + 1) * LANES] = _dot(hcat, c_ref[j])
    z = jax.nn.gelu(y_s[...] + d_ref[...] * u)
    out = z * jax.nn.sigmoid(_dot(z.astype(BF16), gw_ref[...]) + gb_ref[...])
    y_ref[...] = _rms(out, gn_ref[...]).astype(y_ref.dtype).reshape(tc, bb, D_SSM)


def _s5(u_tm, h0r, h0i, sp, tc):
    t, bb, _ = u_tm.shape
    const = lambda a: pl.BlockSpec(a.shape, lambda i: (0,) * a.ndim)
    args = (h0r, h0i, sp['lr'], sp['li'], sp['bre'], sp['bim'], sp['c'], sp['d'], sp['glu_w'], sp['glu_b'], sp['gn'])
    st = jax.ShapeDtypeStruct(h0r.shape, F32)
    return pl.pallas_call(
        functools.partial(_s5_body, tc=tc, bb=bb),
        grid=(t // tc,),
        in_specs=[pl.BlockSpec((tc, bb, D_SSM), lambda i: (i, 0, 0))] + [const(a) for a in args],
        out_specs=[pl.BlockSpec((tc, bb, D_SSM), lambda i: (i, 0, 0)), const(h0r), const(h0i)],
        out_shape=[jax.ShapeDtypeStruct((t, bb, D_SSM), F32), st, st],
        scratch_shapes=[pltpu.VMEM((tc * bb, D_SSM), F32)],
        compiler_params=_cparams(("arbitrary",)),
        name="s5",
    )(u_tm, *args)


def _s5_params(log_dt, a_re, a_im, b_re, b_im, c_re, c_im, d, glu_w, glu_b, gn):
    dt = jnp.exp(log_dt)[:, None]
    mag = jnp.exp(a_re * dt)
    lr, li = mag * jnp.cos(a_im * dt), mag * jnp.sin(a_im * dt)
    den = a_re * a_re + a_im * a_im
    inv_r, inv_i = a_re / den, -a_im / den
    nr, ni = lr - 1.0, li
    fr, fi = nr * inv_r - ni * inv_i, nr * inv_i + ni * inv_r
    bbr = fr[..., None] * b_re - fi[..., None] * b_im
    bbi = fr[..., None] * b_im + fi[..., None] * b_re
    gl = LANES // SSM_CH
    eye = jnp.eye(gl, dtype=F32)

    def bmat(bb):
        x = bb.reshape(S5_LB, gl, SSM_STATE, SSM_CH)
        return jnp.einsum('jgpc,gh->jgchp', x, eye).reshape(S5_LB, LANES, S5_SW).astype(BF16)

    def cmat(cc):
        x = cc.reshape(S5_LB, gl, SSM_CH, SSM_STATE)
        return jnp.einsum('jgcp,gh->jgphc', x, eye).reshape(S5_LB, S5_SW, LANES)

    cm = jnp.concatenate([cmat(c_re), -cmat(c_im)], axis=1).astype(BF16)
    return dict(lr=lr.reshape(1, -1), li=li.reshape(1, -1), bre=bmat(bbr), bim=bmat(bbi), c=cm,
                d=d.reshape(1, -1), glu_w=glu_w.astype(BF16), glu_b=glu_b[None, :], gn=gn[None, :])


def _outproj_body(x_ref, a_ref, s_ref, wa_ref, ws_ref, gn_ref, o_ref, *, norm_attn):
    a = a_ref[...]
    if norm_attn:
        a = _rms(a, gn_ref[...])
    o_ref[...] = x_ref[...] + _dot(a.astype(BF16), wa_ref[...]) + _dot(s_ref[...].astype(BF16), ws_ref[...])


def _outproj(x, a, s, wa, ws, gn, norm_attn):
    m = x.shape[0]
    tm = min(m, 1024)
    row = lambda w: pl.BlockSpec((tm, w), lambda i: (i, 0))
    const = lambda a_: pl.BlockSpec(a_.shape, lambda i: (0, 0))
    return pl.pallas_call(
        functools.partial(_outproj_body, norm_attn=norm_attn),
        grid=(m // tm,),
        in_specs=[row(D_MODEL), row(D_ATTN), row(D_SSM), const(wa), const(ws), const(gn)],
        out_specs=row(D_MODEL),
        out_shape=jax.ShapeDtypeStruct((m, D_MODEL), F32),
        compiler_params=_cparams(("parallel",)),
        name="outproj",
    )(x, a, s, wa, ws, gn)


N_PAGES = PAST_LEN // PAGE_SIZE
CMP_CHUNK = 32
CMP_PITCH = 24


def _page_copy(pool_ref, buf_ref, sem_ref, pt_ref, b, i, slot):
    return pltpu.make_async_copy(pool_ref.at[pt_ref[b * N_PAGES + i]], buf_ref.at[slot, i], sem_ref.at[slot])


def _cmp_paged_body(pt_ref, pool_ref, w1_ref, b_ref, w2_ref, o_ref, buf_ref, rows_ref, p_ref, sem_ref):
    b = pl.program_id(0)
    slot = b % 2

    def start_all(bb, sl):
        def body(i, c):
            _page_copy(pool_ref, buf_ref, sem_ref, pt_ref, bb, i, sl).start()
            return c
        lax.fori_loop(0, N_PAGES, body, 0, unroll=8)

    @pl.when(b == 0)
    def _():
        start_all(0, 0)

    @pl.when(b + 1 < pl.num_programs(0))
    def _():
        start_all(b + 1, 1 - slot)

    def wait_one(i, c):
        _page_copy(pool_ref, buf_ref, sem_ref, pt_ref, b, i, slot).wait()
        return c
    lax.fori_loop(0, N_PAGES, wait_one, 0, unroll=8)

    hp = PAGE_SIZE // CMP_STRIDE
    hc = CMP_CHUNK * hp
    lane = lax.broadcasted_iota(jnp.int32, (hc, LANES), 1)
    low = lane < HEAD_DIM

    def stage_a(c):
        for i in range(CMP_CHUNK):
            t = buf_ref[slot, c * CMP_CHUNK + i].T
            for h in range(hp):
                rows_ref[c % 2, (i * hp + h) * CMP_PITCH:(i * hp + h) * CMP_PITCH + CMP_STRIDE, :] = (
                    t[h * CMP_STRIDE:(h + 1) * CMP_STRIDE])

    def stage_b(c):
        xs = [rows_ref[c % 2, pl.ds(s, hc, stride=CMP_PITCH), :] for s in range(CMP_STRIDE)]
        heads = [[], []]
        for j in range(CMP_STRIDE // 2):
            a, b = xs[2 * j], xs[2 * j + 1]
            heads[0].append(jnp.where(low, a, pltpu.roll(b, HEAD_DIM, 1)))
            heads[1].append(jnp.where(low, pltpu.roll(a, HEAD_DIM, 1), b))
        x = jnp.concatenate([jnp.concatenate(hd, axis=1) for hd in heads], axis=0).astype(BF16)
        p = _dot(x, w1_ref[...])
        for k in range(N_KV_HEADS):
            p_ref[k, c * hc:(c + 1) * hc, :] = p[k * hc:(k + 1) * hc]

    n_chunks = N_PAGES // CMP_CHUNK
    stage_a(0)
    for c in range(n_chunks):
        if c + 1 < n_chunks:
            stage_a(c + 1)
        stage_b(c)

    n_half = N_PAGES * hp
    out = None
    for k in range(N_KV_HEADS):
        pre = p_ref[k, :, :CMP_HIDDEN] + pltpu.roll(p_ref[k, :, CMP_HIDDEN:], n_half - 1, 0) + b_ref[...]
        ok = _dot(jax.nn.gelu(pre).astype(BF16), w2_ref[k])
        out = ok if out is None else out + ok
    o_ref[0] = out


def _cmp_paged(page_table_flat, pool_t, w1, bias, w2, n_seq):
    n_half = PAST_LEN // CMP_STRIDE
    hc = CMP_CHUNK * PAGE_SIZE // CMP_STRIDE
    const = lambda a: pl.BlockSpec(a.shape, lambda i, pt: (0,) * a.ndim)
    return pl.pallas_call(
        _cmp_paged_body,
        grid_spec=pltpu.PrefetchScalarGridSpec(
            num_scalar_prefetch=1,
            grid=(n_seq,),
            in_specs=[pl.BlockSpec(memory_space=pl.ANY), const(w1), const(bias), const(w2)],
            out_specs=pl.BlockSpec((1, n_half, KV_W), lambda i, pt: (i, 0, 0)),
            scratch_shapes=[pltpu.VMEM((2, N_PAGES, PAGE_SIZE, KV_W), F32),
                            pltpu.VMEM((2, hc * CMP_PITCH, KV_W), F32),
                            pltpu.VMEM((N_KV_HEADS, n_half, CMP_RATIO * CMP_HIDDEN), F32),
                            pltpu.SemaphoreType.DMA((2,))]),
        out_shape=jax.ShapeDtypeStruct((n_seq, n_half, KV_W), F32),
        compiler_params=_cparams(("arbitrary",)),
        name="cmp_paged",
    )(page_table_flat, pool_t, w1, bias, w2)


def _cmp_weights_split(pos_emb, w1, b1, w2):
    w1r = w1.reshape(CMP_RATIO, CMP_STRIDE * HEAD_DIM, CMP_HIDDEN)
    w1s = w1r.transpose(1, 0, 2).reshape(CMP_STRIDE * HEAD_DIM, CMP_RATIO * CMP_HIDDEN)
    pe = pos_emb.reshape(CMP_RATIO, CMP_STRIDE * HEAD_DIM)
    bias = b1 + jnp.einsum('rx,rxh->h', pe, w1r, precision=lax.Precision.HIGHEST)
    z = jnp.zeros_like(w2)
    w2k = jnp.stack([jnp.concatenate([w2, z], axis=1), jnp.concatenate([z, w2], axis=1)])
    return w1s.astype(BF16), bias[None, :], w2k.astype(BF16)


CMPATTN_SEQS = 8


def _cmpattn_body(q_ref, kc_ref, vc_ref, amat_ref, o_ref, idx_ref, *, ts, n_cmp, n_sel):
    rows = GQA * ts
    qpos = PAST_LEN + lax.broadcasted_iota(jnp.int32, (rows, 1), 0) % ts
    n_pad = kc_ref.shape[1]
    n_io = lax.broadcasted_iota(jnp.int32, (1, n_pad), 1)
    m = (n_io * CMP_STRIDE + (CMP_LEN - 1) <= qpos) & (n_io < n_cmp)
    n_sb = q_ref.shape[0]
    imps = []
    for b in range(n_sb):
        kc = kc_ref[b].astype(BF16)
        vc = vc_ref[b].astype(BF16)
        for k in range(N_KV_HEADS):
            s = _dot_nt((q_ref[b, k] * ATTN_SCALE).astype(BF16), kc)
            p = jnp.where(m, _softmax_rows(jnp.where(m, s, NEG_INF)), 0.0)
            o_ref[b, k] = _dot(p.astype(BF16), vc)
            imp = p[0:ts]
            for g in range(1, GQA):
                imp = imp + p[g * ts:(g + 1) * ts]
            imps.append(imp)
    imp_all = jnp.concatenate(imps, axis=0)
    n_rows = imp_all.shape[0]
    sel_w = amat_ref.shape[1]
    imp_sel = jnp.dot(imp_all, amat_ref[...], precision=lax.Precision.HIGHEST, preferred_element_type=F32)
    j = lax.broadcasted_iota(jnp.int32, (n_rows, sel_w), 1)
    jq = (PAST_LEN + lax.broadcasted_iota(jnp.int32, (n_rows, sel_w), 0) % ts) // SEL_BLOCK
    forced = (j == 0) | (j == jq) | (j == jq - 1)
    jf = j.astype(F32)
    lane_k = lax.broadcasted_iota(jnp.int32, (n_rows, LANES), 1)
    score = jnp.where(j > jq, -1.0, jnp.where(forced, FORCE_SCORE, imp_sel))
    score = jnp.where(j < n_sel, score, -jnp.inf)
    res = jnp.zeros((n_rows, LANES), F32)
    for i in range(N_SELECT):
        mx = jnp.max(score, axis=-1, keepdims=True)
        ix = jnp.min(jnp.where(score == mx, jf, 1e9), axis=-1, keepdims=True)
        res = jnp.where(lane_k == i, ix, res)
        score = jnp.where(jf == ix, -jnp.inf, score)
    idx_ref[...] = res.astype(jnp.int32).reshape(idx_ref.shape)


def _cmpattn(q_pad, kcmp, vcmp, ts):
    n_seq = q_pad.shape[0]
    n_cmp = (PAST_LEN + ts) // CMP_STRIDE - CMP_RATIO + 1
    n_sel = -(-(PAST_LEN + ts) // SEL_BLOCK)
    sel_w = -(-n_sel // LANES) * LANES
    per = SEL_BLOCK // CMP_STRIDE
    n = np.arange(kcmp.shape[1])[:, None]
    j = np.arange(sel_w)[None, :]
    amat = jnp.asarray(((n >= per * j - (CMP_RATIO - 1)) & (n <= per * j + per - 1) & (j < n_sel)).astype(np.float32))
    rows = GQA * ts
    sb = CMPATTN_SEQS
    return pl.pallas_call(
        functools.partial(_cmpattn_body, ts=ts, n_cmp=n_cmp, n_sel=n_sel),
        grid=(n_seq // sb,),
        in_specs=[pl.BlockSpec((sb, N_KV_HEADS, rows, LANES), lambda i: (i, 0, 0, 0)),
                  pl.BlockSpec((sb,) + kcmp.shape[1:], lambda i: (i, 0, 0)),
                  pl.BlockSpec((sb,) + vcmp.shape[1:], lambda i: (i, 0, 0)),
                  pl.BlockSpec(amat.shape, lambda i: (0, 0))],
        out_specs=[pl.BlockSpec((sb, N_KV_HEADS, rows, LANES), lambda i: (i, 0, 0, 0)),
                   pl.BlockSpec((sb, N_KV_HEADS * ts, LANES), lambda i: (i, 0, 0))],
        out_shape=[jax.ShapeDtypeStruct((n_seq, N_KV_HEADS, rows, LANES), F32),
                   jax.ShapeDtypeStruct((n_seq, N_KV_HEADS * ts, LANES), jnp.int32)],
        compiler_params=_cparams(("parallel",)),
        name="cmpattn_sample",
    )(q_pad, kcmp, vcmp, amat)


def _slc_copies(pools, bufs, sems, pt_ref, idx_ref, step, r, slot, ts, for_wait):
    if for_wait:
        pg = 0
    else:
        blk = idx_ref[step * (ts * N_SELECT) + r]
        page = jnp.minimum(blk // (PAGE_SIZE // SEL_BLOCK), N_PAGES - 1)
        pg = pt_ref[(step // N_KV_HEADS) * N_PAGES + page]
    return [pltpu.make_async_copy(pool.at[pg], buf.at[slot, r], sem.at[slot])
            for pool, buf, sem in zip(pools, bufs, sems)]


def _tail_body(pt_ref, idx_ref, kpool_ref, vpool_ref, q_ref, ksn_ref, vsn_ref, kwn_ref, vwn_ref, wk_ref, wv_ref,
               oc_ref, gt_ref, o_ref, kbuf_ref, vbuf_ref, ksem_ref, vsem_ref, *, ts):
    step = pl.program_id(0) * N_KV_HEADS + pl.program_id(1)
    n_steps = pl.num_programs(0) * N_KV_HEADS
    slot = step % 2
    n_fetch = ts * N_SELECT

    pools, bufs, sems = (kpool_ref, vpool_ref), (kbuf_ref, vbuf_ref), (ksem_ref, vsem_ref)

    def start_all(st, sl):
        def body(r, c):
            for cp in _slc_copies(pools, bufs, sems, pt_ref, idx_ref, st, r, sl, ts, False):
                cp.start()
            return c
        lax.fori_loop(0, n_fetch, body, 0, unroll=8)

    @pl.when(step == 0)
    def _():
        start_all(0, 0)

    @pl.when(step + 1 < n_steps)
    def _():
        start_all(step + 1, 1 - slot)

    def wait_one(r, c):
        for cp in _slc_copies(pools, bufs, sems, pt_ref, idx_ref, step, r, slot, ts, True):
            cp.wait()
        return c
    lax.fori_loop(0, n_fetch, wait_one, 0, unroll=8)

    rows = GQA * ts
    q = (q_ref[0, 0] * ATTN_SCALE).astype(BF16)
    row_t = lax.broadcasted_iota(jnp.int32, (rows, 1), 0) % ts
    qpos = PAST_LEN + row_t
    lane = lax.broadcasted_iota(jnp.int32, (1, LANES), 1)
    lane_half, lane_off = lane // SEL_BLOCK, lane % SEL_BLOCK
    pad = jnp.zeros((LANES - ksn_ref.shape[1], LANES), F32)
    new_pos = PAST_LEN + lane
    new_ok = lane < ts
    blk_new = PAST_LEN // SEL_BLOCK

    def new_rows(ref):
        return jnp.concatenate([ref[0], pad], axis=0).astype(BF16)

    ksn, vsn = new_rows(ksn_ref), new_rows(vsn_ref)
    s_new = _dot_nt(q, ksn)
    o_s = jnp.zeros((rows, LANES), F32)
    grp = 4

    def pages(buf_ref, r0):
        return jnp.concatenate([buf_ref[slot, r0 + u].astype(BF16) for u in range(grp)], axis=1)

    for t in range(ts):
        pieces = []
        has_new = jnp.int32(0)
        for i0 in range(0, N_SELECT, grp):
            sc = _dot(q, pages(kbuf_ref, t * N_SELECT + i0))
            for u in range(grp):
                blk = idx_ref[step * n_fetch + t * N_SELECT + i0 + u]
                has_new = has_new | (blk == blk_new).astype(jnp.int32)
                want_half = jnp.where(blk < blk_new, blk % (PAGE_SIZE // SEL_BLOCK), -1)
                ok = (lane_half == want_half) & (blk * SEL_BLOCK + lane_off <= qpos)
                pieces.append(jnp.where(ok, sc[:, u * LANES:(u + 1) * LANES], NEG_INF))
        ok_new = (lane < jnp.where(has_new > 0, ts, 0)) & (new_pos <= qpos)
        pieces.append(jnp.where(ok_new, s_new, NEG_INF))
        p = _softmax_rows(jnp.concatenate(pieces, axis=1)).astype(BF16)
        o_t = _dot(p[:, N_SELECT * LANES:], vsn)
        for i0 in range(0, N_SELECT, grp):
            o_t = o_t + _dot_nt(p[:, i0 * LANES:(i0 + grp) * LANES], pages(vbuf_ref, t * N_SELECT + i0))
        o_s = jnp.where(row_t == t, o_t, o_s)

    wb = wk_ref.shape[2]
    kp = PAST_LEN - wb + lax.broadcasted_iota(jnp.int32, (1, wb), 1)
    ok_w = (kp >= 0) & (kp <= qpos) & (kp > qpos - WINDOW)
    ok_wn = new_ok & (new_pos <= qpos) & (new_pos > qpos - WINDOW)
    s_w = jnp.where(ok_w, _dot(q, wk_ref[0].astype(BF16)), NEG_INF)
    s_wn = jnp.where(ok_wn, _dot_nt(q, new_rows(kwn_ref)), NEG_INF)
    p = _softmax_rows(jnp.concatenate([s_w, s_wn], axis=1)).astype(BF16)
    o_w = _dot_nt(p[:, :wb], wv_ref[0].astype(BF16)) + _dot(p[:, wb:], new_rows(vwn_ref))

    gt = gt_ref[0, 0]
    o_ref[0, 0] = gt[:, 0:1] * oc_ref[0, 0] + gt[:, 1:2] * o_s + gt[:, 2:3] * o_w


def _tail(page_table_flat, idx_flat, kpool_t, vpool_t, q_pad, ksn, vsn, kwn, vwn, wk_t, wv_t, o_cmp, gates_pad, ts):
    n_seq = q_pad.shape[0]
    rows = GQA * ts
    n_fetch = ts * N_SELECT
    per_head = pl.BlockSpec((1, 1, rows, LANES), lambda i, k, pt, ix: (i, k, 0, 0))
    per_seq = lambda a: pl.BlockSpec((1,) + a.shape[1:], lambda i, k, pt, ix: (i, 0, 0))
    hbm = pl.BlockSpec(memory_space=pl.ANY)
    return pl.pallas_call(
        functools.partial(_tail_body, ts=ts),
        grid_spec=pltpu.PrefetchScalarGridSpec(
            num_scalar_prefetch=2,
            grid=(n_seq, N_KV_HEADS),
            in_specs=[hbm, hbm, per_head, per_seq(ksn), per_seq(vsn), per_seq(kwn), per_seq(vwn),
                      per_seq(wk_t), per_seq(wv_t), per_head, per_head],
            out_specs=per_head,
            scratch_shapes=[pltpu.VMEM((2, n_fetch, PAGE_SIZE, KV_W), F32), pltpu.VMEM((2, n_fetch, PAGE_SIZE, KV_W), F32),
                            pltpu.SemaphoreType.DMA((2,)), pltpu.SemaphoreType.DMA((2,))]),
        out_shape=jax.ShapeDtypeStruct((n_seq, N_KV_HEADS, rows, LANES), F32),
        compiler_params=_cparams(("arbitrary", "arbitrary")),
        name="slc_win_sample",
    )(page_table_flat, idx_flat, kpool_t, vpool_t, q_pad, ksn, vsn, kwn, vwn, wk_t, wv_t, o_cmp, gates_pad)


def _pad_heads(x, ts):
    n = x.shape[0] // ts
    w = x.shape[1] // N_HEADS
    return x.reshape(n, ts, N_KV_HEADS, GQA, w).transpose(0, 2, 3, 1, 4).reshape(n, N_KV_HEADS, GQA * ts, w)


def _q_on_kv_lanes(q, ts):
    x = _pad_heads(q, ts)
    z = jnp.zeros_like(x[:, 0])
    return jnp.stack([jnp.concatenate([x[:, 0], z], axis=-1), jnp.concatenate([z, x[:, 1]], axis=-1)], axis=1)


def _sample_attention_jnp(q, q_rot, kc, vc, ks, vs, kw, vw, gates, ck, cv, sk, sv, wk, wv, page_table, cw, gn):
    B, T = q.shape[:2]
    pos = PAST_LEN + jnp.arange(T, dtype=jnp.int32)
    q = q.reshape(B, T, N_HEADS, HEAD_DIM)
    q_rot = q_rot.reshape(B, T, N_HEADS, HEAD_DIM)
    kvr = lambda a: a.reshape(B, T, N_KV_HEADS, HEAD_DIM)
    kc, vc, ks, vs, kw, vw = map(kvr, (kc, vc, ks, vs, kw, vw))
    gates = gates[..., :3 * N_HEADS].reshape(B, T, N_HEADS, 3)
    past = lambda pool: pool[page_table].reshape(B, -1, N_KV_HEADS, HEAD_DIM)
    kc_all, vc_all = jnp.concatenate([past(ck), kc], axis=1), jnp.concatenate([past(cv), vc], axis=1)
    ks_all, vs_all = jnp.concatenate([past(sk), ks], axis=1), jnp.concatenate([past(sv), vs], axis=1)

    def compress_blocks(k, pos_emb, w1, b1, w2):
        Bk, Tk = k.shape[:2]
        nh = Tk // CMP_STRIDE
        nc = nh - CMP_RATIO + 1
        halves = k[:, :nh * CMP_STRIDE].reshape(Bk, nh, CMP_STRIDE, N_KV_HEADS, HEAD_DIM)
        pe = pos_emb.reshape(CMP_RATIO, CMP_STRIDE, HEAD_DIM)
        w1r = w1.reshape(CMP_RATIO, CMP_STRIDE, HEAD_DIM, CMP_HIDDEN)
        pre = b1
        for r in range(CMP_RATIO):
            pre = pre + jnp.einsum('bnskd,sdh->bnkh', halves[:, r:r + nc] + pe[r][:, None, :], w1r[r])
        return jax.nn.gelu(pre) @ w2

    k_cmp = compress_blocks(kc_all, *cw[0])
    v_cmp = compress_blocks(vc_all, *cw[1])
    nc = k_cmp.shape[1]
    qg = q.reshape(B, T, N_KV_HEADS, GQA, HEAD_DIM)
    s = jnp.einsum('bqkgd,bnkd->bqkgn', qg, k_cmp).astype(F32) * ATTN_SCALE
    blk_end = jnp.arange(nc) * CMP_STRIDE + CMP_LEN - 1
    m = (blk_end[None, :] <= pos[:, None])[None, :, None, None, :]
    p = jnp.where(m, jax.nn.softmax(jnp.where(m, s, NEG_INF), axis=-1), 0.0)
    o_cmp = jnp.einsum('bqkgn,bnkd->bqkgd', p, v_cmp).reshape(B, T, N_HEADS, HEAD_DIM)
    imp = p.sum(axis=3)
    n_sel = -(-(PAST_LEN + T) // SEL_BLOCK)
    per = SEL_BLOCK // CMP_STRIDE
    total = n_sel * per + CMP_RATIO + per
    pp = jnp.pad(imp, ((0, 0), (0, 0), (0, 0), (CMP_RATIO - 1, total - nc - (CMP_RATIO - 1))))
    impb = sum(pp[..., o:o + n_sel * per:per] for o in range(per + CMP_RATIO - 1))
    j = jnp.arange(n_sel)[None, None, None, :]
    jq = (pos // SEL_BLOCK)[None, :, None, None]
    forced = (j == 0) | (j == jq) | (j == jq - 1)
    score = jnp.where(j > jq, -1.0, jnp.where(forced, FORCE_SCORE, impb))
    _, idx = lax.top_k(score, min(N_SELECT, n_sel))
    valid = idx <= jq

    def to_blocks(k):
        kp = jnp.pad(k, ((0, 0), (0, n_sel * SEL_BLOCK - k.shape[1]), (0, 0), (0, 0)))
        return kp.reshape(B, n_sel, SEL_BLOCK, N_KV_HEADS, HEAD_DIM).transpose(0, 3, 1, 2, 4)

    kb, vb = to_blocks(ks_all), to_blocks(vs_all)
    qg = q_rot.reshape(B, T, N_KV_HEADS, GQA, HEAD_DIM)
    bi = jnp.arange(B)[:, None, None, None]
    hi = jnp.arange(N_KV_HEADS)[None, None, :, None]
    ksel, vsel = kb[bi, hi, idx], vb[bi, hi, idx]
    s = jnp.einsum('bqkgd,bqknsd->bqkgns', qg, ksel).astype(F32) * ATTN_SCALE
    kpos = idx[..., None] * SEL_BLOCK + jnp.arange(SEL_BLOCK)
    mask = valid[..., None] & (kpos <= pos[None, :, None, None, None])
    s = jnp.where(mask[:, :, :, None], s, NEG_INF)
    shp = s.shape
    p = jax.nn.softmax(s.reshape(shp[:-2] + (-1,)), axis=-1).reshape(shp)
    o_slc = jnp.einsum('bqkgns,bqknsd->bqkgd', p, vsel).reshape(B, T, N_HEADS, HEAD_DIM)
    wb = wk.shape[1]
    kw_all, vw_all = jnp.concatenate([wk, kw], axis=1), jnp.concatenate([wv, vw], axis=1)
    k_pos = PAST_LEN - wb + jnp.arange(wb + T, dtype=jnp.int32)
    s = jnp.einsum('bqkgd,bjkd->bqkgj', qg, kw_all).astype(F32) * ATTN_SCALE
    kp, qp = k_pos[None, :], pos[:, None]
    m = ((kp >= 0) & (kp <= qp) & (kp > qp - WINDOW))[None, :, None, None, :]
    p = jax.nn.softmax(jnp.where(m, s, NEG_INF), axis=-1)
    o_win = jnp.einsum('bqkgj,bjkd->bqkgd', p, vw_all).reshape(B, T, N_HEADS, HEAD_DIM)
    o = (gates[..., 0:1] * o_cmp + gates[..., 1:2] * o_slc + gates[..., 2:3] * o_win).reshape(B, T, D_ATTN)
    return _rms(o, gn).astype(BF16)


def kernel(x_prompt, x_sample, cache_cmp_k, cache_cmp_v, cache_slc_k, cache_slc_v, state_win_k, state_win_v, state_s5_re, state_s5_im, page_table, norm_ffn1, ffn1_gate, ffn1_up, ffn1_down, norm_mix, w_in, cmp_k_pos, cmp_k_w1, cmp_k_b1, cmp_k_w2, cmp_v_pos, cmp_v_w1, cmp_v_b1, cmp_v_w2, s5_log_dt, s5_a_re, s5_a_im, s5_b_re, s5_b_im, s5_c_re, s5_c_im, s5_d, s5_glu_w, s5_glu_b, norm_attn_out, norm_ssm_out, w_out, norm_ffn2, ffn2_gate, ffn2_up, ffn2_down, norm_final):
    depth = w_in.shape[0]
    assert depth == 1
    l = 0
    B, T, _ = x_prompt.shape
    BS, TS, _ = x_sample.shape
    row = lambda a: a[None, :]

    w = w_in[l]
    cut = D_ATTN + 6 * KV_W
    w_p = jnp.concatenate([w[:, :cut], w[:, cut + 3 * N_HEADS:], w[:, cut:cut + 3 * N_HEADS],
                           jnp.zeros((D_MODEL, LANES - 3 * N_HEADS), F32)], axis=1).astype(BF16)
    f1 = (row(norm_ffn1[l]), ffn1_gate[l].astype(BF16), ffn1_up[l].astype(BF16), ffn1_down[l].astype(BF16))
    f2 = (row(norm_ffn2[l]), ffn2_gate[l].astype(BF16), ffn2_up[l].astype(BF16), ffn2_down[l].astype(BF16))
    gfin = row(norm_final)
    wo = w_out[l].astype(BF16)
    wo_a, wo_s = wo[:D_ATTN], wo[D_ATTN:]
    cwk = _cmp_weights(cmp_k_pos[l], cmp_k_w1[l], cmp_k_b1[l], cmp_k_w2[l])
    cwv = _cmp_weights(cmp_v_pos[l], cmp_v_w1[l], cmp_v_b1[l], cmp_v_w2[l])
    sp = _s5_params(s5_log_dt[l], s5_a_re[l], s5_a_im[l], s5_b_re[l], s5_b_im[l], s5_c_re[l], s5_c_im[l],
                    s5_d[l], s5_glu_w[l], s5_glu_b[l], norm_ssm_out[l])
    gn_attn = row(norm_attn_out[l])
    g_mix = row(norm_mix[l])

    xp = x_prompt.reshape(B * T, D_MODEL)
    x1 = _ffn(xp, *f1, gfin, False)
    tabs = _rope_tables(jnp.arange(T, dtype=jnp.int32))
    (q, qr, kc, vc, _, _, _, _, gt, u,
     kc_t, vc_t, ks_t, vs_t, kw_t, vw_t) = _proj(x1, g_mix, w_p, *tabs, seq_t=T)
    b3 = lambda a: a.reshape(B, T, a.shape[-1])
    kcmp, vcmp = _cmp_prompt(b3(kc), b3(vc), cwk, cwv)
    attn = _attn_prompt(b3(q), b3(qr), kcmp, vcmp, ks_t, vs_t, kw_t, vw_t, b3(gt), gn_attn)
    zeros = jnp.zeros((B, SSM_GROUPS * SSM_STATE), F32)
    y_tm, p_re, p_im = _s5(b3(u).transpose(1, 0, 2), zeros, zeros, sp, 64)
    ssm = y_tm.transpose(1, 0, 2).reshape(B * T, D_SSM)
    x2 = _outproj(x1, attn.reshape(B * T, D_ATTN), ssm, wo_a, wo_s, gn_attn, False)
    y_prompt = _ffn(x2, *f2, gfin, True).reshape(B, T, D_MODEL)
    from_t = lambda a: a.reshape(a.shape[0], N_KV_HEADS, HEAD_DIM, a.shape[2]).transpose(0, 3, 1, 2)[None]
    wb = min(WINDOW, T)
    p_state = (from_t(kc_t), from_t(vc_t), from_t(ks_t), from_t(vs_t),
               from_t(kw_t[:, :, T - wb:]), from_t(vw_t[:, :, T - wb:]),
               p_re.reshape(1, B, SSM_GROUPS, SSM_STATE), p_im.reshape(1, B, SSM_GROUPS, SSM_STATE))

    xs = x_sample.reshape(BS * TS, D_MODEL)
    s1 = _ffn(xs, *f1, gfin, False)
    pos_s = PAST_LEN + jnp.arange(TS, dtype=jnp.int32)
    tabs_s = [jnp.tile(a, (BS, 1)) for a in _rope_tables(pos_s)]
    q, qr, kc, vc, ks, vs, kw, vw, gt, u = _proj(s1, g_mix, w_p, *tabs_s)
    s3 = lambda a: a.reshape(BS, TS, a.shape[-1])
    to_t = lambda a: a.transpose(0, 2, 3, 1).reshape(a.shape[0], KV_W, a.shape[1])
    pt_flat = page_table.reshape(-1)
    kcmp_s = _cmp_paged(pt_flat, to_t(cache_cmp_k[l]),
                        *_cmp_weights_split(cmp_k_pos[l], cmp_k_w1[l], cmp_k_b1[l], cmp_k_w2[l]), BS)
    vcmp_s = _cmp_paged(pt_flat, to_t(cache_cmp_v[l]),
                        *_cmp_weights_split(cmp_v_pos[l], cmp_v_w1[l], cmp_v_b1[l], cmp_v_w2[l]), BS)
    o_cmp_s, idx = _cmpattn(_q_on_kv_lanes(q, TS), kcmp_s, vcmp_s, TS)
    pad8 = lambda a: jnp.pad(s3(a), ((0, 0), (0, 8 - TS), (0, 0)))
    gates_s = _pad_heads(gt[:, :3 * N_HEADS], TS)
    gates_s = jnp.pad(gates_s, ((0, 0), (0, 0), (0, 0), (0, LANES - 3)))
    wk_t, wv_t = to_t(state_win_k[l]), to_t(state_win_v[l])
    comb = _tail(pt_flat, idx[:, :, :N_SELECT].reshape(-1), to_t(cache_slc_k[l]), to_t(cache_slc_v[l]),
                 _q_on_kv_lanes(qr, TS), pad8(ks), pad8(vs), pad8(kw), pad8(vw), wk_t, wv_t, o_cmp_s, gates_s, TS)
    comb = comb.reshape(BS, N_KV_HEADS, GQA, TS, N_KV_HEADS, HEAD_DIM)
    comb = jnp.stack([comb[:, k, :, :, k] for k in range(N_KV_HEADS)], axis=1)
    attn_s = comb.transpose(0, 3, 1, 2, 4).reshape(BS * TS, D_ATTN)
    y_tm, s_re, s_im = _s5(s3(u).transpose(1, 0, 2), state_s5_re[l].reshape(BS, -1), state_s5_im[l].reshape(BS, -1), sp, TS)
    ssm_s = y_tm.transpose(1, 0, 2).reshape(BS * TS, D_SSM)
    s2 = _outproj(s1, attn_s, ssm_s, wo_a, wo_s, gn_attn, True)
    y_sample = _ffn(s2, *f2, gfin, True).reshape(BS, TS, D_MODEL)
    kv5 = lambda a, n: a.reshape(1, n, -1, N_KV_HEADS, HEAD_DIM)
    win_k = from_t(jnp.concatenate([wk_t[:, :, TS:], s3(kw).transpose(0, 2, 1)], axis=2))
    win_v = from_t(jnp.concatenate([wv_t[:, :, TS:], s3(vw).transpose(0, 2, 1)], axis=2))
    s_state = (kv5(kc, BS), kv5(vc, BS), kv5(ks, BS), kv5(vs, BS), win_k, win_v,
               s_re.reshape(1, BS, SSM_GROUPS, SSM_STATE), s_im.reshape(1, BS, SSM_GROUPS, SSM_STATE))

    return (y_prompt, y_sample) + p_state + s_state
```

```python
import functools
import math

import numpy as np
import jax
import jax.numpy as jnp
from jax import lax
from jax.experimental import pallas as pl
from jax.experimental.pallas import tpu as pltpu

F32 = jnp.float32
BF16 = jnp.bfloat16

D_MODEL = 1024
PAST_LEN = 16384
PAGE_SIZE = 128
D_ATTN = 512
D_SSM = 512
HEAD_DIM = 64
N_HEADS = 8
N_KV_HEADS = 2
GQA = 4
KV_W = 128
ROT_DIM = 16
ROPE_THETA = 500000.0
ATTN_SCALE = HEAD_DIM ** -0.5
CMP_LEN = 32
CMP_STRIDE = 16
CMP_RATIO = 2
CMP_HIDDEN = 256
SEL_BLOCK = 64
N_SELECT = 16
WINDOW = 512
SSM_CH = 16
SSM_GROUPS = 32
SSM_STATE = 64
D_FF = 2816
RMS_EPS = 1e-6
NEG_INF = -1e30
FORCE_SCORE = 1e4

LANES = 128
VMEM_LIMIT = 56 * 1024 * 1024

Z_Q = 0
Z_KV = D_ATTN
Z_U = D_ATTN + 6 * KV_W
Z_G = Z_U + D_SSM
Z_W = Z_G + LANES


def _cparams(sem):
    return pltpu.CompilerParams(dimension_semantics=sem, vmem_limit_bytes=VMEM_LIMIT)


def _rms(x, g):
    return x * lax.rsqrt(jnp.mean(x * x, axis=-1, keepdims=True) + RMS_EPS) * g


def _dot(a, b):
    return jnp.dot(a, b, preferred_element_type=F32)


def _dot_nt(a, b):
    return lax.dot_general(a, b, (((1,), (1,)), ((), ())), preferred_element_type=F32)


def _ffn_body(x_ref, g_ref, wg_ref, wu_ref, wd_ref, gf_ref, o_ref, *, final_norm):
    x = x_ref[...]
    h = _rms(x, g_ref[...]).astype(BF16)
    a = _dot(h, wg_ref[...])
    b = _dot(h, wu_ref[...])
    t = (a * jax.nn.sigmoid(a) * b).astype(BF16)
    y = x + 0.5 * _dot(t, wd_ref[...])
    if final_norm:
        y = _rms(y, gf_ref[...])
    o_ref[...] = y


def _ffn(x, g, wg, wu, wd, gf, final_norm):
    m = x.shape[0]
    tm = min(m, 512)
    row = pl.BlockSpec((tm, D_MODEL), lambda i: (i, 0))
    vec = pl.BlockSpec((1, D_MODEL), lambda i: (0, 0))
    resident = lambda a: pl.BlockSpec(a.shape, lambda i: (0, 0), pipeline_mode=pl.Buffered(1))
    return pl.pallas_call(
        functools.partial(_ffn_body, final_norm=final_norm),
        grid=(m // tm,),
        in_specs=[row, vec, resident(wg), resident(wu), resident(wd), vec],
        out_specs=row,
        out_shape=jax.ShapeDtypeStruct((m, D_MODEL), F32),
        compiler_params=_cparams(("parallel",)),
        name="ffn",
    )(x, g, wg, wu, wd, gf)


def _proj_body(x_ref, g_ref, w_ref, c_ref, sa_ref, sb_ref,
               q_ref, qr_ref, kc_ref, vc_ref, ks_ref, vs_ref, kw_ref, vw_ref, gt_ref, u_ref, *t_refs):
    h = _rms(x_ref[...], g_ref[...]).astype(BF16)
    z = _dot(h, w_ref[...])
    c, sa, sb = c_ref[...], sa_ref[...], sb_ref[...]

    def rope(v):
        return v * c + pltpu.roll(v, LANES - ROT_DIM // 2, 1) * sa + pltpu.roll(v, ROT_DIM // 2, 1) * sb

    q_ref[...] = z[:, Z_Q:Z_Q + D_ATTN]
    for i in range(D_ATTN // LANES):
        qr_ref[:, i * LANES:(i + 1) * LANES] = rope(z[:, Z_Q + i * LANES:Z_Q + (i + 1) * LANES])
    kv = [z[:, Z_KV + i * KV_W:Z_KV + (i + 1) * KV_W] for i in range(6)]
    kv[2] = rope(kv[2])
    kv[4] = rope(kv[4])
    for ref, v in zip((kc_ref, vc_ref, ks_ref, vs_ref, kw_ref, vw_ref), kv):
        ref[...] = v
    for ref, v in zip(t_refs, kv):
        ref[0] = v.T
    u_ref[...] = z[:, Z_U:Z_U + D_SSM]
    gt_ref[...] = jax.nn.sigmoid(z[:, Z_G:Z_G + LANES])


def _proj(x, g, w, rope_c, rope_sa, rope_sb, seq_t=None):
    m = x.shape[0]
    tm = min(m, 512)
    n_rep = rope_c.shape[0] // tm
    row = lambda i: (i, 0)
    const = lambda i: (0, 0)
    tab = lambda i: (i % n_rep, 0)
    widths = [D_ATTN, D_ATTN] + [KV_W] * 6 + [LANES, D_SSM]
    out_specs = [pl.BlockSpec((tm, wd), row) for wd in widths]
    out_shape = [jax.ShapeDtypeStruct((m, wd), F32) for wd in widths]
    if seq_t is not None:
        assert rope_c.shape[0] == seq_t
        out_specs += [pl.BlockSpec((1, KV_W, tm), lambda i: (i // n_rep, 0, i % n_rep))] * 6
        out_shape += [jax.ShapeDtypeStruct((m // seq_t, KV_W, seq_t), F32)] * 6
    return pl.pallas_call(
        _proj_body,
        grid=(m // tm,),
        in_specs=[pl.BlockSpec((tm, D_MODEL), row), pl.BlockSpec((1, D_MODEL), const),
                  pl.BlockSpec((D_MODEL, Z_W), const),
                  pl.BlockSpec((tm, LANES), tab), pl.BlockSpec((tm, LANES), tab), pl.BlockSpec((tm, LANES), tab)],
        out_specs=out_specs,
        out_shape=out_shape,
        compiler_params=_cparams(("parallel",)),
        name="proj",
    )(x, g, w, rope_c, rope_sa, rope_sb)


def _rope_tables(pos):
    half = ROT_DIM // 2
    inv_freq = jnp.power(ROPE_THETA, -(jnp.arange(half, dtype=F32) * 2.0 / ROT_DIM))
    ang = pos.astype(F32)[:, None] * inv_freq[None, :]
    cos, sin = jnp.cos(ang), jnp.sin(ang)
    n = pos.shape[0]
    one = jnp.ones((n, HEAD_DIM - ROT_DIM), F32)
    zero = jnp.zeros((n, HEAD_DIM - ROT_DIM), F32)
    zh = jnp.zeros((n, half), F32)
    c = jnp.concatenate([cos, cos, one], axis=1)
    sa = jnp.concatenate([-sin, zh, zero], axis=1)
    sb = jnp.concatenate([zh, sin, zero], axis=1)
    t2 = lambda a: jnp.concatenate([a, a], axis=1)
    return t2(c), t2(sa), t2(sb)


def _compress_rows(x_ref, n_half, w1_ref, b_ref, w2_ref):
    xs = [x_ref[0, pl.ds(s, n_half, stride=CMP_STRIDE), :] for s in range(CMP_STRIDE)]
    xcat = jnp.concatenate(xs, axis=1).astype(BF16)
    p = _dot(xcat, w1_ref[...])
    nh = N_KV_HEADS * CMP_HIDDEN
    p1_next = pltpu.roll(p[:, nh:], n_half - 1, 0)
    pre = p[:, :nh] + p1_next + b_ref[...]
    return _dot(jax.nn.gelu(pre).astype(BF16), w2_ref[...])


def _cmp_prompt_body(kc_ref, vc_ref, wk1_ref, bk_ref, wk2_ref, wv1_ref, bv_ref, wv2_ref, ko_ref, vo_ref, *, n_half):
    ko_ref[0] = _compress_rows(kc_ref, n_half, wk1_ref, bk_ref, wk2_ref)
    vo_ref[0] = _compress_rows(vc_ref, n_half, wv1_ref, bv_ref, wv2_ref)


def _cmp_prompt(kc, vc, wk, wv):
    b, t, _ = kc.shape
    n_half = t // CMP_STRIDE
    seq = pl.BlockSpec((1, t, KV_W), lambda i: (i, 0, 0))
    const = lambda shp: pl.BlockSpec(shp, lambda i: (0,) * len(shp))
    wspecs = [const(wk[0].shape), const(wk[1].shape), const(wk[2].shape)]
    out = pl.BlockSpec((1, n_half, KV_W), lambda i: (i, 0, 0))
    return pl.pallas_call(
        functools.partial(_cmp_prompt_body, n_half=n_half),
        grid=(b,),
        in_specs=[seq, seq] + wspecs + wspecs,
        out_specs=[out, out],
        out_shape=[jax.ShapeDtypeStruct((b, n_half, KV_W), F32)] * 2,
        compiler_params=_cparams(("parallel",)),
        name="cmp_prompt",
    )(kc, vc, *wk, *wv)


def _cmp_weights(pos_emb, w1, b1, w2):
    w1r = w1.reshape(CMP_RATIO, CMP_STRIDE, HEAD_DIM, CMP_HIDDEN)
    eye = jnp.eye(N_KV_HEADS, dtype=F32)
    w1p = jnp.einsum('rsdh,kj->skdrjh', w1r, eye).reshape(CMP_STRIDE * KV_W, CMP_RATIO * N_KV_HEADS * CMP_HIDDEN)
    pe = pos_emb.reshape(CMP_RATIO, CMP_STRIDE, HEAD_DIM)
    bias = b1 + jnp.einsum('rsd,rsdh->h', pe, w1r, precision=lax.Precision.HIGHEST)
    bias = jnp.tile(bias, N_KV_HEADS)[None, :]
    w2p = jnp.einsum('hd,kj->khjd', w2, eye).reshape(N_KV_HEADS * CMP_HIDDEN, KV_W)
    return w1p.astype(BF16), bias, w2p.astype(BF16)


TQ = 512
KCHUNK = 512
N_SEL_PROMPT = 32


def _softmax_rows(s):
    mx = jnp.max(s, axis=-1, keepdims=True)
    e = jnp.exp(s - mx)
    return e / jnp.sum(e, axis=-1, keepdims=True)


def _attn_prompt_body(q_ref, qr_ref, kcmp_ref, vcmp_ref, ks_ref, vs_ref, kw_ref, vw_ref, gt_ref,
                      amat_ref, emat_ref, gn_ref, o_ref, *, seq):
    t0 = pl.program_id(1) * TQ
    lane = lax.broadcasted_iota(jnp.int32, (TQ, LANES), 1)
    low = lane < HEAD_DIM
    tq = t0 + lax.broadcasted_iota(jnp.int32, (TQ, 1), 0)
    gates = gt_ref[0]

    def head_rows(src_ref, k):
        rows = []
        for g in range(GQA):
            h = GQA * k + g
            v = src_ref[0, :, (h // 2) * LANES:(h // 2 + 1) * LANES]
            if h % 2 != k:
                v = pltpu.roll(v, HEAD_DIM, 1)
            rows.append(jnp.where(low if k == 0 else jnp.logical_not(low), v * ATTN_SCALE, 0.0))
        return jnp.concatenate(rows, axis=0).astype(BF16)

    jj = lax.broadcasted_iota(jnp.int32, (N_SEL_PROMPT, TQ), 0)
    jq = (t0 + lax.broadcasted_iota(jnp.int32, (N_SEL_PROMPT, TQ), 1)) // SEL_BLOCK
    forced = (jj == 0) | (jj == jq) | (jj == jq - 1)

    n_cmp = kcmp_ref.shape[1]
    blk_end = lax.broadcasted_iota(jnp.int32, (1, n_cmp), 1) * CMP_STRIDE + (CMP_LEN - 1)
    m_cmp = (blk_end <= tq)[None]

    n_chunks = t0 // KCHUNK + 1
    w_keys = WINDOW + TQ
    w_start = pl.multiple_of(jnp.clip(t0 - WINDOW, 0, seq - w_keys), TQ)
    kp_w = w_start + lax.broadcasted_iota(jnp.int32, (1, w_keys), 1)
    m_win = ((kp_w <= tq) & (kp_w > tq - WINDOW))[None]

    comb, q_augs, q_rots, o_cs = [], [], [], []
    for k in range(N_KV_HEADS):
        qc = head_rows(q_ref, k)
        qr = head_rows(qr_ref, k)

        s = _dot_nt(qc, kcmp_ref[0].astype(BF16)).reshape(GQA, TQ, n_cmp)
        p = jnp.where(m_cmp, _softmax_rows(jnp.where(m_cmp, s, NEG_INF)), 0.0)
        o_c = _dot(p.reshape(GQA * TQ, n_cmp).astype(BF16), vcmp_ref[0].astype(BF16))
        imp = jnp.sum(p, axis=0)

        imp_sel = jnp.dot(amat_ref[...], imp.T, precision=lax.Precision.HIGHEST, preferred_element_type=F32)
        score = jnp.where(jj > jq, -1.0, jnp.where(forced, FORCE_SCORE, imp_sel))
        rank = jnp.zeros((N_SEL_PROMPT, TQ), F32)
        for i in range(N_SEL_PROMPT):
            si = score[i:i + 1, :]
            beats = (si > score) | ((si == score) & (jj > i))
            rank = rank + jnp.where(beats, 1.0, 0.0)
        sel_t = jnp.where((rank < N_SELECT) & (jj <= jq), 1.0, 0.0)
        sel_t = jnp.concatenate([sel_t, jnp.zeros((LANES - N_SEL_PROMPT, TQ), F32)], axis=0)
        sel = sel_t.T

        bias = ((sel - 1.0) * (-NEG_INF)).astype(BF16)
        q_augs.append(jnp.concatenate([qr, jnp.concatenate([bias] * GQA, axis=0)], axis=1))
        q_rots.append(qr)
        o_cs.append(o_c.reshape(GQA, TQ, LANES))

    sum_lanes = [HEAD_DIM * (1 - k) for k in range(N_KV_HEADS)]
    row_id = lax.broadcasted_iota(jnp.int32, (LANES, 1), 0)

    def chunk(c, carry, causal):
        k0 = pl.multiple_of(c * KCHUNK, KCHUNK)
        kk = jnp.concatenate([ks_ref[0, :, pl.ds(k0, KCHUNK)].astype(BF16),
                              emat_ref[:, pl.ds(k0, KCHUNK)]], axis=0)
        vv = vs_ref[0, :, pl.ds(k0, KCHUNK)].astype(BF16)
        out = []
        for k in range(N_KV_HEADS):
            m_i, acc = carry[k]
            sc = _dot(q_augs[k], kk).reshape(GQA, TQ, KCHUNK)
            if causal:
                kpos = k0 + lax.broadcasted_iota(jnp.int32, (1, KCHUNK), 1)
                sc = jnp.where((kpos <= tq)[None], sc, NEG_INF)
            m_n = jnp.maximum(m_i, jnp.max(sc, axis=-1, keepdims=True))
            alpha = jnp.exp(m_i - m_n)
            pe = jnp.exp((sc - m_n).astype(BF16))
            vk = jnp.where(row_id == sum_lanes[k], 1.0, vv).astype(BF16)
            pv = _dot_nt(pe.reshape(GQA * TQ, KCHUNK), vk).reshape(GQA, TQ, LANES)
            out.append((m_n, alpha * acc + pv))
        return tuple(out)

    init = tuple((jnp.full((GQA, TQ, 1), NEG_INF, F32), jnp.zeros((GQA, TQ, LANES), F32)) for _ in range(N_KV_HEADS))
    carry = lax.fori_loop(0, n_chunks - 1, functools.partial(chunk, causal=False), init)
    acc_ss = [a for _, a in chunk(n_chunks - 1, carry, True)]

    kk = kw_ref[0, :, pl.ds(w_start, w_keys)].astype(BF16)
    vv = vw_ref[0, :, pl.ds(w_start, w_keys)].astype(BF16)
    for k in range(N_KV_HEADS):
        sw = jnp.where(m_win, _dot(q_rots[k], kk).reshape(GQA, TQ, w_keys), NEG_INF)
        ew = jnp.exp((sw - jnp.max(sw, axis=-1, keepdims=True)).astype(BF16))
        vk = jnp.where(row_id == sum_lanes[k], 1.0, vv).astype(BF16)
        acc_w = _dot_nt(ew.reshape(GQA * TQ, w_keys), vk).reshape(GQA, TQ, LANES)
        sl = slice(sum_lanes[k], sum_lanes[k] + 1)
        for g in range(GQA):
            h = GQA * k + g
            g_s = gates[:, 3 * h + 1:3 * h + 2] / acc_ss[k][g][:, sl]
            g_w = gates[:, 3 * h + 2:3 * h + 3] / acc_w[g][:, sl]
            comb.append(gates[:, 3 * h:3 * h + 1] * o_cs[k][g] + g_s * acc_ss[k][g] + g_w * acc_w[g])

    cols = []
    for pr in range(N_HEADS // 2):
        k = (2 * pr) // GQA
        a, b = comb[2 * pr], comb[2 * pr + 1]
        a = pltpu.roll(a, HEAD_DIM, 1) if k == 1 else a
        b = pltpu.roll(b, HEAD_DIM, 1) if k == 0 else b
        cols.append(jnp.where(low, a, b))
    o = jnp.concatenate(cols, axis=1)
    o_ref[0] = _rms(o, gn_ref[...]).astype(o_ref.dtype)


def _sel_matrices(n_cmp_pad, n_sel, n_keys):
    j = np.arange(n_sel)[:, None]
    n = np.arange(n_cmp_pad)[None, :]
    per = SEL_BLOCK // CMP_STRIDE
    amat = ((n >= per * j - (CMP_RATIO - 1)) & (n <= per * j + per - 1)).astype(np.float32)
    jrow = np.arange(LANES)[:, None]
    key = np.arange(n_keys)[None, :]
    emat = (key // SEL_BLOCK == jrow).astype(np.float32)
    return jnp.asarray(amat), jnp.asarray(emat, dtype=BF16)


def _attn_prompt(q, qr, kcmp, vcmp, ks, vs, kw, vw, gt, gn):
    b, t, _ = q.shape
    amat, emat = _sel_matrices(kcmp.shape[1], N_SEL_PROMPT, t)
    qt = lambda w: pl.BlockSpec((1, TQ, w), lambda i, j: (i, j, 0))
    full = lambda a: pl.BlockSpec((1,) + a.shape[1:], lambda i, j: (i, 0, 0))
    const = lambda a: pl.BlockSpec(a.shape, lambda i, j: (0, 0))
    return pl.pallas_call(
        functools.partial(_attn_prompt_body, seq=t),
        grid=(b, t // TQ),
        in_specs=[qt(D_ATTN), qt(D_ATTN), full(kcmp), full(vcmp), full(ks), full(vs), full(kw), full(vw),
                  qt(LANES), const(amat), const(emat), const(gn)],
        out_specs=qt(D_ATTN),
        out_shape=jax.ShapeDtypeStruct((b, t, D_ATTN), BF16),
        compiler_params=_cparams(("parallel", "arbitrary")),
        name="attn_prompt",
    )(q, qr, kcmp, vcmp, ks, vs, kw, vw, gt, amat, emat, gn)


S5_LB = D_SSM // LANES
S5_SW = LANES // SSM_CH * SSM_STATE


def _s5_body(u_ref, h0r_ref, h0i_ref, lr_ref, li_ref, bre_ref, bim_ref, c_ref, d_ref, gw_ref, gb_ref, gn_ref,
             y_ref, hr_ref, hi_ref, y_s, *, tc, bb):
    step = pl.program_id(0)
    m = tc * bb

    @pl.when(step == 0)
    def _():
        hr_ref[...] = h0r_ref[...]
        hi_ref[...] = h0i_ref[...]

    u = u_ref[...].reshape(m, D_SSM)
    ub = u.astype(BF16)
    for j in range(S5_LB):
        uj = ub[:, j * LANES:(j + 1) * LANES]
        sl = slice(j * S5_SW, (j + 1) * S5_SW)
        xr = _dot(uj, bre_ref[j]).reshape(tc, bb, S5_SW)
        xi = _dot(uj, bim_ref[j]).reshape(tc, bb, S5_SW)
        lr = jnp.broadcast_to(lr_ref[:, sl], (bb, S5_SW))
        li = jnp.broadcast_to(li_ref[:, sl], (bb, S5_SW))
        hr, hi = hr_ref[:, sl], hi_ref[:, sl]
        hrs, his = [], []
        for t in range(tc):
            hr, hi = lr * hr - li * hi + xr[t], lr * hi + li * hr + xi[t]
            hrs.append(hr)
            his.append(hi)
        hr_ref[:, sl] = hr
        hi_ref[:, sl] = hi
        hcat = jnp.concatenate([jnp.concatenate(hrs, axis=0), jnp.concatenate(his, axis=0)], axis=1).astype(BF16)
        y_s[:, j * LANES:(j + 1) * LANES] = _dot(hcat, c_ref[j])
    z = jax.nn.gelu(y_s[...] + d_ref[...] * u)
    out = z * jax.nn.sigmoid(_dot(z.astype(BF16), gw_ref[...]) + gb_ref[...])
    y_ref[...] = _rms(out, gn_ref[...]).astype(y_ref.dtype).reshape(tc, bb, D_SSM)


def _s5(u_tm, h0r, h0i, sp, tc):
    t, bb, _ = u_tm.shape
    const = lambda a: pl.BlockSpec(a.shape, lambda i: (0,) * a.ndim)
    args = (h0r, h0i, sp['lr'], sp['li'], sp['bre'], sp['bim'], sp['c'], sp['d'], sp['glu_w'], sp['glu_b'], sp['gn'])
    st = jax.ShapeDtypeStruct(h0r.shape, F32)
    return pl.pallas_call(
        functools.partial(_s5_body, tc=tc, bb=bb),
        grid=(t // tc,),
        in_specs=[pl.BlockSpec((tc, bb, D_SSM), lambda i: (i, 0, 0))] + [const(a) for a in args],
        out_specs=[pl.BlockSpec((tc, bb, D_SSM), lambda i: (i, 0, 0)), const(h0r), const(h0i)],
        out_shape=[jax.ShapeDtypeStruct((t, bb, D_SSM), F32), st, st],
        scratch_shapes=[pltpu.VMEM((tc * bb, D_SSM), F32)],
        compiler_params=_cparams(("arbitrary",)),
        name="s5",
    )(u_tm, *args)


def _s5_params(log_dt, a_re, a_im, b_re, b_im, c_re, c_im, d, glu_w, glu_b, gn):
    dt = jnp.exp(log_dt)[:, None]
    mag = jnp.exp(a_re * dt)
    lr, li = mag * jnp.cos(a_im * dt), mag * jnp.sin(a_im * dt)
    den = a_re * a_re + a_im * a_im
    inv_r, inv_i = a_re / den, -a_im / den
    nr, ni = lr - 1.0, li
    fr, fi = nr * inv_r - ni * inv_i, nr * inv_i + ni * inv_r
    bbr = fr[..., None] * b_re - fi[..., None] * b_im
    bbi = fr[..., None] * b_im + fi[..., None] * b_re
    gl = LANES // SSM_CH
    eye = jnp.eye(gl, dtype=F32)

    def bmat(bb):
        x = bb.reshape(S5_LB, gl, SSM_STATE, SSM_CH)
        return jnp.einsum('jgpc,gh->jgchp', x, eye).reshape(S5_LB, LANES, S5_SW).astype(BF16)

    def cmat(cc):
        x = cc.reshape(S5_LB, gl, SSM_CH, SSM_STATE)
        return jnp.einsum('jgcp,gh->jgphc', x, eye).reshape(S5_LB, S5_SW, LANES)

    cm = jnp.concatenate([cmat(c_re), -cmat(c_im)], axis=1).astype(BF16)
    return dict(lr=lr.reshape(1, -1), li=li.reshape(1, -1), bre=bmat(bbr), bim=bmat(bbi), c=cm,
                d=d.reshape(1, -1), glu_w=glu_w.astype(BF16), glu_b=glu_b[None, :], gn=gn[None, :])


def _outproj_body(x_ref, a_ref, s_ref, wa_ref, ws_ref, gn_ref, o_ref, *, norm_attn):
    a = a_ref[...]
    if norm_attn:
        a = _rms(a, gn_ref[...])
    o_ref[...] = x_ref[...] + _dot(a.astype(BF16), wa_ref[...]) + _dot(s_ref[...].astype(BF16), ws_ref[...])


def _outproj(x, a, s, wa, ws, gn, norm_attn):
    m = x.shape[0]
    tm = min(m, 1024)
    row = lambda w: pl.BlockSpec((tm, w), lambda i: (i, 0))
    const = lambda a_: pl.BlockSpec(a_.shape, lambda i: (0, 0))
    return pl.pallas_call(
        functools.partial(_outproj_body, norm_attn=norm_attn),
        grid=(m // tm,),
        in_specs=[row(D_MODEL), row(D_ATTN), row(D_SSM), const(wa), const(ws), const(gn)],
        out_specs=row(D_MODEL),
        out_shape=jax.ShapeDtypeStruct((m, D_MODEL), F32),
        compiler_params=_cparams(("parallel",)),
        name="outproj",
    )(x, a, s, wa, ws, gn)


N_PAGES = PAST_LEN // PAGE_SIZE
CMP_CHUNK = 32
CMP_PITCH = 24


def _page_copy(pool_ref, buf_ref, sem_ref, pt_ref, b, i, slot):
    return pltpu.make_async_copy(pool_ref.at[pt_ref[b * N_PAGES + i]], buf_ref.at[slot, i], sem_ref.at[slot])


def _cmp_paged_body(pt_ref, pool_ref, w1_ref, b_ref, w2_ref, o_ref, buf_ref, rows_ref, p_ref, sem_ref):
    b = pl.program_id(0)
    slot = b % 2

    def start_all(bb, sl):
        def body(i, c):
            _page_copy(pool_ref, buf_ref, sem_ref, pt_ref, bb, i, sl).start()
            return c
        lax.fori_loop(0, N_PAGES, body, 0, unroll=8)

    @pl.when(b == 0)
    def _():
        start_all(0, 0)

    @pl.when(b + 1 < pl.num_programs(0))
    def _():
        start_all(b + 1, 1 - slot)

    def wait_one(i, c):
        _page_copy(pool_ref, buf_ref, sem_ref, pt_ref, b, i, slot).wait()
        return c
    lax.fori_loop(0, N_PAGES, wait_one, 0, unroll=8)

    hp = PAGE_SIZE // CMP_STRIDE
    hc = CMP_CHUNK * hp
    lane = lax.broadcasted_iota(jnp.int32, (hc, LANES), 1)
    low = lane < HEAD_DIM

    def stage_a(c):
        for i in range(CMP_CHUNK):
            t = buf_ref[slot, c * CMP_CHUNK + i].T
            for h in range(hp):
                rows_ref[c % 2, (i * hp + h) * CMP_PITCH:(i * hp + h) * CMP_PITCH + CMP_STRIDE, :] = (
                    t[h * CMP_STRIDE:(h + 1) * CMP_STRIDE])

    def stage_b(c):
        xs = [rows_ref[c % 2, pl.ds(s, hc, stride=CMP_PITCH), :] for s in range(CMP_STRIDE)]
        heads = [[], []]
        for j in range(CMP_STRIDE // 2):
            a, b = xs[2 * j], xs[2 * j + 1]
            heads[0].append(jnp.where(low, a, pltpu.roll(b, HEAD_DIM, 1)))
            heads[1].append(jnp.where(low, pltpu.roll(a, HEAD_DIM, 1), b))
        x = jnp.concatenate([jnp.concatenate(hd, axis=1) for hd in heads], axis=0).astype(BF16)
        p = _dot(x, w1_ref[...])
        for k in range(N_KV_HEADS):
            p_ref[k, c * hc:(c + 1) * hc, :] = p[k * hc:(k + 1) * hc]

    n_chunks = N_PAGES // CMP_CHUNK
    stage_a(0)
    for c in range(n_chunks):
        if c + 1 < n_chunks:
            stage_a(c + 1)
        stage_b(c)

    n_half = N_PAGES * hp
    out = None
    for k in range(N_KV_HEADS):
        pre = p_ref[k, :, :CMP_HIDDEN] + pltpu.roll(p_ref[k, :, CMP_HIDDEN:], n_half - 1, 0) + b_ref[...]
        ok = _dot(jax.nn.gelu(pre).astype(BF16), w2_ref[k])
        out = ok if out is None else out + ok
    o_ref[0] = out


def _cmp_paged(page_table_flat, pool_t, w1, bias, w2, n_seq):
    n_half = PAST_LEN // CMP_STRIDE
    hc = CMP_CHUNK * PAGE_SIZE // CMP_STRIDE
    const = lambda a: pl.BlockSpec(a.shape, lambda i, pt: (0,) * a.ndim)
    return pl.pallas_call(
        _cmp_paged_body,
        grid_spec=pltpu.PrefetchScalarGridSpec(
            num_scalar_prefetch=1,
            grid=(n_seq,),
            in_specs=[pl.BlockSpec(memory_space=pl.ANY), const(w1), const(bias), const(w2)],
            out_specs=pl.BlockSpec((1, n_half, KV_W), lambda i, pt: (i, 0, 0)),
            scratch_shapes=[pltpu.VMEM((2, N_PAGES, PAGE_SIZE, KV_W), F32),
                            pltpu.VMEM((2, hc * CMP_PITCH, KV_W), F32),
                            pltpu.VMEM((N_KV_HEADS, n_half, CMP_RATIO * CMP_HIDDEN), F32),
                            pltpu.SemaphoreType.DMA((2,))]),
        out_shape=jax.ShapeDtypeStruct((n_seq, n_half, KV_W), F32),
        compiler_params=_cparams(("arbitrary",)),
        name="cmp_paged",
    )(page_table_flat, pool_t, w1, bias, w2)


def _cmp_weights_split(pos_emb, w1, b1, w2):
    w1r = w1.reshape(CMP_RATIO, CMP_STRIDE * HEAD_DIM, CMP_HIDDEN)
    w1s = w1r.transpose(1, 0, 2).reshape(CMP_STRIDE * HEAD_DIM, CMP_RATIO * CMP_HIDDEN)
    pe = pos_emb.reshape(CMP_RATIO, CMP_STRIDE * HEAD_DIM)
    bias = b1 + jnp.einsum('rx,rxh->h', pe, w1r, precision=lax.Precision.HIGHEST)
    z = jnp.zeros_like(w2)
    w2k = jnp.stack([jnp.concatenate([w2, z], axis=1), jnp.concatenate([z, w2], axis=1)])
    return w1s.astype(BF16), bias[None, :], w2k.astype(BF16)


CMPATTN_SEQS = 8


def _cmpattn_body(q_ref, kc_ref, vc_ref, amat_ref, o_ref, idx_ref, *, ts, n_cmp, n_sel):
    rows = GQA * ts
    qpos = PAST_LEN + lax.broadcasted_iota(jnp.int32, (rows, 1), 0) % ts
    n_pad = kc_ref.shape[1]
    n_io = lax.broadcasted_iota(jnp.int32, (1, n_pad), 1)
    m = (n_io * CMP_STRIDE + (CMP_LEN - 1) <= qpos) & (n_io < n_cmp)
    n_sb = q_ref.shape[0]
    imps = []
    for b in range(n_sb):
        kc = kc_ref[b].astype(BF16)
        vc = vc_ref[b].astype(BF16)
        for k in range(N_KV_HEADS):
            s = _dot_nt((q_ref[b, k] * ATTN_SCALE).astype(BF16), kc)
            p = jnp.where(m, _softmax_rows(jnp.where(m, s, NEG_INF)), 0.0)
            o_ref[b, k] = _dot(p.astype(BF16), vc)
            imp = p[0:ts]
            for g in range(1, GQA):
                imp = imp + p[g * ts:(g + 1) * ts]
            imps.append(imp)
    imp_all = jnp.concatenate(imps, axis=0)
    n_rows = imp_all.shape[0]
    sel_w = amat_ref.shape[1]
    imp_sel = jnp.dot(imp_all, amat_ref[...], precision=lax.Precision.HIGHEST, preferred_element_type=F32)
    j = lax.broadcasted_iota(jnp.int32, (n_rows, sel_w), 1)
    jq = (PAST_LEN + lax.broadcasted_iota(jnp.int32, (n_rows, sel_w), 0) % ts) // SEL_BLOCK
    forced = (j == 0) | (j == jq) | (j == jq - 1)
    jf = j.astype(F32)
    lane_k = lax.broadcasted_iota(jnp.int32, (n_rows, LANES), 1)
    score = jnp.where(j > jq, -1.0, jnp.where(forced, FORCE_SCORE, imp_sel))
    score = jnp.where(j < n_sel, score, -jnp.inf)
    res = jnp.zeros((n_rows, LANES), F32)
    for i in range(N_SELECT):
        mx = jnp.max(score, axis=-1, keepdims=True)
        ix = jnp.min(jnp.where(score == mx, jf, 1e9), axis=-1, keepdims=True)
        res = jnp.where(lane_k == i, ix, res)
        score = jnp.where(jf == ix, -jnp.inf, score)
    idx_ref[...] = res.astype(jnp.int32).reshape(idx_ref.shape)


def _cmpattn(q_pad, kcmp, vcmp, ts):
    n_seq = q_pad.shape[0]
    n_cmp = (PAST_LEN + ts) // CMP_STRIDE - CMP_RATIO + 1
    n_sel = -(-(PAST_LEN + ts) // SEL_BLOCK)
    sel_w = -(-n_sel // LANES) * LANES
    per = SEL_BLOCK // CMP_STRIDE
    n = np.arange(kcmp.shape[1])[:, None]
    j = np.arange(sel_w)[None, :]
    amat = jnp.asarray(((n >= per * j - (CMP_RATIO - 1)) & (n <= per * j + per - 1) & (j < n_sel)).astype(np.float32))
    rows = GQA * ts
    sb = CMPATTN_SEQS
    return pl.pallas_call(
        functools.partial(_cmpattn_body, ts=ts, n_cmp=n_cmp, n_sel=n_sel),
        grid=(n_seq // sb,),
        in_specs=[pl.BlockSpec((sb, N_KV_HEADS, rows, LANES), lambda i: (i, 0, 0, 0)),
                  pl.BlockSpec((sb,) + kcmp.shape[1:], lambda i: (i, 0, 0)),
                  pl.BlockSpec((sb,) + vcmp.shape[1:], lambda i: (i, 0, 0)),
                  pl.BlockSpec(amat.shape, lambda i: (0, 0))],
        out_specs=[pl.BlockSpec((sb, N_KV_HEADS, rows, LANES), lambda i: (i, 0, 0, 0)),
                   pl.BlockSpec((sb, N_KV_HEADS * ts, LANES), lambda i: (i, 0, 0))],
        out_shape=[jax.ShapeDtypeStruct((n_seq, N_KV_HEADS, rows, LANES), F32),
                   jax.ShapeDtypeStruct((n_seq, N_KV_HEADS * ts, LANES), jnp.int32)],
        compiler_params=_cparams(("parallel",)),
        name="cmpattn_sample",
    )(q_pad, kcmp, vcmp, amat)


def _slc_copies(pools, bufs, sems, pt_ref, idx_ref, step, r, slot, ts, for_wait):
    if for_wait:
        pg = 0
    else:
        blk = idx_ref[step * (ts * N_SELECT) + r]
        page = jnp.minimum(blk // (PAGE_SIZE // SEL_BLOCK), N_PAGES - 1)
        pg = pt_ref[(step // N_KV_HEADS) * N_PAGES + page]
    return [pltpu.make_async_copy(pool.at[pg], buf.at[slot, r], sem.at[slot])
            for pool, buf, sem in zip(pools, bufs, sems)]


def _tail_body(pt_ref, idx_ref, kpool_ref, vpool_ref, q_ref, ksn_ref, vsn_ref, kwn_ref, vwn_ref, wk_ref, wv_ref,
               oc_ref, gt_ref, o_ref, kbuf_ref, vbuf_ref, ksem_ref, vsem_ref, *, ts):
    step = pl.program_id(0) * N_KV_HEADS + pl.program_id(1)
    n_steps = pl.num_programs(0) * N_KV_HEADS
    slot = step % 2
    n_fetch = ts * N_SELECT

    pools, bufs, sems = (kpool_ref, vpool_ref), (kbuf_ref, vbuf_ref), (ksem_ref, vsem_ref)

    def start_all(st, sl):
        def body(r, c):
            for cp in _slc_copies(pools, bufs, sems, pt_ref, idx_ref, st, r, sl, ts, False):
                cp.start()
            return c
        lax.fori_loop(0, n_fetch, body, 0, unroll=8)

    @pl.when(step == 0)
    def _():
        start_all(0, 0)

    @pl.when(step + 1 < n_steps)
    def _():
        start_all(step + 1, 1 - slot)

    def wait_one(r, c):
        for cp in _slc_copies(pools, bufs, sems, pt_ref, idx_ref, step, r, slot, ts, True):
            cp.wait()
        return c
    lax.fori_loop(0, n_fetch, wait_one, 0, unroll=8)

    rows = GQA * ts
    q = (q_ref[0, 0] * ATTN_SCALE).astype(BF16)
    row_t = lax.broadcasted_iota(jnp.int32, (rows, 1), 0) % ts
    qpos = PAST_LEN + row_t
    lane = lax.broadcasted_iota(jnp.int32, (1, LANES), 1)
    lane_half, lane_off = lane // SEL_BLOCK, lane % SEL_BLOCK
    pad = jnp.zeros((LANES - ksn_ref.shape[1], LANES), F32)
    new_pos = PAST_LEN + lane
    new_ok = lane < ts
    blk_new = PAST_LEN // SEL_BLOCK

    def new_rows(ref):
        return jnp.concatenate([ref[0], pad], axis=0).astype(BF16)

    ksn, vsn = new_rows(ksn_ref), new_rows(vsn_ref)
    s_new = _dot_nt(q, ksn)
    o_s = jnp.zeros((rows, LANES), F32)
    grp = 4

    def pages(buf_ref, r0):
        return jnp.concatenate([buf_ref[slot, r0 + u].astype(BF16) for u in range(grp)], axis=1)

    for t in range(ts):
        pieces = []
        has_new = jnp.int32(0)
        for i0 in range(0, N_SELECT, grp):
            sc = _dot(q, pages(kbuf_ref, t * N_SELECT + i0))
            for u in range(grp):
                blk = idx_ref[step * n_fetch + t * N_SELECT + i0 + u]
                has_new = has_new | (blk == blk_new).astype(jnp.int32)
                want_half = jnp.where(blk < blk_new, blk % (PAGE_SIZE // SEL_BLOCK), -1)
                ok = (lane_half == want_half) & (blk * SEL_BLOCK + lane_off <= qpos)
                pieces.append(jnp.where(ok, sc[:, u * LANES:(u + 1) * LANES], NEG_INF))
        ok_new = (lane < jnp.where(has_new > 0, ts, 0)) & (new_pos <= qpos)
        pieces.append(jnp.where(ok_new, s_new, NEG_INF))
        p = _softmax_rows(jnp.concatenate(pieces, axis=1)).astype(BF16)
        o_t = _dot(p[:, N_SELECT * LANES:], vsn)
        for i0 in range(0, N_SELECT, grp):
            o_t = o_t + _dot_nt(p[:, i0 * LANES:(i0 + grp) * LANES], pages(vbuf_ref, t * N_SELECT + i0))
        o_s = jnp.where(row_t == t, o_t, o_s)

    wb = wk_ref.shape[2]
    kp = PAST_LEN - wb + lax.broadcasted_iota(jnp.int32, (1, wb), 1)
    ok_w = (kp >= 0) & (kp <= qpos) & (kp > qpos - WINDOW)
    ok_wn = new_ok & (new_pos <= qpos) & (new_pos > qpos - WINDOW)
    s_w = jnp.where(ok_w, _dot(q, wk_ref[0].astype(BF16)), NEG_INF)
    s_wn = jnp.where(ok_wn, _dot_nt(q, new_rows(kwn_ref)), NEG_INF)
    p = _softmax_rows(jnp.concatenate([s_w, s_wn], axis=1)).astype(BF16)
    o_w = _dot_nt(p[:, :wb], wv_ref[0].astype(BF16)) + _dot(p[:, wb:], new_rows(vwn_ref))

    gt = gt_ref[0, 0]
    o_ref[0, 0] = gt[:, 0:1] * oc_ref[0, 0] + gt[:, 1:2] * o_s + gt[:, 2:3] * o_w


def _tail(page_table_flat, idx_flat, kpool_t, vpool_t, q_pad, ksn, vsn, kwn, vwn, wk_t, wv_t, o_cmp, gates_pad, ts):
    n_seq = q_pad.shape[0]
    rows = GQA * ts
    n_fetch = ts * N_SELECT
    per_head = pl.BlockSpec((1, 1, rows, LANES), lambda i, k, pt, ix: (i, k, 0, 0))
    per_seq = lambda a: pl.BlockSpec((1,) + a.shape[1:], lambda i, k, pt, ix: (i, 0, 0))
    hbm = pl.BlockSpec(memory_space=pl.ANY)
    return pl.pallas_call(
        functools.partial(_tail_body, ts=ts),
        grid_spec=pltpu.PrefetchScalarGridSpec(
            num_scalar_prefetch=2,
            grid=(n_seq, N_KV_HEADS),
            in_specs=[hbm, hbm, per_head, per_seq(ksn), per_seq(vsn), per_seq(kwn), per_seq(vwn),
                      per_seq(wk_t), per_seq(wv_t), per_head, per_head],
            out_specs=per_head,
            scratch_shapes=[pltpu.VMEM((2, n_fetch, PAGE_SIZE, KV_W), F32), pltpu.VMEM((2, n_fetch, PAGE_SIZE, KV_W), F32),
                            pltpu.SemaphoreType.DMA((2,)), pltpu.SemaphoreType.DMA((2,))]),
        out_shape=jax.ShapeDtypeStruct((n_seq, N_KV_HEADS, rows, LANES), F32),
        compiler_params=_cparams(("arbitrary", "arbitrary")),
        name="slc_win_sample",
    )(page_table_flat, idx_flat, kpool_t, vpool_t, q_pad, ksn, vsn, kwn, vwn, wk_t, wv_t, o_cmp, gates_pad)


def _pad_heads(x, ts):
    n = x.shape[0] // ts
    w = x.shape[1] // N_HEADS
    return x.reshape(n, ts, N_KV_HEADS, GQA, w).transpose(0, 2, 3, 1, 4).reshape(n, N_KV_HEADS, GQA * ts, w)


def _q_on_kv_lanes(q, ts):
    x = _pad_heads(q, ts)
    z = jnp.zeros_like(x[:, 0])
    return jnp.stack([jnp.concatenate([x[:, 0], z], axis=-1), jnp.concatenate([z, x[:, 1]], axis=-1)], axis=1)


def _sample_attention_jnp(q, q_rot, kc, vc, ks, vs, kw, vw, gates, ck, cv, sk, sv, wk, wv, page_table, cw, gn):
    B, T = q.shape[:2]
    pos = PAST_LEN + jnp.arange(T, dtype=jnp.int32)
    q = q.reshape(B, T, N_HEADS, HEAD_DIM)
    q_rot = q_rot.reshape(B, T, N_HEADS, HEAD_DIM)
    kvr = lambda a: a.reshape(B, T, N_KV_HEADS, HEAD_DIM)
    kc, vc, ks, vs, kw, vw = map(kvr, (kc, vc, ks, vs, kw, vw))
    gates = gates[..., :3 * N_HEADS].reshape(B, T, N_HEADS, 3)
    past = lambda pool: pool[page_table].reshape(B, -1, N_KV_HEADS, HEAD_DIM)
    kc_all, vc_all = jnp.concatenate([past(ck), kc], axis=1), jnp.concatenate([past(cv), vc], axis=1)
    ks_all, vs_all = jnp.concatenate([past(sk), ks], axis=1), jnp.concatenate([past(sv), vs], axis=1)

    def compress_blocks(k, pos_emb, w1, b1, w2):
        Bk, Tk = k.shape[:2]
        nh = Tk // CMP_STRIDE
        nc = nh - CMP_RATIO + 1
        halves = k[:, :nh * CMP_STRIDE].reshape(Bk, nh, CMP_STRIDE, N_KV_HEADS, HEAD_DIM)
        pe = pos_emb.reshape(CMP_RATIO, CMP_STRIDE, HEAD_DIM)
        w1r = w1.reshape(CMP_RATIO, CMP_STRIDE, HEAD_DIM, CMP_HIDDEN)
        pre = b1
        for r in range(CMP_RATIO):
            pre = pre + jnp.einsum('bnskd,sdh->bnkh', halves[:, r:r + nc] + pe[r][:, None, :], w1r[r])
        return jax.nn.gelu(pre) @ w2

    k_cmp = compress_blocks(kc_all, *cw[0])
    v_cmp = compress_blocks(vc_all, *cw[1])
    nc = k_cmp.shape[1]
    qg = q.reshape(B, T, N_KV_HEADS, GQA, HEAD_DIM)
    s = jnp.einsum('bqkgd,bnkd->bqkgn', qg, k_cmp).astype(F32) * ATTN_SCALE
    blk_end = jnp.arange(nc) * CMP_STRIDE + CMP_LEN - 1
    m = (blk_end[None, :] <= pos[:, None])[None, :, None, None, :]
    p = jnp.where(m, jax.nn.softmax(jnp.where(m, s, NEG_INF), axis=-1), 0.0)
    o_cmp = jnp.einsum('bqkgn,bnkd->bqkgd', p, v_cmp).reshape(B, T, N_HEADS, HEAD_DIM)
    imp = p.sum(axis=3)
    n_sel = -(-(PAST_LEN + T) // SEL_BLOCK)
    per = SEL_BLOCK // CMP_STRIDE
    total = n_sel * per + CMP_RATIO + per
    pp = jnp.pad(imp, ((0, 0), (0, 0), (0, 0), (CMP_RATIO - 1, total - nc - (CMP_RATIO - 1))))
    impb = sum(pp[..., o:o + n_sel * per:per] for o in range(per + CMP_RATIO - 1))
    j = jnp.arange(n_sel)[None, None, None, :]
    jq = (pos // SEL_BLOCK)[None, :, None, None]
    forced = (j == 0) | (j == jq) | (j == jq - 1)
    score = jnp.where(j > jq, -1.0, jnp.where(forced, FORCE_SCORE, impb))
    _, idx = lax.top_k(score, min(N_SELECT, n_sel))
    valid = idx <= jq

    def to_blocks(k):
        kp = jnp.pad(k, ((0, 0), (0, n_sel * SEL_BLOCK - k.shape[1]), (0, 0), (0, 0)))
        return kp.reshape(B, n_sel, SEL_BLOCK, N_KV_HEADS, HEAD_DIM).transpose(0, 3, 1, 2, 4)

    kb, vb = to_blocks(ks_all), to_blocks(vs_all)
    qg = q_rot.reshape(B, T, N_KV_HEADS, GQA, HEAD_DIM)
    bi = jnp.arange(B)[:, None, None, None]
    hi = jnp.arange(N_KV_HEADS)[None, None, :, None]
    ksel, vsel = kb[bi, hi, idx], vb[bi, hi, idx]
    s = jnp.einsum('bqkgd,bqknsd->bqkgns', qg, ksel).astype(F32) * ATTN_SCALE
    kpos = idx[..., None] * SEL_BLOCK + jnp.arange(SEL_BLOCK)
    mask = valid[..., None] & (kpos <= pos[None, :, None, None, None])
    s = jnp.where(mask[:, :, :, None], s, NEG_INF)
    shp = s.shape
    p = jax.nn.softmax(s.reshape(shp[:-2] + (-1,)), axis=-1).reshape(shp)
    o_slc = jnp.einsum('bqkgns,bqknsd->bqkgd', p, vsel).reshape(B, T, N_HEADS, HEAD_DIM)
    wb = wk.shape[1]
    kw_all, vw_all = jnp.concatenate([wk, kw], axis=1), jnp.concatenate([wv, vw], axis=1)
    k_pos = PAST_LEN - wb + jnp.arange(wb + T, dtype=jnp.int32)
    s = jnp.einsum('bqkgd,bjkd->bqkgj', qg, kw_all).astype(F32) * ATTN_SCALE
    kp, qp = k_pos[None, :], pos[:, None]
    m = ((kp >= 0) & (kp <= qp) & (kp > qp - WINDOW))[None, :, None, None, :]
    p = jax.nn.softmax(jnp.where(m, s, NEG_INF), axis=-1)
    o_win = jnp.einsum('bqkgj,bjkd->bqkgd', p, vw_all).reshape(B, T, N_HEADS, HEAD_DIM)
    o = (gates[..., 0:1] * o_cmp + gates[..., 1:2] * o_slc + gates[..., 2:3] * o_win).reshape(B, T, D_ATTN)
    return _rms(o, gn).astype(BF16)


def kernel(x_prompt, x_sample, cache_cmp_k, cache_cmp_v, cache_slc_k, cache_slc_v, state_win_k, state_win_v, state_s5_re, state_s5_im, page_table, norm_ffn1, ffn1_gate, ffn1_up, ffn1_down, norm_mix, w_in, cmp_k_pos, cmp_k_w1, cmp_k_b1, cmp_k_w2, cmp_v_pos, cmp_v_w1, cmp_v_b1, cmp_v_w2, s5_log_dt, s5_a_re, s5_a_im, s5_b_re, s5_b_im, s5_c_re, s5_c_im, s5_d, s5_glu_w, s5_glu_b, norm_attn_out, norm_ssm_out, w_out, norm_ffn2, ffn2_gate, ffn2_up, ffn2_down, norm_final):
    depth = w_in.shape[0]
    assert depth == 1
    l = 0
    B, T, _ = x_prompt.shape
    BS, TS, _ = x_sample.shape
    row = lambda a: a[None, :]

    w = w_in[l]
    cut = D_ATTN + 6 * KV_W
    w_p = jnp.concatenate([w[:, :cut], w[:, cut + 3 * N_HEADS:], w[:, cut:cut + 3 * N_HEADS],
                           jnp.zeros((D_MODEL, LANES - 3 * N_HEADS), F32)], axis=1).astype(BF16)
    f1 = (row(norm_ffn1[l]), ffn1_gate[l].astype(BF16), ffn1_up[l].astype(BF16), ffn1_down[l].astype(BF16))
    f2 = (row(norm_ffn2[l]), ffn2_gate[l].astype(BF16), ffn2_up[l].astype(BF16), ffn2_down[l].astype(BF16))
    gfin = row(norm_final)
    wo = w_out[l].astype(BF16)
    wo_a, wo_s = wo[:D_ATTN], wo[D_ATTN:]
    cwk = _cmp_weights(cmp_k_pos[l], cmp_k_w1[l], cmp_k_b1[l], cmp_k_w2[l])
    cwv = _cmp_weights(cmp_v_pos[l], cmp_v_w1[l], cmp_v_b1[l], cmp_v_w2[l])
    sp = _s5_params(s5_log_dt[l], s5_a_re[l], s5_a_im[l], s5_b_re[l], s5_b_im[l], s5_c_re[l], s5_c_im[l],
                    s5_d[l], s5_glu_w[l], s5_glu_b[l], norm_ssm_out[l])
    gn_attn = row(norm_attn_out[l])
    g_mix = row(norm_mix[l])

    xp = x_prompt.reshape(B * T, D_MODEL)
    x1 = _ffn(xp, *f1, gfin, False)
    tabs = _rope_tables(jnp.arange(T, dtype=jnp.int32))
    (q, qr, kc, vc, _, _, _, _, gt, u,
     kc_t, vc_t, ks_t, vs_t, kw_t, vw_t) = _proj(x1, g_mix, w_p, *tabs, seq_t=T)
    b3 = lambda a: a.reshape(B, T, a.shape[-1])
    kcmp, vcmp = _cmp_prompt(b3(kc), b3(vc), cwk, cwv)
    attn = _attn_prompt(b3(q), b3(qr), kcmp, vcmp, ks_t, vs_t, kw_t, vw_t, b3(gt), gn_attn)
    zeros = jnp.zeros((B, SSM_GROUPS * SSM_STATE), F32)
    y_tm, p_re, p_im = _s5(b3(u).transpose(1, 0, 2), zeros, zeros, sp, 64)
    ssm = y_tm.transpose(1, 0, 2).reshape(B * T, D_SSM)
    x2 = _outproj(x1, attn.reshape(B * T, D_ATTN), ssm, wo_a, wo_s, gn_attn, False)
    y_prompt = _ffn(x2, *f2, gfin, True).reshape(B, T, D_MODEL)
    from_t = lambda a: a.reshape(a.shape[0], N_KV_HEADS, HEAD_DIM, a.shape[2]).transpose(0, 3, 1, 2)[None]
    wb = min(WINDOW, T)
    p_state = (from_t(kc_t), from_t(vc_t), from_t(ks_t), from_t(vs_t),
               from_t(kw_t[:, :, T - wb:]), from_t(vw_t[:, :, T - wb:]),
               p_re.reshape(1, B, SSM_GROUPS, SSM_STATE), p_im.reshape(1, B, SSM_GROUPS, SSM_STATE))

    xs = x_sample.reshape(BS * TS, D_MODEL)
    s1 = _ffn(xs, *f1, gfin, False)
    pos_s = PAST_LEN + jnp.arange(TS, dtype=jnp.int32)
    tabs_s = [jnp.tile(a, (BS, 1)) for a in _rope_tables(pos_s)]
    q, qr, kc, vc, ks, vs, kw, vw, gt, u = _proj(s1, g_mix, w_p, *tabs_s)
    s3 = lambda a: a.reshape(BS, TS, a.shape[-1])
    to_t = lambda a: a.transpose(0, 2, 3, 1).reshape(a.shape[0], KV_W, a.shape[1])
    pt_flat = page_table.reshape(-1)
    kcmp_s = _cmp_paged(pt_flat, to_t(cache_cmp_k[l]),
                        *_cmp_weights_split(cmp_k_pos[l], cmp_k_w1[l], cmp_k_b1[l], cmp_k_w2[l]), BS)
    vcmp_s = _cmp_paged(pt_flat, to_t(cache_cmp_v[l]),
                        *_cmp_weights_split(cmp_v_pos[l], cmp_v_w1[l], cmp_v_b1[l], cmp_v_w2[l]), BS)
    o_cmp_s, idx = _cmpattn(_q_on_kv_lanes(q, TS), kcmp_s, vcmp_s, TS)
    pad8 = lambda a: jnp.pad(s3(a), ((0, 0), (0, 8 - TS), (0, 0)))
    gates_s = _pad_heads(gt[:, :3 * N_HEADS], TS)
    gates_s = jnp.pad(gates_s, ((0, 0), (0, 0), (0, 0), (0, LANES - 3)))
    wk_t, wv_t = to_t(state_win_k[l]), to_t(state_win_v[l])
    comb = _tail(pt_flat, idx[:, :, :N_SELECT].reshape(-1), to_t(cache_slc_k[l]), to_t(cache_slc_v[l]),
                 _q_on_kv_lanes(qr, TS), pad8(ks), pad8(vs), pad8(kw), pad8(vw), wk_t, wv_t, o_cmp_s, gates_s, TS)
    comb = comb.reshape(BS, N_KV_HEADS, GQA, TS, N_KV_HEADS, HEAD_DIM)
    comb = jnp.stack([comb[:, k, :, :, k] for k in range(N_KV_HEADS)], axis=1)
    attn_s = comb.transpose(0, 3, 1, 2, 4).reshape(BS * TS, D_ATTN)
    y_tm, s_re, s_im = _s5(s3(u).transpose(1, 0, 2), state_s5_re[l].reshape(BS, -1), state_s5_im[l].reshape(BS, -1), sp, TS)
    ssm_s = y_tm.transpose(1, 0, 2).reshape(BS * TS, D_SSM)
    s2 = _outproj(s1, attn_s, ssm_s, wo_a, wo_s, gn_attn, True)
    y_sample = _ffn(s2, *f2, gfin, True).reshape(BS, TS, D_MODEL)
    kv5 = lambda a, n: a.reshape(1, n, -1, N_KV_HEADS, HEAD_DIM)
    win_k = from_t(jnp.concatenate([wk_t[:, :, TS:], s3(kw).transpose(0, 2, 1)], axis=2))
    win_v = from_t(jnp.concatenate([wv_t[:, :, TS:], s3(vw).transpose(0, 2, 1)], axis=2))
    s_state = (kv5(kc, BS), kv5(vc, BS), kv5(ks, BS), kv5(vs, BS), win_k, win_v,
               s_re.reshape(1, BS, SSM_GROUPS, SSM_STATE), s_im.reshape(1, BS, SSM_GROUPS, SSM_STATE))

    return (y_prompt, y_sample) + p_state + s_state
```

```python
import functools
import math

import numpy as np
import jax
import jax.numpy as jnp
from jax import lax
from jax.experimental import pallas as pl
from jax.experimental.pallas import tpu as pltpu

F32 = jnp.float32
BF16 = jnp.bfloat16

D_MODEL = 1024
PAST_LEN = 16384
PAGE_SIZE = 128
D_ATTN = 512
D_SSM = 512
HEAD_DIM = 64
N_HEADS = 8
N_KV_HEADS = 2
GQA = 4
KV_W = 128
ROT_DIM = 16
ROPE_THETA = 500000.0
ATTN_SCALE = HEAD_DIM ** -0.5
CMP_LEN = 32
CMP_STRIDE = 16
CMP_RATIO = 2
CMP_HIDDEN = 256
SEL_BLOCK = 64
N_SELECT = 16
WINDOW = 512
SSM_CH = 16
SSM_GROUPS = 32
SSM_STATE = 64
D_FF = 2816
RMS_EPS = 1e-6
NEG_INF = -1e30
FORCE_SCORE = 1e4

LANES = 128
VMEM_LIMIT = 56 * 1024 * 1024

Z_Q = 0
Z_KV = D_ATTN
Z_U = D_ATTN + 6 * KV_W
Z_G = Z_U + D_SSM
Z_W = Z_G + LANES


def _cparams(sem):
    return pltpu.CompilerParams(dimension_semantics=sem, vmem_limit_bytes=VMEM_LIMIT)


def _rms(x, g):
    return x * lax.rsqrt(jnp.mean(x * x, axis=-1, keepdims=True) + RMS_EPS) * g


def _dot(a, b):
    return jnp.dot(a, b, preferred_element_type=F32)


def _dot_nt(a, b):
    return lax.dot_general(a, b, (((1,), (1,)), ((), ())), preferred_element_type=F32)


def _ffn_body(x_ref, g_ref, wg_ref, wu_ref, wd_ref, gf_ref, o_ref, *, final_norm):
    x = x_ref[...]
    h = _rms(x, g_ref[...]).astype(BF16)
    a = _dot(h, wg_ref[...])
    b = _dot(h, wu_ref[...])
    t = (a * jax.nn.sigmoid(a) * b).astype(BF16)
    y = x + 0.5 * _dot(t, wd_ref[...])
    if final_norm:
        y = _rms(y, gf_ref[...])
    o_ref[...] = y


def _ffn(x, g, wg, wu, wd, gf, final_norm):
    m = x.shape[0]
    tm = min(m, 512)
    row = pl.BlockSpec((tm, D_MODEL), lambda i: (i, 0))
    vec = pl.BlockSpec((1, D_MODEL), lambda i: (0, 0))
    resident = lambda a: pl.BlockSpec(a.shape, lambda i: (0, 0), pipeline_mode=pl.Buffered(1))
    return pl.pallas_call(
        functools.partial(_ffn_body, final_norm=final_norm),
        grid=(m // tm,),
        in_specs=[row, vec, resident(wg), resident(wu), resident(wd), vec],
        out_specs=row,
        out_shape=jax.ShapeDtypeStruct((m, D_MODEL), F32),
        compiler_params=_cparams(("parallel",)),
        name="ffn",
    )(x, g, wg, wu, wd, gf)


def _proj_body(x_ref, g_ref, w_ref, c_ref, sa_ref, sb_ref,
               q_ref, qr_ref, kc_ref, vc_ref, ks_ref, vs_ref, kw_ref, vw_ref, gt_ref, u_ref, *t_refs):
    h = _rms(x_ref[...], g_ref[...]).astype(BF16)
    z = _dot(h, w_ref[...])
    c, sa, sb = c_ref[...], sa_ref[...], sb_ref[...]

    def rope(v):
        return v * c + pltpu.roll(v, LANES - ROT_DIM // 2, 1) * sa + pltpu.roll(v, ROT_DIM // 2, 1) * sb

    q_ref[...] = z[:, Z_Q:Z_Q + D_ATTN]
    for i in range(D_ATTN // LANES):
        qr_ref[:, i * LANES:(i + 1) * LANES] = rope(z[:, Z_Q + i * LANES:Z_Q + (i + 1) * LANES])
    kv = [z[:, Z_KV + i * KV_W:Z_KV + (i + 1) * KV_W] for i in range(6)]
    kv[2] = rope(kv[2])
    kv[4] = rope(kv[4])
    for ref, v in zip((kc_ref, vc_ref, ks_ref, vs_ref, kw_ref, vw_ref), kv):
        ref[...] = v
    for ref, v in zip(t_refs, kv):
        ref[0] = v.T
    u_ref[...] = z[:, Z_U:Z_U + D_SSM]
    gt_ref[...] = jax.nn.sigmoid(z[:, Z_G:Z_G + LANES])


def _proj(x, g, w, rope_c, rope_sa, rope_sb, seq_t=None):
    m = x.shape[0]
    tm = min(m, 512)
    n_rep = rope_c.shape[0] // tm
    row = lambda i: (i, 0)
    const = lambda i: (0, 0)
    tab = lambda i: (i % n_rep, 0)
    widths = [D_ATTN, D_ATTN] + [KV_W] * 6 + [LANES, D_SSM]
    out_specs = [pl.BlockSpec((tm, wd), row) for wd in widths]
    out_shape = [jax.ShapeDtypeStruct((m, wd), F32) for wd in widths]
    if seq_t is not None:
        assert rope_c.shape[0] == seq_t
        out_specs += [pl.BlockSpec((1, KV_W, tm), lambda i: (i // n_rep, 0, i % n_rep))] * 6
        out_shape += [jax.ShapeDtypeStruct((m // seq_t, KV_W, seq_t), F32)] * 6
    return pl.pallas_call(
        _proj_body,
        grid=(m // tm,),
        in_specs=[pl.BlockSpec((tm, D_MODEL), row), pl.BlockSpec((1, D_MODEL), const),
                  pl.BlockSpec((D_MODEL, Z_W), const),
                  pl.BlockSpec((tm, LANES), tab), pl.BlockSpec((tm, LANES), tab), pl.BlockSpec((tm, LANES), tab)],
        out_specs=out_specs,
        out_shape=out_shape,
        compiler_params=_cparams(("parallel",)),
        name="proj",
    )(x, g, w, rope_c, rope_sa, rope_sb)


def _rope_tables(pos):
    half = ROT_DIM // 2
    inv_freq = jnp.power(ROPE_THETA, -(jnp.arange(half, dtype=F32) * 2.0 / ROT_DIM))
    ang = pos.astype(F32)[:, None] * inv_freq[None, :]
    cos, sin = jnp.cos(ang), jnp.sin(ang)
    n = pos.shape[0]
    one = jnp.ones((n, HEAD_DIM - ROT_DIM), F32)
    zero = jnp.zeros((n, HEAD_DIM - ROT_DIM), F32)
    zh = jnp.zeros((n, half), F32)
    c = jnp.concatenate([cos, cos, one], axis=1)
    sa = jnp.concatenate([-sin, zh, zero], axis=1)
    sb = jnp.concatenate([zh, sin, zero], axis=1)
    t2 = lambda a: jnp.concatenate([a, a], axis=1)
    return t2(c), t2(sa), t2(sb)


def _compress_rows(x_ref, n_half, w1_ref, b_ref, w2_ref):
    xs = [x_ref[0, pl.ds(s, n_half, stride=CMP_STRIDE), :] for s in range(CMP_STRIDE)]
    xcat = jnp.concatenate(xs, axis=1).astype(BF16)
    p = _dot(xcat, w1_ref[...])
    nh = N_KV_HEADS * CMP_HIDDEN
    p1_next = pltpu.roll(p[:, nh:], n_half - 1, 0)
    pre = p[:, :nh] + p1_next + b_ref[...]
    return _dot(jax.nn.gelu(pre).astype(BF16), w2_ref[...])


def _cmp_prompt_body(kc_ref, vc_ref, wk1_ref, bk_ref, wk2_ref, wv1_ref, bv_ref, wv2_ref, ko_ref, vo_ref, *, n_half):
    ko_ref[0] = _compress_rows(kc_ref, n_half, wk1_ref, bk_ref, wk2_ref)
    vo_ref[0] = _compress_rows(vc_ref, n_half, wv1_ref, bv_ref, wv2_ref)


def _cmp_prompt(kc, vc, wk, wv):
    b, t, _ = kc.shape
    n_half = t // CMP_STRIDE
    seq = pl.BlockSpec((1, t, KV_W), lambda i: (i, 0, 0))
    const = lambda shp: pl.BlockSpec(shp, lambda i: (0,) * len(shp))
    wspecs = [const(wk[0].shape), const(wk[1].shape), const(wk[2].shape)]
    out = pl.BlockSpec((1, n_half, KV_W), lambda i: (i, 0, 0))
    return pl.pallas_call(
        functools.partial(_cmp_prompt_body, n_half=n_half),
        grid=(b,),
        in_specs=[seq, seq] + wspecs + wspecs,
        out_specs=[out, out],
        out_shape=[jax.ShapeDtypeStruct((b, n_half, KV_W), F32)] * 2,
        compiler_params=_cparams(("parallel",)),
        name="cmp_prompt",
    )(kc, vc, *wk, *wv)


def _cmp_weights(pos_emb, w1, b1, w2):
    w1r = w1.reshape(CMP_RATIO, CMP_STRIDE, HEAD_DIM, CMP_HIDDEN)
    eye = jnp.eye(N_KV_HEADS, dtype=F32)
    w1p = jnp.einsum('rsdh,kj->skdrjh', w1r, eye).reshape(CMP_STRIDE * KV_W, CMP_RATIO * N_KV_HEADS * CMP_HIDDEN)
    pe = pos_emb.reshape(CMP_RATIO, CMP_STRIDE, HEAD_DIM)
    bias = b1 + jnp.einsum('rsd,rsdh->h', pe, w1r, precision=lax.Precision.HIGHEST)
    bias = jnp.tile(bias, N_KV_HEADS)[None, :]
    w2p = jnp.einsum('hd,kj->khjd', w2, eye).reshape(N_KV_HEADS * CMP_HIDDEN, KV_W)
    return w1p.astype(BF16), bias, w2p.astype(BF16)


TQ = 512
KCHUNK = 512
SUBQ = 256
N_SEL_PROMPT = 32
assert TQ == KCHUNK and TQ % SUBQ == 0


def _softmax_rows(s):
    mx = jnp.max(s, axis=-1, keepdims=True)
    e = jnp.exp(s - mx)
    return e / jnp.sum(e, axis=-1, keepdims=True)


def _attn_prompt_body(q_ref, qr_ref, kcmp_ref, vcmp_ref, ks_ref, vs_ref, kw_ref, vw_ref, gt_ref,
                      amat_ref, emat_ref, gn_ref, o_ref, *, seq):
    t0 = pl.program_id(1) * TQ
    lane = lax.broadcasted_iota(jnp.int32, (TQ, LANES), 1)
    low = lane < HEAD_DIM
    tq = t0 + lax.broadcasted_iota(jnp.int32, (TQ, 1), 0)
    gates = gt_ref[0]

    def head_rows(src_ref, k):
        rows = []
        for g in range(GQA):
            h = GQA * k + g
            v = src_ref[0, :, (h // 2) * LANES:(h // 2 + 1) * LANES]
            if h % 2 != k:
                v = pltpu.roll(v, HEAD_DIM, 1)
            rows.append(jnp.where(low if k == 0 else jnp.logical_not(low), v * ATTN_SCALE, 0.0))
        return jnp.concatenate(rows, axis=0).astype(BF16)

    jj = lax.broadcasted_iota(jnp.int32, (N_SEL_PROMPT, TQ), 0)
    jq = (t0 + lax.broadcasted_iota(jnp.int32, (N_SEL_PROMPT, TQ), 1)) // SEL_BLOCK
    forced = (jj == 0) | (jj == jq) | (jj == jq - 1)

    n_cmp = kcmp_ref.shape[1]
    blk_end = lax.broadcasted_iota(jnp.int32, (1, n_cmp), 1) * CMP_STRIDE + (CMP_LEN - 1)
    m_cmp = (blk_end <= tq)[None]

    n_chunks = t0 // KCHUNK + 1

    comb, q_augs, q_rots, o_cs = [], [], [], []
    for k in range(N_KV_HEADS):
        qc = head_rows(q_ref, k)
        qr = head_rows(qr_ref, k)

        s = _dot_nt(qc, kcmp_ref[0].astype(BF16)).reshape(GQA, TQ, n_cmp)
        p = jnp.where(m_cmp, _softmax_rows(jnp.where(m_cmp, s, NEG_INF)), 0.0)
        o_c = _dot(p.reshape(GQA * TQ, n_cmp).astype(BF16), vcmp_ref[0].astype(BF16))
        imp = jnp.sum(p, axis=0)

        imp_sel = jnp.dot(amat_ref[...], imp.T, precision=lax.Precision.HIGHEST, preferred_element_type=F32)
        score = jnp.where(jj > jq, -1.0, jnp.where(forced, FORCE_SCORE, imp_sel))
        rank = jnp.zeros((N_SEL_PROMPT, TQ), F32)
        for i in range(N_SEL_PROMPT):
            si = score[i:i + 1, :]
            beats = (si > score) | ((si == score) & (jj > i))
            rank = rank + jnp.where(beats, 1.0, 0.0)
        sel_t = jnp.where((rank < N_SELECT) & (jj <= jq), 1.0, 0.0)
        sel_t = jnp.concatenate([sel_t, jnp.zeros((LANES - N_SEL_PROMPT, TQ), F32)], axis=0)
        sel = sel_t.T

        bias = ((sel - 1.0) * (-NEG_INF)).astype(BF16)
        q_augs.append(jnp.concatenate([qr, jnp.concatenate([bias] * GQA, axis=0)], axis=1))
        q_rots.append(qr)
        o_cs.append(o_c.reshape(GQA, TQ, LANES))

    sum_lanes = [HEAD_DIM * (1 - k) for k in range(N_KV_HEADS)]
    row_id = lax.broadcasted_iota(jnp.int32, (LANES, 1), 0)

    def attend(q2d, kk, vv, k, mask, m_i, acc):
        n_q, n_k = q2d.shape[0] // GQA, kk.shape[1]
        sc = _dot(q2d, kk).reshape(GQA, n_q, n_k)
        if mask is not None:
            sc = jnp.where(mask[None], sc, NEG_INF)
        m_n = jnp.max(sc, axis=-1, keepdims=True)
        if m_i is not None:
            m_n = jnp.maximum(m_i, m_n)
        pe = jnp.exp((sc - m_n).astype(BF16))
        vk = jnp.where(row_id == sum_lanes[k], 1.0, vv).astype(BF16)
        pv = _dot_nt(pe.reshape(GQA * n_q, n_k), vk).reshape(GQA, n_q, LANES)
        if m_i is not None:
            pv = jnp.exp(m_i - m_n) * acc + pv
        return m_n, pv

    def sub_rows(x2d, s):
        w = x2d.shape[1]
        return x2d.reshape(GQA, TQ, w)[:, s * SUBQ:(s + 1) * SUBQ].reshape(GQA * SUBQ, w)

    def chunk(c, carry):
        k0 = pl.multiple_of(c * KCHUNK, KCHUNK)
        kk = jnp.concatenate([ks_ref[0, :, pl.ds(k0, KCHUNK)].astype(BF16),
                              emat_ref[:, pl.ds(k0, KCHUNK)]], axis=0)
        vv = vs_ref[0, :, pl.ds(k0, KCHUNK)].astype(BF16)
        return tuple(attend(q_augs[k], kk, vv, k, None, *carry[k]) for k in range(N_KV_HEADS))

    init = tuple((jnp.full((GQA, TQ, 1), NEG_INF, F32), jnp.zeros((GQA, TQ, LANES), F32)) for _ in range(N_KV_HEADS))
    carry = lax.fori_loop(0, n_chunks - 1, chunk, init)

    acc_ss = [[] for _ in range(N_KV_HEADS)]
    acc_ws = [[] for _ in range(N_KV_HEADS)]
    w_sub = WINDOW + SUBQ
    for s in range(TQ // SUBQ):
        tq_s = t0 + s * SUBQ + lax.broadcasted_iota(jnp.int32, (SUBQ, 1), 0)
        n_k = (s + 1) * SUBQ
        k0 = pl.multiple_of(t0, TQ)
        kk = jnp.concatenate([ks_ref[0, :, pl.ds(k0, n_k)].astype(BF16), emat_ref[:, pl.ds(k0, n_k)]], axis=0)
        vv = vs_ref[0, :, pl.ds(k0, n_k)].astype(BF16)
        causal = (k0 + lax.broadcasted_iota(jnp.int32, (1, n_k), 1)) <= tq_s
        ws = pl.multiple_of(jnp.clip(t0 + s * SUBQ - WINDOW, 0, seq - w_sub), SUBQ)
        kw = kw_ref[0, :, pl.ds(ws, w_sub)].astype(BF16)
        vw = vw_ref[0, :, pl.ds(ws, w_sub)].astype(BF16)
        kp_w = ws + lax.broadcasted_iota(jnp.int32, (1, w_sub), 1)
        in_win = (kp_w <= tq_s) & (kp_w > tq_s - WINDOW)
        for k in range(N_KV_HEADS):
            m_i, acc = carry[k]
            rows = slice(s * SUBQ, (s + 1) * SUBQ)
            acc_ss[k].append(attend(sub_rows(q_augs[k], s), kk, vv, k, causal, m_i[:, rows], acc[:, rows])[1])
            acc_ws[k].append(attend(sub_rows(q_rots[k], s), kw, vw, k, in_win, None, None)[1])

    for k in range(N_KV_HEADS):
        acc_s, acc_w = jnp.concatenate(acc_ss[k], axis=1), jnp.concatenate(acc_ws[k], axis=1)
        sl = slice(sum_lanes[k], sum_lanes[k] + 1)
        for g in range(GQA):
            h = GQA * k + g
            g_s = gates[:, 3 * h + 1:3 * h + 2] / acc_s[g][:, sl]
            g_w = gates[:, 3 * h + 2:3 * h + 3] / acc_w[g][:, sl]
            comb.append(gates[:, 3 * h:3 * h + 1] * o_cs[k][g] + g_s * acc_s[g] + g_w * acc_w[g])

    cols = []
    for pr in range(N_HEADS // 2):
        k = (2 * pr) // GQA
        a, b = comb[2 * pr], comb[2 * pr + 1]
        a = pltpu.roll(a, HEAD_DIM, 1) if k == 1 else a
        b = pltpu.roll(b, HEAD_DIM, 1) if k == 0 else b
        cols.append(jnp.where(low, a, b))
    o = jnp.concatenate(cols, axis=1)
    o_ref[0] = _rms(o, gn_ref[...]).astype(o_ref.dtype)


def _sel_matrices(n_cmp_pad, n_sel, n_keys):
    j = np.arange(n_sel)[:, None]
    n = np.arange(n_cmp_pad)[None, :]
    per = SEL_BLOCK // CMP_STRIDE
    amat = ((n >= per * j - (CMP_RATIO - 1)) & (n <= per * j + per - 1)).astype(np.float32)
    jrow = np.arange(LANES)[:, None]
    key = np.arange(n_keys)[None, :]
    emat = (key // SEL_BLOCK == jrow).astype(np.float32)
    return jnp.asarray(amat), jnp.asarray(emat, dtype=BF16)


def _attn_prompt(q, qr, kcmp, vcmp, ks, vs, kw, vw, gt, gn):
    b, t, _ = q.shape
    amat, emat = _sel_matrices(kcmp.shape[1], N_SEL_PROMPT, t)
    qt = lambda w: pl.BlockSpec((1, TQ, w), lambda i, j: (i, j, 0))
    full = lambda a: pl.BlockSpec((1,) + a.shape[1:], lambda i, j: (i, 0, 0))
    const = lambda a: pl.BlockSpec(a.shape, lambda i, j: (0, 0))
    return pl.pallas_call(
        functools.partial(_attn_prompt_body, seq=t),
        grid=(b, t // TQ),
        in_specs=[qt(D_ATTN), qt(D_ATTN), full(kcmp), full(vcmp), full(ks), full(vs), full(kw), full(vw),
                  qt(LANES), const(amat), const(emat), const(gn)],
        out_specs=qt(D_ATTN),
        out_shape=jax.ShapeDtypeStruct((b, t, D_ATTN), BF16),
        compiler_params=_cparams(("parallel", "arbitrary")),
        name="attn_prompt",
    )(q, qr, kcmp, vcmp, ks, vs, kw, vw, gt, amat, emat, gn)


S5_LB = D_SSM // LANES
S5_SW = LANES // SSM_CH * SSM_STATE


def _s5_body(u_ref, h0r_ref, h0i_ref, lr_ref, li_ref, bre_ref, bim_ref, c_ref, d_ref, gw_ref, gb_ref, gn_ref,
             y_ref, hr_ref, hi_ref, y_s, *, tc, bb):
    step = pl.program_id(0)
    m = tc * bb

    @pl.when(step == 0)
    def _():
        hr_ref[...] = h0r_ref[...]
        hi_ref[...] = h0i_ref[...]

    u = u_ref[...].reshape(m, D_SSM)
    ub = u.astype(BF16)
    for j in range(S5_LB):
        uj = ub[:, j * LANES:(j + 1) * LANES]
        sl = slice(j * S5_SW, (j + 1) * S5_SW)
        xr = _dot(uj, bre_ref[j]).reshape(tc, bb, S5_SW)
        xi = _dot(uj, bim_ref[j]).reshape(tc, bb, S5_SW)
        lr = jnp.broadcast_to(lr_ref[:, sl], (bb, S5_SW))
        li = jnp.broadcast_to(li_ref[:, sl], (bb, S5_SW))
        hr, hi = hr_ref[:, sl], hi_ref[:, sl]
        hrs, his = [], []
        for t in range(tc):
            hr, hi = lr * hr - li * hi + xr[t], lr * hi + li * hr + xi[t]
            hrs.append(hr)
            his.append(hi)
        hr_ref[:, sl] = hr
        hi_ref[:, sl] = hi
        hcat = jnp.concatenate([jnp.concatenate(hrs, axis=0), jnp.concatenate(his, axis=0)], axis=1).astype(BF16)
        y_s[:, j * LANES:(j + 1) * LANES] = _dot(hcat, c_ref[j])
    z = jax.nn.gelu(y_s[...] + d_ref[...] * u)
    out = z * jax.nn.sigmoid(_dot(z.astype(BF16), gw_ref[...]) + gb_ref[...])
    y_ref[...] = _rms(out, gn_ref[...]).astype(y_ref.dtype).reshape(tc, bb, D_SSM)


def _s5(u_tm, h0r, h0i, sp, tc):
    t, bb, _ = u_tm.shape
    const = lambda a: pl.BlockSpec(a.shape, lambda i: (0,) * a.ndim)
    args = (h0r, h0i, sp['lr'], sp['li'], sp['bre'], sp['bim'], sp['c'], sp['d'], sp['glu_w'], sp['glu_b'], sp['gn'])
    st = jax.ShapeDtypeStruct(h0r.shape, F32)
    return pl.pallas_call(
        functools.partial(_s5_body, tc=tc, bb=bb),
        grid=(t // tc,),
        in_specs=[pl.BlockSpec((tc, bb, D_SSM), lambda i: (i, 0, 0))] + [const(a) for a in args],
        out_specs=[pl.BlockSpec((tc, bb, D_SSM), lambda i: (i, 0, 0)), const(h0r), const(h0i)],
        out_shape=[jax.ShapeDtypeStruct((t, bb, D_SSM), F32), st, st],
        scratch_shapes=[pltpu.VMEM((tc * bb, D_SSM), F32)],
        compiler_params=_cparams(("arbitrary",)),
        name="s5",
    )(u_tm, *args)


def _s5_params(log_dt, a_re, a_im, b_re, b_im, c_re, c_im, d, glu_w, glu_b, gn):
    dt = jnp.exp(log_dt)[:, None]
    mag = jnp.exp(a_re * dt)
    lr, li = mag * jnp.cos(a_im * dt), mag * jnp.sin(a_im * dt)
    den = a_re * a_re + a_im * a_im
    inv_r, inv_i = a_re / den, -a_im / den
    nr, ni = lr - 1.0, li
    fr, fi = nr * inv_r - ni * inv_i, nr * inv_i + ni * inv_r
    bbr = fr[..., None] * b_re - fi[..., None] * b_im
    bbi = fr[..., None] * b_im + fi[..., None] * b_re
    gl = LANES // SSM_CH
    eye = jnp.eye(gl, dtype=F32)

    def bmat(bb):
        x = bb.reshape(S5_LB, gl, SSM_STATE, SSM_CH)
        return jnp.einsum('jgpc,gh->jgchp', x, eye).reshape(S5_LB, LANES, S5_SW).astype(BF16)

    def cmat(cc):
        x = cc.reshape(S5_LB, gl, SSM_CH, SSM_STATE)
        return jnp.einsum('jgcp,gh->jgphc', x, eye).reshape(S5_LB, S5_SW, LANES)

    cm = jnp.concatenate([cmat(c_re), -cmat(c_im)], axis=1).astype(BF16)
    return dict(lr=lr.reshape(1, -1), li=li.reshape(1, -1), bre=bmat(bbr), bim=bmat(bbi), c=cm,
                d=d.reshape(1, -1), glu_w=glu_w.astype(BF16), glu_b=glu_b[None, :], gn=gn[None, :])


def _outproj_body(x_ref, a_ref, s_ref, wa_ref, ws_ref, gn_ref, o_ref, *, norm_attn):
    a = a_ref[...]
    if norm_attn:
        a = _rms(a, gn_ref[...])
    o_ref[...] = x_ref[...] + _dot(a.astype(BF16), wa_ref[...]) + _dot(s_ref[...].astype(BF16), ws_ref[...])


def _outproj(x, a, s, wa, ws, gn, norm_attn):
    m = x.shape[0]
    tm = min(m, 1024)
    row = lambda w: pl.BlockSpec((tm, w), lambda i: (i, 0))
    const = lambda a_: pl.BlockSpec(a_.shape, lambda i: (0, 0))
    return pl.pallas_call(
        functools.partial(_outproj_body, norm_attn=norm_attn),
        grid=(m // tm,),
        in_specs=[row(D_MODEL), row(D_ATTN), row(D_SSM), const(wa), const(ws), const(gn)],
        out_specs=row(D_MODEL),
        out_shape=jax.ShapeDtypeStruct((m, D_MODEL), F32),
        compiler_params=_cparams(("parallel",)),
        name="outproj",
    )(x, a, s, wa, ws, gn)


N_PAGES = PAST_LEN // PAGE_SIZE
CMP_CHUNK = 32
CMP_PITCH = 24


def _page_copy(pool_ref, buf_ref, sem_ref, pt_ref, b, i, slot):
    return pltpu.make_async_copy(pool_ref.at[pt_ref[b * N_PAGES + i]], buf_ref.at[slot, i], sem_ref.at[slot])


def _cmp_paged_body(pt_ref, pool_ref, w1_ref, b_ref, w2_ref, o_ref, buf_ref, rows_ref, p_ref, sem_ref):
    b = pl.program_id(0)
    slot = b % 2

    def start_all(bb, sl):
        def body(i, c):
            _page_copy(pool_ref, buf_ref, sem_ref, pt_ref, bb, i, sl).start()
            return c
        lax.fori_loop(0, N_PAGES, body, 0, unroll=8)

    @pl.when(b == 0)
    def _():
        start_all(0, 0)

    @pl.when(b + 1 < pl.num_programs(0))
    def _():
        start_all(b + 1, 1 - slot)

    def wait_one(i, c):
        _page_copy(pool_ref, buf_ref, sem_ref, pt_ref, b, i, slot).wait()
        return c
    lax.fori_loop(0, N_PAGES, wait_one, 0, unroll=8)

    hp = PAGE_SIZE // CMP_STRIDE
    hc = CMP_CHUNK * hp
    lane = lax.broadcasted_iota(jnp.int32, (hc, LANES), 1)
    low = lane < HEAD_DIM

    def stage_a(c):
        for i in range(CMP_CHUNK):
            t = buf_ref[slot, c * CMP_CHUNK + i].T
            for h in range(hp):
                rows_ref[c % 2, (i * hp + h) * CMP_PITCH:(i * hp + h) * CMP_PITCH + CMP_STRIDE, :] = (
                    t[h * CMP_STRIDE:(h + 1) * CMP_STRIDE])

    def stage_b(c):
        xs = [rows_ref[c % 2, pl.ds(s, hc, stride=CMP_PITCH), :] for s in range(CMP_STRIDE)]
        heads = [[], []]
        for j in range(CMP_STRIDE // 2):
            a, b = xs[2 * j], xs[2 * j + 1]
            heads[0].append(jnp.where(low, a, pltpu.roll(b, HEAD_DIM, 1)))
            heads[1].append(jnp.where(low, pltpu.roll(a, HEAD_DIM, 1), b))
        x = jnp.concatenate([jnp.concatenate(hd, axis=1) for hd in heads], axis=0).astype(BF16)
        p = _dot(x, w1_ref[...])
        for k in range(N_KV_HEADS):
            p_ref[k, c * hc:(c + 1) * hc, :] = p[k * hc:(k + 1) * hc]

    n_chunks = N_PAGES // CMP_CHUNK
    stage_a(0)
    for c in range(n_chunks):
        if c + 1 < n_chunks:
            stage_a(c + 1)
        stage_b(c)

    n_half = N_PAGES * hp
    out = None
    for k in range(N_KV_HEADS):
        pre = p_ref[k, :, :CMP_HIDDEN] + pltpu.roll(p_ref[k, :, CMP_HIDDEN:], n_half - 1, 0) + b_ref[...]
        ok = _dot(jax.nn.gelu(pre).astype(BF16), w2_ref[k])
        out = ok if out is None else out + ok
    o_ref[0] = out


def _cmp_paged(page_table_flat, pool_t, w1, bias, w2, n_seq):
    n_half = PAST_LEN // CMP_STRIDE
    hc = CMP_CHUNK * PAGE_SIZE // CMP_STRIDE
    const = lambda a: pl.BlockSpec(a.shape, lambda i, pt: (0,) * a.ndim)
    return pl.pallas_call(
        _cmp_paged_body,
        grid_spec=pltpu.PrefetchScalarGridSpec(
            num_scalar_prefetch=1,
            grid=(n_seq,),
            in_specs=[pl.BlockSpec(memory_space=pl.ANY), const(w1), const(bias), const(w2)],
            out_specs=pl.BlockSpec((1, n_half, KV_W), lambda i, pt: (i, 0, 0)),
            scratch_shapes=[pltpu.VMEM((2, N_PAGES, PAGE_SIZE, KV_W), F32),
                            pltpu.VMEM((2, hc * CMP_PITCH, KV_W), F32),
                            pltpu.VMEM((N_KV_HEADS, n_half, CMP_RATIO * CMP_HIDDEN), F32),
                            pltpu.SemaphoreType.DMA((2,))]),
        out_shape=jax.ShapeDtypeStruct((n_seq, n_half, KV_W), F32),
        compiler_params=_cparams(("arbitrary",)),
        name="cmp_paged",
    )(page_table_flat, pool_t, w1, bias, w2)


def _cmp_weights_split(pos_emb, w1, b1, w2):
    w1r = w1.reshape(CMP_RATIO, CMP_STRIDE * HEAD_DIM, CMP_HIDDEN)
    w1s = w1r.transpose(1, 0, 2).reshape(CMP_STRIDE * HEAD_DIM, CMP_RATIO * CMP_HIDDEN)
    pe = pos_emb.reshape(CMP_RATIO, CMP_STRIDE * HEAD_DIM)
    bias = b1 + jnp.einsum('rx,rxh->h', pe, w1r, precision=lax.Precision.HIGHEST)
    z = jnp.zeros_like(w2)
    w2k = jnp.stack([jnp.concatenate([w2, z], axis=1), jnp.concatenate([z, w2], axis=1)])
    return w1s.astype(BF16), bias[None, :], w2k.astype(BF16)


CMPATTN_SEQS = 8


def _cmpattn_body(q_ref, kc_ref, vc_ref, amat_ref, o_ref, idx_ref, *, ts, n_cmp, n_sel):
    rows = GQA * ts
    qpos = PAST_LEN + lax.broadcasted_iota(jnp.int32, (rows, 1), 0) % ts
    n_pad = kc_ref.shape[1]
    n_io = lax.broadcasted_iota(jnp.int32, (1, n_pad), 1)
    m = (n_io * CMP_STRIDE + (CMP_LEN - 1) <= qpos) & (n_io < n_cmp)
    n_sb = q_ref.shape[0]
    imps = []
    for b in range(n_sb):
        kc = kc_ref[b].astype(BF16)
        vc = vc_ref[b].astype(BF16)
        for k in range(N_KV_HEADS):
            s = _dot_nt((q_ref[b, k] * ATTN_SCALE).astype(BF16), kc)
            p = jnp.where(m, _softmax_rows(jnp.where(m, s, NEG_INF)), 0.0)
            o_ref[b, k] = _dot(p.astype(BF16), vc)
            imp = p[0:ts]
            for g in range(1, GQA):
                imp = imp + p[g * ts:(g + 1) * ts]
            imps.append(imp)
    imp_all = jnp.concatenate(imps, axis=0)
    n_rows = imp_all.shape[0]
    sel_w = amat_ref.shape[1]
    imp_sel = jnp.dot(imp_all, amat_ref[...], precision=lax.Precision.HIGHEST, preferred_element_type=F32)
    j = lax.broadcasted_iota(jnp.int32, (n_rows, sel_w), 1)
    jq = (PAST_LEN + lax.broadcasted_iota(jnp.int32, (n_rows, sel_w), 0) % ts) // SEL_BLOCK
    forced = (j == 0) | (j == jq) | (j == jq - 1)
    jf = j.astype(F32)
    lane_k = lax.broadcasted_iota(jnp.int32, (n_rows, LANES), 1)
    score = jnp.where(j > jq, -1.0, jnp.where(forced, FORCE_SCORE, imp_sel))
    score = jnp.where(j < n_sel, score, -jnp.inf)
    res = jnp.zeros((n_rows, LANES), F32)
    for i in range(N_SELECT):
        mx = jnp.max(score, axis=-1, keepdims=True)
        ix = jnp.min(jnp.where(score == mx, jf, 1e9), axis=-1, keepdims=True)
        res = jnp.where(lane_k == i, ix, res)
        score = jnp.where(jf == ix, -jnp.inf, score)
    idx_ref[...] = res.astype(jnp.int32).reshape(idx_ref.shape)


def _cmpattn(q_pad, kcmp, vcmp, ts):
    n_seq = q_pad.shape[0]
    n_cmp = (PAST_LEN + ts) // CMP_STRIDE - CMP_RATIO + 1
    n_sel = -(-(PAST_LEN + ts) // SEL_BLOCK)
    sel_w = -(-n_sel // LANES) * LANES
    per = SEL_BLOCK // CMP_STRIDE
    n = np.arange(kcmp.shape[1])[:, None]
    j = np.arange(sel_w)[None, :]
    amat = jnp.asarray(((n >= per * j - (CMP_RATIO - 1)) & (n <= per * j + per - 1) & (j < n_sel)).astype(np.float32))
    rows = GQA * ts
    sb = CMPATTN_SEQS
    return pl.pallas_call(
        functools.partial(_cmpattn_body, ts=ts, n_cmp=n_cmp, n_sel=n_sel),
        grid=(n_seq // sb,),
        in_specs=[pl.BlockSpec((sb, N_KV_HEADS, rows, LANES), lambda i: (i, 0, 0, 0)),
                  pl.BlockSpec((sb,) + kcmp.shape[1:], lambda i: (i, 0, 0)),
                  pl.BlockSpec((sb,) + vcmp.shape[1:], lambda i: (i, 0, 0)),
                  pl.BlockSpec(amat.shape, lambda i: (0, 0))],
        out_specs=[pl.BlockSpec((sb, N_KV_HEADS, rows, LANES), lambda i: (i, 0, 0, 0)),
                   pl.BlockSpec((sb, N_KV_HEADS * ts, LANES), lambda i: (i, 0, 0))],
        out_shape=[jax.ShapeDtypeStruct((n_seq, N_KV_HEADS, rows, LANES), F32),
                   jax.ShapeDtypeStruct((n_seq, N_KV_HEADS * ts, LANES), jnp.int32)],
        compiler_params=_cparams(("parallel",)),
        name="cmpattn_sample",
    )(q_pad, kcmp, vcmp, amat)


def _slc_copies(pools, bufs, sems, pt_ref, idx_ref, step, r, slot, ts, for_wait):
    if for_wait:
        pg = 0
    else:
        blk = idx_ref[step * (ts * N_SELECT) + r]
        page = jnp.minimum(blk // (PAGE_SIZE // SEL_BLOCK), N_PAGES - 1)
        pg = pt_ref[(step // N_KV_HEADS) * N_PAGES + page]
    return [pltpu.make_async_copy(pool.at[pg], buf.at[slot, r], sem.at[slot])
            for pool, buf, sem in zip(pools, bufs, sems)]


def _tail_body(pt_ref, idx_ref, kpool_ref, vpool_ref, q_ref, ksn_ref, vsn_ref, kwn_ref, vwn_ref, wk_ref, wv_ref,
               oc_ref, gt_ref, o_ref, kbuf_ref, vbuf_ref, ksem_ref, vsem_ref, *, ts):
    step = pl.program_id(0) * N_KV_HEADS + pl.program_id(1)
    n_steps = pl.num_programs(0) * N_KV_HEADS
    slot = step % 2
    n_fetch = ts * N_SELECT

    pools, bufs, sems = (kpool_ref, vpool_ref), (kbuf_ref, vbuf_ref), (ksem_ref, vsem_ref)

    def start_all(st, sl):
        def body(r, c):
            for cp in _slc_copies(pools, bufs, sems, pt_ref, idx_ref, st, r, sl, ts, False):
                cp.start()
            return c
        lax.fori_loop(0, n_fetch, body, 0, unroll=8)

    @pl.when(step == 0)
    def _():
        start_all(0, 0)

    @pl.when(step + 1 < n_steps)
    def _():
        start_all(step + 1, 1 - slot)

    def wait_one(r, c):
        for cp in _slc_copies(pools, bufs, sems, pt_ref, idx_ref, step, r, slot, ts, True):
            cp.wait()
        return c
    lax.fori_loop(0, n_fetch, wait_one, 0, unroll=8)

    rows = GQA * ts
    q = (q_ref[0, 0] * ATTN_SCALE).astype(BF16)
    row_t = lax.broadcasted_iota(jnp.int32, (rows, 1), 0) % ts
    qpos = PAST_LEN + row_t
    lane = lax.broadcasted_iota(jnp.int32, (1, LANES), 1)
    lane_half, lane_off = lane // SEL_BLOCK, lane % SEL_BLOCK
    pad = jnp.zeros((LANES - ksn_ref.shape[1], LANES), F32)
    new_pos = PAST_LEN + lane
    new_ok = lane < ts
    blk_new = PAST_LEN // SEL_BLOCK

    def new_rows(ref):
        return jnp.concatenate([ref[0], pad], axis=0).astype(BF16)

    ksn, vsn = new_rows(ksn_ref), new_rows(vsn_ref)
    s_new = _dot_nt(q, ksn)
    o_s = jnp.zeros((rows, LANES), F32)
    grp = 4

    def pages(buf_ref, r0):
        return jnp.concatenate([buf_ref[slot, r0 + u].astype(BF16) for u in range(grp)], axis=1)

    for t in range(ts):
        pieces = []
        has_new = jnp.int32(0)
        for i0 in range(0, N_SELECT, grp):
            sc = _dot(q, pages(kbuf_ref, t * N_SELECT + i0))
            for u in range(grp):
                blk = idx_ref[step * n_fetch + t * N_SELECT + i0 + u]
                has_new = has_new | (blk == blk_new).astype(jnp.int32)
                want_half = jnp.where(blk < blk_new, blk % (PAGE_SIZE // SEL_BLOCK), -1)
                ok = (lane_half == want_half) & (blk * SEL_BLOCK + lane_off <= qpos)
                pieces.append(jnp.where(ok, sc[:, u * LANES:(u + 1) * LANES], NEG_INF))
        ok_new = (lane < jnp.where(has_new > 0, ts, 0)) & (new_pos <= qpos)
        pieces.append(jnp.where(ok_new, s_new, NEG_INF))
        p = _softmax_rows(jnp.concatenate(pieces, axis=1)).astype(BF16)
        o_t = _dot(p[:, N_SELECT * LANES:], vsn)
        for i0 in range(0, N_SELECT, grp):
            o_t = o_t + _dot_nt(p[:, i0 * LANES:(i0 + grp) * LANES], pages(vbuf_ref, t * N_SELECT + i0))
        o_s = jnp.where(row_t == t, o_t, o_s)

    wb = wk_ref.shape[2]
    kp = PAST_LEN - wb + lax.broadcasted_iota(jnp.int32, (1, wb), 1)
    ok_w = (kp >= 0) & (kp <= qpos) & (kp > qpos - WINDOW)
    ok_wn = new_ok & (new_pos <= qpos) & (new_pos > qpos - WINDOW)
    s_w = jnp.where(ok_w, _dot(q, wk_ref[0].astype(BF16)), NEG_INF)
    s_wn = jnp.where(ok_wn, _dot_nt(q, new_rows(kwn_ref)), NEG_INF)
    p = _softmax_rows(jnp.concatenate([s_w, s_wn], axis=1)).astype(BF16)
    o_w = _dot_nt(p[:, :wb], wv_ref[0].astype(BF16)) + _dot(p[:, wb:], new_rows(vwn_ref))

    gt = gt_ref[0, 0]
    o_ref[0, 0] = gt[:, 0:1] * oc_ref[0, 0] + gt[:, 1:2] * o_s + gt[:, 2:3] * o_w


def _tail(page_table_flat, idx_flat, kpool_t, vpool_t, q_pad, ksn, vsn, kwn, vwn, wk_t, wv_t, o_cmp, gates_pad, ts):
    n_seq = q_pad.shape[0]
    rows = GQA * ts
    n_fetch = ts * N_SELECT
    per_head = pl.BlockSpec((1, 1, rows, LANES), lambda i, k, pt, ix: (i, k, 0, 0))
    per_seq = lambda a: pl.BlockSpec((1,) + a.shape[1:], lambda i, k, pt, ix: (i, 0, 0))
    hbm = pl.BlockSpec(memory_space=pl.ANY)
    return pl.pallas_call(
        functools.partial(_tail_body, ts=ts),
        grid_spec=pltpu.PrefetchScalarGridSpec(
            num_scalar_prefetch=2,
            grid=(n_seq, N_KV_HEADS),
            in_specs=[hbm, hbm, per_head, per_seq(ksn), per_seq(vsn), per_seq(kwn), per_seq(vwn),
                      per_seq(wk_t), per_seq(wv_t), per_head, per_head],
            out_specs=per_head,
            scratch_shapes=[pltpu.VMEM((2, n_fetch, PAGE_SIZE, KV_W), F32), pltpu.VMEM((2, n_fetch, PAGE_SIZE, KV_W), F32),
                            pltpu.SemaphoreType.DMA((2,)), pltpu.SemaphoreType.DMA((2,))]),
        out_shape=jax.ShapeDtypeStruct((n_seq, N_KV_HEADS, rows, LANES), F32),
        compiler_params=_cparams(("arbitrary", "arbitrary")),
        name="slc_win_sample",
    )(page_table_flat, idx_flat, kpool_t, vpool_t, q_pad, ksn, vsn, kwn, vwn, wk_t, wv_t, o_cmp, gates_pad)


def _pad_heads(x, ts):
    n = x.shape[0] // ts
    w = x.shape[1] // N_HEADS
    return x.reshape(n, ts, N_KV_HEADS, GQA, w).transpose(0, 2, 3, 1, 4).reshape(n, N_KV_HEADS, GQA * ts, w)


def _q_on_kv_lanes(q, ts):
    x = _pad_heads(q, ts)
    z = jnp.zeros_like(x[:, 0])
    return jnp.stack([jnp.concatenate([x[:, 0], z], axis=-1), jnp.concatenate([z, x[:, 1]], axis=-1)], axis=1)


def _sample_attention_jnp(q, q_rot, kc, vc, ks, vs, kw, vw, gates, ck, cv, sk, sv, wk, wv, page_table, cw, gn):
    B, T = q.shape[:2]
    pos = PAST_LEN + jnp.arange(T, dtype=jnp.int32)
    q = q.reshape(B, T, N_HEADS, HEAD_DIM)
    q_rot = q_rot.reshape(B, T, N_HEADS, HEAD_DIM)
    kvr = lambda a: a.reshape(B, T, N_KV_HEADS, HEAD_DIM)
    kc, vc, ks, vs, kw, vw = map(kvr, (kc, vc, ks, vs, kw, vw))
    gates = gates[..., :3 * N_HEADS].reshape(B, T, N_HEADS, 3)
    past = lambda pool: pool[page_table].reshape(B, -1, N_KV_HEADS, HEAD_DIM)
    kc_all, vc_all = jnp.concatenate([past(ck), kc], axis=1), jnp.concatenate([past(cv), vc], axis=1)
    ks_all, vs_all = jnp.concatenate([past(sk), ks], axis=1), jnp.concatenate([past(sv), vs], axis=1)

    def compress_blocks(k, pos_emb, w1, b1, w2):
        Bk, Tk = k.shape[:2]
        nh = Tk // CMP_STRIDE
        nc = nh - CMP_RATIO + 1
        halves = k[:, :nh * CMP_STRIDE].reshape(Bk, nh, CMP_STRIDE, N_KV_HEADS, HEAD_DIM)
        pe = pos_emb.reshape(CMP_RATIO, CMP_STRIDE, HEAD_DIM)
        w1r = w1.reshape(CMP_RATIO, CMP_STRIDE, HEAD_DIM, CMP_HIDDEN)
        pre = b1
        for r in range(CMP_RATIO):
            pre = pre + jnp.einsum('bnskd,sdh->bnkh', halves[:, r:r + nc] + pe[r][:, None, :], w1r[r])
        return jax.nn.gelu(pre) @ w2

    k_cmp = compress_blocks(kc_all, *cw[0])
    v_cmp = compress_blocks(vc_all, *cw[1])
    nc = k_cmp.shape[1]
    qg = q.reshape(B, T, N_KV_HEADS, GQA, HEAD_DIM)
    s = jnp.einsum('bqkgd,bnkd->bqkgn', qg, k_cmp).astype(F32) * ATTN_SCALE
    blk_end = jnp.arange(nc) * CMP_STRIDE + CMP_LEN - 1
    m = (blk_end[None, :] <= pos[:, None])[None, :, None, None, :]
    p = jnp.where(m, jax.nn.softmax(jnp.where(m, s, NEG_INF), axis=-1), 0.0)
    o_cmp = jnp.einsum('bqkgn,bnkd->bqkgd', p, v_cmp).reshape(B, T, N_HEADS, HEAD_DIM)
    imp = p.sum(axis=3)
    n_sel = -(-(PAST_LEN + T) // SEL_BLOCK)
    per = SEL_BLOCK // CMP_STRIDE
    total = n_sel * per + CMP_RATIO + per
    pp = jnp.pad(imp, ((0, 0), (0, 0), (0, 0), (CMP_RATIO - 1, total - nc - (CMP_RATIO - 1))))
    impb = sum(pp[..., o:o + n_sel * per:per] for o in range(per + CMP_RATIO - 1))
    j = jnp.arange(n_sel)[None, None, None, :]
    jq = (pos // SEL_BLOCK)[None, :, None, None]
    forced = (j == 0) | (j == jq) | (j == jq - 1)
    score = jnp.where(j > jq, -1.0, jnp.where(forced, FORCE_SCORE, impb))
    _, idx = lax.top_k(score, min(N_SELECT, n_sel))
    valid = idx <= jq

    def to_blocks(k):
        kp = jnp.pad(k, ((0, 0), (0, n_sel * SEL_BLOCK - k.shape[1]), (0, 0), (0, 0)))
        return kp.reshape(B, n_sel, SEL_BLOCK, N_KV_HEADS, HEAD_DIM).transpose(0, 3, 1, 2, 4)

    kb, vb = to_blocks(ks_all), to_blocks(vs_all)
    qg = q_rot.reshape(B, T, N_KV_HEADS, GQA, HEAD_DIM)
    bi = jnp.arange(B)[:, None, None, None]
    hi = jnp.arange(N_KV_HEADS)[None, None, :, None]
    ksel, vsel = kb[bi, hi, idx], vb[bi, hi, idx]
    s = jnp.einsum('bqkgd,bqknsd->bqkgns', qg, ksel).astype(F32) * ATTN_SCALE
    kpos = idx[..., None] * SEL_BLOCK + jnp.arange(SEL_BLOCK)
    mask = valid[..., None] & (kpos <= pos[None, :, None, None, None])
    s = jnp.where(mask[:, :, :, None], s, NEG_INF)
    shp = s.shape
    p = jax.nn.softmax(s.reshape(shp[:-2] + (-1,)), axis=-1).reshape(shp)
    o_slc = jnp.einsum('bqkgns,bqknsd->bqkgd', p, vsel).reshape(B, T, N_HEADS, HEAD_DIM)
    wb = wk.shape[1]
    kw_all, vw_all = jnp.concatenate([wk, kw], axis=1), jnp.concatenate([wv, vw], axis=1)
    k_pos = PAST_LEN - wb + jnp.arange(wb + T, dtype=jnp.int32)
    s = jnp.einsum('bqkgd,bjkd->bqkgj', qg, kw_all).astype(F32) * ATTN_SCALE
    kp, qp = k_pos[None, :], pos[:, None]
    m = ((kp >= 0) & (kp <= qp) & (kp > qp - WINDOW))[None, :, None, None, :]
    p = jax.nn.softmax(jnp.where(m, s, NEG_INF), axis=-1)
    o_win = jnp.einsum('bqkgj,bjkd->bqkgd', p, vw_all).reshape(B, T, N_HEADS, HEAD_DIM)
    o = (gates[..., 0:1] * o_cmp + gates[..., 1:2] * o_slc + gates[..., 2:3] * o_win).reshape(B, T, D_ATTN)
    return _rms(o, gn).astype(BF16)


def kernel(x_prompt, x_sample, cache_cmp_k, cache_cmp_v, cache_slc_k, cache_slc_v, state_win_k, state_win_v, state_s5_re, state_s5_im, page_table, norm_ffn1, ffn1_gate, ffn1_up, ffn1_down, norm_mix, w_in, cmp_k_pos, cmp_k_w1, cmp_k_b1, cmp_k_w2, cmp_v_pos, cmp_v_w1, cmp_v_b1, cmp_v_w2, s5_log_dt, s5_a_re, s5_a_im, s5_b_re, s5_b_im, s5_c_re, s5_c_im, s5_d, s5_glu_w, s5_glu_b, norm_attn_out, norm_ssm_out, w_out, norm_ffn2, ffn2_gate, ffn2_up, ffn2_down, norm_final):
    depth = w_in.shape[0]
    assert depth == 1
    l = 0
    B, T, _ = x_prompt.shape
    BS, TS, _ = x_sample.shape
    row = lambda a: a[None, :]

    w = w_in[l]
    cut = D_ATTN + 6 * KV_W
    w_p = jnp.concatenate([w[:, :cut], w[:, cut + 3 * N_HEADS:], w[:, cut:cut + 3 * N_HEADS],
                           jnp.zeros((D_MODEL, LANES - 3 * N_HEADS), F32)], axis=1).astype(BF16)
    f1 = (row(norm_ffn1[l]), ffn1_gate[l].astype(BF16), ffn1_up[l].astype(BF16), ffn1_down[l].astype(BF16))
    f2 = (row(norm_ffn2[l]), ffn2_gate[l].astype(BF16), ffn2_up[l].astype(BF16), ffn2_down[l].astype(BF16))
    gfin = row(norm_final)
    wo = w_out[l].astype(BF16)
    wo_a, wo_s = wo[:D_ATTN], wo[D_ATTN:]
    cwk = _cmp_weights(cmp_k_pos[l], cmp_k_w1[l], cmp_k_b1[l], cmp_k_w2[l])
    cwv = _cmp_weights(cmp_v_pos[l], cmp_v_w1[l], cmp_v_b1[l], cmp_v_w2[l])
    sp = _s5_params(s5_log_dt[l], s5_a_re[l], s5_a_im[l], s5_b_re[l], s5_b_im[l], s5_c_re[l], s5_c_im[l],
                    s5_d[l], s5_glu_w[l], s5_glu_b[l], norm_ssm_out[l])
    gn_attn = row(norm_attn_out[l])
    g_mix = row(norm_mix[l])

    xp = x_prompt.reshape(B * T, D_MODEL)
    x1 = _ffn(xp, *f1, gfin, False)
    tabs = _rope_tables(jnp.arange(T, dtype=jnp.int32))
    (q, qr, kc, vc, _, _, _, _, gt, u,
     kc_t, vc_t, ks_t, vs_t, kw_t, vw_t) = _proj(x1, g_mix, w_p, *tabs, seq_t=T)
    b3 = lambda a: a.reshape(B, T, a.shape[-1])
    kcmp, vcmp = _cmp_prompt(b3(kc), b3(vc), cwk, cwv)
    attn = _attn_prompt(b3(q), b3(qr), kcmp, vcmp, ks_t, vs_t, kw_t, vw_t, b3(gt), gn_attn)
    zeros = jnp.zeros((B, SSM_GROUPS * SSM_STATE), F32)
    y_tm, p_re, p_im = _s5(b3(u).transpose(1, 0, 2), zeros, zeros, sp, 64)
    ssm = y_tm.transpose(1, 0, 2).reshape(B * T, D_SSM)
    x2 = _outproj(x1, attn.reshape(B * T, D_ATTN), ssm, wo_a, wo_s, gn_attn, False)
    y_prompt = _ffn(x2, *f2, gfin, True).reshape(B, T, D_MODEL)
    from_t = lambda a: a.reshape(a.shape[0], N_KV_HEADS, HEAD_DIM, a.shape[2]).transpose(0, 3, 1, 2)[None]
    wb = min(WINDOW, T)
    p_state = (from_t(kc_t), from_t(vc_t), from_t(ks_t), from_t(vs_t),
               from_t(kw_t[:, :, T - wb:]), from_t(vw_t[:, :, T - wb:]),
               p_re.reshape(1, B, SSM_GROUPS, SSM_STATE), p_im.reshape(1, B, SSM_GROUPS, SSM_STATE))

    xs = x_sample.reshape(BS * TS, D_MODEL)
    s1 = _ffn(xs, *f1, gfin, False)
    pos_s = PAST_LEN + jnp.arange(TS, dtype=jnp.int32)
    tabs_s = [jnp.tile(a, (BS, 1)) for a in _rope_tables(pos_s)]
    q, qr, kc, vc, ks, vs, kw, vw, gt, u = _proj(s1, g_mix, w_p, *tabs_s)
    s3 = lambda a: a.reshape(BS, TS, a.shape[-1])
    to_t = lambda a: a.transpose(0, 2, 3, 1).reshape(a.shape[0], KV_W, a.shape[1])
    pt_flat = page_table.reshape(-1)
    kcmp_s = _cmp_paged(pt_flat, to_t(cache_cmp_k[l]),
                        *_cmp_weights_split(cmp_k_pos[l], cmp_k_w1[l], cmp_k_b1[l], cmp_k_w2[l]), BS)
    vcmp_s = _cmp_paged(pt_flat, to_t(cache_cmp_v[l]),
                        *_cmp_weights_split(cmp_v_pos[l], cmp_v_w1[l], cmp_v_b1[l], cmp_v_w2[l]), BS)
    o_cmp_s, idx = _cmpattn(_q_on_kv_lanes(q, TS), kcmp_s, vcmp_s, TS)
    pad8 = lambda a: jnp.pad(s3(a), ((0, 0), (0, 8 - TS), (0, 0)))
    gates_s = _pad_heads(gt[:, :3 * N_HEADS], TS)
    gates_s = jnp.pad(gates_s, ((0, 0), (0, 0), (0, 0), (0, LANES - 3)))
    wk_t, wv_t = to_t(state_win_k[l]), to_t(state_win_v[l])
    comb = _tail(pt_flat, idx[:, :, :N_SELECT].reshape(-1), to_t(cache_slc_k[l]), to_t(cache_slc_v[l]),
                 _q_on_kv_lanes(qr, TS), pad8(ks), pad8(vs), pad8(kw), pad8(vw), wk_t, wv_t, o_cmp_s, gates_s, TS)
    comb = comb.reshape(BS, N_KV_HEADS, GQA, TS, N_KV_HEADS, HEAD_DIM)
    comb = jnp.stack([comb[:, k, :, :, k] for k in range(N_KV_HEADS)], axis=1)
    attn_s = comb.transpose(0, 3, 1, 2, 4).reshape(BS * TS, D_ATTN)
    y_tm, s_re, s_im = _s5(s3(u).transpose(1, 0, 2), state_s5_re[l].reshape(BS, -1), state_s5_im[l].reshape(BS, -1), sp, TS)
    ssm_s = y_tm.transpose(1, 0, 2).reshape(BS * TS, D_SSM)
    s2 = _outproj(s1, attn_s, ssm_s, wo_a, wo_s, gn_attn, True)
    y_sample = _ffn(s2, *f2, gfin, True).reshape(BS, TS, D_MODEL)
    kv5 = lambda a, n: a.reshape(1, n, -1, N_KV_HEADS, HEAD_DIM)
    win_k = from_t(jnp.concatenate([wk_t[:, :, TS:], s3(kw).transpose(0, 2, 1)], axis=2))
    win_v = from_t(jnp.concatenate([wv_t[:, :, TS:], s3(vw).transpose(0, 2, 1)], axis=2))
    s_state = (kv5(kc, BS), kv5(vc, BS), kv5(ks, BS), kv5(vs, BS), win_k, win_v,
               s_re.reshape(1, BS, SSM_GROUPS, SSM_STATE), s_im.reshape(1, BS, SSM_GROUPS, SSM_STATE))

    return (y_prompt, y_sample) + p_state + s_state
```

```python
import functools
import math

import numpy as np
import jax
import jax.numpy as jnp
from jax import lax
from jax.experimental import pallas as pl
from jax.experimental.pallas import tpu as pltpu

F32 = jnp.float32
BF16 = jnp.bfloat16

D_MODEL = 1024
PAST_LEN = 16384
PAGE_SIZE = 128
D_ATTN = 512
D_SSM = 512
HEAD_DIM = 64
N_HEADS = 8
N_KV_HEADS = 2
GQA = 4
KV_W = 128
ROT_DIM = 16
ROPE_THETA = 500000.0
ATTN_SCALE = HEAD_DIM ** -0.5
CMP_LEN = 32
CMP_STRIDE = 16
CMP_RATIO = 2
CMP_HIDDEN = 256
SEL_BLOCK = 64
N_SELECT = 16
WINDOW = 512
SSM_CH = 16
SSM_GROUPS = 32
SSM_STATE = 64
D_FF = 2816
RMS_EPS = 1e-6
NEG_INF = -1e30
FORCE_SCORE = 1e4

LANES = 128
VMEM_LIMIT = 56 * 1024 * 1024

Z_Q = 0
Z_KV = D_ATTN
Z_U = D_ATTN + 6 * KV_W
Z_G = Z_U + D_SSM
Z_W = Z_G + LANES


def _cparams(sem):
    return pltpu.CompilerParams(dimension_semantics=sem, vmem_limit_bytes=VMEM_LIMIT)


def _rms(x, g):
    return x * lax.rsqrt(jnp.mean(x * x, axis=-1, keepdims=True) + RMS_EPS) * g


def _dot(a, b):
    return jnp.dot(a, b, preferred_element_type=F32)


def _dot_nt(a, b):
    return lax.dot_general(a, b, (((1,), (1,)), ((), ())), preferred_element_type=F32)


def _ffn_body(x_ref, g_ref, wg_ref, wu_ref, wd_ref, gf_ref, o_ref, *, final_norm):
    x = x_ref[...]
    h = _rms(x, g_ref[...]).astype(BF16)
    a = _dot(h, wg_ref[...])
    b = _dot(h, wu_ref[...])
    t = (a * jax.nn.sigmoid(a) * b).astype(BF16)
    y = x + 0.5 * _dot(t, wd_ref[...])
    if final_norm:
        y = _rms(y, gf_ref[...])
    o_ref[...] = y


def _ffn(x, g, wg, wu, wd, gf, final_norm):
    m = x.shape[0]
    tm = min(m, 512)
    row = pl.BlockSpec((tm, D_MODEL), lambda i: (i, 0))
    vec = pl.BlockSpec((1, D_MODEL), lambda i: (0, 0))
    resident = lambda a: pl.BlockSpec(a.shape, lambda i: (0, 0), pipeline_mode=pl.Buffered(1))
    return pl.pallas_call(
        functools.partial(_ffn_body, final_norm=final_norm),
        grid=(m // tm,),
        in_specs=[row, vec, resident(wg), resident(wu), resident(wd), vec],
        out_specs=row,
        out_shape=jax.ShapeDtypeStruct((m, D_MODEL), F32),
        compiler_params=_cparams(("parallel",)),
        name="ffn",
    )(x, g, wg, wu, wd, gf)


def _proj_body(x_ref, g_ref, w_ref, c_ref, sa_ref, sb_ref,
               q_ref, qr_ref, kc_ref, vc_ref, ks_ref, vs_ref, kw_ref, vw_ref, gt_ref, u_ref, *t_refs):
    h = _rms(x_ref[...], g_ref[...]).astype(BF16)
    z = _dot(h, w_ref[...])
    c, sa, sb = c_ref[...], sa_ref[...], sb_ref[...]

    def rope(v):
        return v * c + pltpu.roll(v, LANES - ROT_DIM // 2, 1) * sa + pltpu.roll(v, ROT_DIM // 2, 1) * sb

    q_ref[...] = z[:, Z_Q:Z_Q + D_ATTN]
    for i in range(D_ATTN // LANES):
        qr_ref[:, i * LANES:(i + 1) * LANES] = rope(z[:, Z_Q + i * LANES:Z_Q + (i + 1) * LANES])
    kv = [z[:, Z_KV + i * KV_W:Z_KV + (i + 1) * KV_W] for i in range(6)]
    kv[2] = rope(kv[2])
    kv[4] = rope(kv[4])
    for ref, v in zip((kc_ref, vc_ref, ks_ref, vs_ref, kw_ref, vw_ref), kv):
        ref[...] = v
    for ref, v in zip(t_refs, kv):
        ref[0] = v.T
    u_ref[...] = z[:, Z_U:Z_U + D_SSM]
    gt_ref[...] = jax.nn.sigmoid(z[:, Z_G:Z_G + LANES])


def _proj(x, g, w, rope_c, rope_sa, rope_sb, seq_t=None):
    m = x.shape[0]
    tm = min(m, 512)
    n_rep = rope_c.shape[0] // tm
    row = lambda i: (i, 0)
    const = lambda i: (0, 0)
    tab = lambda i: (i % n_rep, 0)
    widths = [D_ATTN, D_ATTN] + [KV_W] * 6 + [LANES, D_SSM]
    out_specs = [pl.BlockSpec((tm, wd), row) for wd in widths]
    out_shape = [jax.ShapeDtypeStruct((m, wd), F32) for wd in widths]
    if seq_t is not None:
        assert rope_c.shape[0] == seq_t
        out_specs += [pl.BlockSpec((1, KV_W, tm), lambda i: (i // n_rep, 0, i % n_rep))] * 6
        out_shape += [jax.ShapeDtypeStruct((m // seq_t, KV_W, seq_t), F32)] * 6
    return pl.pallas_call(
        _proj_body,
        grid=(m // tm,),
        in_specs=[pl.BlockSpec((tm, D_MODEL), row), pl.BlockSpec((1, D_MODEL), const),
                  pl.BlockSpec((D_MODEL, Z_W), const),
                  pl.BlockSpec((tm, LANES), tab), pl.BlockSpec((tm, LANES), tab), pl.BlockSpec((tm, LANES), tab)],
        out_specs=out_specs,
        out_shape=out_shape,
        compiler_params=_cparams(("parallel",)),
        name="proj",
    )(x, g, w, rope_c, rope_sa, rope_sb)


def _rope_tables(pos):
    half = ROT_DIM // 2
    inv_freq = jnp.power(ROPE_THETA, -(jnp.arange(half, dtype=F32) * 2.0 / ROT_DIM))
    ang = pos.astype(F32)[:, None] * inv_freq[None, :]
    cos, sin = jnp.cos(ang), jnp.sin(ang)
    n = pos.shape[0]
    one = jnp.ones((n, HEAD_DIM - ROT_DIM), F32)
    zero = jnp.zeros((n, HEAD_DIM - ROT_DIM), F32)
    zh = jnp.zeros((n, half), F32)
    c = jnp.concatenate([cos, cos, one], axis=1)
    sa = jnp.concatenate([-sin, zh, zero], axis=1)
    sb = jnp.concatenate([zh, sin, zero], axis=1)
    t2 = lambda a: jnp.concatenate([a, a], axis=1)
    return t2(c), t2(sa), t2(sb)


def _per_head_halves(xs):
    low = lax.broadcasted_iota(jnp.int32, xs[0].shape, 1) < HEAD_DIM
    heads = [[], []]
    for j in range(CMP_STRIDE // 2):
        a, b = xs[2 * j], xs[2 * j + 1]
        heads[0].append(jnp.where(low, a, pltpu.roll(b, HEAD_DIM, 1)))
        heads[1].append(jnp.where(low, pltpu.roll(a, HEAD_DIM, 1), b))
    return jnp.concatenate([jnp.concatenate(hd, axis=1) for hd in heads], axis=0).astype(BF16)


def _compress_tail(p_k, n_half, b_ref, w2_ref):
    out = None
    for k in range(N_KV_HEADS):
        pre = p_k[k][:, :CMP_HIDDEN] + pltpu.roll(p_k[k][:, CMP_HIDDEN:], n_half - 1, 0) + b_ref[...]
        ok = _dot(jax.nn.gelu(pre).astype(BF16), w2_ref[k])
        out = ok if out is None else out + ok
    return out


def _compress_rows(x_ref, n_half, w1_ref, b_ref, w2_ref):
    xs = [x_ref[0, pl.ds(s, n_half, stride=CMP_STRIDE), :] for s in range(CMP_STRIDE)]
    p = _dot(_per_head_halves(xs), w1_ref[...])
    return _compress_tail([p[k * n_half:(k + 1) * n_half] for k in range(N_KV_HEADS)], n_half, b_ref, w2_ref)


def _cmp_prompt_body(kc_ref, vc_ref, wk1_ref, bk_ref, wk2_ref, wv1_ref, bv_ref, wv2_ref, ko_ref, vo_ref, *, n_half):
    ko_ref[0] = _compress_rows(kc_ref, n_half, wk1_ref, bk_ref, wk2_ref)
    vo_ref[0] = _compress_rows(vc_ref, n_half, wv1_ref, bv_ref, wv2_ref)


def _cmp_prompt(kc, vc, wk, wv):
    b, t, _ = kc.shape
    n_half = t // CMP_STRIDE
    seq = pl.BlockSpec((1, t, KV_W), lambda i: (i, 0, 0))
    const = lambda shp: pl.BlockSpec(shp, lambda i: (0,) * len(shp))
    wspecs = [const(wk[0].shape), const(wk[1].shape), const(wk[2].shape)]
    out = pl.BlockSpec((1, n_half, KV_W), lambda i: (i, 0, 0))
    return pl.pallas_call(
        functools.partial(_cmp_prompt_body, n_half=n_half),
        grid=(b,),
        in_specs=[seq, seq] + wspecs + wspecs,
        out_specs=[out, out],
        out_shape=[jax.ShapeDtypeStruct((b, n_half, KV_W), F32)] * 2,
        compiler_params=_cparams(("parallel",)),
        name="cmp_prompt",
    )(kc, vc, *wk, *wv)


def _cmp_weights(pos_emb, w1, b1, w2):
    w1r = w1.reshape(CMP_RATIO, CMP_STRIDE * HEAD_DIM, CMP_HIDDEN)
    w1s = w1r.transpose(1, 0, 2).reshape(CMP_STRIDE * HEAD_DIM, CMP_RATIO * CMP_HIDDEN)
    pe = pos_emb.reshape(CMP_RATIO, CMP_STRIDE * HEAD_DIM)
    bias = b1 + jnp.einsum('rx,rxh->h', pe, w1r, precision=lax.Precision.HIGHEST)
    z = jnp.zeros_like(w2)
    w2k = jnp.stack([jnp.concatenate([w2, z], axis=1), jnp.concatenate([z, w2], axis=1)])
    return w1s.astype(BF16), bias[None, :], w2k.astype(BF16)


TQ = 512
KCHUNK = 512
SUBQ = 256
N_SEL_PROMPT = 32
assert TQ == KCHUNK and TQ % SUBQ == 0


def _softmax_rows(s):
    mx = jnp.max(s, axis=-1, keepdims=True)
    e = jnp.exp(s - mx)
    return e / jnp.sum(e, axis=-1, keepdims=True)


def _attn_prompt_body(q_ref, qr_ref, kcmp_ref, vcmp_ref, ks_ref, vs_ref, kw_ref, vw_ref, gt_ref,
                      amat_ref, emat_ref, gn_ref, o_ref, *, seq):
    t0 = pl.program_id(1) * TQ
    lane = lax.broadcasted_iota(jnp.int32, (TQ, LANES), 1)
    low = lane < HEAD_DIM
    tq = t0 + lax.broadcasted_iota(jnp.int32, (TQ, 1), 0)
    gates = gt_ref[0]

    def head_rows(src_ref, k):
        rows = []
        for g in range(GQA):
            h = GQA * k + g
            v = src_ref[0, :, (h // 2) * LANES:(h // 2 + 1) * LANES]
            if h % 2 != k:
                v = pltpu.roll(v, HEAD_DIM, 1)
            rows.append(jnp.where(low if k == 0 else jnp.logical_not(low), v * ATTN_SCALE, 0.0))
        return jnp.concatenate(rows, axis=0).astype(BF16)

    jj = lax.broadcasted_iota(jnp.int32, (N_SEL_PROMPT, TQ), 0)
    jq = (t0 + lax.broadcasted_iota(jnp.int32, (N_SEL_PROMPT, TQ), 1)) // SEL_BLOCK
    forced = (jj == 0) | (jj == jq) | (jj == jq - 1)

    n_cmp = kcmp_ref.shape[1]
    blk_end = lax.broadcasted_iota(jnp.int32, (1, n_cmp), 1) * CMP_STRIDE + (CMP_LEN - 1)
    m_cmp = (blk_end <= tq)[None]

    n_chunks = t0 // KCHUNK + 1

    comb, q_augs, q_rots, o_cs = [], [], [], []
    for k in range(N_KV_HEADS):
        qc = head_rows(q_ref, k)
        qr = head_rows(qr_ref, k)

        s = _dot_nt(qc, kcmp_ref[0].astype(BF16)).reshape(GQA, TQ, n_cmp)
        p = jnp.where(m_cmp, _softmax_rows(jnp.where(m_cmp, s, NEG_INF)), 0.0)
        o_c = _dot(p.reshape(GQA * TQ, n_cmp).astype(BF16), vcmp_ref[0].astype(BF16))
        imp = jnp.sum(p, axis=0)

        imp_sel = jnp.dot(amat_ref[...], imp.T, precision=lax.Precision.HIGHEST, preferred_element_type=F32)
        score = jnp.where(jj > jq, -1.0, jnp.where(forced, FORCE_SCORE, imp_sel))
        rank = jnp.zeros((N_SEL_PROMPT, TQ), F32)
        for i in range(N_SEL_PROMPT):
            si = score[i:i + 1, :]
            beats = (si > score) | ((si == score) & (jj > i))
            rank = rank + jnp.where(beats, 1.0, 0.0)
        sel_t = jnp.where((rank < N_SELECT) & (jj <= jq), 1.0, 0.0)
        sel_t = jnp.concatenate([sel_t, jnp.zeros((LANES - N_SEL_PROMPT, TQ), F32)], axis=0)
        sel = sel_t.T

        bias = ((sel - 1.0) * (-NEG_INF)).astype(BF16)
        q_augs.append(jnp.concatenate([qr, jnp.concatenate([bias] * GQA, axis=0)], axis=1))
        q_rots.append(qr)
        o_cs.append(o_c.reshape(GQA, TQ, LANES))

    sum_lanes = [HEAD_DIM * (1 - k) for k in range(N_KV_HEADS)]
    row_id = lax.broadcasted_iota(jnp.int32, (LANES, 1), 0)

    def attend(q2d, kk, vv, k, mask, m_i, acc):
        n_q, n_k = q2d.shape[0] // GQA, kk.shape[1]
        sc = _dot(q2d, kk).reshape(GQA, n_q, n_k)
        if mask is not None:
            sc = jnp.where(mask[None], sc, NEG_INF)
        m_n = jnp.max(sc, axis=-1, keepdims=True)
        if m_i is not None:
            m_n = jnp.maximum(m_i, m_n)
        pe = jnp.exp((sc - m_n).astype(BF16))
        vk = jnp.where(row_id == sum_lanes[k], 1.0, vv).astype(BF16)
        pv = _dot_nt(pe.reshape(GQA * n_q, n_k), vk).reshape(GQA, n_q, LANES)
        if m_i is not None:
            pv = jnp.exp(m_i - m_n) * acc + pv
        return m_n, pv

    def sub_rows(x2d, s):
        w = x2d.shape[1]
        return x2d.reshape(GQA, TQ, w)[:, s * SUBQ:(s + 1) * SUBQ].reshape(GQA * SUBQ, w)

    def chunk(c, carry):
        k0 = pl.multiple_of(c * KCHUNK, KCHUNK)
        kk = jnp.concatenate([ks_ref[0, :, pl.ds(k0, KCHUNK)].astype(BF16),
                              emat_ref[:, pl.ds(k0, KCHUNK)]], axis=0)
        vv = vs_ref[0, :, pl.ds(k0, KCHUNK)].astype(BF16)
        return tuple(attend(q_augs[k], kk, vv, k, None, *carry[k]) for k in range(N_KV_HEADS))

    init = tuple((jnp.full((GQA, TQ, 1), NEG_INF, F32), jnp.zeros((GQA, TQ, LANES), F32)) for _ in range(N_KV_HEADS))
    carry = lax.fori_loop(0, n_chunks - 1, chunk, init)

    acc_ss = [[] for _ in range(N_KV_HEADS)]
    acc_ws = [[] for _ in range(N_KV_HEADS)]
    w_sub = WINDOW + SUBQ
    for s in range(TQ // SUBQ):
        tq_s = t0 + s * SUBQ + lax.broadcasted_iota(jnp.int32, (SUBQ, 1), 0)
        n_k = (s + 1) * SUBQ
        k0 = pl.multiple_of(t0, TQ)
        kk = jnp.concatenate([ks_ref[0, :, pl.ds(k0, n_k)].astype(BF16), emat_ref[:, pl.ds(k0, n_k)]], axis=0)
        vv = vs_ref[0, :, pl.ds(k0, n_k)].astype(BF16)
        causal = (k0 + lax.broadcasted_iota(jnp.int32, (1, n_k), 1)) <= tq_s
        ws = pl.multiple_of(jnp.clip(t0 + s * SUBQ - WINDOW, 0, seq - w_sub), SUBQ)
        kw = kw_ref[0, :, pl.ds(ws, w_sub)].astype(BF16)
        vw = vw_ref[0, :, pl.ds(ws, w_sub)].astype(BF16)
        kp_w = ws + lax.broadcasted_iota(jnp.int32, (1, w_sub), 1)
        in_win = (kp_w <= tq_s) & (kp_w > tq_s - WINDOW)
        for k in range(N_KV_HEADS):
            m_i, acc = carry[k]
            rows = slice(s * SUBQ, (s + 1) * SUBQ)
            acc_ss[k].append(attend(sub_rows(q_augs[k], s), kk, vv, k, causal, m_i[:, rows], acc[:, rows])[1])
            acc_ws[k].append(attend(sub_rows(q_rots[k], s), kw, vw, k, in_win, None, None)[1])

    for k in range(N_KV_HEADS):
        acc_s, acc_w = jnp.concatenate(acc_ss[k], axis=1), jnp.concatenate(acc_ws[k], axis=1)
        sl = slice(sum_lanes[k], sum_lanes[k] + 1)
        for g in range(GQA):
            h = GQA * k + g
            g_s = gates[:, 3 * h + 1:3 * h + 2] / acc_s[g][:, sl]
            g_w = gates[:, 3 * h + 2:3 * h + 3] / acc_w[g][:, sl]
            comb.append(gates[:, 3 * h:3 * h + 1] * o_cs[k][g] + g_s * acc_s[g] + g_w * acc_w[g])

    cols = []
    for pr in range(N_HEADS // 2):
        k = (2 * pr) // GQA
        a, b = comb[2 * pr], comb[2 * pr + 1]
        a = pltpu.roll(a, HEAD_DIM, 1) if k == 1 else a
        b = pltpu.roll(b, HEAD_DIM, 1) if k == 0 else b
        cols.append(jnp.where(low, a, b))
    o = jnp.concatenate(cols, axis=1)
    o_ref[0] = _rms(o, gn_ref[...]).astype(o_ref.dtype)


def _sel_matrices(n_cmp_pad, n_sel, n_keys):
    j = np.arange(n_sel)[:, None]
    n = np.arange(n_cmp_pad)[None, :]
    per = SEL_BLOCK // CMP_STRIDE
    amat = ((n >= per * j - (CMP_RATIO - 1)) & (n <= per * j + per - 1)).astype(np.float32)
    jrow = np.arange(LANES)[:, None]
    key = np.arange(n_keys)[None, :]
    emat = (key // SEL_BLOCK == jrow).astype(np.float32)
    return jnp.asarray(amat), jnp.asarray(emat, dtype=BF16)


def _attn_prompt(q, qr, kcmp, vcmp, ks, vs, kw, vw, gt, gn):
    b, t, _ = q.shape
    amat, emat = _sel_matrices(kcmp.shape[1], N_SEL_PROMPT, t)
    qt = lambda w: pl.BlockSpec((1, TQ, w), lambda i, j: (i, j, 0))
    full = lambda a: pl.BlockSpec((1,) + a.shape[1:], lambda i, j: (i, 0, 0))
    const = lambda a: pl.BlockSpec(a.shape, lambda i, j: (0, 0))
    return pl.pallas_call(
        functools.partial(_attn_prompt_body, seq=t),
        grid=(b, t // TQ),
        in_specs=[qt(D_ATTN), qt(D_ATTN), full(kcmp), full(vcmp), full(ks), full(vs), full(kw), full(vw),
                  qt(LANES), const(amat), const(emat), const(gn)],
        out_specs=qt(D_ATTN),
        out_shape=jax.ShapeDtypeStruct((b, t, D_ATTN), BF16),
        compiler_params=_cparams(("parallel", "arbitrary")),
        name="attn_prompt",
    )(q, qr, kcmp, vcmp, ks, vs, kw, vw, gt, amat, emat, gn)


S5_LB = D_SSM // LANES
S5_SW = LANES // SSM_CH * SSM_STATE


def _s5_body(u_ref, h0r_ref, h0i_ref, lr_ref, li_ref, bre_ref, bim_ref, c_ref, d_ref, gw_ref, gb_ref, gn_ref,
             y_ref, hr_ref, hi_ref, y_s, *, tc, bb):
    step = pl.program_id(0)
    m = tc * bb

    @pl.when(step == 0)
    def _():
        hr_ref[...] = h0r_ref[...]
        hi_ref[...] = h0i_ref[...]

    u = u_ref[...].reshape(m, D_SSM)
    ub = u.astype(BF16)
    for j in range(S5_LB):
        uj = ub[:, j * LANES:(j + 1) * LANES]
        sl = slice(j * S5_SW, (j + 1) * S5_SW)
        xr = _dot(uj, bre_ref[j]).reshape(tc, bb, S5_SW)
        xi = _dot(uj, bim_ref[j]).reshape(tc, bb, S5_SW)
        lr = jnp.broadcast_to(lr_ref[:, sl], (bb, S5_SW))
        li = jnp.broadcast_to(li_ref[:, sl], (bb, S5_SW))
        hr, hi = hr_ref[:, sl], hi_ref[:, sl]
        hrs, his = [], []
        for t in range(tc):
            hr, hi = lr * hr - li * hi + xr[t], lr * hi + li * hr + xi[t]
            hrs.append(hr)
            his.append(hi)
        hr_ref[:, sl] = hr
        hi_ref[:, sl] = hi
        hcat = jnp.concatenate([jnp.concatenate(hrs, axis=0), jnp.concatenate(his, axis=0)], axis=1).astype(BF16)
        y_s[:, j * LANES:(j + 1) * LANES] = _dot(hcat, c_ref[j])
    z = jax.nn.gelu(y_s[...] + d_ref[...] * u)
    out = z * jax.nn.sigmoid(_dot(z.astype(BF16), gw_ref[...]) + gb_ref[...])
    y_ref[...] = _rms(out, gn_ref[...]).astype(y_ref.dtype).reshape(tc, bb, D_SSM)


def _s5(u_tm, h0r, h0i, sp, tc):
    t, bb, _ = u_tm.shape
    const = lambda a: pl.BlockSpec(a.shape, lambda i: (0,) * a.ndim)
    args = (h0r, h0i, sp['lr'], sp['li'], sp['bre'], sp['bim'], sp['c'], sp['d'], sp['glu_w'], sp['glu_b'], sp['gn'])
    st = jax.ShapeDtypeStruct(h0r.shape, F32)
    return pl.pallas_call(
        functools.partial(_s5_body, tc=tc, bb=bb),
        grid=(t // tc,),
        in_specs=[pl.BlockSpec((tc, bb, D_SSM), lambda i: (i, 0, 0))] + [const(a) for a in args],
        out_specs=[pl.BlockSpec((tc, bb, D_SSM), lambda i: (i, 0, 0)), const(h0r), const(h0i)],
        out_shape=[jax.ShapeDtypeStruct((t, bb, D_SSM), F32), st, st],
        scratch_shapes=[pltpu.VMEM((tc * bb, D_SSM), F32)],
        compiler_params=_cparams(("arbitrary",)),
        name="s5",
    )(u_tm, *args)


def _s5_params(log_dt, a_re, a_im, b_re, b_im, c_re, c_im, d, glu_w, glu_b, gn):
    dt = jnp.exp(log_dt)[:, None]
    mag = jnp.exp(a_re * dt)
    lr, li = mag * jnp.cos(a_im * dt), mag * jnp.sin(a_im * dt)
    den = a_re * a_re + a_im * a_im
    inv_r, inv_i = a_re / den, -a_im / den
    nr, ni = lr - 1.0, li
    fr, fi = nr * inv_r - ni * inv_i, nr * inv_i + ni * inv_r
    bbr = fr[..., None] * b_re - fi[..., None] * b_im
    bbi = fr[..., None] * b_im + fi[..., None] * b_re
    gl = LANES // SSM_CH
    eye = jnp.eye(gl, dtype=F32)

    def bmat(bb):
        x = bb.reshape(S5_LB, gl, SSM_STATE, SSM_CH)
        return jnp.einsum('jgpc,gh->jgchp', x, eye).reshape(S5_LB, LANES, S5_SW).astype(BF16)

    def cmat(cc):
        x = cc.reshape(S5_LB, gl, SSM_CH, SSM_STATE)
        return jnp.einsum('jgcp,gh->jgphc', x, eye).reshape(S5_LB, S5_SW, LANES)

    cm = jnp.concatenate([cmat(c_re), -cmat(c_im)], axis=1).astype(BF16)
    return dict(lr=lr.reshape(1, -1), li=li.reshape(1, -1), bre=bmat(bbr), bim=bmat(bbi), c=cm,
                d=d.reshape(1, -1), glu_w=glu_w.astype(BF16), glu_b=glu_b[None, :], gn=gn[None, :])


def _outproj_body(x_ref, a_ref, s_ref, wa_ref, ws_ref, gn_ref, o_ref, *, norm_attn):
    a = a_ref[...]
    if norm_attn:
        a = _rms(a, gn_ref[...])
    o_ref[...] = x_ref[...] + _dot(a.astype(BF16), wa_ref[...]) + _dot(s_ref[...].astype(BF16), ws_ref[...])


def _outproj(x, a, s, wa, ws, gn, norm_attn):
    m = x.shape[0]
    tm = min(m, 1024)
    row = lambda w: pl.BlockSpec((tm, w), lambda i: (i, 0))
    const = lambda a_: pl.BlockSpec(a_.shape, lambda i: (0, 0))
    return pl.pallas_call(
        functools.partial(_outproj_body, norm_attn=norm_attn),
        grid=(m // tm,),
        in_specs=[row(D_MODEL), row(D_ATTN), row(D_SSM), const(wa), const(ws), const(gn)],
        out_specs=row(D_MODEL),
        out_shape=jax.ShapeDtypeStruct((m, D_MODEL), F32),
        compiler_params=_cparams(("parallel",)),
        name="outproj",
    )(x, a, s, wa, ws, gn)


N_PAGES = PAST_LEN // PAGE_SIZE
CMP_CHUNK = 32
CMP_PITCH = 24


def _page_copy(pool_ref, buf_ref, sem_ref, pt_ref, b, i, slot):
    return pltpu.make_async_copy(pool_ref.at[pt_ref[b * N_PAGES + i]], buf_ref.at[slot, i], sem_ref.at[slot])


def _cmp_paged_body(pt_ref, pool_ref, w1_ref, b_ref, w2_ref, o_ref, buf_ref, rows_ref, p_ref, sem_ref):
    b = pl.program_id(0)
    slot = b % 2

    def start_all(bb, sl):
        def body(i, c):
            _page_copy(pool_ref, buf_ref, sem_ref, pt_ref, bb, i, sl).start()
            return c
        lax.fori_loop(0, N_PAGES, body, 0, unroll=8)

    @pl.when(b == 0)
    def _():
        start_all(0, 0)

    @pl.when(b + 1 < pl.num_programs(0))
    def _():
        start_all(b + 1, 1 - slot)

    def wait_one(i, c):
        _page_copy(pool_ref, buf_ref, sem_ref, pt_ref, b, i, slot).wait()
        return c
    lax.fori_loop(0, N_PAGES, wait_one, 0, unroll=8)

    hp = PAGE_SIZE // CMP_STRIDE
    hc = CMP_CHUNK * hp

    def stage_a(c):
        for i in range(CMP_CHUNK):
            t = buf_ref[slot, c * CMP_CHUNK + i].T
            for h in range(hp):
                rows_ref[c % 2, (i * hp + h) * CMP_PITCH:(i * hp + h) * CMP_PITCH + CMP_STRIDE, :] = (
                    t[h * CMP_STRIDE:(h + 1) * CMP_STRIDE])

    def stage_b(c):
        xs = [rows_ref[c % 2, pl.ds(s, hc, stride=CMP_PITCH), :] for s in range(CMP_STRIDE)]
        p = _dot(_per_head_halves(xs), w1_ref[...])
        for k in range(N_KV_HEADS):
            p_ref[k, c * hc:(c + 1) * hc, :] = p[k * hc:(k + 1) * hc]

    n_chunks = N_PAGES // CMP_CHUNK
    stage_a(0)
    for c in range(n_chunks):
        if c + 1 < n_chunks:
            stage_a(c + 1)
        stage_b(c)

    o_ref[0] = _compress_tail([p_ref[k] for k in range(N_KV_HEADS)], N_PAGES * hp, b_ref, w2_ref)


def _cmp_paged(page_table_flat, pool_t, w1, bias, w2, n_seq):
    n_half = PAST_LEN // CMP_STRIDE
    hc = CMP_CHUNK * PAGE_SIZE // CMP_STRIDE
    const = lambda a: pl.BlockSpec(a.shape, lambda i, pt: (0,) * a.ndim)
    return pl.pallas_call(
        _cmp_paged_body,
        grid_spec=pltpu.PrefetchScalarGridSpec(
            num_scalar_prefetch=1,
            grid=(n_seq,),
            in_specs=[pl.BlockSpec(memory_space=pl.ANY), const(w1), const(bias), const(w2)],
            out_specs=pl.BlockSpec((1, n_half, KV_W), lambda i, pt: (i, 0, 0)),
            scratch_shapes=[pltpu.VMEM((2, N_PAGES, PAGE_SIZE, KV_W), F32),
                            pltpu.VMEM((2, hc * CMP_PITCH, KV_W), F32),
                            pltpu.VMEM((N_KV_HEADS, n_half, CMP_RATIO * CMP_HIDDEN), F32),
                            pltpu.SemaphoreType.DMA((2,))]),
        out_shape=jax.ShapeDtypeStruct((n_seq, n_half, KV_W), F32),
        compiler_params=_cparams(("arbitrary",)),
        name="cmp_paged",
    )(page_table_flat, pool_t, w1, bias, w2)


CMPATTN_SEQS = 8


def _cmpattn_body(q_ref, kc_ref, vc_ref, amat_ref, o_ref, idx_ref, *, ts, n_cmp, n_sel):
    rows = GQA * ts
    qpos = PAST_LEN + lax.broadcasted_iota(jnp.int32, (rows, 1), 0) % ts
    n_pad = kc_ref.shape[1]
    n_io = lax.broadcasted_iota(jnp.int32, (1, n_pad), 1)
    m = (n_io * CMP_STRIDE + (CMP_LEN - 1) <= qpos) & (n_io < n_cmp)
    n_sb = q_ref.shape[0]
    imps = []
    for b in range(n_sb):
        kc = kc_ref[b].astype(BF16)
        vc = vc_ref[b].astype(BF16)
        for k in range(N_KV_HEADS):
            s = _dot_nt((q_ref[b, k] * ATTN_SCALE).astype(BF16), kc)
            p = jnp.where(m, _softmax_rows(jnp.where(m, s, NEG_INF)), 0.0)
            o_ref[b, k] = _dot(p.astype(BF16), vc)
            imp = p[0:ts]
            for g in range(1, GQA):
                imp = imp + p[g * ts:(g + 1) * ts]
            imps.append(imp)
    imp_all = jnp.concatenate(imps, axis=0)
    n_rows = imp_all.shape[0]
    sel_w = amat_ref.shape[1]
    imp_sel = jnp.dot(imp_all, amat_ref[...], precision=lax.Precision.HIGHEST, preferred_element_type=F32)
    j = lax.broadcasted_iota(jnp.int32, (n_rows, sel_w), 1)
    jq = (PAST_LEN + lax.broadcasted_iota(jnp.int32, (n_rows, sel_w), 0) % ts) // SEL_BLOCK
    forced = (j == 0) | (j == jq) | (j == jq - 1)
    jf = j.astype(F32)
    lane_k = lax.broadcasted_iota(jnp.int32, (n_rows, LANES), 1)
    score = jnp.where(j > jq, -1.0, jnp.where(forced, FORCE_SCORE, imp_sel))
    score = jnp.where(j < n_sel, score, -jnp.inf)
    res = jnp.zeros((n_rows, LANES), F32)
    for i in range(N_SELECT):
        mx = jnp.max(score, axis=-1, keepdims=True)
        ix = jnp.min(jnp.where(score == mx, jf, 1e9), axis=-1, keepdims=True)
        res = jnp.where(lane_k == i, ix, res)
        score = jnp.where(jf == ix, -jnp.inf, score)
    idx_ref[...] = res.astype(jnp.int32).reshape(idx_ref.shape)


def _cmpattn(q_pad, kcmp, vcmp, ts):
    n_seq = q_pad.shape[0]
    n_cmp = (PAST_LEN + ts) // CMP_STRIDE - CMP_RATIO + 1
    n_sel = -(-(PAST_LEN + ts) // SEL_BLOCK)
    sel_w = -(-n_sel // LANES) * LANES
    per = SEL_BLOCK // CMP_STRIDE
    n = np.arange(kcmp.shape[1])[:, None]
    j = np.arange(sel_w)[None, :]
    amat = jnp.asarray(((n >= per * j - (CMP_RATIO - 1)) & (n <= per * j + per - 1) & (j < n_sel)).astype(np.float32))
    rows = GQA * ts
    sb = CMPATTN_SEQS
    return pl.pallas_call(
        functools.partial(_cmpattn_body, ts=ts, n_cmp=n_cmp, n_sel=n_sel),
        grid=(n_seq // sb,),
        in_specs=[pl.BlockSpec((sb, N_KV_HEADS, rows, LANES), lambda i: (i, 0, 0, 0)),
                  pl.BlockSpec((sb,) + kcmp.shape[1:], lambda i: (i, 0, 0)),
                  pl.BlockSpec((sb,) + vcmp.shape[1:], lambda i: (i, 0, 0)),
                  pl.BlockSpec(amat.shape, lambda i: (0, 0))],
        out_specs=[pl.BlockSpec((sb, N_KV_HEADS, rows, LANES), lambda i: (i, 0, 0, 0)),
                   pl.BlockSpec((sb, N_KV_HEADS * ts, LANES), lambda i: (i, 0, 0))],
        out_shape=[jax.ShapeDtypeStruct((n_seq, N_KV_HEADS, rows, LANES), F32),
                   jax.ShapeDtypeStruct((n_seq, N_KV_HEADS * ts, LANES), jnp.int32)],
        compiler_params=_cparams(("parallel",)),
        name="cmpattn_sample",
    )(q_pad, kcmp, vcmp, amat)


def _slc_copies(pools, bufs, sems, pt_ref, idx_ref, step, r, slot, ts, for_wait):
    if for_wait:
        pg = 0
    else:
        blk = idx_ref[step * (ts * N_SELECT) + r]
        page = jnp.minimum(blk // (PAGE_SIZE // SEL_BLOCK), N_PAGES - 1)
        pg = pt_ref[(step // N_KV_HEADS) * N_PAGES + page]
    return [pltpu.make_async_copy(pool.at[pg], buf.at[slot, r], sem.at[slot])
            for pool, buf, sem in zip(pools, bufs, sems)]


def _tail_body(pt_ref, idx_ref, kpool_ref, vpool_ref, q_ref, ksn_ref, vsn_ref, kwn_ref, vwn_ref, wk_ref, wv_ref,
               oc_ref, gt_ref, o_ref, wko_ref, wvo_ref, kbuf_ref, vbuf_ref, ksem_ref, vsem_ref, *, ts):
    step = pl.program_id(0) * N_KV_HEADS + pl.program_id(1)
    n_steps = pl.num_programs(0) * N_KV_HEADS
    slot = step % 2
    n_fetch = ts * N_SELECT

    pools, bufs, sems = (kpool_ref, vpool_ref), (kbuf_ref, vbuf_ref), (ksem_ref, vsem_ref)

    def start_all(st, sl):
        def body(r, c):
            for cp in _slc_copies(pools, bufs, sems, pt_ref, idx_ref, st, r, sl, ts, False):
                cp.start()
            return c
        lax.fori_loop(0, n_fetch, body, 0, unroll=8)

    @pl.when(step == 0)
    def _():
        start_all(0, 0)

    @pl.when(step + 1 < n_steps)
    def _():
        start_all(step + 1, 1 - slot)

    def wait_one(r, c):
        for cp in _slc_copies(pools, bufs, sems, pt_ref, idx_ref, step, r, slot, ts, True):
            cp.wait()
        return c
    lax.fori_loop(0, n_fetch, wait_one, 0, unroll=8)

    rows = GQA * ts
    q = (q_ref[0, 0] * ATTN_SCALE).astype(BF16)
    row_t = lax.broadcasted_iota(jnp.int32, (rows, 1), 0) % ts
    qpos = PAST_LEN + row_t
    lane = lax.broadcasted_iota(jnp.int32, (1, LANES), 1)
    lane_half, lane_off = lane // SEL_BLOCK, lane % SEL_BLOCK
    pad = jnp.zeros((LANES - ksn_ref.shape[1], LANES), F32)
    new_pos = PAST_LEN + lane
    new_ok = lane < ts
    blk_new = PAST_LEN // SEL_BLOCK

    def new_rows(ref):
        return jnp.concatenate([ref[0], pad], axis=0).astype(BF16)

    ksn, vsn = new_rows(ksn_ref), new_rows(vsn_ref)
    s_new = _dot_nt(q, ksn)
    o_s = jnp.zeros((rows, LANES), F32)
    grp = 4

    def pages(buf_ref, r0):
        return jnp.concatenate([buf_ref[slot, r0 + u].astype(BF16) for u in range(grp)], axis=1)

    for t in range(ts):
        pieces = []
        has_new = jnp.int32(0)
        for i0 in range(0, N_SELECT, grp):
            sc = _dot(q, pages(kbuf_ref, t * N_SELECT + i0))
            for u in range(grp):
                blk = idx_ref[step * n_fetch + t * N_SELECT + i0 + u]
                has_new = has_new | (blk == blk_new).astype(jnp.int32)
                want_half = jnp.where(blk < blk_new, blk % (PAGE_SIZE // SEL_BLOCK), -1)
                ok = (lane_half == want_half) & (blk * SEL_BLOCK + lane_off <= qpos)
                pieces.append(jnp.where(ok, sc[:, u * LANES:(u + 1) * LANES], NEG_INF))
        ok_new = (lane < jnp.where(has_new > 0, ts, 0)) & (new_pos <= qpos)
        pieces.append(jnp.where(ok_new, s_new, NEG_INF))
        p = _softmax_rows(jnp.concatenate(pieces, axis=1)).astype(BF16)
        o_t = _dot(p[:, N_SELECT * LANES:], vsn)
        for i0 in range(0, N_SELECT, grp):
            o_t = o_t + _dot_nt(p[:, i0 * LANES:(i0 + grp) * LANES], pages(vbuf_ref, t * N_SELECT + i0))
        o_s = jnp.where(row_t == t, o_t, o_s)

    wb = wk_ref.shape[2]
    kp = PAST_LEN - wb + lax.broadcasted_iota(jnp.int32, (1, wb), 1)
    ok_w = (kp >= 0) & (kp <= qpos) & (kp > qpos - WINDOW)
    ok_wn = new_ok & (new_pos <= qpos) & (new_pos > qpos - WINDOW)
    s_w = jnp.where(ok_w, _dot(q, wk_ref[0].astype(BF16)), NEG_INF)
    s_wn = jnp.where(ok_wn, _dot_nt(q, new_rows(kwn_ref)), NEG_INF)
    p = _softmax_rows(jnp.concatenate([s_w, s_wn], axis=1)).astype(BF16)
    o_w = _dot_nt(p[:, :wb], wv_ref[0].astype(BF16)) + _dot(p[:, wb:], new_rows(vwn_ref))

    gt = gt_ref[0, 0]
    o_ref[0, 0] = gt[:, 0:1] * oc_ref[0, 0] + gt[:, 1:2] * o_s + gt[:, 2:3] * o_w

    lane_w = lax.broadcasted_iota(jnp.int32, (1, LANES), 1)

    def slide(w_ref, new_ref, out_ref):
        shifted = pltpu.roll(w_ref[0], wb - ts, 1)
        new_t = pltpu.roll(jnp.concatenate([new_ref[0], pad], axis=0).T, LANES - ts, 1)
        out_ref[0, :, :wb - LANES] = shifted[:, :wb - LANES]
        out_ref[0, :, wb - LANES:] = jnp.where(lane_w >= LANES - ts, new_t, shifted[:, wb - LANES:])

    slide(wk_ref, kwn_ref, wko_ref)
    slide(wv_ref, vwn_ref, wvo_ref)


def _tail(page_table_flat, idx_flat, kpool_t, vpool_t, q_pad, ksn, vsn, kwn, vwn, wk_t, wv_t, o_cmp, gates_pad, ts):
    n_seq = q_pad.shape[0]
    rows = GQA * ts
    n_fetch = ts * N_SELECT
    per_head = pl.BlockSpec((1, 1, rows, LANES), lambda i, k, pt, ix: (i, k, 0, 0))
    per_seq = lambda a: pl.BlockSpec((1,) + a.shape[1:], lambda i, k, pt, ix: (i, 0, 0))
    hbm = pl.BlockSpec(memory_space=pl.ANY)
    return pl.pallas_call(
        functools.partial(_tail_body, ts=ts),
        grid_spec=pltpu.PrefetchScalarGridSpec(
            num_scalar_prefetch=2,
            grid=(n_seq, N_KV_HEADS),
            in_specs=[hbm, hbm, per_head, per_seq(ksn), per_seq(vsn), per_seq(kwn), per_seq(vwn),
                      per_seq(wk_t), per_seq(wv_t), per_head, per_head],
            out_specs=[per_head, per_seq(wk_t), per_seq(wv_t)],
            scratch_shapes=[pltpu.VMEM((2, n_fetch, PAGE_SIZE, KV_W), F32), pltpu.VMEM((2, n_fetch, PAGE_SIZE, KV_W), F32),
                            pltpu.SemaphoreType.DMA((2,)), pltpu.SemaphoreType.DMA((2,))]),
        out_shape=[jax.ShapeDtypeStruct((n_seq, N_KV_HEADS, rows, LANES), F32),
                   jax.ShapeDtypeStruct(wk_t.shape, F32), jax.ShapeDtypeStruct(wv_t.shape, F32)],
        compiler_params=_cparams(("arbitrary", "arbitrary")),
        name="slc_win_sample",
    )(page_table_flat, idx_flat, kpool_t, vpool_t, q_pad, ksn, vsn, kwn, vwn, wk_t, wv_t, o_cmp, gates_pad)


def _pad_heads(x, ts):
    n = x.shape[0] // ts
    w = x.shape[1] // N_HEADS
    return x.reshape(n, ts, N_KV_HEADS, GQA, w).transpose(0, 2, 3, 1, 4).reshape(n, N_KV_HEADS, GQA * ts, w)


def _q_on_kv_lanes(q, ts):
    x = _pad_heads(q, ts)
    z = jnp.zeros_like(x[:, 0])
    return jnp.stack([jnp.concatenate([x[:, 0], z], axis=-1), jnp.concatenate([z, x[:, 1]], axis=-1)], axis=1)


def _sample_attention_jnp(q, q_rot, kc, vc, ks, vs, kw, vw, gates, ck, cv, sk, sv, wk, wv, page_table, cw, gn):
    B, T = q.shape[:2]
    pos = PAST_LEN + jnp.arange(T, dtype=jnp.int32)
    q = q.reshape(B, T, N_HEADS, HEAD_DIM)
    q_rot = q_rot.reshape(B, T, N_HEADS, HEAD_DIM)
    kvr = lambda a: a.reshape(B, T, N_KV_HEADS, HEAD_DIM)
    kc, vc, ks, vs, kw, vw = map(kvr, (kc, vc, ks, vs, kw, vw))
    gates = gates[..., :3 * N_HEADS].reshape(B, T, N_HEADS, 3)
    past = lambda pool: pool[page_table].reshape(B, -1, N_KV_HEADS, HEAD_DIM)
    kc_all, vc_all = jnp.concatenate([past(ck), kc], axis=1), jnp.concatenate([past(cv), vc], axis=1)
    ks_all, vs_all = jnp.concatenate([past(sk), ks], axis=1), jnp.concatenate([past(sv), vs], axis=1)

    def compress_blocks(k, pos_emb, w1, b1, w2):
        Bk, Tk = k.shape[:2]
        nh = Tk // CMP_STRIDE
        nc = nh - CMP_RATIO + 1
        halves = k[:, :nh * CMP_STRIDE].reshape(Bk, nh, CMP_STRIDE, N_KV_HEADS, HEAD_DIM)
        pe = pos_emb.reshape(CMP_RATIO, CMP_STRIDE, HEAD_DIM)
        w1r = w1.reshape(CMP_RATIO, CMP_STRIDE, HEAD_DIM, CMP_HIDDEN)
        pre = b1
        for r in range(CMP_RATIO):
            pre = pre + jnp.einsum('bnskd,sdh->bnkh', halves[:, r:r + nc] + pe[r][:, None, :], w1r[r])
        return jax.nn.gelu(pre) @ w2

    k_cmp = compress_blocks(kc_all, *cw[0])
    v_cmp = compress_blocks(vc_all, *cw[1])
    nc = k_cmp.shape[1]
    qg = q.reshape(B, T, N_KV_HEADS, GQA, HEAD_DIM)
    s = jnp.einsum('bqkgd,bnkd->bqkgn', qg, k_cmp).astype(F32) * ATTN_SCALE
    blk_end = jnp.arange(nc) * CMP_STRIDE + CMP_LEN - 1
    m = (blk_end[None, :] <= pos[:, None])[None, :, None, None, :]
    p = jnp.where(m, jax.nn.softmax(jnp.where(m, s, NEG_INF), axis=-1), 0.0)
    o_cmp = jnp.einsum('bqkgn,bnkd->bqkgd', p, v_cmp).reshape(B, T, N_HEADS, HEAD_DIM)
    imp = p.sum(axis=3)
    n_sel = -(-(PAST_LEN + T) // SEL_BLOCK)
    per = SEL_BLOCK // CMP_STRIDE
    total = n_sel * per + CMP_RATIO + per
    pp = jnp.pad(imp, ((0, 0), (0, 0), (0, 0), (CMP_RATIO - 1, total - nc - (CMP_RATIO - 1))))
    impb = sum(pp[..., o:o + n_sel * per:per] for o in range(per + CMP_RATIO - 1))
    j = jnp.arange(n_sel)[None, None, None, :]
    jq = (pos // SEL_BLOCK)[None, :, None, None]
    forced = (j == 0) | (j == jq) | (j == jq - 1)
    score = jnp.where(j > jq, -1.0, jnp.where(forced, FORCE_SCORE, impb))
    _, idx = lax.top_k(score, min(N_SELECT, n_sel))
    valid = idx <= jq

    def to_blocks(k):
        kp = jnp.pad(k, ((0, 0), (0, n_sel * SEL_BLOCK - k.shape[1]), (0, 0), (0, 0)))
        return kp.reshape(B, n_sel, SEL_BLOCK, N_KV_HEADS, HEAD_DIM).transpose(0, 3, 1, 2, 4)

    kb, vb = to_blocks(ks_all), to_blocks(vs_all)
    qg = q_rot.reshape(B, T, N_KV_HEADS, GQA, HEAD_DIM)
    bi = jnp.arange(B)[:, None, None, None]
    hi = jnp.arange(N_KV_HEADS)[None, None, :, None]
    ksel, vsel = kb[bi, hi, idx], vb[bi, hi, idx]
    s = jnp.einsum('bqkgd,bqknsd->bqkgns', qg, ksel).astype(F32) * ATTN_SCALE
    kpos = idx[..., None] * SEL_BLOCK + jnp.arange(SEL_BLOCK)
    mask = valid[..., None] & (kpos <= pos[None, :, None, None, None])
    s = jnp.where(mask[:, :, :, None], s, NEG_INF)
    shp = s.shape
    p = jax.nn.softmax(s.reshape(shp[:-2] + (-1,)), axis=-1).reshape(shp)
    o_slc = jnp.einsum('bqkgns,bqknsd->bqkgd', p, vsel).reshape(B, T, N_HEADS, HEAD_DIM)
    wb = wk.shape[1]
    kw_all, vw_all = jnp.concatenate([wk, kw], axis=1), jnp.concatenate([wv, vw], axis=1)
    k_pos = PAST_LEN - wb + jnp.arange(wb + T, dtype=jnp.int32)
    s = jnp.einsum('bqkgd,bjkd->bqkgj', qg, kw_all).astype(F32) * ATTN_SCALE
    kp, qp = k_pos[None, :], pos[:, None]
    m = ((kp >= 0) & (kp <= qp) & (kp > qp - WINDOW))[None, :, None, None, :]
    p = jax.nn.softmax(jnp.where(m, s, NEG_INF), axis=-1)
    o_win = jnp.einsum('bqkgj,bjkd->bqkgd', p, vw_all).reshape(B, T, N_HEADS, HEAD_DIM)
    o = (gates[..., 0:1] * o_cmp + gates[..., 1:2] * o_slc + gates[..., 2:3] * o_win).reshape(B, T, D_ATTN)
    return _rms(o, gn).astype(BF16)


def kernel(x_prompt, x_sample, cache_cmp_k, cache_cmp_v, cache_slc_k, cache_slc_v, state_win_k, state_win_v, state_s5_re, state_s5_im, page_table, norm_ffn1, ffn1_gate, ffn1_up, ffn1_down, norm_mix, w_in, cmp_k_pos, cmp_k_w1, cmp_k_b1, cmp_k_w2, cmp_v_pos, cmp_v_w1, cmp_v_b1, cmp_v_w2, s5_log_dt, s5_a_re, s5_a_im, s5_b_re, s5_b_im, s5_c_re, s5_c_im, s5_d, s5_glu_w, s5_glu_b, norm_attn_out, norm_ssm_out, w_out, norm_ffn2, ffn2_gate, ffn2_up, ffn2_down, norm_final):
    depth = w_in.shape[0]
    assert depth == 1
    l = 0
    B, T, _ = x_prompt.shape
    BS, TS, _ = x_sample.shape
    row = lambda a: a[None, :]

    w = w_in[l]
    cut = D_ATTN + 6 * KV_W
    w_p = jnp.concatenate([w[:, :cut], w[:, cut + 3 * N_HEADS:], w[:, cut:cut + 3 * N_HEADS],
                           jnp.zeros((D_MODEL, LANES - 3 * N_HEADS), F32)], axis=1).astype(BF16)
    f1 = (row(norm_ffn1[l]), ffn1_gate[l].astype(BF16), ffn1_up[l].astype(BF16), ffn1_down[l].astype(BF16))
    f2 = (row(norm_ffn2[l]), ffn2_gate[l].astype(BF16), ffn2_up[l].astype(BF16), ffn2_down[l].astype(BF16))
    gfin = row(norm_final)
    wo = w_out[l].astype(BF16)
    wo_a, wo_s = wo[:D_ATTN], wo[D_ATTN:]
    cwk = _cmp_weights(cmp_k_pos[l], cmp_k_w1[l], cmp_k_b1[l], cmp_k_w2[l])
    cwv = _cmp_weights(cmp_v_pos[l], cmp_v_w1[l], cmp_v_b1[l], cmp_v_w2[l])
    sp = _s5_params(s5_log_dt[l], s5_a_re[l], s5_a_im[l], s5_b_re[l], s5_b_im[l], s5_c_re[l], s5_c_im[l],
                    s5_d[l], s5_glu_w[l], s5_glu_b[l], norm_ssm_out[l])
    gn_attn = row(norm_attn_out[l])
    g_mix = row(norm_mix[l])

    xp = x_prompt.reshape(B * T, D_MODEL)
    x1 = _ffn(xp, *f1, gfin, False)
    tabs = _rope_tables(jnp.arange(T, dtype=jnp.int32))
    (q, qr, kc, vc, _, _, _, _, gt, u,
     kc_t, vc_t, ks_t, vs_t, kw_t, vw_t) = _proj(x1, g_mix, w_p, *tabs, seq_t=T)
    b3 = lambda a: a.reshape(B, T, a.shape[-1])
    kcmp, vcmp = _cmp_prompt(b3(kc), b3(vc), cwk, cwv)
    attn = _attn_prompt(b3(q), b3(qr), kcmp, vcmp, ks_t, vs_t, kw_t, vw_t, b3(gt), gn_attn)
    zeros = jnp.zeros((B, SSM_GROUPS * SSM_STATE), F32)
    y_tm, p_re, p_im = _s5(b3(u).transpose(1, 0, 2), zeros, zeros, sp, 64)
    ssm = y_tm.transpose(1, 0, 2).reshape(B * T, D_SSM)
    x2 = _outproj(x1, attn.reshape(B * T, D_ATTN), ssm, wo_a, wo_s, gn_attn, False)
    y_prompt = _ffn(x2, *f2, gfin, True).reshape(B, T, D_MODEL)
    from_t = lambda a: a.reshape(a.shape[0], N_KV_HEADS, HEAD_DIM, a.shape[2]).transpose(0, 3, 1, 2)[None]
    wb = min(WINDOW, T)
    p_state = (from_t(kc_t), from_t(vc_t), from_t(ks_t), from_t(vs_t),
               from_t(kw_t[:, :, T - wb:]), from_t(vw_t[:, :, T - wb:]),
               p_re.reshape(1, B, SSM_GROUPS, SSM_STATE), p_im.reshape(1, B, SSM_GROUPS, SSM_STATE))

    xs = x_sample.reshape(BS * TS, D_MODEL)
    s1 = _ffn(xs, *f1, gfin, False)
    pos_s = PAST_LEN + jnp.arange(TS, dtype=jnp.int32)
    tabs_s = [jnp.tile(a, (BS, 1)) for a in _rope_tables(pos_s)]
    q, qr, kc, vc, ks, vs, kw, vw, gt, u = _proj(s1, g_mix, w_p, *tabs_s)
    s3 = lambda a: a.reshape(BS, TS, a.shape[-1])
    to_t = lambda a: a.transpose(0, 2, 3, 1).reshape(a.shape[0], KV_W, a.shape[1])
    pt_flat = page_table.reshape(-1)
    kcmp_s = _cmp_paged(pt_flat, to_t(cache_cmp_k[l]), *cwk, BS)
    vcmp_s = _cmp_paged(pt_flat, to_t(cache_cmp_v[l]), *cwv, BS)
    o_cmp_s, idx = _cmpattn(_q_on_kv_lanes(q, TS), kcmp_s, vcmp_s, TS)
    pad8 = lambda a: jnp.pad(s3(a), ((0, 0), (0, 8 - TS), (0, 0)))
    gates_s = _pad_heads(gt[:, :3 * N_HEADS], TS)
    gates_s = jnp.pad(gates_s, ((0, 0), (0, 0), (0, 0), (0, LANES - 3)))
    wk_t, wv_t = to_t(state_win_k[l]), to_t(state_win_v[l])
    comb, wk_new, wv_new = _tail(
        pt_flat, idx[:, :, :N_SELECT].reshape(-1), to_t(cache_slc_k[l]), to_t(cache_slc_v[l]),
        _q_on_kv_lanes(qr, TS), pad8(ks), pad8(vs), pad8(kw), pad8(vw), wk_t, wv_t, o_cmp_s, gates_s, TS)
    comb = comb.reshape(BS, N_KV_HEADS, GQA, TS, N_KV_HEADS, HEAD_DIM)
    comb = jnp.stack([comb[:, k, :, :, k] for k in range(N_KV_HEADS)], axis=1)
    attn_s = comb.transpose(0, 3, 1, 2, 4).reshape(BS * TS, D_ATTN)
    y_tm, s_re, s_im = _s5(s3(u).transpose(1, 0, 2), state_s5_re[l].reshape(BS, -1), state_s5_im[l].reshape(BS, -1), sp, TS)
    ssm_s = y_tm.transpose(1, 0, 2).reshape(BS * TS, D_SSM)
    s2 = _outproj(s1, attn_s, ssm_s, wo_a, wo_s, gn_attn, True)
    y_sample = _ffn(s2, *f2, gfin, True).reshape(BS, TS, D_MODEL)
    kv5 = lambda a, n: a.reshape(1, n, -1, N_KV_HEADS, HEAD_DIM)
    win_k, win_v = from_t(wk_new), from_t(wv_new)
    s_state = (kv5(kc, BS), kv5(vc, BS), kv5(ks, BS), kv5(vs, BS), win_k, win_v,
               s_re.reshape(1, BS, SSM_GROUPS, SSM_STATE), s_im.reshape(1, BS, SSM_GROUPS, SSM_STATE))

    return (y_prompt, y_sample) + p_state + s_state
```

```python
import functools
import math

import numpy as np
import jax
import jax.numpy as jnp
from jax import lax
from jax.experimental import pallas as pl
from jax.experimental.pallas import tpu as pltpu

F32 = jnp.float32
BF16 = jnp.bfloat16

D_MODEL = 1024
PAST_LEN = 16384
PAGE_SIZE = 128
D_ATTN = 512
D_SSM = 512
HEAD_DIM = 64
N_HEADS = 8
N_KV_HEADS = 2
GQA = 4
KV_W = 128
ROT_DIM = 16
ROPE_THETA = 500000.0
ATTN_SCALE = HEAD_DIM ** -0.5
CMP_LEN = 32
CMP_STRIDE = 16
CMP_RATIO = 2
CMP_HIDDEN = 256
SEL_BLOCK = 64
N_SELECT = 16
WINDOW = 512
SSM_CH = 16
SSM_GROUPS = 32
SSM_STATE = 64
D_FF = 2816
RMS_EPS = 1e-6
NEG_INF = -1e30
FORCE_SCORE = 1e4

LANES = 128
VMEM_LIMIT = 56 * 1024 * 1024

Z_Q = 0
Z_KV = D_ATTN
Z_U = D_ATTN + 6 * KV_W
Z_G = Z_U + D_SSM
Z_W = Z_G + LANES


def _cparams(sem):
    return pltpu.CompilerParams(dimension_semantics=sem, vmem_limit_bytes=VMEM_LIMIT)


def _rms(x, g):
    return x * lax.rsqrt(jnp.mean(x * x, axis=-1, keepdims=True) + RMS_EPS) * g


def _dot(a, b):
    return jnp.dot(a, b, preferred_element_type=F32)


def _dot_nt(a, b):
    return lax.dot_general(a, b, (((1,), (1,)), ((), ())), preferred_element_type=F32)


def _ffn_body(x_ref, g_ref, wg_ref, wu_ref, wd_ref, gf_ref, o_ref, *, final_norm):
    x = x_ref[...]
    h = _rms(x, g_ref[...]).astype(BF16)
    a = _dot(h, wg_ref[...])
    b = _dot(h, wu_ref[...])
    t = (a * jax.nn.sigmoid(a) * b).astype(BF16)
    y = x + 0.5 * _dot(t, wd_ref[...])
    if final_norm:
        y = _rms(y, gf_ref[...])
    o_ref[...] = y


def _ffn(x, g, wg, wu, wd, gf, final_norm):
    m = x.shape[0]
    tm = min(m, 512)
    row = pl.BlockSpec((tm, D_MODEL), lambda i: (i, 0))
    vec = pl.BlockSpec((1, D_MODEL), lambda i: (0, 0))
    resident = lambda a: pl.BlockSpec(a.shape, lambda i: (0, 0), pipeline_mode=pl.Buffered(1))
    return pl.pallas_call(
        functools.partial(_ffn_body, final_norm=final_norm),
        grid=(m // tm,),
        in_specs=[row, vec, resident(wg), resident(wu), resident(wd), vec],
        out_specs=row,
        out_shape=jax.ShapeDtypeStruct((m, D_MODEL), F32),
        compiler_params=_cparams(("parallel",)),
        name="ffn",
    )(x, g, wg, wu, wd, gf)


def _proj_body(x_ref, g_ref, w_ref, c_ref, sa_ref, sb_ref,
               q_ref, qr_ref, kc_ref, vc_ref, ks_ref, vs_ref, kw_ref, vw_ref, gt_ref, u_ref, *t_refs):
    h = _rms(x_ref[...], g_ref[...]).astype(BF16)
    z = _dot(h, w_ref[...])
    c, sa, sb = c_ref[...], sa_ref[...], sb_ref[...]

    def rope(v):
        return v * c + pltpu.roll(v, LANES - ROT_DIM // 2, 1) * sa + pltpu.roll(v, ROT_DIM // 2, 1) * sb

    q_ref[...] = z[:, Z_Q:Z_Q + D_ATTN]
    for i in range(D_ATTN // LANES):
        qr_ref[:, i * LANES:(i + 1) * LANES] = rope(z[:, Z_Q + i * LANES:Z_Q + (i + 1) * LANES])
    kv = [z[:, Z_KV + i * KV_W:Z_KV + (i + 1) * KV_W] for i in range(6)]
    kv[2] = rope(kv[2])
    kv[4] = rope(kv[4])
    for ref, v in zip((kc_ref, vc_ref, ks_ref, vs_ref, kw_ref, vw_ref), kv):
        ref[...] = v
    for ref, v in zip(t_refs, kv):
        ref[0] = v.T
    u_ref[...] = z[:, Z_U:Z_U + D_SSM]
    gt_ref[...] = jax.nn.sigmoid(z[:, Z_G:Z_G + LANES])


def _proj(x, g, w, rope_c, rope_sa, rope_sb, seq_t=None):
    m = x.shape[0]
    tm = min(m, 512)
    n_rep = rope_c.shape[0] // tm
    row = lambda i: (i, 0)
    const = lambda i: (0, 0)
    tab = lambda i: (i % n_rep, 0)
    widths = [D_ATTN, D_ATTN] + [KV_W] * 6 + [LANES, D_SSM]
    out_specs = [pl.BlockSpec((tm, wd), row) for wd in widths]
    out_shape = [jax.ShapeDtypeStruct((m, wd), F32) for wd in widths]
    if seq_t is not None:
        assert rope_c.shape[0] == seq_t
        out_specs += [pl.BlockSpec((1, KV_W, tm), lambda i: (i // n_rep, 0, i % n_rep))] * 6
        out_shape += [jax.ShapeDtypeStruct((m // seq_t, KV_W, seq_t), F32)] * 6
    return pl.pallas_call(
        _proj_body,
        grid=(m // tm,),
        in_specs=[pl.BlockSpec((tm, D_MODEL), row), pl.BlockSpec((1, D_MODEL), const),
                  pl.BlockSpec((D_MODEL, Z_W), const),
                  pl.BlockSpec((tm, LANES), tab), pl.BlockSpec((tm, LANES), tab), pl.BlockSpec((tm, LANES), tab)],
        out_specs=out_specs,
        out_shape=out_shape,
        compiler_params=_cparams(("parallel",)),
        name="proj",
    )(x, g, w, rope_c, rope_sa, rope_sb)


def _rope_tables(pos):
    half = ROT_DIM // 2
    inv_freq = jnp.power(ROPE_THETA, -(jnp.arange(half, dtype=F32) * 2.0 / ROT_DIM))
    ang = pos.astype(F32)[:, None] * inv_freq[None, :]
    cos, sin = jnp.cos(ang), jnp.sin(ang)
    n = pos.shape[0]
    one = jnp.ones((n, HEAD_DIM - ROT_DIM), F32)
    zero = jnp.zeros((n, HEAD_DIM - ROT_DIM), F32)
    zh = jnp.zeros((n, half), F32)
    c = jnp.concatenate([cos, cos, one], axis=1)
    sa = jnp.concatenate([-sin, zh, zero], axis=1)
    sb = jnp.concatenate([zh, sin, zero], axis=1)
    t2 = lambda a: jnp.concatenate([a, a], axis=1)
    return t2(c), t2(sa), t2(sb)


def _per_head_halves(xs):
    low = lax.broadcasted_iota(jnp.int32, xs[0].shape, 1) < HEAD_DIM
    heads = [[], []]
    for j in range(CMP_STRIDE // 2):
        a, b = xs[2 * j], xs[2 * j + 1]
        heads[0].append(jnp.where(low, a, pltpu.roll(b, HEAD_DIM, 1)))
        heads[1].append(jnp.where(low, pltpu.roll(a, HEAD_DIM, 1), b))
    return jnp.concatenate([jnp.concatenate(hd, axis=1) for hd in heads], axis=0).astype(BF16)


def _compress_tail(p_k, n_half, b_ref, w2_ref):
    out = None
    for k in range(N_KV_HEADS):
        pre = p_k[k][:, :CMP_HIDDEN] + pltpu.roll(p_k[k][:, CMP_HIDDEN:], n_half - 1, 0) + b_ref[...]
        ok = _dot(jax.nn.gelu(pre).astype(BF16), w2_ref[k])
        out = ok if out is None else out + ok
    return out


def _compress_rows(x_ref, n_half, w1_ref, b_ref, w2_ref):
    xs = [x_ref[0, pl.ds(s, n_half, stride=CMP_STRIDE), :] for s in range(CMP_STRIDE)]
    p = _dot(_per_head_halves(xs), w1_ref[...])
    return _compress_tail([p[k * n_half:(k + 1) * n_half] for k in range(N_KV_HEADS)], n_half, b_ref, w2_ref)


def _cmp_prompt_body(kc_ref, vc_ref, wk1_ref, bk_ref, wk2_ref, wv1_ref, bv_ref, wv2_ref, ko_ref, vo_ref, *, n_half):
    ko_ref[0] = _compress_rows(kc_ref, n_half, wk1_ref, bk_ref, wk2_ref)
    vo_ref[0] = _compress_rows(vc_ref, n_half, wv1_ref, bv_ref, wv2_ref)


def _cmp_prompt(kc, vc, wk, wv):
    b, t, _ = kc.shape
    n_half = t // CMP_STRIDE
    seq = pl.BlockSpec((1, t, KV_W), lambda i: (i, 0, 0))
    const = lambda shp: pl.BlockSpec(shp, lambda i: (0,) * len(shp))
    wspecs = [const(wk[0].shape), const(wk[1].shape), const(wk[2].shape)]
    out = pl.BlockSpec((1, n_half, KV_W), lambda i: (i, 0, 0))
    return pl.pallas_call(
        functools.partial(_cmp_prompt_body, n_half=n_half),
        grid=(b,),
        in_specs=[seq, seq] + wspecs + wspecs,
        out_specs=[out, out],
        out_shape=[jax.ShapeDtypeStruct((b, n_half, KV_W), F32)] * 2,
        compiler_params=_cparams(("parallel",)),
        name="cmp_prompt",
    )(kc, vc, *wk, *wv)


def _cmp_weights(pos_emb, w1, b1, w2):
    w1r = w1.reshape(CMP_RATIO, CMP_STRIDE * HEAD_DIM, CMP_HIDDEN)
    w1s = w1r.transpose(1, 0, 2).reshape(CMP_STRIDE * HEAD_DIM, CMP_RATIO * CMP_HIDDEN)
    pe = pos_emb.reshape(CMP_RATIO, CMP_STRIDE * HEAD_DIM)
    bias = b1 + jnp.einsum('rx,rxh->h', pe, w1r, precision=lax.Precision.HIGHEST)
    z = jnp.zeros_like(w2)
    w2k = jnp.stack([jnp.concatenate([w2, z], axis=1), jnp.concatenate([z, w2], axis=1)])
    return w1s.astype(BF16), bias[None, :], w2k.astype(BF16)


TQ = 512
KCHUNK = 512
SUBQ = 256
N_SEL_PROMPT = 32
assert TQ == KCHUNK and TQ % SUBQ == 0


def _softmax_rows(s):
    mx = jnp.max(s, axis=-1, keepdims=True)
    e = jnp.exp(s - mx)
    return e / jnp.sum(e, axis=-1, keepdims=True)


def _attn_prompt_body(q_ref, qr_ref, kcmp_ref, vcmp_ref, ks_ref, vs_ref, kw_ref, vw_ref, gt_ref,
                      amat_ref, emat_ref, gn_ref, o_ref, *, seq):
    t0 = pl.program_id(1) * TQ
    lane = lax.broadcasted_iota(jnp.int32, (TQ, LANES), 1)
    low = lane < HEAD_DIM
    tq = t0 + lax.broadcasted_iota(jnp.int32, (TQ, 1), 0)
    gates = gt_ref[0]

    def head_rows(src_ref, k):
        rows = []
        for g in range(GQA):
            h = GQA * k + g
            v = src_ref[0, :, (h // 2) * LANES:(h // 2 + 1) * LANES]
            if h % 2 != k:
                v = pltpu.roll(v, HEAD_DIM, 1)
            rows.append(jnp.where(low if k == 0 else jnp.logical_not(low), v * ATTN_SCALE, 0.0))
        return jnp.concatenate(rows, axis=0).astype(BF16)

    jj = lax.broadcasted_iota(jnp.int32, (N_SEL_PROMPT, TQ), 0)
    jq = (t0 + lax.broadcasted_iota(jnp.int32, (N_SEL_PROMPT, TQ), 1)) // SEL_BLOCK
    forced = (jj == 0) | (jj == jq) | (jj == jq - 1)

    n_cmp = kcmp_ref.shape[1]
    blk_end = lax.broadcasted_iota(jnp.int32, (1, n_cmp), 1) * CMP_STRIDE + (CMP_LEN - 1)
    m_cmp = (blk_end <= tq)[None]

    n_chunks = t0 // KCHUNK + 1

    comb, q_augs, q_rots, o_cs = [], [], [], []
    for k in range(N_KV_HEADS):
        qc = head_rows(q_ref, k)
        qr = head_rows(qr_ref, k)

        s = _dot_nt(qc, kcmp_ref[0].astype(BF16)).reshape(GQA, TQ, n_cmp)
        p = jnp.where(m_cmp, _softmax_rows(jnp.where(m_cmp, s, NEG_INF)), 0.0)
        o_c = _dot(p.reshape(GQA * TQ, n_cmp).astype(BF16), vcmp_ref[0].astype(BF16))
        imp = jnp.sum(p, axis=0)

        imp_sel = jnp.dot(amat_ref[...], imp.T, precision=lax.Precision.HIGHEST, preferred_element_type=F32)
        score = jnp.where(jj > jq, -1.0, jnp.where(forced, FORCE_SCORE, imp_sel))
        rank = jnp.zeros((N_SEL_PROMPT, TQ), F32)
        for i in range(N_SEL_PROMPT):
            si = score[i:i + 1, :]
            beats = (si > score) | ((si == score) & (jj > i))
            rank = rank + jnp.where(beats, 1.0, 0.0)
        sel_t = jnp.where((rank < N_SELECT) & (jj <= jq), 1.0, 0.0)
        sel_t = jnp.concatenate([sel_t, jnp.zeros((LANES - N_SEL_PROMPT, TQ), F32)], axis=0)
        sel = sel_t.T

        bias = ((sel - 1.0) * (-NEG_INF)).astype(BF16)
        q_augs.append(jnp.concatenate([qr, jnp.concatenate([bias] * GQA, axis=0)], axis=1))
        q_rots.append(qr)
        o_cs.append(o_c.reshape(GQA, TQ, LANES))

    sum_lanes = [HEAD_DIM * (1 - k) for k in range(N_KV_HEADS)]
    row_id = lax.broadcasted_iota(jnp.int32, (LANES, 1), 0)

    def attend(q2d, kk, vv, k, mask, m_i, acc):
        n_q, n_k = q2d.shape[0] // GQA, kk.shape[1]
        sc = _dot(q2d, kk).reshape(GQA, n_q, n_k)
        if mask is not None:
            sc = jnp.where(mask[None], sc, NEG_INF)
        m_n = jnp.max(sc, axis=-1, keepdims=True)
        if m_i is not None:
            m_n = jnp.maximum(m_i, m_n)
        pe = jnp.exp((sc - m_n).astype(BF16))
        vk = jnp.where(row_id == sum_lanes[k], 1.0, vv).astype(BF16)
        pv = _dot_nt(pe.reshape(GQA * n_q, n_k), vk).reshape(GQA, n_q, LANES)
        if m_i is not None:
            pv = jnp.exp(m_i - m_n) * acc + pv
        return m_n, pv

    def sub_rows(x2d, s):
        w = x2d.shape[1]
        return x2d.reshape(GQA, TQ, w)[:, s * SUBQ:(s + 1) * SUBQ].reshape(GQA * SUBQ, w)

    def chunk(c, carry):
        k0 = pl.multiple_of(c * KCHUNK, KCHUNK)
        kk = jnp.concatenate([ks_ref[0, :, pl.ds(k0, KCHUNK)].astype(BF16),
                              emat_ref[:, pl.ds(k0, KCHUNK)]], axis=0)
        vv = vs_ref[0, :, pl.ds(k0, KCHUNK)].astype(BF16)
        return tuple(attend(q_augs[k], kk, vv, k, None, *carry[k]) for k in range(N_KV_HEADS))

    init = tuple((jnp.full((GQA, TQ, 1), NEG_INF, F32), jnp.zeros((GQA, TQ, LANES), F32)) for _ in range(N_KV_HEADS))
    carry = lax.fori_loop(0, n_chunks - 1, chunk, init)

    acc_ss = [[] for _ in range(N_KV_HEADS)]
    acc_ws = [[] for _ in range(N_KV_HEADS)]
    w_sub = WINDOW + SUBQ
    for s in range(TQ // SUBQ):
        tq_s = t0 + s * SUBQ + lax.broadcasted_iota(jnp.int32, (SUBQ, 1), 0)
        n_k = (s + 1) * SUBQ
        k0 = pl.multiple_of(t0, TQ)
        kk = jnp.concatenate([ks_ref[0, :, pl.ds(k0, n_k)].astype(BF16), emat_ref[:, pl.ds(k0, n_k)]], axis=0)
        vv = vs_ref[0, :, pl.ds(k0, n_k)].astype(BF16)
        causal = (k0 + lax.broadcasted_iota(jnp.int32, (1, n_k), 1)) <= tq_s
        ws = pl.multiple_of(jnp.clip(t0 + s * SUBQ - WINDOW, 0, seq - w_sub), SUBQ)
        kw = kw_ref[0, :, pl.ds(ws, w_sub)].astype(BF16)
        vw = vw_ref[0, :, pl.ds(ws, w_sub)].astype(BF16)
        kp_w = ws + lax.broadcasted_iota(jnp.int32, (1, w_sub), 1)
        in_win = (kp_w <= tq_s) & (kp_w > tq_s - WINDOW)
        for k in range(N_KV_HEADS):
            m_i, acc = carry[k]
            rows = slice(s * SUBQ, (s + 1) * SUBQ)
            acc_ss[k].append(attend(sub_rows(q_augs[k], s), kk, vv, k, causal, m_i[:, rows], acc[:, rows])[1])
            acc_ws[k].append(attend(sub_rows(q_rots[k], s), kw, vw, k, in_win, None, None)[1])

    for k in range(N_KV_HEADS):
        acc_s, acc_w = jnp.concatenate(acc_ss[k], axis=1), jnp.concatenate(acc_ws[k], axis=1)
        sl = slice(sum_lanes[k], sum_lanes[k] + 1)
        for g in range(GQA):
            h = GQA * k + g
            g_s = gates[:, 3 * h + 1:3 * h + 2] / acc_s[g][:, sl]
            g_w = gates[:, 3 * h + 2:3 * h + 3] / acc_w[g][:, sl]
            comb.append(gates[:, 3 * h:3 * h + 1] * o_cs[k][g] + g_s * acc_s[g] + g_w * acc_w[g])

    cols = []
    for pr in range(N_HEADS // 2):
        k = (2 * pr) // GQA
        a, b = comb[2 * pr], comb[2 * pr + 1]
        a = pltpu.roll(a, HEAD_DIM, 1) if k == 1 else a
        b = pltpu.roll(b, HEAD_DIM, 1) if k == 0 else b
        cols.append(jnp.where(low, a, b))
    o = jnp.concatenate(cols, axis=1)
    o_ref[0] = _rms(o, gn_ref[...]).astype(o_ref.dtype)


def _sel_matrices(n_cmp_pad, n_sel, n_keys):
    j = np.arange(n_sel)[:, None]
    n = np.arange(n_cmp_pad)[None, :]
    per = SEL_BLOCK // CMP_STRIDE
    amat = ((n >= per * j - (CMP_RATIO - 1)) & (n <= per * j + per - 1)).astype(np.float32)
    jrow = np.arange(LANES)[:, None]
    key = np.arange(n_keys)[None, :]
    emat = (key // SEL_BLOCK == jrow).astype(np.float32)
    return jnp.asarray(amat), jnp.asarray(emat, dtype=BF16)


def _attn_prompt(q, qr, kcmp, vcmp, ks, vs, kw, vw, gt, gn):
    b, t, _ = q.shape
    amat, emat = _sel_matrices(kcmp.shape[1], N_SEL_PROMPT, t)
    qt = lambda w: pl.BlockSpec((1, TQ, w), lambda i, j: (i, j, 0))
    full = lambda a: pl.BlockSpec((1,) + a.shape[1:], lambda i, j: (i, 0, 0))
    const = lambda a: pl.BlockSpec(a.shape, lambda i, j: (0, 0))
    return pl.pallas_call(
        functools.partial(_attn_prompt_body, seq=t),
        grid=(b, t // TQ),
        in_specs=[qt(D_ATTN), qt(D_ATTN), full(kcmp), full(vcmp), full(ks), full(vs), full(kw), full(vw),
                  qt(LANES), const(amat), const(emat), const(gn)],
        out_specs=qt(D_ATTN),
        out_shape=jax.ShapeDtypeStruct((b, t, D_ATTN), BF16),
        compiler_params=_cparams(("parallel", "arbitrary")),
        name="attn_prompt",
    )(q, qr, kcmp, vcmp, ks, vs, kw, vw, gt, amat, emat, gn)


S5_LB = D_SSM // LANES
S5_SW = LANES // SSM_CH * SSM_STATE


def _s5_body(u_ref, h0r_ref, h0i_ref, lr_ref, li_ref, bre_ref, bim_ref, c_ref, d_ref, gw_ref, gb_ref, gn_ref,
             y_ref, hr_ref, hi_ref, y_s, *, tc, bb):
    step = pl.program_id(0)
    m = tc * bb

    @pl.when(step == 0)
    def _():
        hr_ref[...] = h0r_ref[...]
        hi_ref[...] = h0i_ref[...]

    u = u_ref[...].reshape(m, D_SSM)
    ub = u.astype(BF16)
    for j in range(S5_LB):
        uj = ub[:, j * LANES:(j + 1) * LANES]
        sl = slice(j * S5_SW, (j + 1) * S5_SW)
        xr = _dot(uj, bre_ref[j]).reshape(tc, bb, S5_SW)
        xi = _dot(uj, bim_ref[j]).reshape(tc, bb, S5_SW)
        lr = jnp.broadcast_to(lr_ref[:, sl], (bb, S5_SW))
        li = jnp.broadcast_to(li_ref[:, sl], (bb, S5_SW))
        hr, hi = hr_ref[:, sl], hi_ref[:, sl]
        hrs, his = [], []
        for t in range(tc):
            hr, hi = lr * hr - li * hi + xr[t], lr * hi + li * hr + xi[t]
            hrs.append(hr)
            his.append(hi)
        hr_ref[:, sl] = hr
        hi_ref[:, sl] = hi
        hcat = jnp.concatenate([jnp.concatenate(hrs, axis=0), jnp.concatenate(his, axis=0)], axis=1).astype(BF16)
        y_s[:, j * LANES:(j + 1) * LANES] = _dot(hcat, c_ref[j])
    z = jax.nn.gelu(y_s[...] + d_ref[...] * u)
    out = z * jax.nn.sigmoid(_dot(z.astype(BF16), gw_ref[...]) + gb_ref[...])
    y_ref[...] = _rms(out, gn_ref[...]).astype(y_ref.dtype).reshape(tc, bb, D_SSM)


def _s5(u_tm, h0r, h0i, sp, tc):
    t, bb, _ = u_tm.shape
    const = lambda a: pl.BlockSpec(a.shape, lambda i: (0,) * a.ndim)
    args = (h0r, h0i, sp['lr'], sp['li'], sp['bre'], sp['bim'], sp['c'], sp['d'], sp['glu_w'], sp['glu_b'], sp['gn'])
    st = jax.ShapeDtypeStruct(h0r.shape, F32)
    return pl.pallas_call(
        functools.partial(_s5_body, tc=tc, bb=bb),
        grid=(t // tc,),
        in_specs=[pl.BlockSpec((tc, bb, D_SSM), lambda i: (i, 0, 0))] + [const(a) for a in args],
        out_specs=[pl.BlockSpec((tc, bb, D_SSM), lambda i: (i, 0, 0)), const(h0r), const(h0i)],
        out_shape=[jax.ShapeDtypeStruct((t, bb, D_SSM), F32), st, st],
        scratch_shapes=[pltpu.VMEM((tc * bb, D_SSM), F32)],
        compiler_params=_cparams(("arbitrary",)),
        name="s5",
    )(u_tm, *args)


def _s5_params(log_dt, a_re, a_im, b_re, b_im, c_re, c_im, d, glu_w, glu_b, gn):
    dt = jnp.exp(log_dt)[:, None]
    mag = jnp.exp(a_re * dt)
    lr, li = mag * jnp.cos(a_im * dt), mag * jnp.sin(a_im * dt)
    den = a_re * a_re + a_im * a_im
    inv_r, inv_i = a_re / den, -a_im / den
    nr, ni = lr - 1.0, li
    fr, fi = nr * inv_r - ni * inv_i, nr * inv_i + ni * inv_r
    bbr = fr[..., None] * b_re - fi[..., None] * b_im
    bbi = fr[..., None] * b_im + fi[..., None] * b_re
    gl = LANES // SSM_CH
    eye = jnp.eye(gl, dtype=F32)

    def bmat(bb):
        x = bb.reshape(S5_LB, gl, SSM_STATE, SSM_CH)
        return jnp.einsum('jgpc,gh->jgchp', x, eye).reshape(S5_LB, LANES, S5_SW).astype(BF16)

    def cmat(cc):
        x = cc.reshape(S5_LB, gl, SSM_CH, SSM_STATE)
        return jnp.einsum('jgcp,gh->jgphc', x, eye).reshape(S5_LB, S5_SW, LANES)

    cm = jnp.concatenate([cmat(c_re), -cmat(c_im)], axis=1).astype(BF16)
    return dict(lr=lr.reshape(1, -1), li=li.reshape(1, -1), bre=bmat(bbr), bim=bmat(bbi), c=cm,
                d=d.reshape(1, -1), glu_w=glu_w.astype(BF16), glu_b=glu_b[None, :], gn=gn[None, :])


def _outproj_body(x_ref, a_ref, s_ref, wa_ref, ws_ref, gn_ref, o_ref, *, norm_attn):
    a = a_ref[...]
    if norm_attn:
        a = _rms(a, gn_ref[...])
    o_ref[...] = x_ref[...] + _dot(a.astype(BF16), wa_ref[...]) + _dot(s_ref[...].astype(BF16), ws_ref[...])


def _outproj(x, a, s, wa, ws, gn, norm_attn):
    m = x.shape[0]
    tm = min(m, 1024)
    row = lambda w: pl.BlockSpec((tm, w), lambda i: (i, 0))
    const = lambda a_: pl.BlockSpec(a_.shape, lambda i: (0, 0))
    return pl.pallas_call(
        functools.partial(_outproj_body, norm_attn=norm_attn),
        grid=(m // tm,),
        in_specs=[row(D_MODEL), row(D_ATTN), row(D_SSM), const(wa), const(ws), const(gn)],
        out_specs=row(D_MODEL),
        out_shape=jax.ShapeDtypeStruct((m, D_MODEL), F32),
        compiler_params=_cparams(("parallel",)),
        name="outproj",
    )(x, a, s, wa, ws, gn)


N_PAGES = PAST_LEN // PAGE_SIZE
CMP_CHUNK = 32
CMP_PITCH = CMP_STRIDE // 2


def _page_copy(pool_ref, buf_ref, sem_ref, pt_ref, b, i, slot):
    return pltpu.make_async_copy(pool_ref.at[pt_ref[b * N_PAGES + i]], buf_ref.at[slot, i], sem_ref.at[slot])


def _cmp_paged_body(pt_ref, pool_ref, w1_ref, b_ref, w2_ref, o_ref, buf_ref, rows_ref, p_ref, sem_ref):
    b = pl.program_id(0)
    slot = b % 2

    def start_all(bb, sl):
        def body(i, c):
            _page_copy(pool_ref, buf_ref, sem_ref, pt_ref, bb, i, sl).start()
            return c
        lax.fori_loop(0, N_PAGES, body, 0, unroll=8)

    @pl.when(b == 0)
    def _():
        start_all(0, 0)

    @pl.when(b + 1 < pl.num_programs(0))
    def _():
        start_all(b + 1, 1 - slot)

    def wait_one(i, c):
        _page_copy(pool_ref, buf_ref, sem_ref, pt_ref, b, i, slot).wait()
        return c
    lax.fori_loop(0, N_PAGES, wait_one, 0, unroll=8)

    hp = PAGE_SIZE // CMP_STRIDE
    hc = CMP_CHUNK * hp

    wp = PAGE_SIZE // 2

    def stage_a(c):
        for i in range(CMP_CHUNK):
            t = buf_ref[slot, c * CMP_CHUNK + i].astype(BF16).T
            rows_ref[c % 2, i * wp:(i + 1) * wp, :] = pltpu.bitcast(t, jnp.uint32)

    def stage_b(c):
        xs = []
        for j in range(CMP_PITCH):
            w = rows_ref[c % 2, pl.ds(j, hc, stride=CMP_PITCH), :]
            xs.append(pltpu.bitcast(w << 16, F32))
            xs.append(pltpu.bitcast(w & jnp.uint32(0xFFFF0000), F32))
        p = _dot(_per_head_halves(xs), w1_ref[...])
        for k in range(N_KV_HEADS):
            p_ref[k, c * hc:(c + 1) * hc, :] = p[k * hc:(k + 1) * hc]

    n_chunks = N_PAGES // CMP_CHUNK
    stage_a(0)
    for c in range(n_chunks):
        if c + 1 < n_chunks:
            stage_a(c + 1)
        stage_b(c)

    o_ref[0] = _compress_tail([p_ref[k] for k in range(N_KV_HEADS)], N_PAGES * hp, b_ref, w2_ref)


def _cmp_paged(page_table_flat, pool_t, w1, bias, w2, n_seq):
    n_half = PAST_LEN // CMP_STRIDE
    hc = CMP_CHUNK * PAGE_SIZE // CMP_STRIDE
    const = lambda a: pl.BlockSpec(a.shape, lambda i, pt: (0,) * a.ndim)
    return pl.pallas_call(
        _cmp_paged_body,
        grid_spec=pltpu.PrefetchScalarGridSpec(
            num_scalar_prefetch=1,
            grid=(n_seq,),
            in_specs=[pl.BlockSpec(memory_space=pl.ANY), const(w1), const(bias), const(w2)],
            out_specs=pl.BlockSpec((1, n_half, KV_W), lambda i, pt: (i, 0, 0)),
            scratch_shapes=[pltpu.VMEM((2, N_PAGES, PAGE_SIZE, KV_W), F32),
                            pltpu.VMEM((2, hc * CMP_PITCH, KV_W), jnp.uint32),
                            pltpu.VMEM((N_KV_HEADS, n_half, CMP_RATIO * CMP_HIDDEN), F32),
                            pltpu.SemaphoreType.DMA((2,))]),
        out_shape=jax.ShapeDtypeStruct((n_seq, n_half, KV_W), F32),
        compiler_params=_cparams(("arbitrary",)),
        name="cmp_paged",
    )(page_table_flat, pool_t, w1, bias, w2)


CMPATTN_SEQS = 8


def _cmpattn_body(q_ref, kc_ref, vc_ref, amat_ref, o_ref, idx_ref, *, ts, n_cmp, n_sel):
    rows = GQA * ts
    qpos = PAST_LEN + lax.broadcasted_iota(jnp.int32, (rows, 1), 0) % ts
    n_pad = kc_ref.shape[1]
    n_io = lax.broadcasted_iota(jnp.int32, (1, n_pad), 1)
    m = (n_io * CMP_STRIDE + (CMP_LEN - 1) <= qpos) & (n_io < n_cmp)
    n_sb = q_ref.shape[0]
    imps = []
    for b in range(n_sb):
        kc = kc_ref[b].astype(BF16)
        vc = vc_ref[b].astype(BF16)
        for k in range(N_KV_HEADS):
            s = _dot_nt((q_ref[b, k] * ATTN_SCALE).astype(BF16), kc)
            p = jnp.where(m, _softmax_rows(jnp.where(m, s, NEG_INF)), 0.0)
            o_ref[b, k] = _dot(p.astype(BF16), vc)
            imp = p[0:ts]
            for g in range(1, GQA):
                imp = imp + p[g * ts:(g + 1) * ts]
            imps.append(imp)
    imp_all = jnp.concatenate(imps, axis=0)
    n_rows = imp_all.shape[0]
    sel_w = amat_ref.shape[1]
    imp_sel = jnp.dot(imp_all, amat_ref[...], precision=lax.Precision.HIGHEST, preferred_element_type=F32)
    j = lax.broadcasted_iota(jnp.int32, (n_rows, sel_w), 1)
    jq = (PAST_LEN + lax.broadcasted_iota(jnp.int32, (n_rows, sel_w), 0) % ts) // SEL_BLOCK
    forced = (j == 0) | (j == jq) | (j == jq - 1)
    jf = j.astype(F32)
    lane_k = lax.broadcasted_iota(jnp.int32, (n_rows, LANES), 1)
    score = jnp.where(j > jq, -1.0, jnp.where(forced, FORCE_SCORE, imp_sel))
    score = jnp.where(j < n_sel, score, -jnp.inf)
    res = jnp.zeros((n_rows, LANES), F32)
    for i in range(N_SELECT):
        mx = jnp.max(score, axis=-1, keepdims=True)
        ix = jnp.min(jnp.where(score == mx, jf, 1e9), axis=-1, keepdims=True)
        res = jnp.where(lane_k == i, ix, res)
        score = jnp.where(jf == ix, -jnp.inf, score)
    idx_ref[...] = res.astype(jnp.int32).reshape(idx_ref.shape)


def _cmpattn(q_pad, kcmp, vcmp, ts):
    n_seq = q_pad.shape[0]
    n_cmp = (PAST_LEN + ts) // CMP_STRIDE - CMP_RATIO + 1
    n_sel = -(-(PAST_LEN + ts) // SEL_BLOCK)
    sel_w = -(-n_sel // LANES) * LANES
    per = SEL_BLOCK // CMP_STRIDE
    n = np.arange(kcmp.shape[1])[:, None]
    j = np.arange(sel_w)[None, :]
    amat = jnp.asarray(((n >= per * j - (CMP_RATIO - 1)) & (n <= per * j + per - 1) & (j < n_sel)).astype(np.float32))
    rows = GQA * ts
    sb = CMPATTN_SEQS
    return pl.pallas_call(
        functools.partial(_cmpattn_body, ts=ts, n_cmp=n_cmp, n_sel=n_sel),
        grid=(n_seq // sb,),
        in_specs=[pl.BlockSpec((sb, N_KV_HEADS, rows, LANES), lambda i: (i, 0, 0, 0)),
                  pl.BlockSpec((sb,) + kcmp.shape[1:], lambda i: (i, 0, 0)),
                  pl.BlockSpec((sb,) + vcmp.shape[1:], lambda i: (i, 0, 0)),
                  pl.BlockSpec(amat.shape, lambda i: (0, 0))],
        out_specs=[pl.BlockSpec((sb, N_KV_HEADS, rows, LANES), lambda i: (i, 0, 0, 0)),
                   pl.BlockSpec((sb, N_KV_HEADS * ts, LANES), lambda i: (i, 0, 0))],
        out_shape=[jax.ShapeDtypeStruct((n_seq, N_KV_HEADS, rows, LANES), F32),
                   jax.ShapeDtypeStruct((n_seq, N_KV_HEADS * ts, LANES), jnp.int32)],
        compiler_params=_cparams(("parallel",)),
        name="cmpattn_sample",
    )(q_pad, kcmp, vcmp, amat)


def _slc_copies(pools, bufs, sems, pt_ref, idx_ref, step, r, slot, ts, for_wait):
    if for_wait:
        pg = 0
    else:
        blk = idx_ref[step * (ts * N_SELECT) + r]
        page = jnp.minimum(blk // (PAGE_SIZE // SEL_BLOCK), N_PAGES - 1)
        pg = pt_ref[(step // N_KV_HEADS) * N_PAGES + page]
    return [pltpu.make_async_copy(pool.at[pg], buf.at[slot, r], sem.at[slot])
            for pool, buf, sem in zip(pools, bufs, sems)]


def _tail_body(pt_ref, idx_ref, kpool_ref, vpool_ref, q_ref, ksn_ref, vsn_ref, kwn_ref, vwn_ref, wk_ref, wv_ref,
               oc_ref, gt_ref, o_ref, wko_ref, wvo_ref, kbuf_ref, vbuf_ref, ksem_ref, vsem_ref, *, ts):
    step = pl.program_id(0) * N_KV_HEADS + pl.program_id(1)
    n_steps = pl.num_programs(0) * N_KV_HEADS
    slot = step % 2
    n_fetch = ts * N_SELECT

    pools, bufs, sems = (kpool_ref, vpool_ref), (kbuf_ref, vbuf_ref), (ksem_ref, vsem_ref)

    def start_all(st, sl):
        def body(r, c):
            for cp in _slc_copies(pools, bufs, sems, pt_ref, idx_ref, st, r, sl, ts, False):
                cp.start()
            return c
        lax.fori_loop(0, n_fetch, body, 0, unroll=8)

    @pl.when(step == 0)
    def _():
        start_all(0, 0)

    @pl.when(step + 1 < n_steps)
    def _():
        start_all(step + 1, 1 - slot)

    def wait_one(r, c):
        for cp in _slc_copies(pools, bufs, sems, pt_ref, idx_ref, step, r, slot, ts, True):
            cp.wait()
        return c
    lax.fori_loop(0, n_fetch, wait_one, 0, unroll=8)

    rows = GQA * ts
    q = (q_ref[0, 0] * ATTN_SCALE).astype(BF16)
    row_t = lax.broadcasted_iota(jnp.int32, (rows, 1), 0) % ts
    qpos = PAST_LEN + row_t
    lane = lax.broadcasted_iota(jnp.int32, (1, LANES), 1)
    lane_half, lane_off = lane // SEL_BLOCK, lane % SEL_BLOCK
    pad = jnp.zeros((LANES - ksn_ref.shape[1], LANES), F32)
    new_pos = PAST_LEN + lane
    new_ok = lane < ts
    blk_new = PAST_LEN // SEL_BLOCK

    def new_rows(ref):
        return jnp.concatenate([ref[0], pad], axis=0).astype(BF16)

    ksn, vsn = new_rows(ksn_ref), new_rows(vsn_ref)
    s_new = _dot_nt(q, ksn)
    o_s = jnp.zeros((rows, LANES), F32)
    grp = 4

    def pages(buf_ref, r0):
        return jnp.concatenate([buf_ref[slot, r0 + u].astype(BF16) for u in range(grp)], axis=1)

    for t in range(ts):
        pieces = []
        has_new = jnp.int32(0)
        for i0 in range(0, N_SELECT, grp):
            sc = _dot(q, pages(kbuf_ref, t * N_SELECT + i0))
            for u in range(grp):
                blk = idx_ref[step * n_fetch + t * N_SELECT + i0 + u]
                has_new = has_new | (blk == blk_new).astype(jnp.int32)
                want_half = jnp.where(blk < blk_new, blk % (PAGE_SIZE // SEL_BLOCK), -1)
                ok = (lane_half == want_half) & (blk * SEL_BLOCK + lane_off <= qpos)
                pieces.append(jnp.where(ok, sc[:, u * LANES:(u + 1) * LANES], NEG_INF))
        ok_new = (lane < jnp.where(has_new > 0, ts, 0)) & (new_pos <= qpos)
        pieces.append(jnp.where(ok_new, s_new, NEG_INF))
        p = _softmax_rows(jnp.concatenate(pieces, axis=1)).astype(BF16)
        o_t = _dot(p[:, N_SELECT * LANES:], vsn)
        for i0 in range(0, N_SELECT, grp):
            o_t = o_t + _dot_nt(p[:, i0 * LANES:(i0 + grp) * LANES], pages(vbuf_ref, t * N_SELECT + i0))
        o_s = jnp.where(row_t == t, o_t, o_s)

    wb = wk_ref.shape[2]
    kp = PAST_LEN - wb + lax.broadcasted_iota(jnp.int32, (1, wb), 1)
    ok_w = (kp >= 0) & (kp <= qpos) & (kp > qpos - WINDOW)
    ok_wn = new_ok & (new_pos <= qpos) & (new_pos > qpos - WINDOW)
    s_w = jnp.where(ok_w, _dot(q, wk_ref[0].astype(BF16)), NEG_INF)
    s_wn = jnp.where(ok_wn, _dot_nt(q, new_rows(kwn_ref)), NEG_INF)
    p = _softmax_rows(jnp.concatenate([s_w, s_wn], axis=1)).astype(BF16)
    o_w = _dot_nt(p[:, :wb], wv_ref[0].astype(BF16)) + _dot(p[:, wb:], new_rows(vwn_ref))

    gt = gt_ref[0, 0]
    o_ref[0, 0] = gt[:, 0:1] * oc_ref[0, 0] + gt[:, 1:2] * o_s + gt[:, 2:3] * o_w

    lane_w = lax.broadcasted_iota(jnp.int32, (1, LANES), 1)

    def slide(w_ref, new_ref, out_ref):
        shifted = pltpu.roll(w_ref[0], wb - ts, 1)
        new_t = pltpu.roll(jnp.concatenate([new_ref[0], pad], axis=0).T, LANES - ts, 1)
        out_ref[0, :, :wb - LANES] = shifted[:, :wb - LANES]
        out_ref[0, :, wb - LANES:] = jnp.where(lane_w >= LANES - ts, new_t, shifted[:, wb - LANES:])

    slide(wk_ref, kwn_ref, wko_ref)
    slide(wv_ref, vwn_ref, wvo_ref)


def _tail(page_table_flat, idx_flat, kpool_t, vpool_t, q_pad, ksn, vsn, kwn, vwn, wk_t, wv_t, o_cmp, gates_pad, ts):
    n_seq = q_pad.shape[0]
    rows = GQA * ts
    n_fetch = ts * N_SELECT
    per_head = pl.BlockSpec((1, 1, rows, LANES), lambda i, k, pt, ix: (i, k, 0, 0))
    per_seq = lambda a: pl.BlockSpec((1,) + a.shape[1:], lambda i, k, pt, ix: (i, 0, 0))
    hbm = pl.BlockSpec(memory_space=pl.ANY)
    return pl.pallas_call(
        functools.partial(_tail_body, ts=ts),
        grid_spec=pltpu.PrefetchScalarGridSpec(
            num_scalar_prefetch=2,
            grid=(n_seq, N_KV_HEADS),
            in_specs=[hbm, hbm, per_head, per_seq(ksn), per_seq(vsn), per_seq(kwn), per_seq(vwn),
                      per_seq(wk_t), per_seq(wv_t), per_head, per_head],
            out_specs=[per_head, per_seq(wk_t), per_seq(wv_t)],
            scratch_shapes=[pltpu.VMEM((2, n_fetch, PAGE_SIZE, KV_W), F32), pltpu.VMEM((2, n_fetch, PAGE_SIZE, KV_W), F32),
                            pltpu.SemaphoreType.DMA((2,)), pltpu.SemaphoreType.DMA((2,))]),
        out_shape=[jax.ShapeDtypeStruct((n_seq, N_KV_HEADS, rows, LANES), F32),
                   jax.ShapeDtypeStruct(wk_t.shape, F32), jax.ShapeDtypeStruct(wv_t.shape, F32)],
        compiler_params=_cparams(("arbitrary", "arbitrary")),
        name="slc_win_sample",
    )(page_table_flat, idx_flat, kpool_t, vpool_t, q_pad, ksn, vsn, kwn, vwn, wk_t, wv_t, o_cmp, gates_pad)


def _pad_heads(x, ts):
    n = x.shape[0] // ts
    w = x.shape[1] // N_HEADS
    return x.reshape(n, ts, N_KV_HEADS, GQA, w).transpose(0, 2, 3, 1, 4).reshape(n, N_KV_HEADS, GQA * ts, w)


def _q_on_kv_lanes(q, ts):
    x = _pad_heads(q, ts)
    z = jnp.zeros_like(x[:, 0])
    return jnp.stack([jnp.concatenate([x[:, 0], z], axis=-1), jnp.concatenate([z, x[:, 1]], axis=-1)], axis=1)


def _sample_attention_jnp(q, q_rot, kc, vc, ks, vs, kw, vw, gates, ck, cv, sk, sv, wk, wv, page_table, cw, gn):
    B, T = q.shape[:2]
    pos = PAST_LEN + jnp.arange(T, dtype=jnp.int32)
    q = q.reshape(B, T, N_HEADS, HEAD_DIM)
    q_rot = q_rot.reshape(B, T, N_HEADS, HEAD_DIM)
    kvr = lambda a: a.reshape(B, T, N_KV_HEADS, HEAD_DIM)
    kc, vc, ks, vs, kw, vw = map(kvr, (kc, vc, ks, vs, kw, vw))
    gates = gates[..., :3 * N_HEADS].reshape(B, T, N_HEADS, 3)
    past = lambda pool: pool[page_table].reshape(B, -1, N_KV_HEADS, HEAD_DIM)
    kc_all, vc_all = jnp.concatenate([past(ck), kc], axis=1), jnp.concatenate([past(cv), vc], axis=1)
    ks_all, vs_all = jnp.concatenate([past(sk), ks], axis=1), jnp.concatenate([past(sv), vs], axis=1)

    def compress_blocks(k, pos_emb, w1, b1, w2):
        Bk, Tk = k.shape[:2]
        nh = Tk // CMP_STRIDE
        nc = nh - CMP_RATIO + 1
        halves = k[:, :nh * CMP_STRIDE].reshape(Bk, nh, CMP_STRIDE, N_KV_HEADS, HEAD_DIM)
        pe = pos_emb.reshape(CMP_RATIO, CMP_STRIDE, HEAD_DIM)
        w1r = w1.reshape(CMP_RATIO, CMP_STRIDE, HEAD_DIM, CMP_HIDDEN)
        pre = b1
        for r in range(CMP_RATIO):
            pre = pre + jnp.einsum('bnskd,sdh->bnkh', halves[:, r:r + nc] + pe[r][:, None, :], w1r[r])
        return jax.nn.gelu(pre) @ w2

    k_cmp = compress_blocks(kc_all, *cw[0])
    v_cmp = compress_blocks(vc_all, *cw[1])
    nc = k_cmp.shape[1]
    qg = q.reshape(B, T, N_KV_HEADS, GQA, HEAD_DIM)
    s = jnp.einsum('bqkgd,bnkd->bqkgn', qg, k_cmp).astype(F32) * ATTN_SCALE
    blk_end = jnp.arange(nc) * CMP_STRIDE + CMP_LEN - 1
    m = (blk_end[None, :] <= pos[:, None])[None, :, None, None, :]
    p = jnp.where(m, jax.nn.softmax(jnp.where(m, s, NEG_INF), axis=-1), 0.0)
    o_cmp = jnp.einsum('bqkgn,bnkd->bqkgd', p, v_cmp).reshape(B, T, N_HEADS, HEAD_DIM)
    imp = p.sum(axis=3)
    n_sel = -(-(PAST_LEN + T) // SEL_BLOCK)
    per = SEL_BLOCK // CMP_STRIDE
    total = n_sel * per + CMP_RATIO + per
    pp = jnp.pad(imp, ((0, 0), (0, 0), (0, 0), (CMP_RATIO - 1, total - nc - (CMP_RATIO - 1))))
    impb = sum(pp[..., o:o + n_sel * per:per] for o in range(per + CMP_RATIO - 1))
    j = jnp.arange(n_sel)[None, None, None, :]
    jq = (pos // SEL_BLOCK)[None, :, None, None]
    forced = (j == 0) | (j == jq) | (j == jq - 1)
    score = jnp.where(j > jq, -1.0, jnp.where(forced, FORCE_SCORE, impb))
    _, idx = lax.top_k(score, min(N_SELECT, n_sel))
    valid = idx <= jq

    def to_blocks(k):
        kp = jnp.pad(k, ((0, 0), (0, n_sel * SEL_BLOCK - k.shape[1]), (0, 0), (0, 0)))
        return kp.reshape(B, n_sel, SEL_BLOCK, N_KV_HEADS, HEAD_DIM).transpose(0, 3, 1, 2, 4)

    kb, vb = to_blocks(ks_all), to_blocks(vs_all)
    qg = q_rot.reshape(B, T, N_KV_HEADS, GQA, HEAD_DIM)
    bi = jnp.arange(B)[:, None, None, None]
    hi = jnp.arange(N_KV_HEADS)[None, None, :, None]
    ksel, vsel = kb[bi, hi, idx], vb[bi, hi, idx]
    s = jnp.einsum('bqkgd,bqknsd->bqkgns', qg, ksel).astype(F32) * ATTN_SCALE
    kpos = idx[..., None] * SEL_BLOCK + jnp.arange(SEL_BLOCK)
    mask = valid[..., None] & (kpos <= pos[None, :, None, None, None])
    s = jnp.where(mask[:, :, :, None], s, NEG_INF)
    shp = s.shape
    p = jax.nn.softmax(s.reshape(shp[:-2] + (-1,)), axis=-1).reshape(shp)
    o_slc = jnp.einsum('bqkgns,bqknsd->bqkgd', p, vsel).reshape(B, T, N_HEADS, HEAD_DIM)
    wb = wk.shape[1]
    kw_all, vw_all = jnp.concatenate([wk, kw], axis=1), jnp.concatenate([wv, vw], axis=1)
    k_pos = PAST_LEN - wb + jnp.arange(wb + T, dtype=jnp.int32)
    s = jnp.einsum('bqkgd,bjkd->bqkgj', qg, kw_all).astype(F32) * ATTN_SCALE
    kp, qp = k_pos[None, :], pos[:, None]
    m = ((kp >= 0) & (kp <= qp) & (kp > qp - WINDOW))[None, :, None, None, :]
    p = jax.nn.softmax(jnp.where(m, s, NEG_INF), axis=-1)
    o_win = jnp.einsum('bqkgj,bjkd->bqkgd', p, vw_all).reshape(B, T, N_HEADS, HEAD_DIM)
    o = (gates[..., 0:1] * o_cmp + gates[..., 1:2] * o_slc + gates[..., 2:3] * o_win).reshape(B, T, D_ATTN)
    return _rms(o, gn).astype(BF16)


def kernel(x_prompt, x_sample, cache_cmp_k, cache_cmp_v, cache_slc_k, cache_slc_v, state_win_k, state_win_v, state_s5_re, state_s5_im, page_table, norm_ffn1, ffn1_gate, ffn1_up, ffn1_down, norm_mix, w_in, cmp_k_pos, cmp_k_w1, cmp_k_b1, cmp_k_w2, cmp_v_pos, cmp_v_w1, cmp_v_b1, cmp_v_w2, s5_log_dt, s5_a_re, s5_a_im, s5_b_re, s5_b_im, s5_c_re, s5_c_im, s5_d, s5_glu_w, s5_glu_b, norm_attn_out, norm_ssm_out, w_out, norm_ffn2, ffn2_gate, ffn2_up, ffn2_down, norm_final):
    depth = w_in.shape[0]
    assert depth == 1
    l = 0
    B, T, _ = x_prompt.shape
    BS, TS, _ = x_sample.shape
    row = lambda a: a[None, :]

    w = w_in[l]
    cut = D_ATTN + 6 * KV_W
    w_p = jnp.concatenate([w[:, :cut], w[:, cut + 3 * N_HEADS:], w[:, cut:cut + 3 * N_HEADS],
                           jnp.zeros((D_MODEL, LANES - 3 * N_HEADS), F32)], axis=1).astype(BF16)
    f1 = (row(norm_ffn1[l]), ffn1_gate[l].astype(BF16), ffn1_up[l].astype(BF16), ffn1_down[l].astype(BF16))
    f2 = (row(norm_ffn2[l]), ffn2_gate[l].astype(BF16), ffn2_up[l].astype(BF16), ffn2_down[l].astype(BF16))
    gfin = row(norm_final)
    wo = w_out[l].astype(BF16)
    wo_a, wo_s = wo[:D_ATTN], wo[D_ATTN:]
    cwk = _cmp_weights(cmp_k_pos[l], cmp_k_w1[l], cmp_k_b1[l], cmp_k_w2[l])
    cwv = _cmp_weights(cmp_v_pos[l], cmp_v_w1[l], cmp_v_b1[l], cmp_v_w2[l])
    sp = _s5_params(s5_log_dt[l], s5_a_re[l], s5_a_im[l], s5_b_re[l], s5_b_im[l], s5_c_re[l], s5_c_im[l],
                    s5_d[l], s5_glu_w[l], s5_glu_b[l], norm_ssm_out[l])
    gn_attn = row(norm_attn_out[l])
    g_mix = row(norm_mix[l])

    xp = x_prompt.reshape(B * T, D_MODEL)
    x1 = _ffn(xp, *f1, gfin, False)
    tabs = _rope_tables(jnp.arange(T, dtype=jnp.int32))
    (q, qr, kc, vc, _, _, _, _, gt, u,
     kc_t, vc_t, ks_t, vs_t, kw_t, vw_t) = _proj(x1, g_mix, w_p, *tabs, seq_t=T)
    b3 = lambda a: a.reshape(B, T, a.shape[-1])
    kcmp, vcmp = _cmp_prompt(b3(kc), b3(vc), cwk, cwv)
    attn = _attn_prompt(b3(q), b3(qr), kcmp, vcmp, ks_t, vs_t, kw_t, vw_t, b3(gt), gn_attn)
    zeros = jnp.zeros((B, SSM_GROUPS * SSM_STATE), F32)
    y_tm, p_re, p_im = _s5(b3(u).transpose(1, 0, 2), zeros, zeros, sp, 64)
    ssm = y_tm.transpose(1, 0, 2).reshape(B * T, D_SSM)
    x2 = _outproj(x1, attn.reshape(B * T, D_ATTN), ssm, wo_a, wo_s, gn_attn, False)
    y_prompt = _ffn(x2, *f2, gfin, True).reshape(B, T, D_MODEL)
    from_t = lambda a: a.reshape(a.shape[0], N_KV_HEADS, HEAD_DIM, a.shape[2]).transpose(0, 3, 1, 2)[None]
    wb = min(WINDOW, T)
    p_state = (from_t(kc_t), from_t(vc_t), from_t(ks_t), from_t(vs_t),
               from_t(kw_t[:, :, T - wb:]), from_t(vw_t[:, :, T - wb:]),
               p_re.reshape(1, B, SSM_GROUPS, SSM_STATE), p_im.reshape(1, B, SSM_GROUPS, SSM_STATE))

    xs = x_sample.reshape(BS * TS, D_MODEL)
    s1 = _ffn(xs, *f1, gfin, False)
    pos_s = PAST_LEN + jnp.arange(TS, dtype=jnp.int32)
    tabs_s = [jnp.tile(a, (BS, 1)) for a in _rope_tables(pos_s)]
    q, qr, kc, vc, ks, vs, kw, vw, gt, u = _proj(s1, g_mix, w_p, *tabs_s)
    s3 = lambda a: a.reshape(BS, TS, a.shape[-1])
    to_t = lambda a: a.transpose(0, 2, 3, 1).reshape(a.shape[0], KV_W, a.shape[1])
    pt_flat = page_table.reshape(-1)
    kcmp_s = _cmp_paged(pt_flat, to_t(cache_cmp_k[l]), *cwk, BS)
    vcmp_s = _cmp_paged(pt_flat, to_t(cache_cmp_v[l]), *cwv, BS)
    o_cmp_s, idx = _cmpattn(_q_on_kv_lanes(q, TS), kcmp_s, vcmp_s, TS)
    pad8 = lambda a: jnp.pad(s3(a), ((0, 0), (0, 8 - TS), (0, 0)))
    gates_s = _pad_heads(gt[:, :3 * N_HEADS], TS)
    gates_s = jnp.pad(gates_s, ((0, 0), (0, 0), (0, 0), (0, LANES - 3)))
    wk_t, wv_t = to_t(state_win_k[l]), to_t(state_win_v[l])
    comb, wk_new, wv_new = _tail(
        pt_flat, idx[:, :, :N_SELECT].reshape(-1), to_t(cache_slc_k[l]), to_t(cache_slc_v[l]),
        _q_on_kv_lanes(qr, TS), pad8(ks), pad8(vs), pad8(kw), pad8(vw), wk_t, wv_t, o_cmp_s, gates_s, TS)
    comb = comb.reshape(BS, N_KV_HEADS, GQA, TS, N_KV_HEADS, HEAD_DIM)
    comb = jnp.stack([comb[:, k, :, :, k] for k in range(N_KV_HEADS)], axis=1)
    attn_s = comb.transpose(0, 3, 1, 2, 4).reshape(BS * TS, D_ATTN)
    y_tm, s_re, s_im = _s5(s3(u).transpose(1, 0, 2), state_s5_re[l].reshape(BS, -1), state_s5_im[l].reshape(BS, -1), sp, TS)
    ssm_s = y_tm.transpose(1, 0, 2).reshape(BS * TS, D_SSM)
    s2 = _outproj(s1, attn_s, ssm_s, wo_a, wo_s, gn_attn, True)
    y_sample = _ffn(s2, *f2, gfin, True).reshape(BS, TS, D_MODEL)
    kv5 = lambda a, n: a.reshape(1, n, -1, N_KV_HEADS, HEAD_DIM)
    win_k, win_v = from_t(wk_new), from_t(wv_new)
    s_state = (kv5(kc, BS), kv5(vc, BS), kv5(ks, BS), kv5(vs, BS), win_k, win_v,
               s_re.reshape(1, BS, SSM_GROUPS, SSM_STATE), s_im.reshape(1, BS, SSM_GROUPS, SSM_STATE))

    return (y_prompt, y_sample) + p_state + s_state
```

```python
import functools
import math

import numpy as np
import jax
import jax.numpy as jnp
from jax import lax
from jax.experimental import pallas as pl
from jax.experimental.pallas import tpu as pltpu

F32 = jnp.float32
BF16 = jnp.bfloat16

D_MODEL = 1024
PAST_LEN = 16384
PAGE_SIZE = 128
D_ATTN = 512
D_SSM = 512
HEAD_DIM = 64
N_HEADS = 8
N_KV_HEADS = 2
GQA = 4
KV_W = 128
ROT_DIM = 16
ROPE_THETA = 500000.0
ATTN_SCALE = HEAD_DIM ** -0.5
CMP_LEN = 32
CMP_STRIDE = 16
CMP_RATIO = 2
CMP_HIDDEN = 256
SEL_BLOCK = 64
N_SELECT = 16
WINDOW = 512
SSM_CH = 16
SSM_GROUPS = 32
SSM_STATE = 64
D_FF = 2816
RMS_EPS = 1e-6
NEG_INF = -1e30
FORCE_SCORE = 1e4

LANES = 128
VMEM_LIMIT = 56 * 1024 * 1024

Z_Q = 0
Z_KV = D_ATTN
Z_U = D_ATTN + 6 * KV_W
Z_G = Z_U + D_SSM
Z_W = Z_G + LANES


def _cparams(sem):
    return pltpu.CompilerParams(dimension_semantics=sem, vmem_limit_bytes=VMEM_LIMIT)


def _rms(x, g):
    return x * lax.rsqrt(jnp.mean(x * x, axis=-1, keepdims=True) + RMS_EPS) * g


def _dot(a, b):
    return jnp.dot(a, b, preferred_element_type=F32)


def _dot_nt(a, b):
    return lax.dot_general(a, b, (((1,), (1,)), ((), ())), preferred_element_type=F32)


def _ffn_body(x_ref, g_ref, wg_ref, wu_ref, wd_ref, gf_ref, o_ref, *, final_norm):
    x = x_ref[...]
    h = _rms(x, g_ref[...]).astype(BF16)
    a = _dot(h, wg_ref[...])
    b = _dot(h, wu_ref[...])
    t = (a * jax.nn.sigmoid(a) * b).astype(BF16)
    y = x + 0.5 * _dot(t, wd_ref[...])
    if final_norm:
        y = _rms(y, gf_ref[...])
    o_ref[...] = y


def _ffn(x, g, wg, wu, wd, gf, final_norm):
    m = x.shape[0]
    tm = min(m, 512)
    row = pl.BlockSpec((tm, D_MODEL), lambda i: (i, 0))
    vec = pl.BlockSpec((1, D_MODEL), lambda i: (0, 0))
    resident = lambda a: pl.BlockSpec(a.shape, lambda i: (0, 0), pipeline_mode=pl.Buffered(1))
    return pl.pallas_call(
        functools.partial(_ffn_body, final_norm=final_norm),
        grid=(m // tm,),
        in_specs=[row, vec, resident(wg), resident(wu), resident(wd), vec],
        out_specs=row,
        out_shape=jax.ShapeDtypeStruct((m, D_MODEL), F32),
        compiler_params=_cparams(("parallel",)),
        name="ffn",
    )(x, g, wg, wu, wd, gf)


def _proj_body(x_ref, g_ref, w_ref, c_ref, sa_ref, sb_ref,
               q_ref, qr_ref, kc_ref, vc_ref, ks_ref, vs_ref, kw_ref, vw_ref, gt_ref, u_ref, *t_refs):
    h = _rms(x_ref[...], g_ref[...]).astype(BF16)
    z = _dot(h, w_ref[...])
    c, sa, sb = c_ref[...], sa_ref[...], sb_ref[...]

    def rope(v):
        return v * c + pltpu.roll(v, LANES - ROT_DIM // 2, 1) * sa + pltpu.roll(v, ROT_DIM // 2, 1) * sb

    q_ref[...] = z[:, Z_Q:Z_Q + D_ATTN]
    for i in range(D_ATTN // LANES):
        qr_ref[:, i * LANES:(i + 1) * LANES] = rope(z[:, Z_Q + i * LANES:Z_Q + (i + 1) * LANES])
    kv = [z[:, Z_KV + i * KV_W:Z_KV + (i + 1) * KV_W] for i in range(6)]
    kv[2] = rope(kv[2])
    kv[4] = rope(kv[4])
    for ref, v in zip((kc_ref, vc_ref, ks_ref, vs_ref, kw_ref, vw_ref), kv):
        ref[...] = v
    for ref, v in zip(t_refs, kv):
        ref[0] = v.T
    u_ref[...] = z[:, Z_U:Z_U + D_SSM]
    gt_ref[...] = jax.nn.sigmoid(z[:, Z_G:Z_G + LANES])


def _proj(x, g, w, rope_c, rope_sa, rope_sb, seq_t=None):
    m = x.shape[0]
    tm = min(m, 512)
    n_rep = rope_c.shape[0] // tm
    row = lambda i: (i, 0)
    const = lambda i: (0, 0)
    tab = lambda i: (i % n_rep, 0)
    widths = [D_ATTN, D_ATTN] + [KV_W] * 6 + [LANES, D_SSM]
    out_specs = [pl.BlockSpec((tm, wd), row) for wd in widths]
    out_shape = [jax.ShapeDtypeStruct((m, wd), F32) for wd in widths]
    if seq_t is not None:
        assert rope_c.shape[0] == seq_t
        out_specs += [pl.BlockSpec((1, KV_W, tm), lambda i: (i // n_rep, 0, i % n_rep))] * 6
        out_shape += [jax.ShapeDtypeStruct((m // seq_t, KV_W, seq_t), F32)] * 6
    return pl.pallas_call(
        _proj_body,
        grid=(m // tm,),
        in_specs=[pl.BlockSpec((tm, D_MODEL), row), pl.BlockSpec((1, D_MODEL), const),
                  pl.BlockSpec((D_MODEL, Z_W), const),
                  pl.BlockSpec((tm, LANES), tab), pl.BlockSpec((tm, LANES), tab), pl.BlockSpec((tm, LANES), tab)],
        out_specs=out_specs,
        out_shape=out_shape,
        compiler_params=_cparams(("parallel",)),
        name="proj",
    )(x, g, w, rope_c, rope_sa, rope_sb)


def _rope_tables(pos):
    half = ROT_DIM // 2
    inv_freq = jnp.power(ROPE_THETA, -(jnp.arange(half, dtype=F32) * 2.0 / ROT_DIM))
    ang = pos.astype(F32)[:, None] * inv_freq[None, :]
    cos, sin = jnp.cos(ang), jnp.sin(ang)
    n = pos.shape[0]
    one = jnp.ones((n, HEAD_DIM - ROT_DIM), F32)
    zero = jnp.zeros((n, HEAD_DIM - ROT_DIM), F32)
    zh = jnp.zeros((n, half), F32)
    c = jnp.concatenate([cos, cos, one], axis=1)
    sa = jnp.concatenate([-sin, zh, zero], axis=1)
    sb = jnp.concatenate([zh, sin, zero], axis=1)
    t2 = lambda a: jnp.concatenate([a, a], axis=1)
    return t2(c), t2(sa), t2(sb)


def _per_head_halves(xs):
    low = lax.broadcasted_iota(jnp.int32, xs[0].shape, 1) < HEAD_DIM
    heads = [[], []]
    for j in range(CMP_STRIDE // 2):
        a, b = xs[2 * j], xs[2 * j + 1]
        heads[0].append(jnp.where(low, a, pltpu.roll(b, HEAD_DIM, 1)))
        heads[1].append(jnp.where(low, pltpu.roll(a, HEAD_DIM, 1), b))
    return jnp.concatenate([jnp.concatenate(hd, axis=1) for hd in heads], axis=0).astype(BF16)


def _compress_tail(p_k, n_half, b_ref, w2_ref):
    out = None
    for k in range(N_KV_HEADS):
        pre = p_k[k][:, :CMP_HIDDEN] + pltpu.roll(p_k[k][:, CMP_HIDDEN:], n_half - 1, 0) + b_ref[...]
        ok = _dot(jax.nn.gelu(pre).astype(BF16), w2_ref[k])
        out = ok if out is None else out + ok
    return out


def _compress_rows(x_ref, n_half, w1_ref, b_ref, w2_ref):
    xs = [x_ref[0, pl.ds(s, n_half, stride=CMP_STRIDE), :] for s in range(CMP_STRIDE)]
    p = _dot(_per_head_halves(xs), w1_ref[...])
    return _compress_tail([p[k * n_half:(k + 1) * n_half] for k in range(N_KV_HEADS)], n_half, b_ref, w2_ref)


def _cmp_prompt_body(kc_ref, vc_ref, wk1_ref, bk_ref, wk2_ref, wv1_ref, bv_ref, wv2_ref, ko_ref, vo_ref, *, n_half):
    ko_ref[0] = _compress_rows(kc_ref, n_half, wk1_ref, bk_ref, wk2_ref)
    vo_ref[0] = _compress_rows(vc_ref, n_half, wv1_ref, bv_ref, wv2_ref)


def _cmp_prompt(kc, vc, wk, wv):
    b, t, _ = kc.shape
    n_half = t // CMP_STRIDE
    seq = pl.BlockSpec((1, t, KV_W), lambda i: (i, 0, 0))
    const = lambda shp: pl.BlockSpec(shp, lambda i: (0,) * len(shp))
    wspecs = [const(wk[0].shape), const(wk[1].shape), const(wk[2].shape)]
    out = pl.BlockSpec((1, n_half, KV_W), lambda i: (i, 0, 0))
    return pl.pallas_call(
        functools.partial(_cmp_prompt_body, n_half=n_half),
        grid=(b,),
        in_specs=[seq, seq] + wspecs + wspecs,
        out_specs=[out, out],
        out_shape=[jax.ShapeDtypeStruct((b, n_half, KV_W), F32)] * 2,
        compiler_params=_cparams(("parallel",)),
        name="cmp_prompt",
    )(kc, vc, *wk, *wv)


def _cmp_weights(pos_emb, w1, b1, w2):
    w1r = w1.reshape(CMP_RATIO, CMP_STRIDE * HEAD_DIM, CMP_HIDDEN)
    w1s = w1r.transpose(1, 0, 2).reshape(CMP_STRIDE * HEAD_DIM, CMP_RATIO * CMP_HIDDEN)
    pe = pos_emb.reshape(CMP_RATIO, CMP_STRIDE * HEAD_DIM)
    bias = b1 + jnp.einsum('rx,rxh->h', pe, w1r, precision=lax.Precision.HIGHEST)
    z = jnp.zeros_like(w2)
    w2k = jnp.stack([jnp.concatenate([w2, z], axis=1), jnp.concatenate([z, w2], axis=1)])
    return w1s.astype(BF16), bias[None, :], w2k.astype(BF16)


TQ = 512
KCHUNK = 512
SUBQ = 256
N_SEL_PROMPT = 32
assert TQ == KCHUNK and TQ % SUBQ == 0


def _softmax_rows(s):
    mx = jnp.max(s, axis=-1, keepdims=True)
    e = jnp.exp(s - mx)
    return e / jnp.sum(e, axis=-1, keepdims=True)


def _attn_prompt_body(q_ref, qr_ref, kcmp_ref, vcmp_ref, ks_ref, vs_ref, kw_ref, vw_ref, gt_ref,
                      amat_ref, emat_ref, gn_ref, o_ref, *, seq):
    t0 = pl.program_id(1) * TQ
    lane = lax.broadcasted_iota(jnp.int32, (TQ, LANES), 1)
    low = lane < HEAD_DIM
    tq = t0 + lax.broadcasted_iota(jnp.int32, (TQ, 1), 0)
    gates = gt_ref[0]

    def head_rows(src_ref, k):
        rows = []
        for g in range(GQA):
            h = GQA * k + g
            v = src_ref[0, :, (h // 2) * LANES:(h // 2 + 1) * LANES]
            if h % 2 != k:
                v = pltpu.roll(v, HEAD_DIM, 1)
            rows.append(jnp.where(low if k == 0 else jnp.logical_not(low), v * ATTN_SCALE, 0.0))
        return jnp.concatenate(rows, axis=0).astype(BF16)

    jj = lax.broadcasted_iota(jnp.int32, (N_SEL_PROMPT, TQ), 0)
    jq = (t0 + lax.broadcasted_iota(jnp.int32, (N_SEL_PROMPT, TQ), 1)) // SEL_BLOCK
    forced = (jj == 0) | (jj == jq) | (jj == jq - 1)

    n_cmp = kcmp_ref.shape[1]
    blk_end = lax.broadcasted_iota(jnp.int32, (1, n_cmp), 1) * CMP_STRIDE + (CMP_LEN - 1)
    m_cmp = (blk_end <= tq)[None]

    n_chunks = t0 // KCHUNK + 1

    comb, q_augs, q_rots, o_cs = [], [], [], []
    for k in range(N_KV_HEADS):
        qc = head_rows(q_ref, k)
        qr = head_rows(qr_ref, k)

        s = _dot_nt(qc, kcmp_ref[0].astype(BF16)).reshape(GQA, TQ, n_cmp)
        p = jnp.where(m_cmp, _softmax_rows(jnp.where(m_cmp, s, NEG_INF)), 0.0)
        o_c = _dot(p.reshape(GQA * TQ, n_cmp).astype(BF16), vcmp_ref[0].astype(BF16))
        imp = jnp.sum(p, axis=0)

        imp_sel = jnp.dot(amat_ref[...], imp.T, precision=lax.Precision.HIGHEST, preferred_element_type=F32)
        score = jnp.where(jj > jq, -1.0, jnp.where(forced, FORCE_SCORE, imp_sel))
        rank = jnp.zeros((N_SEL_PROMPT, TQ), F32)
        for i in range(N_SEL_PROMPT):
            si = score[i:i + 1, :]
            beats = (si > score) | ((si == score) & (jj > i))
            rank = rank + jnp.where(beats, 1.0, 0.0)
        sel_t = jnp.where((rank < N_SELECT) & (jj <= jq), 1.0, 0.0)
        sel_t = jnp.concatenate([sel_t, jnp.zeros((LANES - N_SEL_PROMPT, TQ), F32)], axis=0)
        sel = sel_t.T

        bias = ((sel - 1.0) * (-NEG_INF)).astype(BF16)
        q_augs.append(jnp.concatenate([qr, jnp.concatenate([bias] * GQA, axis=0)], axis=1))
        q_rots.append(qr)
        o_cs.append(o_c.reshape(GQA, TQ, LANES))

    sum_lanes = [HEAD_DIM * (1 - k) for k in range(N_KV_HEADS)]
    row_id = lax.broadcasted_iota(jnp.int32, (LANES, 1), 0)

    def attend(q2d, kk, vv, k, mask, m_i, acc):
        n_q, n_k = q2d.shape[0] // GQA, kk.shape[1]
        sc = _dot(q2d, kk).reshape(GQA, n_q, n_k)
        if mask is not None:
            sc = jnp.where(mask[None], sc, NEG_INF)
        m_n = jnp.max(sc, axis=-1, keepdims=True)
        if m_i is not None:
            m_n = jnp.maximum(m_i, m_n)
        pe = jnp.exp((sc - m_n).astype(BF16))
        vk = jnp.where(row_id == sum_lanes[k], 1.0, vv).astype(BF16)
        pv = _dot_nt(pe.reshape(GQA * n_q, n_k), vk).reshape(GQA, n_q, LANES)
        if m_i is not None:
            pv = jnp.exp(m_i - m_n) * acc + pv
        return m_n, pv

    def sub_rows(x2d, s):
        w = x2d.shape[1]
        return x2d.reshape(GQA, TQ, w)[:, s * SUBQ:(s + 1) * SUBQ].reshape(GQA * SUBQ, w)

    def chunk(c, carry):
        k0 = pl.multiple_of(c * KCHUNK, KCHUNK)
        kk = jnp.concatenate([ks_ref[0, :, pl.ds(k0, KCHUNK)].astype(BF16),
                              emat_ref[:, pl.ds(k0, KCHUNK)]], axis=0)
        vv = vs_ref[0, :, pl.ds(k0, KCHUNK)].astype(BF16)
        return tuple(attend(q_augs[k], kk, vv, k, None, *carry[k]) for k in range(N_KV_HEADS))

    init = tuple((jnp.full((GQA, TQ, 1), NEG_INF, F32), jnp.zeros((GQA, TQ, LANES), F32)) for _ in range(N_KV_HEADS))
    carry = lax.fori_loop(0, n_chunks - 1, chunk, init)

    acc_ss = [[] for _ in range(N_KV_HEADS)]
    acc_ws = [[] for _ in range(N_KV_HEADS)]
    w_sub = WINDOW + SUBQ
    for s in range(TQ // SUBQ):
        tq_s = t0 + s * SUBQ + lax.broadcasted_iota(jnp.int32, (SUBQ, 1), 0)
        n_k = (s + 1) * SUBQ
        k0 = pl.multiple_of(t0, TQ)
        kk = jnp.concatenate([ks_ref[0, :, pl.ds(k0, n_k)].astype(BF16), emat_ref[:, pl.ds(k0, n_k)]], axis=0)
        vv = vs_ref[0, :, pl.ds(k0, n_k)].astype(BF16)
        causal = (k0 + lax.broadcasted_iota(jnp.int32, (1, n_k), 1)) <= tq_s
        ws = pl.multiple_of(jnp.clip(t0 + s * SUBQ - WINDOW, 0, seq - w_sub), SUBQ)
        kw = kw_ref[0, :, pl.ds(ws, w_sub)].astype(BF16)
        vw = vw_ref[0, :, pl.ds(ws, w_sub)].astype(BF16)
        kp_w = ws + lax.broadcasted_iota(jnp.int32, (1, w_sub), 1)
        in_win = (kp_w <= tq_s) & (kp_w > tq_s - WINDOW)
        for k in range(N_KV_HEADS):
            m_i, acc = carry[k]
            rows = slice(s * SUBQ, (s + 1) * SUBQ)
            acc_ss[k].append(attend(sub_rows(q_augs[k], s), kk, vv, k, causal, m_i[:, rows], acc[:, rows])[1])
            acc_ws[k].append(attend(sub_rows(q_rots[k], s), kw, vw, k, in_win, None, None)[1])

    for k in range(N_KV_HEADS):
        acc_s, acc_w = jnp.concatenate(acc_ss[k], axis=1), jnp.concatenate(acc_ws[k], axis=1)
        sl = slice(sum_lanes[k], sum_lanes[k] + 1)
        for g in range(GQA):
            h = GQA * k + g
            g_s = gates[:, 3 * h + 1:3 * h + 2] / acc_s[g][:, sl]
            g_w = gates[:, 3 * h + 2:3 * h + 3] / acc_w[g][:, sl]
            comb.append(gates[:, 3 * h:3 * h + 1] * o_cs[k][g] + g_s * acc_s[g] + g_w * acc_w[g])

    cols = []
    for pr in range(N_HEADS // 2):
        k = (2 * pr) // GQA
        a, b = comb[2 * pr], comb[2 * pr + 1]
        a = pltpu.roll(a, HEAD_DIM, 1) if k == 1 else a
        b = pltpu.roll(b, HEAD_DIM, 1) if k == 0 else b
        cols.append(jnp.where(low, a, b))
    o = jnp.concatenate(cols, axis=1)
    o_ref[0] = _rms(o, gn_ref[...]).astype(o_ref.dtype)


def _sel_matrices(n_cmp_pad, n_sel, n_keys):
    j = np.arange(n_sel)[:, None]
    n = np.arange(n_cmp_pad)[None, :]
    per = SEL_BLOCK // CMP_STRIDE
    amat = ((n >= per * j - (CMP_RATIO - 1)) & (n <= per * j + per - 1)).astype(np.float32)
    jrow = np.arange(LANES)[:, None]
    key = np.arange(n_keys)[None, :]
    emat = (key // SEL_BLOCK == jrow).astype(np.float32)
    return jnp.asarray(amat), jnp.asarray(emat, dtype=BF16)


def _attn_prompt(q, qr, kcmp, vcmp, ks, vs, kw, vw, gt, gn):
    b, t, _ = q.shape
    amat, emat = _sel_matrices(kcmp.shape[1], N_SEL_PROMPT, t)
    qt = lambda w: pl.BlockSpec((1, TQ, w), lambda i, j: (i, j, 0))
    full = lambda a: pl.BlockSpec((1,) + a.shape[1:], lambda i, j: (i, 0, 0))
    const = lambda a: pl.BlockSpec(a.shape, lambda i, j: (0, 0))
    return pl.pallas_call(
        functools.partial(_attn_prompt_body, seq=t),
        grid=(b, t // TQ),
        in_specs=[qt(D_ATTN), qt(D_ATTN), full(kcmp), full(vcmp), full(ks), full(vs), full(kw), full(vw),
                  qt(LANES), const(amat), const(emat), const(gn)],
        out_specs=qt(D_ATTN),
        out_shape=jax.ShapeDtypeStruct((b, t, D_ATTN), BF16),
        compiler_params=_cparams(("parallel", "arbitrary")),
        name="attn_prompt",
    )(q, qr, kcmp, vcmp, ks, vs, kw, vw, gt, amat, emat, gn)


S5_LB = D_SSM // LANES
S5_SW = LANES // SSM_CH * SSM_STATE


def _s5_body(u_ref, h0r_ref, h0i_ref, lr_ref, li_ref, bre_ref, bim_ref, c_ref, d_ref, gw_ref, gb_ref, gn_ref,
             y_ref, hr_ref, hi_ref, y_s, *, tc, bb):
    step = pl.program_id(0)
    m = tc * bb

    @pl.when(step == 0)
    def _():
        hr_ref[...] = h0r_ref[...]
        hi_ref[...] = h0i_ref[...]

    u = pltpu.einshape("btd->(tb)d", u_ref[...])
    ub = u.astype(BF16)
    for j in range(S5_LB):
        uj = ub[:, j * LANES:(j + 1) * LANES]
        sl = slice(j * S5_SW, (j + 1) * S5_SW)
        xr = _dot(uj, bre_ref[j]).reshape(tc, bb, S5_SW)
        xi = _dot(uj, bim_ref[j]).reshape(tc, bb, S5_SW)
        lr = jnp.broadcast_to(lr_ref[:, sl], (bb, S5_SW))
        li = jnp.broadcast_to(li_ref[:, sl], (bb, S5_SW))
        hr, hi = hr_ref[:, sl], hi_ref[:, sl]
        hrs, his = [], []
        for t in range(tc):
            hr, hi = lr * hr - li * hi + xr[t], lr * hi + li * hr + xi[t]
            hrs.append(hr)
            his.append(hi)
        hr_ref[:, sl] = hr
        hi_ref[:, sl] = hi
        hcat = jnp.concatenate([jnp.concatenate(hrs, axis=0), jnp.concatenate(his, axis=0)], axis=1).astype(BF16)
        y_s[:, j * LANES:(j + 1) * LANES] = _dot(hcat, c_ref[j])
    z = jax.nn.gelu(y_s[...] + d_ref[...] * u)
    out = z * jax.nn.sigmoid(_dot(z.astype(BF16), gw_ref[...]) + gb_ref[...])
    y_ref[...] = pltpu.einshape("(tb)d->btd", _rms(out, gn_ref[...]), b=bb)


def _s5(u, h0r, h0i, sp, tc):
    bb, t, _ = u.shape
    const = lambda a: pl.BlockSpec(a.shape, lambda i: (0,) * a.ndim)
    args = (h0r, h0i, sp['lr'], sp['li'], sp['bre'], sp['bim'], sp['c'], sp['d'], sp['glu_w'], sp['glu_b'], sp['gn'])
    st = jax.ShapeDtypeStruct(h0r.shape, F32)
    blk = pl.BlockSpec((bb, tc, D_SSM), lambda i: (0, i, 0))
    return pl.pallas_call(
        functools.partial(_s5_body, tc=tc, bb=bb),
        grid=(t // tc,),
        in_specs=[blk] + [const(a) for a in args],
        out_specs=[blk, const(h0r), const(h0i)],
        out_shape=[jax.ShapeDtypeStruct((bb, t, D_SSM), F32), st, st],
        scratch_shapes=[pltpu.VMEM((tc * bb, D_SSM), F32)],
        compiler_params=_cparams(("arbitrary",)),
        name="s5",
    )(u, *args)


def _s5_params(log_dt, a_re, a_im, b_re, b_im, c_re, c_im, d, glu_w, glu_b, gn):
    dt = jnp.exp(log_dt)[:, None]
    mag = jnp.exp(a_re * dt)
    lr, li = mag * jnp.cos(a_im * dt), mag * jnp.sin(a_im * dt)
    den = a_re * a_re + a_im * a_im
    inv_r, inv_i = a_re / den, -a_im / den
    nr, ni = lr - 1.0, li
    fr, fi = nr * inv_r - ni * inv_i, nr * inv_i + ni * inv_r
    bbr = fr[..., None] * b_re - fi[..., None] * b_im
    bbi = fr[..., None] * b_im + fi[..., None] * b_re
    gl = LANES // SSM_CH
    eye = jnp.eye(gl, dtype=F32)

    def bmat(bb):
        x = bb.reshape(S5_LB, gl, SSM_STATE, SSM_CH)
        return jnp.einsum('jgpc,gh->jgchp', x, eye).reshape(S5_LB, LANES, S5_SW).astype(BF16)

    def cmat(cc):
        x = cc.reshape(S5_LB, gl, SSM_CH, SSM_STATE)
        return jnp.einsum('jgcp,gh->jgphc', x, eye).reshape(S5_LB, S5_SW, LANES)

    cm = jnp.concatenate([cmat(c_re), -cmat(c_im)], axis=1).astype(BF16)
    return dict(lr=lr.reshape(1, -1), li=li.reshape(1, -1), bre=bmat(bbr), bim=bmat(bbi), c=cm,
                d=d.reshape(1, -1), glu_w=glu_w.astype(BF16), glu_b=glu_b[None, :], gn=gn[None, :])


def _outproj_body(x_ref, a_ref, s_ref, wa_ref, ws_ref, gn_ref, o_ref, *, norm_attn):
    a = a_ref[...]
    if norm_attn:
        a = _rms(a, gn_ref[...])
    o_ref[...] = x_ref[...] + _dot(a.astype(BF16), wa_ref[...]) + _dot(s_ref[...].astype(BF16), ws_ref[...])


def _outproj(x, a, s, wa, ws, gn, norm_attn):
    m = x.shape[0]
    tm = min(m, 1024)
    row = lambda w: pl.BlockSpec((tm, w), lambda i: (i, 0))
    const = lambda a_: pl.BlockSpec(a_.shape, lambda i: (0, 0))
    return pl.pallas_call(
        functools.partial(_outproj_body, norm_attn=norm_attn),
        grid=(m // tm,),
        in_specs=[row(D_MODEL), row(D_ATTN), row(D_SSM), const(wa), const(ws), const(gn)],
        out_specs=row(D_MODEL),
        out_shape=jax.ShapeDtypeStruct((m, D_MODEL), F32),
        compiler_params=_cparams(("parallel",)),
        name="outproj",
    )(x, a, s, wa, ws, gn)


N_PAGES = PAST_LEN // PAGE_SIZE
CMP_CHUNK = 32
CMP_PITCH = CMP_STRIDE // 2


def _page_copy(pool_ref, buf_ref, sem_ref, pt_ref, b, i, slot):
    return pltpu.make_async_copy(pool_ref.at[pt_ref[b * N_PAGES + i]], buf_ref.at[slot, i], sem_ref.at[slot])


def _cmp_paged_body(pt_ref, pool_ref, w1_ref, b_ref, w2_ref, o_ref, buf_ref, rows_ref, p_ref, sem_ref):
    b = pl.program_id(0)
    slot = b % 2

    def start_all(bb, sl):
        def body(i, c):
            _page_copy(pool_ref, buf_ref, sem_ref, pt_ref, bb, i, sl).start()
            return c
        lax.fori_loop(0, N_PAGES, body, 0, unroll=8)

    @pl.when(b == 0)
    def _():
        start_all(0, 0)

    @pl.when(b + 1 < pl.num_programs(0))
    def _():
        start_all(b + 1, 1 - slot)

    def wait_one(i, c):
        _page_copy(pool_ref, buf_ref, sem_ref, pt_ref, b, i, slot).wait()
        return c
    lax.fori_loop(0, N_PAGES, wait_one, 0, unroll=8)

    hp = PAGE_SIZE // CMP_STRIDE
    hc = CMP_CHUNK * hp

    wp = PAGE_SIZE // 2

    def stage_a(c):
        for i in range(CMP_CHUNK):
            t = buf_ref[slot, c * CMP_CHUNK + i].astype(BF16).T
            rows_ref[c % 2, i * wp:(i + 1) * wp, :] = pltpu.bitcast(t, jnp.uint32)

    def stage_b(c):
        xs = []
        for j in range(CMP_PITCH):
            w = rows_ref[c % 2, pl.ds(j, hc, stride=CMP_PITCH), :]
            xs.append(pltpu.bitcast(w << 16, F32))
            xs.append(pltpu.bitcast(w & jnp.uint32(0xFFFF0000), F32))
        p = _dot(_per_head_halves(xs), w1_ref[...])
        for k in range(N_KV_HEADS):
            p_ref[k, c * hc:(c + 1) * hc, :] = p[k * hc:(k + 1) * hc]

    n_chunks = N_PAGES // CMP_CHUNK
    stage_a(0)
    for c in range(n_chunks):
        if c + 1 < n_chunks:
            stage_a(c + 1)
        stage_b(c)

    o_ref[0] = _compress_tail([p_ref[k] for k in range(N_KV_HEADS)], N_PAGES * hp, b_ref, w2_ref)


def _cmp_paged(page_table_flat, pool_t, w1, bias, w2, n_seq):
    n_half = PAST_LEN // CMP_STRIDE
    hc = CMP_CHUNK * PAGE_SIZE // CMP_STRIDE
    const = lambda a: pl.BlockSpec(a.shape, lambda i, pt: (0,) * a.ndim)
    return pl.pallas_call(
        _cmp_paged_body,
        grid_spec=pltpu.PrefetchScalarGridSpec(
            num_scalar_prefetch=1,
            grid=(n_seq,),
            in_specs=[pl.BlockSpec(memory_space=pl.ANY), const(w1), const(bias), const(w2)],
            out_specs=pl.BlockSpec((1, n_half, KV_W), lambda i, pt: (i, 0, 0)),
            scratch_shapes=[pltpu.VMEM((2, N_PAGES, PAGE_SIZE, KV_W), F32),
                            pltpu.VMEM((2, hc * CMP_PITCH, KV_W), jnp.uint32),
                            pltpu.VMEM((N_KV_HEADS, n_half, CMP_RATIO * CMP_HIDDEN), F32),
                            pltpu.SemaphoreType.DMA((2,))]),
        out_shape=jax.ShapeDtypeStruct((n_seq, n_half, KV_W), F32),
        compiler_params=_cparams(("arbitrary",)),
        name="cmp_paged",
    )(page_table_flat, pool_t, w1, bias, w2)


CMPATTN_SEQS = 8


def _cmpattn_body(q_ref, kc_ref, vc_ref, amat_ref, o_ref, idx_ref, *, ts, n_cmp, n_sel):
    rows = GQA * ts
    qpos = PAST_LEN + lax.broadcasted_iota(jnp.int32, (rows, 1), 0) % ts
    n_pad = kc_ref.shape[1]
    n_io = lax.broadcasted_iota(jnp.int32, (1, n_pad), 1)
    m = (n_io * CMP_STRIDE + (CMP_LEN - 1) <= qpos) & (n_io < n_cmp)
    n_sb = q_ref.shape[0]
    imps = []
    for b in range(n_sb):
        kc = kc_ref[b].astype(BF16)
        vc = vc_ref[b].astype(BF16)
        for k in range(N_KV_HEADS):
            s = _dot_nt((q_ref[b, k] * ATTN_SCALE).astype(BF16), kc)
            p = jnp.where(m, _softmax_rows(jnp.where(m, s, NEG_INF)), 0.0)
            o_ref[b, k] = _dot(p.astype(BF16), vc)
            imp = p[0:ts]
            for g in range(1, GQA):
                imp = imp + p[g * ts:(g + 1) * ts]
            imps.append(imp)
    imp_all = jnp.concatenate(imps, axis=0)
    n_rows = imp_all.shape[0]
    sel_w = amat_ref.shape[1]
    imp_sel = jnp.dot(imp_all, amat_ref[...], precision=lax.Precision.HIGHEST, preferred_element_type=F32)
    j = lax.broadcasted_iota(jnp.int32, (n_rows, sel_w), 1)
    jq = (PAST_LEN + lax.broadcasted_iota(jnp.int32, (n_rows, sel_w), 0) % ts) // SEL_BLOCK
    forced = (j == 0) | (j == jq) | (j == jq - 1)
    jf = j.astype(F32)
    lane_k = lax.broadcasted_iota(jnp.int32, (n_rows, LANES), 1)
    score = jnp.where(j > jq, -1.0, jnp.where(forced, FORCE_SCORE, imp_sel))
    score = jnp.where(j < n_sel, score, -jnp.inf)
    res = jnp.zeros((n_rows, LANES), F32)
    for i in range(N_SELECT):
        mx = jnp.max(score, axis=-1, keepdims=True)
        ix = jnp.min(jnp.where(score == mx, jf, 1e9), axis=-1, keepdims=True)
        res = jnp.where(lane_k == i, ix, res)
        score = jnp.where(jf == ix, -jnp.inf, score)
    idx_ref[...] = res.astype(jnp.int32).reshape(idx_ref.shape)


def _cmpattn(q_pad, kcmp, vcmp, ts):
    n_seq = q_pad.shape[0]
    n_cmp = (PAST_LEN + ts) // CMP_STRIDE - CMP_RATIO + 1
    n_sel = -(-(PAST_LEN + ts) // SEL_BLOCK)
    sel_w = -(-n_sel // LANES) * LANES
    per = SEL_BLOCK // CMP_STRIDE
    n = np.arange(kcmp.shape[1])[:, None]
    j = np.arange(sel_w)[None, :]
    amat = jnp.asarray(((n >= per * j - (CMP_RATIO - 1)) & (n <= per * j + per - 1) & (j < n_sel)).astype(np.float32))
    rows = GQA * ts
    sb = CMPATTN_SEQS
    return pl.pallas_call(
        functools.partial(_cmpattn_body, ts=ts, n_cmp=n_cmp, n_sel=n_sel),
        grid=(n_seq // sb,),
        in_specs=[pl.BlockSpec((sb, N_KV_HEADS, rows, LANES), lambda i: (i, 0, 0, 0)),
                  pl.BlockSpec((sb,) + kcmp.shape[1:], lambda i: (i, 0, 0)),
                  pl.BlockSpec((sb,) + vcmp.shape[1:], lambda i: (i, 0, 0)),
                  pl.BlockSpec(amat.shape, lambda i: (0, 0))],
        out_specs=[pl.BlockSpec((sb, N_KV_HEADS, rows, LANES), lambda i: (i, 0, 0, 0)),
                   pl.BlockSpec((sb, N_KV_HEADS * ts, LANES), lambda i: (i, 0, 0))],
        out_shape=[jax.ShapeDtypeStruct((n_seq, N_KV_HEADS, rows, LANES), F32),
                   jax.ShapeDtypeStruct((n_seq, N_KV_HEADS * ts, LANES), jnp.int32)],
        compiler_params=_cparams(("parallel",)),
        name="cmpattn_sample",
    )(q_pad, kcmp, vcmp, amat)


def _slc_copies(pools, bufs, sems, pt_ref, idx_ref, step, r, slot, ts, for_wait):
    if for_wait:
        pg = 0
    else:
        blk = idx_ref[step * (ts * N_SELECT) + r]
        page = jnp.minimum(blk // (PAGE_SIZE // SEL_BLOCK), N_PAGES - 1)
        pg = pt_ref[(step // N_KV_HEADS) * N_PAGES + page]
    return [pltpu.make_async_copy(pool.at[pg], buf.at[slot, r], sem.at[slot])
            for pool, buf, sem in zip(pools, bufs, sems)]


def _tail_body(pt_ref, idx_ref, kpool_ref, vpool_ref, q_ref, ksn_ref, vsn_ref, kwn_ref, vwn_ref, wk_ref, wv_ref,
               oc_ref, gt_ref, o_ref, wko_ref, wvo_ref, kbuf_ref, vbuf_ref, ksem_ref, vsem_ref, *, ts):
    step = pl.program_id(0) * N_KV_HEADS + pl.program_id(1)
    n_steps = pl.num_programs(0) * N_KV_HEADS
    slot = step % 2
    n_fetch = ts * N_SELECT

    pools, bufs, sems = (kpool_ref, vpool_ref), (kbuf_ref, vbuf_ref), (ksem_ref, vsem_ref)

    def start_all(st, sl):
        def body(r, c):
            for cp in _slc_copies(pools, bufs, sems, pt_ref, idx_ref, st, r, sl, ts, False):
                cp.start()
            return c
        lax.fori_loop(0, n_fetch, body, 0, unroll=8)

    @pl.when(step == 0)
    def _():
        start_all(0, 0)

    @pl.when(step + 1 < n_steps)
    def _():
        start_all(step + 1, 1 - slot)

    def wait_one(r, c):
        for cp in _slc_copies(pools, bufs, sems, pt_ref, idx_ref, step, r, slot, ts, True):
            cp.wait()
        return c
    lax.fori_loop(0, n_fetch, wait_one, 0, unroll=8)

    rows = GQA * ts
    q = (q_ref[0, 0] * ATTN_SCALE).astype(BF16)
    row_t = lax.broadcasted_iota(jnp.int32, (rows, 1), 0) % ts
    qpos = PAST_LEN + row_t
    lane = lax.broadcasted_iota(jnp.int32, (1, LANES), 1)
    lane_half, lane_off = lane // SEL_BLOCK, lane % SEL_BLOCK
    pad = jnp.zeros((LANES - ksn_ref.shape[1], LANES), F32)
    new_pos = PAST_LEN + lane
    new_ok = lane < ts
    blk_new = PAST_LEN // SEL_BLOCK

    def new_rows(ref):
        return jnp.concatenate([ref[0], pad], axis=0).astype(BF16)

    ksn, vsn = new_rows(ksn_ref), new_rows(vsn_ref)
    s_new = _dot_nt(q, ksn)
    o_s = jnp.zeros((rows, LANES), F32)
    grp = 4

    def pages(buf_ref, r0):
        return jnp.concatenate([buf_ref[slot, r0 + u].astype(BF16) for u in range(grp)], axis=1)

    for t in range(ts):
        pieces = []
        has_new = jnp.int32(0)
        for i0 in range(0, N_SELECT, grp):
            sc = _dot(q, pages(kbuf_ref, t * N_SELECT + i0))
            for u in range(grp):
                blk = idx_ref[step * n_fetch + t * N_SELECT + i0 + u]
                has_new = has_new | (blk == blk_new).astype(jnp.int32)
                want_half = jnp.where(blk < blk_new, blk % (PAGE_SIZE // SEL_BLOCK), -1)
                ok = (lane_half == want_half) & (blk * SEL_BLOCK + lane_off <= qpos)
                pieces.append(jnp.where(ok, sc[:, u * LANES:(u + 1) * LANES], NEG_INF))
        ok_new = (lane < jnp.where(has_new > 0, ts, 0)) & (new_pos <= qpos)
        pieces.append(jnp.where(ok_new, s_new, NEG_INF))
        p = _softmax_rows(jnp.concatenate(pieces, axis=1)).astype(BF16)
        o_t = _dot(p[:, N_SELECT * LANES:], vsn)
        for i0 in range(0, N_SELECT, grp):
            o_t = o_t + _dot_nt(p[:, i0 * LANES:(i0 + grp) * LANES], pages(vbuf_ref, t * N_SELECT + i0))
        o_s = jnp.where(row_t == t, o_t, o_s)

    wb = wk_ref.shape[2]
    kp = PAST_LEN - wb + lax.broadcasted_iota(jnp.int32, (1, wb), 1)
    ok_w = (kp >= 0) & (kp <= qpos) & (kp > qpos - WINDOW)
    ok_wn = new_ok & (new_pos <= qpos) & (new_pos > qpos - WINDOW)
    s_w = jnp.where(ok_w, _dot(q, wk_ref[0].astype(BF16)), NEG_INF)
    s_wn = jnp.where(ok_wn, _dot_nt(q, new_rows(kwn_ref)), NEG_INF)
    p = _softmax_rows(jnp.concatenate([s_w, s_wn], axis=1)).astype(BF16)
    o_w = _dot_nt(p[:, :wb], wv_ref[0].astype(BF16)) + _dot(p[:, wb:], new_rows(vwn_ref))

    gt = gt_ref[0, 0]
    o_ref[0, 0] = gt[:, 0:1] * oc_ref[0, 0] + gt[:, 1:2] * o_s + gt[:, 2:3] * o_w

    lane_w = lax.broadcasted_iota(jnp.int32, (1, LANES), 1)

    def slide(w_ref, new_ref, out_ref):
        shifted = pltpu.roll(w_ref[0], wb - ts, 1)
        new_t = pltpu.roll(jnp.concatenate([new_ref[0], pad], axis=0).T, LANES - ts, 1)
        out_ref[0, :, :wb - LANES] = shifted[:, :wb - LANES]
        out_ref[0, :, wb - LANES:] = jnp.where(lane_w >= LANES - ts, new_t, shifted[:, wb - LANES:])

    slide(wk_ref, kwn_ref, wko_ref)
    slide(wv_ref, vwn_ref, wvo_ref)


def _tail(page_table_flat, idx_flat, kpool_t, vpool_t, q_pad, ksn, vsn, kwn, vwn, wk_t, wv_t, o_cmp, gates_pad, ts):
    n_seq = q_pad.shape[0]
    rows = GQA * ts
    n_fetch = ts * N_SELECT
    per_head = pl.BlockSpec((1, 1, rows, LANES), lambda i, k, pt, ix: (i, k, 0, 0))
    per_seq = lambda a: pl.BlockSpec((1,) + a.shape[1:], lambda i, k, pt, ix: (i, 0, 0))
    hbm = pl.BlockSpec(memory_space=pl.ANY)
    return pl.pallas_call(
        functools.partial(_tail_body, ts=ts),
        grid_spec=pltpu.PrefetchScalarGridSpec(
            num_scalar_prefetch=2,
            grid=(n_seq, N_KV_HEADS),
            in_specs=[hbm, hbm, per_head, per_seq(ksn), per_seq(vsn), per_seq(kwn), per_seq(vwn),
                      per_seq(wk_t), per_seq(wv_t), per_head, per_head],
            out_specs=[per_head, per_seq(wk_t), per_seq(wv_t)],
            scratch_shapes=[pltpu.VMEM((2, n_fetch, PAGE_SIZE, KV_W), F32), pltpu.VMEM((2, n_fetch, PAGE_SIZE, KV_W), F32),
                            pltpu.SemaphoreType.DMA((2,)), pltpu.SemaphoreType.DMA((2,))]),
        out_shape=[jax.ShapeDtypeStruct((n_seq, N_KV_HEADS, rows, LANES), F32),
                   jax.ShapeDtypeStruct(wk_t.shape, F32), jax.ShapeDtypeStruct(wv_t.shape, F32)],
        compiler_params=_cparams(("arbitrary", "arbitrary")),
        name="slc_win_sample",
    )(page_table_flat, idx_flat, kpool_t, vpool_t, q_pad, ksn, vsn, kwn, vwn, wk_t, wv_t, o_cmp, gates_pad)


def _pad_heads(x, ts):
    n = x.shape[0] // ts
    w = x.shape[1] // N_HEADS
    return x.reshape(n, ts, N_KV_HEADS, GQA, w).transpose(0, 2, 3, 1, 4).reshape(n, N_KV_HEADS, GQA * ts, w)


def _q_on_kv_lanes(q, ts):
    x = _pad_heads(q, ts)
    z = jnp.zeros_like(x[:, 0])
    return jnp.stack([jnp.concatenate([x[:, 0], z], axis=-1), jnp.concatenate([z, x[:, 1]], axis=-1)], axis=1)


def kernel(x_prompt, x_sample, cache_cmp_k, cache_cmp_v, cache_slc_k, cache_slc_v, state_win_k, state_win_v, state_s5_re, state_s5_im, page_table, norm_ffn1, ffn1_gate, ffn1_up, ffn1_down, norm_mix, w_in, cmp_k_pos, cmp_k_w1, cmp_k_b1, cmp_k_w2, cmp_v_pos, cmp_v_w1, cmp_v_b1, cmp_v_w2, s5_log_dt, s5_a_re, s5_a_im, s5_b_re, s5_b_im, s5_c_re, s5_c_im, s5_d, s5_glu_w, s5_glu_b, norm_attn_out, norm_ssm_out, w_out, norm_ffn2, ffn2_gate, ffn2_up, ffn2_down, norm_final):
    depth = w_in.shape[0]
    assert depth == 1
    l = 0
    B, T, _ = x_prompt.shape
    BS, TS, _ = x_sample.shape
    assert TS < CMP_STRIDE and TS <= 8
    row = lambda a: a[None, :]

    w = w_in[l]
    cut = D_ATTN + 6 * KV_W
    w_p = jnp.concatenate([w[:, :cut], w[:, cut + 3 * N_HEADS:], w[:, cut:cut + 3 * N_HEADS],
                           jnp.zeros((D_MODEL, LANES - 3 * N_HEADS), F32)], axis=1).astype(BF16)
    f1 = (row(norm_ffn1[l]), ffn1_gate[l].astype(BF16), ffn1_up[l].astype(BF16), ffn1_down[l].astype(BF16))
    f2 = (row(norm_ffn2[l]), ffn2_gate[l].astype(BF16), ffn2_up[l].astype(BF16), ffn2_down[l].astype(BF16))
    gfin = row(norm_final)
    wo = w_out[l].astype(BF16)
    wo_a, wo_s = wo[:D_ATTN], wo[D_ATTN:]
    cwk = _cmp_weights(cmp_k_pos[l], cmp_k_w1[l], cmp_k_b1[l], cmp_k_w2[l])
    cwv = _cmp_weights(cmp_v_pos[l], cmp_v_w1[l], cmp_v_b1[l], cmp_v_w2[l])
    sp = _s5_params(s5_log_dt[l], s5_a_re[l], s5_a_im[l], s5_b_re[l], s5_b_im[l], s5_c_re[l], s5_c_im[l],
                    s5_d[l], s5_glu_w[l], s5_glu_b[l], norm_ssm_out[l])
    gn_attn = row(norm_attn_out[l])
    g_mix = row(norm_mix[l])

    xp = x_prompt.reshape(B * T, D_MODEL)
    x1 = _ffn(xp, *f1, gfin, False)
    tabs = _rope_tables(jnp.arange(T, dtype=jnp.int32))
    (q, qr, kc, vc, _, _, _, _, gt, u,
     kc_t, vc_t, ks_t, vs_t, kw_t, vw_t) = _proj(x1, g_mix, w_p, *tabs, seq_t=T)
    b3 = lambda a: a.reshape(B, T, a.shape[-1])
    kcmp, vcmp = _cmp_prompt(b3(kc), b3(vc), cwk, cwv)
    attn = _attn_prompt(b3(q), b3(qr), kcmp, vcmp, ks_t, vs_t, kw_t, vw_t, b3(gt), gn_attn)
    zeros = jnp.zeros((B, SSM_GROUPS * SSM_STATE), F32)
    ssm, p_re, p_im = _s5(b3(u), zeros, zeros, sp, 64)
    ssm = ssm.reshape(B * T, D_SSM)
    x2 = _outproj(x1, attn.reshape(B * T, D_ATTN), ssm, wo_a, wo_s, gn_attn, False)
    y_prompt = _ffn(x2, *f2, gfin, True).reshape(B, T, D_MODEL)
    from_t = lambda a: a.reshape(a.shape[0], N_KV_HEADS, HEAD_DIM, a.shape[2]).transpose(0, 3, 1, 2)[None]
    wb = min(WINDOW, T)
    p_state = (from_t(kc_t), from_t(vc_t), from_t(ks_t), from_t(vs_t),
               from_t(kw_t[:, :, T - wb:]), from_t(vw_t[:, :, T - wb:]),
               p_re.reshape(1, B, SSM_GROUPS, SSM_STATE), p_im.reshape(1, B, SSM_GROUPS, SSM_STATE))

    xs = x_sample.reshape(BS * TS, D_MODEL)
    s1 = _ffn(xs, *f1, gfin, False)
    pos_s = PAST_LEN + jnp.arange(TS, dtype=jnp.int32)
    tabs_s = [jnp.tile(a, (BS, 1)) for a in _rope_tables(pos_s)]
    q, qr, kc, vc, ks, vs, kw, vw, gt, u = _proj(s1, g_mix, w_p, *tabs_s)
    s3 = lambda a: a.reshape(BS, TS, a.shape[-1])
    to_t = lambda a: a.transpose(0, 2, 3, 1).reshape(a.shape[0], KV_W, a.shape[1])
    pt_flat = page_table.reshape(-1)
    kcmp_s = _cmp_paged(pt_flat, to_t(cache_cmp_k[l]), *cwk, BS)
    vcmp_s = _cmp_paged(pt_flat, to_t(cache_cmp_v[l]), *cwv, BS)
    o_cmp_s, idx = _cmpattn(_q_on_kv_lanes(q, TS), kcmp_s, vcmp_s, TS)
    pad8 = lambda a: jnp.pad(s3(a), ((0, 0), (0, 8 - TS), (0, 0)))
    gates_s = _pad_heads(gt[:, :3 * N_HEADS], TS)
    gates_s = jnp.pad(gates_s, ((0, 0), (0, 0), (0, 0), (0, LANES - 3)))
    wk_t, wv_t = to_t(state_win_k[l]), to_t(state_win_v[l])
    comb, wk_new, wv_new = _tail(
        pt_flat, idx[:, :, :N_SELECT].reshape(-1), to_t(cache_slc_k[l]), to_t(cache_slc_v[l]),
        _q_on_kv_lanes(qr, TS), pad8(ks), pad8(vs), pad8(kw), pad8(vw), wk_t, wv_t, o_cmp_s, gates_s, TS)
    comb = comb.reshape(BS, N_KV_HEADS, GQA, TS, N_KV_HEADS, HEAD_DIM)
    comb = jnp.stack([comb[:, k, :, :, k] for k in range(N_KV_HEADS)], axis=1)
    attn_s = comb.transpose(0, 3, 1, 2, 4).reshape(BS * TS, D_ATTN)
    ssm_s, s_re, s_im = _s5(s3(u), state_s5_re[l].reshape(BS, -1), state_s5_im[l].reshape(BS, -1), sp, TS)
    ssm_s = ssm_s.reshape(BS * TS, D_SSM)
    s2 = _outproj(s1, attn_s, ssm_s, wo_a, wo_s, gn_attn, True)
    y_sample = _ffn(s2, *f2, gfin, True).reshape(BS, TS, D_MODEL)
    kv5 = lambda a, n: a.reshape(1, n, -1, N_KV_HEADS, HEAD_DIM)
    win_k, win_v = from_t(wk_new), from_t(wv_new)
    s_state = (kv5(kc, BS), kv5(vc, BS), kv5(ks, BS), kv5(vs, BS), win_k, win_v,
               s_re.reshape(1, BS, SSM_GROUPS, SSM_STATE), s_im.reshape(1, BS, SSM_GROUPS, SSM_STATE))

    return (y_prompt, y_sample) + p_state + s_state
```

```python
import functools

import numpy as np
import jax
import jax.numpy as jnp
from jax import lax
from jax.experimental import pallas as pl
from jax.experimental.pallas import tpu as pltpu

F32 = jnp.float32
BF16 = jnp.bfloat16

D_MODEL = 1024
PAST_LEN = 16384
PAGE_SIZE = 128
D_ATTN = 512
D_SSM = 512
HEAD_DIM = 64
N_HEADS = 8
N_KV_HEADS = 2
GQA = 4
KV_W = 128
ROT_DIM = 16
ROPE_THETA = 500000.0
ATTN_SCALE = HEAD_DIM ** -0.5
CMP_LEN = 32
CMP_STRIDE = 16
CMP_RATIO = 2
CMP_HIDDEN = 256
SEL_BLOCK = 64
N_SELECT = 16
WINDOW = 512
SSM_CH = 16
SSM_GROUPS = 32
SSM_STATE = 64
D_FF = 2816
RMS_EPS = 1e-6
NEG_INF = -1e30
FORCE_SCORE = 1e4

LANES = 128
VMEM_LIMIT = 56 * 1024 * 1024

Z_Q = 0
Z_KV = D_ATTN
Z_U = D_ATTN + 6 * KV_W
Z_G = Z_U + D_SSM
Z_W = Z_G + LANES


def _cparams(sem):
    return pltpu.CompilerParams(dimension_semantics=sem, vmem_limit_bytes=VMEM_LIMIT)


def _rms(x, g):
    return x * lax.rsqrt(jnp.mean(x * x, axis=-1, keepdims=True) + RMS_EPS) * g


def _dot(a, b):
    return jnp.dot(a, b, preferred_element_type=F32)


def _dot_nt(a, b):
    return lax.dot_general(a, b, (((1,), (1,)), ((), ())), preferred_element_type=F32)


def _ffn_body(x_ref, g_ref, wg_ref, wu_ref, wd_ref, gf_ref, o_ref, *, final_norm):
    x = x_ref[...]
    h = _rms(x, g_ref[...]).astype(BF16)
    a = _dot(h, wg_ref[...])
    b = _dot(h, wu_ref[...])
    t = (a * jax.nn.sigmoid(a) * b).astype(BF16)
    y = x + 0.5 * _dot(t, wd_ref[...])
    if final_norm:
        y = _rms(y, gf_ref[...])
    o_ref[...] = y


def _ffn(x, g, wg, wu, wd, gf, final_norm):
    m = x.shape[0]
    tm = min(m, 512)
    row = pl.BlockSpec((tm, D_MODEL), lambda i: (i, 0))
    vec = pl.BlockSpec((1, D_MODEL), lambda i: (0, 0))
    resident = lambda a: pl.BlockSpec(a.shape, lambda i: (0, 0), pipeline_mode=pl.Buffered(1))
    return pl.pallas_call(
        functools.partial(_ffn_body, final_norm=final_norm),
        grid=(m // tm,),
        in_specs=[row, vec, resident(wg), resident(wu), resident(wd), vec],
        out_specs=row,
        out_shape=jax.ShapeDtypeStruct((m, D_MODEL), F32),
        compiler_params=_cparams(("parallel",)),
        name="ffn",
    )(x, g, wg, wu, wd, gf)


def _proj_body(x_ref, g_ref, w_ref, c_ref, sa_ref, sb_ref,
               q_ref, qr_ref, kc_ref, vc_ref, ks_ref, vs_ref, kw_ref, vw_ref, gt_ref, u_ref, *t_refs):
    h = _rms(x_ref[...], g_ref[...]).astype(BF16)
    z = _dot(h, w_ref[...])
    c, sa, sb = c_ref[...], sa_ref[...], sb_ref[...]

    def rope(v):
        return v * c + pltpu.roll(v, LANES - ROT_DIM // 2, 1) * sa + pltpu.roll(v, ROT_DIM // 2, 1) * sb

    q_ref[...] = z[:, Z_Q:Z_Q + D_ATTN]
    for i in range(D_ATTN // LANES):
        qr_ref[:, i * LANES:(i + 1) * LANES] = rope(z[:, Z_Q + i * LANES:Z_Q + (i + 1) * LANES])
    kv = [z[:, Z_KV + i * KV_W:Z_KV + (i + 1) * KV_W] for i in range(6)]
    kv[2] = rope(kv[2])
    kv[4] = rope(kv[4])
    for ref, v in zip((kc_ref, vc_ref, ks_ref, vs_ref, kw_ref, vw_ref), kv):
        ref[...] = v
    for ref, v in zip(t_refs, kv):
        ref[0] = v.T
    u_ref[...] = z[:, Z_U:Z_U + D_SSM]
    gt_ref[...] = jax.nn.sigmoid(z[:, Z_G:Z_G + LANES])


def _proj(x, g, w, rope_c, rope_sa, rope_sb, seq_t=None):
    m = x.shape[0]
    tm = min(m, 512)
    n_rep = rope_c.shape[0] // tm
    row = lambda i: (i, 0)
    const = lambda i: (0, 0)
    tab = lambda i: (i % n_rep, 0)
    widths = [D_ATTN, D_ATTN] + [KV_W] * 6 + [LANES, D_SSM]
    out_specs = [pl.BlockSpec((tm, wd), row) for wd in widths]
    out_shape = [jax.ShapeDtypeStruct((m, wd), F32) for wd in widths]
    if seq_t is not None:
        assert rope_c.shape[0] == seq_t
        out_specs += [pl.BlockSpec((1, KV_W, tm), lambda i: (i // n_rep, 0, i % n_rep))] * 6
        out_shape += [jax.ShapeDtypeStruct((m // seq_t, KV_W, seq_t), F32)] * 6
    return pl.pallas_call(
        _proj_body,
        grid=(m // tm,),
        in_specs=[pl.BlockSpec((tm, D_MODEL), row), pl.BlockSpec((1, D_MODEL), const),
                  pl.BlockSpec((D_MODEL, Z_W), const),
                  pl.BlockSpec((tm, LANES), tab), pl.BlockSpec((tm, LANES), tab), pl.BlockSpec((tm, LANES), tab)],
        out_specs=out_specs,
        out_shape=out_shape,
        compiler_params=_cparams(("parallel",)),
        name="proj",
    )(x, g, w, rope_c, rope_sa, rope_sb)


def _rope_tables(pos):
    half = ROT_DIM // 2
    inv_freq = jnp.power(ROPE_THETA, -(jnp.arange(half, dtype=F32) * 2.0 / ROT_DIM))
    ang = pos.astype(F32)[:, None] * inv_freq[None, :]
    cos, sin = jnp.cos(ang), jnp.sin(ang)
    n = pos.shape[0]
    one = jnp.ones((n, HEAD_DIM - ROT_DIM), F32)
    zero = jnp.zeros((n, HEAD_DIM - ROT_DIM), F32)
    zh = jnp.zeros((n, half), F32)
    c = jnp.concatenate([cos, cos, one], axis=1)
    sa = jnp.concatenate([-sin, zh, zero], axis=1)
    sb = jnp.concatenate([zh, sin, zero], axis=1)
    t2 = lambda a: jnp.concatenate([a, a], axis=1)
    return t2(c), t2(sa), t2(sb)


def _per_head_halves(xs):
    low = lax.broadcasted_iota(jnp.int32, xs[0].shape, 1) < HEAD_DIM
    heads = [[], []]
    for j in range(CMP_STRIDE // 2):
        a, b = xs[2 * j], xs[2 * j + 1]
        heads[0].append(jnp.where(low, a, pltpu.roll(b, HEAD_DIM, 1)))
        heads[1].append(jnp.where(low, pltpu.roll(a, HEAD_DIM, 1), b))
    return jnp.concatenate([jnp.concatenate(hd, axis=1) for hd in heads], axis=0).astype(BF16)


def _compress_tail(p_k, n_half, b_ref, w2_ref):
    out = None
    for k in range(N_KV_HEADS):
        pre = p_k[k][:, :CMP_HIDDEN] + pltpu.roll(p_k[k][:, CMP_HIDDEN:], n_half - 1, 0) + b_ref[...]
        ok = _dot(jax.nn.gelu(pre).astype(BF16), w2_ref[k])
        out = ok if out is None else out + ok
    return out


def _compress_rows(x_ref, n_half, w1_ref, b_ref, w2_ref):
    xs = [x_ref[0, pl.ds(s, n_half, stride=CMP_STRIDE), :] for s in range(CMP_STRIDE)]
    p = _dot(_per_head_halves(xs), w1_ref[...])
    return _compress_tail([p[k * n_half:(k + 1) * n_half] for k in range(N_KV_HEADS)], n_half, b_ref, w2_ref)


def _cmp_prompt_body(kc_ref, vc_ref, wk1_ref, bk_ref, wk2_ref, wv1_ref, bv_ref, wv2_ref, ko_ref, vo_ref, *, n_half):
    ko_ref[0] = _compress_rows(kc_ref, n_half, wk1_ref, bk_ref, wk2_ref)
    vo_ref[0] = _compress_rows(vc_ref, n_half, wv1_ref, bv_ref, wv2_ref)


def _cmp_prompt(kc, vc, wk, wv):
    b, t, _ = kc.shape
    n_half = t // CMP_STRIDE
    seq = pl.BlockSpec((1, t, KV_W), lambda i: (i, 0, 0))
    const = lambda shp: pl.BlockSpec(shp, lambda i: (0,) * len(shp))
    wspecs = [const(wk[0].shape), const(wk[1].shape), const(wk[2].shape)]
    out = pl.BlockSpec((1, n_half, KV_W), lambda i: (i, 0, 0))
    return pl.pallas_call(
        functools.partial(_cmp_prompt_body, n_half=n_half),
        grid=(b,),
        in_specs=[seq, seq] + wspecs + wspecs,
        out_specs=[out, out],
        out_shape=[jax.ShapeDtypeStruct((b, n_half, KV_W), F32)] * 2,
        compiler_params=_cparams(("parallel",)),
        name="cmp_prompt",
    )(kc, vc, *wk, *wv)


def _cmp_weights(pos_emb, w1, b1, w2):
    w1r = w1.reshape(CMP_RATIO, CMP_STRIDE * HEAD_DIM, CMP_HIDDEN)
    w1s = w1r.transpose(1, 0, 2).reshape(CMP_STRIDE * HEAD_DIM, CMP_RATIO * CMP_HIDDEN)
    pe = pos_emb.reshape(CMP_RATIO, CMP_STRIDE * HEAD_DIM)
    bias = b1 + jnp.einsum('rx,rxh->h', pe, w1r, precision=lax.Precision.HIGHEST)
    z = jnp.zeros_like(w2)
    w2k = jnp.stack([jnp.concatenate([w2, z], axis=1), jnp.concatenate([z, w2], axis=1)])
    return w1s.astype(BF16), bias[None, :], w2k.astype(BF16)


TQ = 512
KCHUNK = 512
SUBQ = 256
N_SEL_PROMPT = 32
assert TQ == KCHUNK and TQ % SUBQ == 0


def _softmax_rows(s):
    mx = jnp.max(s, axis=-1, keepdims=True)
    e = jnp.exp(s - mx)
    return e / jnp.sum(e, axis=-1, keepdims=True)


def _attn_prompt_body(q_ref, qr_ref, kcmp_ref, vcmp_ref, ks_ref, vs_ref, kw_ref, vw_ref, gt_ref,
                      amat_ref, emat_ref, gn_ref, o_ref, *, seq):
    t0 = pl.program_id(1) * TQ
    lane = lax.broadcasted_iota(jnp.int32, (TQ, LANES), 1)
    low = lane < HEAD_DIM
    tq = t0 + lax.broadcasted_iota(jnp.int32, (TQ, 1), 0)
    gates = gt_ref[0]

    def head_rows(src_ref, k):
        rows = []
        for g in range(GQA):
            h = GQA * k + g
            v = src_ref[0, :, (h // 2) * LANES:(h // 2 + 1) * LANES]
            if h % 2 != k:
                v = pltpu.roll(v, HEAD_DIM, 1)
            rows.append(jnp.where(low if k == 0 else jnp.logical_not(low), v * ATTN_SCALE, 0.0))
        return jnp.concatenate(rows, axis=0).astype(BF16)

    jj = lax.broadcasted_iota(jnp.int32, (N_SEL_PROMPT, TQ), 0)
    jq = (t0 + lax.broadcasted_iota(jnp.int32, (N_SEL_PROMPT, TQ), 1)) // SEL_BLOCK
    forced = (jj == 0) | (jj == jq) | (jj == jq - 1)

    n_cmp = kcmp_ref.shape[1]
    blk_end = lax.broadcasted_iota(jnp.int32, (1, n_cmp), 1) * CMP_STRIDE + (CMP_LEN - 1)
    m_cmp = (blk_end <= tq)[None]

    n_chunks = t0 // KCHUNK + 1

    comb, q_augs, q_rots, o_cs = [], [], [], []
    for k in range(N_KV_HEADS):
        qc = head_rows(q_ref, k)
        qr = head_rows(qr_ref, k)

        s = _dot_nt(qc, kcmp_ref[0].astype(BF16)).reshape(GQA, TQ, n_cmp)
        p = jnp.where(m_cmp, _softmax_rows(jnp.where(m_cmp, s, NEG_INF)), 0.0)
        o_c = _dot(p.reshape(GQA * TQ, n_cmp).astype(BF16), vcmp_ref[0].astype(BF16))
        imp = jnp.sum(p, axis=0)

        imp_sel = jnp.dot(amat_ref[...], imp.T, precision=lax.Precision.HIGHEST, preferred_element_type=F32)
        score = jnp.where(jj > jq, -1.0, jnp.where(forced, FORCE_SCORE, imp_sel))
        rank = jnp.zeros((N_SEL_PROMPT, TQ), F32)
        for i in range(N_SEL_PROMPT):
            si = score[i:i + 1, :]
            beats = (si > score) | ((si == score) & (jj > i))
            rank = rank + jnp.where(beats, 1.0, 0.0)
        sel_t = jnp.where((rank < N_SELECT) & (jj <= jq), 1.0, 0.0)
        sel_t = jnp.concatenate([sel_t, jnp.zeros((LANES - N_SEL_PROMPT, TQ), F32)], axis=0)
        sel = sel_t.T

        bias = ((sel - 1.0) * (-NEG_INF)).astype(BF16)
        q_augs.append(jnp.concatenate([qr, jnp.concatenate([bias] * GQA, axis=0)], axis=1))
        q_rots.append(qr)
        o_cs.append(o_c.reshape(GQA, TQ, LANES))

    sum_lanes = [HEAD_DIM * (1 - k) for k in range(N_KV_HEADS)]
    row_id = lax.broadcasted_iota(jnp.int32, (LANES, 1), 0)

    def attend(q2d, kk, vv, k, mask, m_i, acc):
        n_q, n_k = q2d.shape[0] // GQA, kk.shape[1]
        sc = _dot(q2d, kk).reshape(GQA, n_q, n_k)
        if mask is not None:
            sc = jnp.where(mask[None], sc, NEG_INF)
        m_n = jnp.max(sc, axis=-1, keepdims=True)
        if m_i is not None:
            m_n = jnp.maximum(m_i, m_n)
        pe = jnp.exp((sc - m_n).astype(BF16))
        vk = jnp.where(row_id == sum_lanes[k], 1.0, vv).astype(BF16)
        pv = _dot_nt(pe.reshape(GQA * n_q, n_k), vk).reshape(GQA, n_q, LANES)
        if m_i is not None:
            pv = jnp.exp(m_i - m_n) * acc + pv
        return m_n, pv

    def sub_rows(x2d, s):
        w = x2d.shape[1]
        return x2d.reshape(GQA, TQ, w)[:, s * SUBQ:(s + 1) * SUBQ].reshape(GQA * SUBQ, w)

    def chunk(c, carry):
        k0 = pl.multiple_of(c * KCHUNK, KCHUNK)
        kk = jnp.concatenate([ks_ref[0, :, pl.ds(k0, KCHUNK)].astype(BF16),
                              emat_ref[:, pl.ds(k0, KCHUNK)]], axis=0)
        vv = vs_ref[0, :, pl.ds(k0, KCHUNK)].astype(BF16)
        return tuple(attend(q_augs[k], kk, vv, k, None, *carry[k]) for k in range(N_KV_HEADS))

    init = tuple((jnp.full((GQA, TQ, 1), NEG_INF, F32), jnp.zeros((GQA, TQ, LANES), F32)) for _ in range(N_KV_HEADS))
    carry = lax.fori_loop(0, n_chunks - 1, chunk, init)

    acc_ss = [[] for _ in range(N_KV_HEADS)]
    acc_ws = [[] for _ in range(N_KV_HEADS)]
    w_sub = WINDOW + SUBQ
    for s in range(TQ // SUBQ):
        tq_s = t0 + s * SUBQ + lax.broadcasted_iota(jnp.int32, (SUBQ, 1), 0)
        n_k = (s + 1) * SUBQ
        k0 = pl.multiple_of(t0, TQ)
        kk = jnp.concatenate([ks_ref[0, :, pl.ds(k0, n_k)].astype(BF16), emat_ref[:, pl.ds(k0, n_k)]], axis=0)
        vv = vs_ref[0, :, pl.ds(k0, n_k)].astype(BF16)
        causal = (k0 + lax.broadcasted_iota(jnp.int32, (1, n_k), 1)) <= tq_s
        ws = pl.multiple_of(jnp.clip(t0 + s * SUBQ - WINDOW, 0, seq - w_sub), SUBQ)
        kw = kw_ref[0, :, pl.ds(ws, w_sub)].astype(BF16)
        vw = vw_ref[0, :, pl.ds(ws, w_sub)].astype(BF16)
        kp_w = ws + lax.broadcasted_iota(jnp.int32, (1, w_sub), 1)
        in_win = (kp_w <= tq_s) & (kp_w > tq_s - WINDOW)
        for k in range(N_KV_HEADS):
            m_i, acc = carry[k]
            rows = slice(s * SUBQ, (s + 1) * SUBQ)
            acc_ss[k].append(attend(sub_rows(q_augs[k], s), kk, vv, k, causal, m_i[:, rows], acc[:, rows])[1])
            acc_ws[k].append(attend(sub_rows(q_rots[k], s), kw, vw, k, in_win, None, None)[1])

    for k in range(N_KV_HEADS):
        acc_s, acc_w = jnp.concatenate(acc_ss[k], axis=1), jnp.concatenate(acc_ws[k], axis=1)
        sl = slice(sum_lanes[k], sum_lanes[k] + 1)
        for g in range(GQA):
            h = GQA * k + g
            g_s = gates[:, 3 * h + 1:3 * h + 2] / acc_s[g][:, sl]
            g_w = gates[:, 3 * h + 2:3 * h + 3] / acc_w[g][:, sl]
            comb.append(gates[:, 3 * h:3 * h + 1] * o_cs[k][g] + g_s * acc_s[g] + g_w * acc_w[g])

    cols = []
    for pr in range(N_HEADS // 2):
        k = (2 * pr) // GQA
        a, b = comb[2 * pr], comb[2 * pr + 1]
        a = pltpu.roll(a, HEAD_DIM, 1) if k == 1 else a
        b = pltpu.roll(b, HEAD_DIM, 1) if k == 0 else b
        cols.append(jnp.where(low, a, b))
    o = jnp.concatenate(cols, axis=1)
    o_ref[0] = _rms(o, gn_ref[...]).astype(o_ref.dtype)


def _sel_matrices(n_cmp_pad, n_sel, n_keys):
    j = np.arange(n_sel)[:, None]
    n = np.arange(n_cmp_pad)[None, :]
    per = SEL_BLOCK // CMP_STRIDE
    amat = ((n >= per * j - (CMP_RATIO - 1)) & (n <= per * j + per - 1)).astype(np.float32)
    jrow = np.arange(LANES)[:, None]
    key = np.arange(n_keys)[None, :]
    emat = (key // SEL_BLOCK == jrow).astype(np.float32)
    return jnp.asarray(amat), jnp.asarray(emat, dtype=BF16)


def _attn_prompt(q, qr, kcmp, vcmp, ks, vs, kw, vw, gt, gn):
    b, t, _ = q.shape
    amat, emat = _sel_matrices(kcmp.shape[1], N_SEL_PROMPT, t)
    qt = lambda w: pl.BlockSpec((1, TQ, w), lambda i, j: (i, j, 0))
    full = lambda a: pl.BlockSpec((1,) + a.shape[1:], lambda i, j: (i, 0, 0))
    const = lambda a: pl.BlockSpec(a.shape, lambda i, j: (0, 0))
    return pl.pallas_call(
        functools.partial(_attn_prompt_body, seq=t),
        grid=(b, t // TQ),
        in_specs=[qt(D_ATTN), qt(D_ATTN), full(kcmp), full(vcmp), full(ks), full(vs), full(kw), full(vw),
                  qt(LANES), const(amat), const(emat), const(gn)],
        out_specs=qt(D_ATTN),
        out_shape=jax.ShapeDtypeStruct((b, t, D_ATTN), BF16),
        compiler_params=_cparams(("parallel", "arbitrary")),
        name="attn_prompt",
    )(q, qr, kcmp, vcmp, ks, vs, kw, vw, gt, amat, emat, gn)


S5_TC = 128
S5_LB = D_SSM // LANES
S5_SW = LANES // SSM_CH * SSM_STATE


def _s5_body(u_ref, h0r_ref, h0i_ref, lr_ref, li_ref, bre_ref, bim_ref, c_ref, d_ref, gw_ref, gb_ref, gn_ref,
             y_ref, hr_ref, hi_ref, y_s, *, tc, bb):
    step = pl.program_id(0)
    m = tc * bb

    @pl.when(step == 0)
    def _():
        hr_ref[...] = h0r_ref[...]
        hi_ref[...] = h0i_ref[...]

    u = pltpu.einshape("btd->(tb)d", u_ref[...])
    ub = u.astype(BF16)
    for j in range(S5_LB):
        uj = ub[:, j * LANES:(j + 1) * LANES]
        sl = slice(j * S5_SW, (j + 1) * S5_SW)
        xr = _dot(uj, bre_ref[j]).reshape(tc, bb, S5_SW)
        xi = _dot(uj, bim_ref[j]).reshape(tc, bb, S5_SW)
        lr = jnp.broadcast_to(lr_ref[:, sl], (bb, S5_SW))
        li = jnp.broadcast_to(li_ref[:, sl], (bb, S5_SW))
        hr, hi = hr_ref[:, sl], hi_ref[:, sl]
        hrs, his = [], []
        for t in range(tc):
            hr, hi = lr * hr - li * hi + xr[t], lr * hi + li * hr + xi[t]
            hrs.append(hr)
            his.append(hi)
        hr_ref[:, sl] = hr
        hi_ref[:, sl] = hi
        hcat = jnp.concatenate([jnp.concatenate(hrs, axis=0), jnp.concatenate(his, axis=0)], axis=1).astype(BF16)
        y_s[:, j * LANES:(j + 1) * LANES] = _dot(hcat, c_ref[j])
    z = jax.nn.gelu(y_s[...] + d_ref[...] * u)
    out = z * jax.nn.sigmoid(_dot(z.astype(BF16), gw_ref[...]) + gb_ref[...])
    y_ref[...] = pltpu.einshape("(tb)d->btd", _rms(out, gn_ref[...]), b=bb)


def _s5(u, h0r, h0i, sp, tc):
    bb, t, _ = u.shape
    const = lambda a: pl.BlockSpec(a.shape, lambda i: (0,) * a.ndim)
    args = (h0r, h0i, sp['lr'], sp['li'], sp['bre'], sp['bim'], sp['c'], sp['d'], sp['glu_w'], sp['glu_b'], sp['gn'])
    st = jax.ShapeDtypeStruct(h0r.shape, F32)
    blk = pl.BlockSpec((bb, tc, D_SSM), lambda i: (0, i, 0))
    return pl.pallas_call(
        functools.partial(_s5_body, tc=tc, bb=bb),
        grid=(t // tc,),
        in_specs=[blk] + [const(a) for a in args],
        out_specs=[blk, const(h0r), const(h0i)],
        out_shape=[jax.ShapeDtypeStruct((bb, t, D_SSM), F32), st, st],
        scratch_shapes=[pltpu.VMEM((tc * bb, D_SSM), F32)],
        compiler_params=_cparams(("arbitrary",)),
        name="s5",
    )(u, *args)


def _s5_params(log_dt, a_re, a_im, b_re, b_im, c_re, c_im, d, glu_w, glu_b, gn):
    dt = jnp.exp(log_dt)[:, None]
    mag = jnp.exp(a_re * dt)
    lr, li = mag * jnp.cos(a_im * dt), mag * jnp.sin(a_im * dt)
    den = a_re * a_re + a_im * a_im
    inv_r, inv_i = a_re / den, -a_im / den
    nr, ni = lr - 1.0, li
    fr, fi = nr * inv_r - ni * inv_i, nr * inv_i + ni * inv_r
    bbr = fr[..., None] * b_re - fi[..., None] * b_im
    bbi = fr[..., None] * b_im + fi[..., None] * b_re
    gl = LANES // SSM_CH
    eye = jnp.eye(gl, dtype=F32)

    def bmat(bb):
        x = bb.reshape(S5_LB, gl, SSM_STATE, SSM_CH)
        return jnp.einsum('jgpc,gh->jgchp', x, eye).reshape(S5_LB, LANES, S5_SW).astype(BF16)

    def cmat(cc):
        x = cc.reshape(S5_LB, gl, SSM_CH, SSM_STATE)
        return jnp.einsum('jgcp,gh->jgphc', x, eye).reshape(S5_LB, S5_SW, LANES)

    cm = jnp.concatenate([cmat(c_re), -cmat(c_im)], axis=1).astype(BF16)
    return dict(lr=lr.reshape(1, -1), li=li.reshape(1, -1), bre=bmat(bbr), bim=bmat(bbi), c=cm,
                d=d.reshape(1, -1), glu_w=glu_w.astype(BF16), glu_b=glu_b[None, :], gn=gn[None, :])


def _outproj_body(x_ref, a_ref, s_ref, wa_ref, ws_ref, gn_ref, o_ref, *, norm_attn):
    a = a_ref[...]
    if norm_attn:
        a = _rms(a, gn_ref[...])
    o_ref[...] = x_ref[...] + _dot(a.astype(BF16), wa_ref[...]) + _dot(s_ref[...].astype(BF16), ws_ref[...])


def _outproj(x, a, s, wa, ws, gn, norm_attn):
    m = x.shape[0]
    tm = min(m, 1024)
    row = lambda w: pl.BlockSpec((tm, w), lambda i: (i, 0))
    const = lambda a_: pl.BlockSpec(a_.shape, lambda i: (0, 0))
    return pl.pallas_call(
        functools.partial(_outproj_body, norm_attn=norm_attn),
        grid=(m // tm,),
        in_specs=[row(D_MODEL), row(D_ATTN), row(D_SSM), const(wa), const(ws), const(gn)],
        out_specs=row(D_MODEL),
        out_shape=jax.ShapeDtypeStruct((m, D_MODEL), F32),
        compiler_params=_cparams(("parallel",)),
        name="outproj",
    )(x, a, s, wa, ws, gn)


N_PAGES = PAST_LEN // PAGE_SIZE
CMP_CHUNK = 32
CMP_PITCH = CMP_STRIDE // 2


def _page_copy(pool_ref, buf_ref, sem_ref, pt_ref, b, i, slot):
    return pltpu.make_async_copy(pool_ref.at[pt_ref[b * N_PAGES + i]], buf_ref.at[slot, i], sem_ref.at[slot])


def _cmp_paged_body(pt_ref, pool_ref, w1_ref, b_ref, w2_ref, o_ref, buf_ref, rows_ref, p_ref, sem_ref):
    b = pl.program_id(0)
    slot = b % 2

    def start_all(bb, sl):
        def body(i, c):
            _page_copy(pool_ref, buf_ref, sem_ref, pt_ref, bb, i, sl).start()
            return c
        lax.fori_loop(0, N_PAGES, body, 0, unroll=8)

    @pl.when(b == 0)
    def _():
        start_all(0, 0)

    @pl.when(b + 1 < pl.num_programs(0))
    def _():
        start_all(b + 1, 1 - slot)

    def wait_one(i, c):
        _page_copy(pool_ref, buf_ref, sem_ref, pt_ref, b, i, slot).wait()
        return c
    lax.fori_loop(0, N_PAGES, wait_one, 0, unroll=8)

    hp = PAGE_SIZE // CMP_STRIDE
    hc = CMP_CHUNK * hp

    wp = PAGE_SIZE // 2

    def stage_a(c):
        for i in range(CMP_CHUNK):
            t = buf_ref[slot, c * CMP_CHUNK + i].astype(BF16).T
            rows_ref[c % 2, i * wp:(i + 1) * wp, :] = pltpu.bitcast(t, jnp.uint32)

    def stage_b(c):
        xs = []
        for j in range(CMP_PITCH):
            w = rows_ref[c % 2, pl.ds(j, hc, stride=CMP_PITCH), :]
            xs.append(pltpu.bitcast(w << 16, F32))
            xs.append(pltpu.bitcast(w & jnp.uint32(0xFFFF0000), F32))
        p = _dot(_per_head_halves(xs), w1_ref[...])
        for k in range(N_KV_HEADS):
            p_ref[k, c * hc:(c + 1) * hc, :] = p[k * hc:(k + 1) * hc]

    n_chunks = N_PAGES // CMP_CHUNK
    stage_a(0)
    for c in range(n_chunks):
        if c + 1 < n_chunks:
            stage_a(c + 1)
        stage_b(c)

    o_ref[0] = _compress_tail([p_ref[k] for k in range(N_KV_HEADS)], N_PAGES * hp, b_ref, w2_ref)


def _cmp_paged(page_table_flat, pool_t, w1, bias, w2, n_seq):
    n_half = PAST_LEN // CMP_STRIDE
    hc = CMP_CHUNK * PAGE_SIZE // CMP_STRIDE
    const = lambda a: pl.BlockSpec(a.shape, lambda i, pt: (0,) * a.ndim)
    return pl.pallas_call(
        _cmp_paged_body,
        grid_spec=pltpu.PrefetchScalarGridSpec(
            num_scalar_prefetch=1,
            grid=(n_seq,),
            in_specs=[pl.BlockSpec(memory_space=pl.ANY), const(w1), const(bias), const(w2)],
            out_specs=pl.BlockSpec((1, n_half, KV_W), lambda i, pt: (i, 0, 0)),
            scratch_shapes=[pltpu.VMEM((2, N_PAGES, PAGE_SIZE, KV_W), F32),
                            pltpu.VMEM((2, hc * CMP_PITCH, KV_W), jnp.uint32),
                            pltpu.VMEM((N_KV_HEADS, n_half, CMP_RATIO * CMP_HIDDEN), F32),
                            pltpu.SemaphoreType.DMA((2,))]),
        out_shape=jax.ShapeDtypeStruct((n_seq, n_half, KV_W), F32),
        compiler_params=_cparams(("arbitrary",)),
        name="cmp_paged",
    )(page_table_flat, pool_t, w1, bias, w2)


CMPATTN_SEQS = 16


def _cmpattn_body(q_ref, kc_ref, vc_ref, amat_ref, o_ref, idx_ref, *, ts, n_cmp, n_sel):
    rows = GQA * ts
    qpos = PAST_LEN + lax.broadcasted_iota(jnp.int32, (rows, 1), 0) % ts
    n_pad = kc_ref.shape[1]
    n_io = lax.broadcasted_iota(jnp.int32, (1, n_pad), 1)
    m = (n_io * CMP_STRIDE + (CMP_LEN - 1) <= qpos) & (n_io < n_cmp)
    n_sb = q_ref.shape[0]
    imps = []
    for b in range(n_sb):
        kc = kc_ref[b].astype(BF16)
        vc = vc_ref[b].astype(BF16)
        for k in range(N_KV_HEADS):
            s = _dot_nt((q_ref[b, k] * ATTN_SCALE).astype(BF16), kc)
            p = jnp.where(m, _softmax_rows(jnp.where(m, s, NEG_INF)), 0.0)
            o_ref[b, k] = _dot(p.astype(BF16), vc)
            imp = p[0:ts]
            for g in range(1, GQA):
                imp = imp + p[g * ts:(g + 1) * ts]
            imps.append(imp)
    imp_all = jnp.concatenate(imps, axis=0)
    n_rows = imp_all.shape[0]
    sel_w = amat_ref.shape[1]
    imp_sel = jnp.dot(imp_all, amat_ref[...], precision=lax.Precision.HIGHEST, preferred_element_type=F32)
    j = lax.broadcasted_iota(jnp.int32, (n_rows, sel_w), 1)
    jq = (PAST_LEN + lax.broadcasted_iota(jnp.int32, (n_rows, sel_w), 0) % ts) // SEL_BLOCK
    forced = (j == 0) | (j == jq) | (j == jq - 1)
    jf = j.astype(F32)
    lane_k = lax.broadcasted_iota(jnp.int32, (n_rows, LANES), 1)
    score = jnp.where(j > jq, -1.0, jnp.where(forced, FORCE_SCORE, imp_sel))
    score = jnp.where(j < n_sel, score, -jnp.inf)
    res = jnp.zeros((n_rows, LANES), F32)
    for i in range(N_SELECT):
        mx = jnp.max(score, axis=-1, keepdims=True)
        ix = jnp.min(jnp.where(score == mx, jf, 1e9), axis=-1, keepdims=True)
        res = jnp.where(lane_k == i, ix, res)
        score = jnp.where(jf == ix, -jnp.inf, score)
    idx_ref[...] = res.astype(jnp.int32).reshape(idx_ref.shape)


def _cmpattn(q_pad, kcmp, vcmp, ts):
    n_seq = q_pad.shape[0]
    n_cmp = (PAST_LEN + ts) // CMP_STRIDE - CMP_RATIO + 1
    n_sel = -(-(PAST_LEN + ts) // SEL_BLOCK)
    sel_w = -(-n_sel // LANES) * LANES
    per = SEL_BLOCK // CMP_STRIDE
    n = np.arange(kcmp.shape[1])[:, None]
    j = np.arange(sel_w)[None, :]
    amat = jnp.asarray(((n >= per * j - (CMP_RATIO - 1)) & (n <= per * j + per - 1) & (j < n_sel)).astype(np.float32))
    rows = GQA * ts
    sb = CMPATTN_SEQS
    return pl.pallas_call(
        functools.partial(_cmpattn_body, ts=ts, n_cmp=n_cmp, n_sel=n_sel),
        grid=(n_seq // sb,),
        in_specs=[pl.BlockSpec((sb, N_KV_HEADS, rows, LANES), lambda i: (i, 0, 0, 0)),
                  pl.BlockSpec((sb,) + kcmp.shape[1:], lambda i: (i, 0, 0)),
                  pl.BlockSpec((sb,) + vcmp.shape[1:], lambda i: (i, 0, 0)),
                  pl.BlockSpec(amat.shape, lambda i: (0, 0))],
        out_specs=[pl.BlockSpec((sb, N_KV_HEADS, rows, LANES), lambda i: (i, 0, 0, 0)),
                   pl.BlockSpec((sb, N_KV_HEADS * ts, LANES), lambda i: (i, 0, 0))],
        out_shape=[jax.ShapeDtypeStruct((n_seq, N_KV_HEADS, rows, LANES), F32),
                   jax.ShapeDtypeStruct((n_seq, N_KV_HEADS * ts, LANES), jnp.int32)],
        compiler_params=_cparams(("parallel",)),
        name="cmpattn_sample",
    )(q_pad, kcmp, vcmp, amat)


def _slc_copies(pools, bufs, sems, pt_ref, idx_ref, step, r, slot, ts, for_wait):
    if for_wait:
        pg = 0
    else:
        blk = idx_ref[step * (ts * N_SELECT) + r]
        page = jnp.minimum(blk // (PAGE_SIZE // SEL_BLOCK), N_PAGES - 1)
        pg = pt_ref[(step // N_KV_HEADS) * N_PAGES + page]
    return [pltpu.make_async_copy(pool.at[pg], buf.at[slot, r], sem.at[slot])
            for pool, buf, sem in zip(pools, bufs, sems)]


def _tail_body(pt_ref, idx_ref, kpool_ref, vpool_ref, q_ref, ksn_ref, vsn_ref, kwn_ref, vwn_ref, wk_ref, wv_ref,
               oc_ref, gt_ref, o_ref, wko_ref, wvo_ref, kbuf_ref, vbuf_ref, ksem_ref, vsem_ref, *, ts):
    step = pl.program_id(0) * N_KV_HEADS + pl.program_id(1)
    n_steps = pl.num_programs(0) * N_KV_HEADS
    slot = step % 2
    n_fetch = ts * N_SELECT

    pools, bufs, sems = (kpool_ref, vpool_ref), (kbuf_ref, vbuf_ref), (ksem_ref, vsem_ref)

    def start_all(st, sl):
        def body(r, c):
            for cp in _slc_copies(pools, bufs, sems, pt_ref, idx_ref, st, r, sl, ts, False):
                cp.start()
            return c
        lax.fori_loop(0, n_fetch, body, 0, unroll=8)

    @pl.when(step == 0)
    def _():
        start_all(0, 0)

    @pl.when(step + 1 < n_steps)
    def _():
        start_all(step + 1, 1 - slot)

    def wait_one(r, c):
        for cp in _slc_copies(pools, bufs, sems, pt_ref, idx_ref, step, r, slot, ts, True):
            cp.wait()
        return c
    lax.fori_loop(0, n_fetch, wait_one, 0, unroll=8)

    rows = GQA * ts
    q = (q_ref[0, 0] * ATTN_SCALE).astype(BF16)
    row_t = lax.broadcasted_iota(jnp.int32, (rows, 1), 0) % ts
    qpos = PAST_LEN + row_t
    lane = lax.broadcasted_iota(jnp.int32, (1, LANES), 1)
    lane_half, lane_off = lane // SEL_BLOCK, lane % SEL_BLOCK
    pad = jnp.zeros((LANES - ksn_ref.shape[1], LANES), F32)
    new_pos = PAST_LEN + lane
    new_ok = lane < ts
    blk_new = PAST_LEN // SEL_BLOCK

    def new_rows(ref):
        return jnp.concatenate([ref[0], pad], axis=0).astype(BF16)

    ksn, vsn = new_rows(ksn_ref), new_rows(vsn_ref)
    s_new = _dot_nt(q, ksn)
    o_s = jnp.zeros((rows, LANES), F32)
    grp = 4

    def pages(buf_ref, r0):
        return jnp.concatenate([buf_ref[slot, r0 + u].astype(BF16) for u in range(grp)], axis=1)

    for t in range(ts):
        pieces = []
        has_new = jnp.int32(0)
        for i0 in range(0, N_SELECT, grp):
            sc = _dot(q, pages(kbuf_ref, t * N_SELECT + i0))
            for u in range(grp):
                blk = idx_ref[step * n_fetch + t * N_SELECT + i0 + u]
                has_new = has_new | (blk == blk_new).astype(jnp.int32)
                want_half = jnp.where(blk < blk_new, blk % (PAGE_SIZE // SEL_BLOCK), -1)
                ok = (lane_half == want_half) & (blk * SEL_BLOCK + lane_off <= qpos)
                pieces.append(jnp.where(ok, sc[:, u * LANES:(u + 1) * LANES], NEG_INF))
        ok_new = (lane < jnp.where(has_new > 0, ts, 0)) & (new_pos <= qpos)
        pieces.append(jnp.where(ok_new, s_new, NEG_INF))
        p = _softmax_rows(jnp.concatenate(pieces, axis=1)).astype(BF16)
        o_t = _dot(p[:, N_SELECT * LANES:], vsn)
        for i0 in range(0, N_SELECT, grp):
            o_t = o_t + _dot_nt(p[:, i0 * LANES:(i0 + grp) * LANES], pages(vbuf_ref, t * N_SELECT + i0))
        o_s = jnp.where(row_t == t, o_t, o_s)

    wb = wk_ref.shape[2]
    kp = PAST_LEN - wb + lax.broadcasted_iota(jnp.int32, (1, wb), 1)
    ok_w = (kp >= 0) & (kp <= qpos) & (kp > qpos - WINDOW)
    ok_wn = new_ok & (new_pos <= qpos) & (new_pos > qpos - WINDOW)
    s_w = jnp.where(ok_w, _dot(q, wk_ref[0].astype(BF16)), NEG_INF)
    s_wn = jnp.where(ok_wn, _dot_nt(q, new_rows(kwn_ref)), NEG_INF)
    p = _softmax_rows(jnp.concatenate([s_w, s_wn], axis=1)).astype(BF16)
    o_w = _dot_nt(p[:, :wb], wv_ref[0].astype(BF16)) + _dot(p[:, wb:], new_rows(vwn_ref))

    gt = gt_ref[0, 0]
    o_ref[0, 0] = gt[:, 0:1] * oc_ref[0, 0] + gt[:, 1:2] * o_s + gt[:, 2:3] * o_w

    lane_w = lax.broadcasted_iota(jnp.int32, (1, LANES), 1)

    def slide(w_ref, new_ref, out_ref):
        shifted = pltpu.roll(w_ref[0], wb - ts, 1)
        new_t = pltpu.roll(jnp.concatenate([new_ref[0], pad], axis=0).T, LANES - ts, 1)
        out_ref[0, :, :wb - LANES] = shifted[:, :wb - LANES]
        out_ref[0, :, wb - LANES:] = jnp.where(lane_w >= LANES - ts, new_t, shifted[:, wb - LANES:])

    slide(wk_ref, kwn_ref, wko_ref)
    slide(wv_ref, vwn_ref, wvo_ref)


def _tail(page_table_flat, idx_flat, kpool_t, vpool_t, q_pad, ksn, vsn, kwn, vwn, wk_t, wv_t, o_cmp, gates_pad, ts):
    n_seq = q_pad.shape[0]
    rows = GQA * ts
    n_fetch = ts * N_SELECT
    per_head = pl.BlockSpec((1, 1, rows, LANES), lambda i, k, pt, ix: (i, k, 0, 0))
    per_seq = lambda a: pl.BlockSpec((1,) + a.shape[1:], lambda i, k, pt, ix: (i, 0, 0))
    hbm = pl.BlockSpec(memory_space=pl.ANY)
    return pl.pallas_call(
        functools.partial(_tail_body, ts=ts),
        grid_spec=pltpu.PrefetchScalarGridSpec(
            num_scalar_prefetch=2,
            grid=(n_seq, N_KV_HEADS),
            in_specs=[hbm, hbm, per_head, per_seq(ksn), per_seq(vsn), per_seq(kwn), per_seq(vwn),
                      per_seq(wk_t), per_seq(wv_t), per_head, per_head],
            out_specs=[per_head, per_seq(wk_t), per_seq(wv_t)],
            scratch_shapes=[pltpu.VMEM((2, n_fetch, PAGE_SIZE, KV_W), F32), pltpu.VMEM((2, n_fetch, PAGE_SIZE, KV_W), F32),
                            pltpu.SemaphoreType.DMA((2,)), pltpu.SemaphoreType.DMA((2,))]),
        out_shape=[jax.ShapeDtypeStruct((n_seq, N_KV_HEADS, rows, LANES), F32),
                   jax.ShapeDtypeStruct(wk_t.shape, F32), jax.ShapeDtypeStruct(wv_t.shape, F32)],
        compiler_params=_cparams(("arbitrary", "arbitrary")),
        name="slc_win_sample",
    )(page_table_flat, idx_flat, kpool_t, vpool_t, q_pad, ksn, vsn, kwn, vwn, wk_t, wv_t, o_cmp, gates_pad)


def _pad_heads(x, ts):
    n = x.shape[0] // ts
    w = x.shape[1] // N_HEADS
    return x.reshape(n, ts, N_KV_HEADS, GQA, w).transpose(0, 2, 3, 1, 4).reshape(n, N_KV_HEADS, GQA * ts, w)


def _q_on_kv_lanes(q, ts):
    x = _pad_heads(q, ts)
    z = jnp.zeros_like(x[:, 0])
    return jnp.stack([jnp.concatenate([x[:, 0], z], axis=-1), jnp.concatenate([z, x[:, 1]], axis=-1)], axis=1)


def kernel(x_prompt, x_sample, cache_cmp_k, cache_cmp_v, cache_slc_k, cache_slc_v, state_win_k, state_win_v, state_s5_re, state_s5_im, page_table, norm_ffn1, ffn1_gate, ffn1_up, ffn1_down, norm_mix, w_in, cmp_k_pos, cmp_k_w1, cmp_k_b1, cmp_k_w2, cmp_v_pos, cmp_v_w1, cmp_v_b1, cmp_v_w2, s5_log_dt, s5_a_re, s5_a_im, s5_b_re, s5_b_im, s5_c_re, s5_c_im, s5_d, s5_glu_w, s5_glu_b, norm_attn_out, norm_ssm_out, w_out, norm_ffn2, ffn2_gate, ffn2_up, ffn2_down, norm_final):
    depth = w_in.shape[0]
    assert depth == 1
    l = 0
    B, T, _ = x_prompt.shape
    BS, TS, _ = x_sample.shape
    assert TS < CMP_STRIDE and TS <= 8
    row = lambda a: a[None, :]

    w = w_in[l]
    cut = D_ATTN + 6 * KV_W
    w_p = jnp.concatenate([w[:, :cut], w[:, cut + 3 * N_HEADS:], w[:, cut:cut + 3 * N_HEADS],
                           jnp.zeros((D_MODEL, LANES - 3 * N_HEADS), F32)], axis=1).astype(BF16)
    f1 = (row(norm_ffn1[l]), ffn1_gate[l].astype(BF16), ffn1_up[l].astype(BF16), ffn1_down[l].astype(BF16))
    f2 = (row(norm_ffn2[l]), ffn2_gate[l].astype(BF16), ffn2_up[l].astype(BF16), ffn2_down[l].astype(BF16))
    gfin = row(norm_final)
    wo = w_out[l].astype(BF16)
    wo_a, wo_s = wo[:D_ATTN], wo[D_ATTN:]
    cwk = _cmp_weights(cmp_k_pos[l], cmp_k_w1[l], cmp_k_b1[l], cmp_k_w2[l])
    cwv = _cmp_weights(cmp_v_pos[l], cmp_v_w1[l], cmp_v_b1[l], cmp_v_w2[l])
    sp = _s5_params(s5_log_dt[l], s5_a_re[l], s5_a_im[l], s5_b_re[l], s5_b_im[l], s5_c_re[l], s5_c_im[l],
                    s5_d[l], s5_glu_w[l], s5_glu_b[l], norm_ssm_out[l])
    gn_attn = row(norm_attn_out[l])
    g_mix = row(norm_mix[l])

    xp = x_prompt.reshape(B * T, D_MODEL)
    x1 = _ffn(xp, *f1, gfin, False)
    tabs = _rope_tables(jnp.arange(T, dtype=jnp.int32))
    (q, qr, kc, vc, _, _, _, _, gt, u,
     kc_t, vc_t, ks_t, vs_t, kw_t, vw_t) = _proj(x1, g_mix, w_p, *tabs, seq_t=T)
    b3 = lambda a: a.reshape(B, T, a.shape[-1])
    kcmp, vcmp = _cmp_prompt(b3(kc), b3(vc), cwk, cwv)
    attn = _attn_prompt(b3(q), b3(qr), kcmp, vcmp, ks_t, vs_t, kw_t, vw_t, b3(gt), gn_attn)
    zeros = jnp.zeros((B, SSM_GROUPS * SSM_STATE), F32)
    ssm, p_re, p_im = _s5(b3(u), zeros, zeros, sp, S5_TC)
    ssm = ssm.reshape(B * T, D_SSM)
    x2 = _outproj(x1, attn.reshape(B * T, D_ATTN), ssm, wo_a, wo_s, gn_attn, False)
    y_prompt = _ffn(x2, *f2, gfin, True).reshape(B, T, D_MODEL)
    from_t = lambda a: a.reshape(a.shape[0], N_KV_HEADS, HEAD_DIM, a.shape[2]).transpose(0, 3, 1, 2)[None]
    wb = min(WINDOW, T)
    p_state = (from_t(kc_t), from_t(vc_t), from_t(ks_t), from_t(vs_t),
               from_t(kw_t[:, :, T - wb:]), from_t(vw_t[:, :, T - wb:]),
               p_re.reshape(1, B, SSM_GROUPS, SSM_STATE), p_im.reshape(1, B, SSM_GROUPS, SSM_STATE))

    xs = x_sample.reshape(BS * TS, D_MODEL)
    s1 = _ffn(xs, *f1, gfin, False)
    pos_s = PAST_LEN + jnp.arange(TS, dtype=jnp.int32)
    tabs_s = [jnp.tile(a, (BS, 1)) for a in _rope_tables(pos_s)]
    q, qr, kc, vc, ks, vs, kw, vw, gt, u = _proj(s1, g_mix, w_p, *tabs_s)
    s3 = lambda a: a.reshape(BS, TS, a.shape[-1])
    to_t = lambda a: a.transpose(0, 2, 3, 1).reshape(a.shape[0], KV_W, a.shape[1])
    pt_flat = page_table.reshape(-1)
    kcmp_s = _cmp_paged(pt_flat, to_t(cache_cmp_k[l]), *cwk, BS)
    vcmp_s = _cmp_paged(pt_flat, to_t(cache_cmp_v[l]), *cwv, BS)
    o_cmp_s, idx = _cmpattn(_q_on_kv_lanes(q, TS), kcmp_s, vcmp_s, TS)
    pad8 = lambda a: jnp.pad(s3(a), ((0, 0), (0, 8 - TS), (0, 0)))
    gates_s = _pad_heads(gt[:, :3 * N_HEADS], TS)
    gates_s = jnp.pad(gates_s, ((0, 0), (0, 0), (0, 0), (0, LANES - 3)))
    wk_t, wv_t = to_t(state_win_k[l]), to_t(state_win_v[l])
    comb, wk_new, wv_new = _tail(
        pt_flat, idx[:, :, :N_SELECT].reshape(-1), to_t(cache_slc_k[l]), to_t(cache_slc_v[l]),
        _q_on_kv_lanes(qr, TS), pad8(ks), pad8(vs), pad8(kw), pad8(vw), wk_t, wv_t, o_cmp_s, gates_s, TS)
    comb = comb.reshape(BS, N_KV_HEADS, GQA, TS, N_KV_HEADS, HEAD_DIM)
    comb = jnp.stack([comb[:, k, :, :, k] for k in range(N_KV_HEADS)], axis=1)
    attn_s = comb.transpose(0, 3, 1, 2, 4).reshape(BS * TS, D_ATTN)
    ssm_s, s_re, s_im = _s5(s3(u), state_s5_re[l].reshape(BS, -1), state_s5_im[l].reshape(BS, -1), sp, TS)
    ssm_s = ssm_s.reshape(BS * TS, D_SSM)
    s2 = _outproj(s1, attn_s, ssm_s, wo_a, wo_s, gn_attn, True)
    y_sample = _ffn(s2, *f2, gfin, True).reshape(BS, TS, D_MODEL)
    kv5 = lambda a, n: a.reshape(1, n, -1, N_KV_HEADS, HEAD_DIM)
    win_k, win_v = from_t(wk_new), from_t(wv_new)
    s_state = (kv5(kc, BS), kv5(vc, BS), kv5(ks, BS), kv5(vs, BS), win_k, win_v,
               s_re.reshape(1, BS, SSM_GROUPS, SSM_STATE), s_im.reshape(1, BS, SSM_GROUPS, SSM_STATE))

    return (y_prompt, y_sample) + p_state + s_state
```

```python
import functools

import numpy as np
import jax
import jax.numpy as jnp
from jax import lax
from jax.experimental import pallas as pl
from jax.experimental.pallas import tpu as pltpu

F32 = jnp.float32
BF16 = jnp.bfloat16

D_MODEL = 1024
PAST_LEN = 16384
PAGE_SIZE = 128
D_ATTN = 512
D_SSM = 512
HEAD_DIM = 64
N_HEADS = 8
N_KV_HEADS = 2
GQA = 4
KV_W = 128
ROT_DIM = 16
ROPE_THETA = 500000.0
ATTN_SCALE = HEAD_DIM ** -0.5
CMP_LEN = 32
CMP_STRIDE = 16
CMP_RATIO = 2
CMP_HIDDEN = 256
SEL_BLOCK = 64
N_SELECT = 16
WINDOW = 512
SSM_CH = 16
SSM_GROUPS = 32
SSM_STATE = 64
D_FF = 2816
RMS_EPS = 1e-6
NEG_INF = -1e30
FORCE_SCORE = 1e4

LANES = 128
VMEM_LIMIT = 56 * 1024 * 1024

Z_Q = 0
Z_KV = D_ATTN
Z_U = D_ATTN + 6 * KV_W
Z_G = Z_U + D_SSM
Z_W = Z_G + LANES


def _cparams(sem):
    return pltpu.CompilerParams(dimension_semantics=sem, vmem_limit_bytes=VMEM_LIMIT)


def _rms(x, g):
    return x * lax.rsqrt(jnp.mean(x * x, axis=-1, keepdims=True) + RMS_EPS) * g


def _dot(a, b):
    return jnp.dot(a, b, preferred_element_type=F32)


def _dot_nt(a, b):
    return lax.dot_general(a, b, (((1,), (1,)), ((), ())), preferred_element_type=F32)


def _ffn_body(x_ref, g_ref, wg_ref, wu_ref, wd_ref, gf_ref, o_ref, *, final_norm):
    x = x_ref[...]
    h = _rms(x, g_ref[...]).astype(BF16)
    a = _dot(h, wg_ref[...])
    b = _dot(h, wu_ref[...])
    t = (a * jax.nn.sigmoid(a) * b).astype(BF16)
    y = x + 0.5 * _dot(t, wd_ref[...])
    if final_norm:
        y = _rms(y, gf_ref[...])
    o_ref[...] = y


def _ffn(x, g, wg, wu, wd, gf, final_norm):
    m = x.shape[0]
    tm = min(m, 512)
    row = pl.BlockSpec((tm, D_MODEL), lambda i: (i, 0))
    vec = pl.BlockSpec((1, D_MODEL), lambda i: (0, 0))
    resident = lambda a: pl.BlockSpec(a.shape, lambda i: (0, 0), pipeline_mode=pl.Buffered(1))
    return pl.pallas_call(
        functools.partial(_ffn_body, final_norm=final_norm),
        grid=(m // tm,),
        in_specs=[row, vec, resident(wg), resident(wu), resident(wd), vec],
        out_specs=row,
        out_shape=jax.ShapeDtypeStruct((m, D_MODEL), F32),
        compiler_params=_cparams(("parallel",)),
        name="ffn",
    )(x, g, wg, wu, wd, gf)


def _proj_body(x_ref, g_ref, w_ref, c_ref, sa_ref, sb_ref,
               q_ref, qr_ref, kc_ref, vc_ref, ks_ref, vs_ref, kw_ref, vw_ref, gt_ref, u_ref, *t_refs):
    h = _rms(x_ref[...], g_ref[...]).astype(BF16)
    z = _dot(h, w_ref[...])
    c, sa, sb = c_ref[...], sa_ref[...], sb_ref[...]

    def rope(v):
        return v * c + pltpu.roll(v, LANES - ROT_DIM // 2, 1) * sa + pltpu.roll(v, ROT_DIM // 2, 1) * sb

    q_ref[...] = z[:, Z_Q:Z_Q + D_ATTN]
    for i in range(D_ATTN // LANES):
        qr_ref[:, i * LANES:(i + 1) * LANES] = rope(z[:, Z_Q + i * LANES:Z_Q + (i + 1) * LANES])
    kv = [z[:, Z_KV + i * KV_W:Z_KV + (i + 1) * KV_W] for i in range(6)]
    kv[2] = rope(kv[2])
    kv[4] = rope(kv[4])
    for ref, v in zip((kc_ref, vc_ref, ks_ref, vs_ref, kw_ref, vw_ref), kv):
        ref[...] = v
    for ref, v in zip(t_refs, kv):
        ref[0] = v.T
    u_ref[...] = z[:, Z_U:Z_U + D_SSM]
    gt_ref[...] = jax.nn.sigmoid(z[:, Z_G:Z_G + LANES])


def _proj(x, g, w, rope_c, rope_sa, rope_sb, seq_t=None):
    m = x.shape[0]
    tm = min(m, 512)
    n_rep = rope_c.shape[0] // tm
    row = lambda i: (i, 0)
    const = lambda i: (0, 0)
    tab = lambda i: (i % n_rep, 0)
    widths = [D_ATTN, D_ATTN] + [KV_W] * 6 + [LANES, D_SSM]
    out_specs = [pl.BlockSpec((tm, wd), row) for wd in widths]
    out_shape = [jax.ShapeDtypeStruct((m, wd), F32) for wd in widths]
    if seq_t is not None:
        assert rope_c.shape[0] == seq_t
        out_specs += [pl.BlockSpec((1, KV_W, tm), lambda i: (i // n_rep, 0, i % n_rep))] * 6
        out_shape += [jax.ShapeDtypeStruct((m // seq_t, KV_W, seq_t), F32)] * 6
    return pl.pallas_call(
        _proj_body,
        grid=(m // tm,),
        in_specs=[pl.BlockSpec((tm, D_MODEL), row), pl.BlockSpec((1, D_MODEL), const),
                  pl.BlockSpec((D_MODEL, Z_W), const),
                  pl.BlockSpec((tm, LANES), tab), pl.BlockSpec((tm, LANES), tab), pl.BlockSpec((tm, LANES), tab)],
        out_specs=out_specs,
        out_shape=out_shape,
        compiler_params=_cparams(("parallel",)),
        name="proj",
    )(x, g, w, rope_c, rope_sa, rope_sb)


def _rope_tables(pos):
    half = ROT_DIM // 2
    inv_freq = jnp.power(ROPE_THETA, -(jnp.arange(half, dtype=F32) * 2.0 / ROT_DIM))
    ang = pos.astype(F32)[:, None] * inv_freq[None, :]
    cos, sin = jnp.cos(ang), jnp.sin(ang)
    n = pos.shape[0]
    one = jnp.ones((n, HEAD_DIM - ROT_DIM), F32)
    zero = jnp.zeros((n, HEAD_DIM - ROT_DIM), F32)
    zh = jnp.zeros((n, half), F32)
    c = jnp.concatenate([cos, cos, one], axis=1)
    sa = jnp.concatenate([-sin, zh, zero], axis=1)
    sb = jnp.concatenate([zh, sin, zero], axis=1)
    t2 = lambda a: jnp.concatenate([a, a], axis=1)
    return t2(c), t2(sa), t2(sb)


def _per_head_halves(xs):
    low = lax.broadcasted_iota(jnp.int32, xs[0].shape, 1) < HEAD_DIM
    heads = [[], []]
    for j in range(CMP_STRIDE // 2):
        a, b = xs[2 * j], xs[2 * j + 1]
        heads[0].append(jnp.where(low, a, pltpu.roll(b, HEAD_DIM, 1)))
        heads[1].append(jnp.where(low, pltpu.roll(a, HEAD_DIM, 1), b))
    return jnp.concatenate([jnp.concatenate(hd, axis=1) for hd in heads], axis=0).astype(BF16)


def _compress_tail(p_k, n_half, b_ref, w2_ref):
    out = None
    for k in range(N_KV_HEADS):
        pre = p_k[k][:, :CMP_HIDDEN] + pltpu.roll(p_k[k][:, CMP_HIDDEN:], n_half - 1, 0) + b_ref[...]
        ok = _dot(jax.nn.gelu(pre).astype(BF16), w2_ref[k])
        out = ok if out is None else out + ok
    return out


def _compress_rows(x_ref, n_half, w1_ref, b_ref, w2_ref):
    xs = [x_ref[0, pl.ds(s, n_half, stride=CMP_STRIDE), :] for s in range(CMP_STRIDE)]
    p = _dot(_per_head_halves(xs), w1_ref[...])
    return _compress_tail([p[k * n_half:(k + 1) * n_half] for k in range(N_KV_HEADS)], n_half, b_ref, w2_ref)


def _cmp_prompt_body(kc_ref, vc_ref, wk1_ref, bk_ref, wk2_ref, wv1_ref, bv_ref, wv2_ref, ko_ref, vo_ref, *, n_half):
    ko_ref[0] = _compress_rows(kc_ref, n_half, wk1_ref, bk_ref, wk2_ref)
    vo_ref[0] = _compress_rows(vc_ref, n_half, wv1_ref, bv_ref, wv2_ref)


def _cmp_prompt(kc, vc, wk, wv):
    b, t, _ = kc.shape
    n_half = t // CMP_STRIDE
    seq = pl.BlockSpec((1, t, KV_W), lambda i: (i, 0, 0))
    const = lambda shp: pl.BlockSpec(shp, lambda i: (0,) * len(shp))
    wspecs = [const(wk[0].shape), const(wk[1].shape), const(wk[2].shape)]
    out = pl.BlockSpec((1, n_half, KV_W), lambda i: (i, 0, 0))
    return pl.pallas_call(
        functools.partial(_cmp_prompt_body, n_half=n_half),
        grid=(b,),
        in_specs=[seq, seq] + wspecs + wspecs,
        out_specs=[out, out],
        out_shape=[jax.ShapeDtypeStruct((b, n_half, KV_W), F32)] * 2,
        compiler_params=_cparams(("parallel",)),
        name="cmp_prompt",
    )(kc, vc, *wk, *wv)


def _cmp_weights(pos_emb, w1, b1, w2):
    w1r = w1.reshape(CMP_RATIO, CMP_STRIDE * HEAD_DIM, CMP_HIDDEN)
    w1s = w1r.transpose(1, 0, 2).reshape(CMP_STRIDE * HEAD_DIM, CMP_RATIO * CMP_HIDDEN)
    pe = pos_emb.reshape(CMP_RATIO, CMP_STRIDE * HEAD_DIM)
    bias = b1 + jnp.einsum('rx,rxh->h', pe, w1r, precision=lax.Precision.HIGHEST)
    z = jnp.zeros_like(w2)
    w2k = jnp.stack([jnp.concatenate([w2, z], axis=1), jnp.concatenate([z, w2], axis=1)])
    return w1s.astype(BF16), bias[None, :], w2k.astype(BF16)


TQ = 512
KCHUNK = 512
SUBQ = 256
N_SEL_PROMPT = 32
assert TQ == KCHUNK and TQ % SUBQ == 0


def _softmax_rows(s):
    mx = jnp.max(s, axis=-1, keepdims=True)
    e = jnp.exp(s - mx)
    return e / jnp.sum(e, axis=-1, keepdims=True)


def _attn_prompt_body(q_ref, qr_ref, kcmp_ref, vcmp_ref, ks_ref, vs_ref, kw_ref, vw_ref, gt_ref,
                      amat_ref, emat_ref, gn_ref, o_ref, *, seq):
    t0 = pl.program_id(1) * TQ
    lane = lax.broadcasted_iota(jnp.int32, (TQ, LANES), 1)
    low = lane < HEAD_DIM
    tq = t0 + lax.broadcasted_iota(jnp.int32, (TQ, 1), 0)
    gates = gt_ref[0]

    def head_rows(src_ref, k):
        rows = []
        for g in range(GQA):
            h = GQA * k + g
            v = src_ref[0, :, (h // 2) * LANES:(h // 2 + 1) * LANES]
            if h % 2 != k:
                v = pltpu.roll(v, HEAD_DIM, 1)
            rows.append(jnp.where(low if k == 0 else jnp.logical_not(low), v * ATTN_SCALE, 0.0))
        return jnp.concatenate(rows, axis=0).astype(BF16)

    jj = lax.broadcasted_iota(jnp.int32, (N_SEL_PROMPT, TQ), 0)
    jq = (t0 + lax.broadcasted_iota(jnp.int32, (N_SEL_PROMPT, TQ), 1)) // SEL_BLOCK
    forced = (jj == 0) | (jj == jq) | (jj == jq - 1)

    n_cmp = kcmp_ref.shape[1]
    blk_end = lax.broadcasted_iota(jnp.int32, (1, n_cmp), 1) * CMP_STRIDE + (CMP_LEN - 1)
    m_cmp = (blk_end <= tq)[None]

    n_chunks = t0 // KCHUNK + 1

    comb, q_augs, q_rots, o_cs = [], [], [], []
    for k in range(N_KV_HEADS):
        qc = head_rows(q_ref, k)
        qr = head_rows(qr_ref, k)

        s = _dot_nt(qc, kcmp_ref[0].astype(BF16)).reshape(GQA, TQ, n_cmp)
        p = jnp.where(m_cmp, _softmax_rows(jnp.where(m_cmp, s, NEG_INF)), 0.0)
        o_c = _dot(p.reshape(GQA * TQ, n_cmp).astype(BF16), vcmp_ref[0].astype(BF16))
        imp = jnp.sum(p, axis=0)

        imp_sel = jnp.dot(amat_ref[...], imp.T, precision=lax.Precision.HIGHEST, preferred_element_type=F32)
        score = jnp.where(jj > jq, -1.0, jnp.where(forced, FORCE_SCORE, imp_sel))
        rank = jnp.zeros((N_SEL_PROMPT, TQ), F32)
        for i in range(N_SEL_PROMPT):
            si = score[i:i + 1, :]
            beats = (si > score) | ((si == score) & (jj > i))
            rank = rank + jnp.where(beats, 1.0, 0.0)
        sel_t = jnp.where((rank < N_SELECT) & (jj <= jq), 1.0, 0.0)
        sel_t = jnp.concatenate([sel_t, jnp.zeros((LANES - N_SEL_PROMPT, TQ), F32)], axis=0)
        sel = sel_t.T

        bias = ((sel - 1.0) * (-NEG_INF)).astype(BF16)
        q_augs.append(jnp.concatenate([qr, jnp.concatenate([bias] * GQA, axis=0)], axis=1))
        q_rots.append(qr)
        o_cs.append(o_c.reshape(GQA, TQ, LANES))

    sum_lanes = [HEAD_DIM * (1 - k) for k in range(N_KV_HEADS)]
    row_id = lax.broadcasted_iota(jnp.int32, (LANES, 1), 0)

    def attend(q2d, kk, vv, k, mask, m_i, acc):
        n_q, n_k = q2d.shape[0] // GQA, kk.shape[1]
        sc = _dot(q2d, kk).reshape(GQA, n_q, n_k)
        if mask is not None:
            sc = jnp.where(mask[None], sc, NEG_INF)
        m_n = jnp.max(sc, axis=-1, keepdims=True)
        if m_i is not None:
            m_n = jnp.maximum(m_i, m_n)
        pe = jnp.exp((sc - m_n).astype(BF16))
        vk = jnp.where(row_id == sum_lanes[k], 1.0, vv).astype(BF16)
        pv = _dot_nt(pe.reshape(GQA * n_q, n_k), vk).reshape(GQA, n_q, LANES)
        if m_i is not None:
            pv = jnp.exp(m_i - m_n) * acc + pv
        return m_n, pv

    def sub_rows(x2d, s):
        w = x2d.shape[1]
        return x2d.reshape(GQA, TQ, w)[:, s * SUBQ:(s + 1) * SUBQ].reshape(GQA * SUBQ, w)

    def chunk(c, carry):
        k0 = pl.multiple_of(c * KCHUNK, KCHUNK)
        kk = jnp.concatenate([ks_ref[0, :, pl.ds(k0, KCHUNK)].astype(BF16),
                              emat_ref[:, pl.ds(k0, KCHUNK)]], axis=0)
        vv = vs_ref[0, :, pl.ds(k0, KCHUNK)].astype(BF16)
        return tuple(attend(q_augs[k], kk, vv, k, None, *carry[k]) for k in range(N_KV_HEADS))

    init = tuple((jnp.full((GQA, TQ, 1), NEG_INF, F32), jnp.zeros((GQA, TQ, LANES), F32)) for _ in range(N_KV_HEADS))
    carry = lax.fori_loop(0, n_chunks - 1, chunk, init)

    acc_ss = [[] for _ in range(N_KV_HEADS)]
    acc_ws = [[] for _ in range(N_KV_HEADS)]
    w_sub = WINDOW + SUBQ
    for s in range(TQ // SUBQ):
        tq_s = t0 + s * SUBQ + lax.broadcasted_iota(jnp.int32, (SUBQ, 1), 0)
        n_k = (s + 1) * SUBQ
        k0 = pl.multiple_of(t0, TQ)
        kk = jnp.concatenate([ks_ref[0, :, pl.ds(k0, n_k)].astype(BF16), emat_ref[:, pl.ds(k0, n_k)]], axis=0)
        vv = vs_ref[0, :, pl.ds(k0, n_k)].astype(BF16)
        causal = (k0 + lax.broadcasted_iota(jnp.int32, (1, n_k), 1)) <= tq_s
        ws = pl.multiple_of(jnp.clip(t0 + s * SUBQ - WINDOW, 0, seq - w_sub), SUBQ)
        kw = kw_ref[0, :, pl.ds(ws, w_sub)].astype(BF16)
        vw = vw_ref[0, :, pl.ds(ws, w_sub)].astype(BF16)
        kp_w = ws + lax.broadcasted_iota(jnp.int32, (1, w_sub), 1)
        in_win = (kp_w <= tq_s) & (kp_w > tq_s - WINDOW)
        for k in range(N_KV_HEADS):
            m_i, acc = carry[k]
            rows = slice(s * SUBQ, (s + 1) * SUBQ)
            acc_ss[k].append(attend(sub_rows(q_augs[k], s), kk, vv, k, causal, m_i[:, rows], acc[:, rows])[1])
            acc_ws[k].append(attend(sub_rows(q_rots[k], s), kw, vw, k, in_win, None, None)[1])

    for k in range(N_KV_HEADS):
        acc_s, acc_w = jnp.concatenate(acc_ss[k], axis=1), jnp.concatenate(acc_ws[k], axis=1)
        sl = slice(sum_lanes[k], sum_lanes[k] + 1)
        for g in range(GQA):
            h = GQA * k + g
            g_s = gates[:, 3 * h + 1:3 * h + 2] / acc_s[g][:, sl]
            g_w = gates[:, 3 * h + 2:3 * h + 3] / acc_w[g][:, sl]
            comb.append(gates[:, 3 * h:3 * h + 1] * o_cs[k][g] + g_s * acc_s[g] + g_w * acc_w[g])

    cols = []
    for pr in range(N_HEADS // 2):
        k = (2 * pr) // GQA
        a, b = comb[2 * pr], comb[2 * pr + 1]
        a = pltpu.roll(a, HEAD_DIM, 1) if k == 1 else a
        b = pltpu.roll(b, HEAD_DIM, 1) if k == 0 else b
        cols.append(jnp.where(low, a, b))
    o = jnp.concatenate(cols, axis=1)
    o_ref[0] = _rms(o, gn_ref[...]).astype(o_ref.dtype)


def _sel_matrices(n_cmp_pad, n_sel, n_keys):
    j = np.arange(n_sel)[:, None]
    n = np.arange(n_cmp_pad)[None, :]
    per = SEL_BLOCK // CMP_STRIDE
    amat = ((n >= per * j - (CMP_RATIO - 1)) & (n <= per * j + per - 1)).astype(np.float32)
    jrow = np.arange(LANES)[:, None]
    key = np.arange(n_keys)[None, :]
    emat = (key // SEL_BLOCK == jrow).astype(np.float32)
    return jnp.asarray(amat), jnp.asarray(emat, dtype=BF16)


def _attn_prompt(q, qr, kcmp, vcmp, ks, vs, kw, vw, gt, gn):
    b, t, _ = q.shape
    amat, emat = _sel_matrices(kcmp.shape[1], N_SEL_PROMPT, t)
    qt = lambda w: pl.BlockSpec((1, TQ, w), lambda i, j: (i, j, 0))
    full = lambda a: pl.BlockSpec((1,) + a.shape[1:], lambda i, j: (i, 0, 0))
    const = lambda a: pl.BlockSpec(a.shape, lambda i, j: (0, 0))
    return pl.pallas_call(
        functools.partial(_attn_prompt_body, seq=t),
        grid=(b, t // TQ),
        in_specs=[qt(D_ATTN), qt(D_ATTN), full(kcmp), full(vcmp), full(ks), full(vs), full(kw), full(vw),
                  qt(LANES), const(amat), const(emat), const(gn)],
        out_specs=qt(D_ATTN),
        out_shape=jax.ShapeDtypeStruct((b, t, D_ATTN), BF16),
        compiler_params=_cparams(("parallel", "arbitrary")),
        name="attn_prompt",
    )(q, qr, kcmp, vcmp, ks, vs, kw, vw, gt, amat, emat, gn)


S5_TC = 128
S5_LB = D_SSM // LANES
S5_SW = LANES // SSM_CH * SSM_STATE


def _s5_body(u_ref, h0r_ref, h0i_ref, lr_ref, li_ref, bre_ref, bim_ref, c_ref, d_ref, gw_ref, gb_ref, gn_ref,
             y_ref, hr_ref, hi_ref, y_s, *, tc, bb):
    step = pl.program_id(0)
    m = tc * bb

    @pl.when(step == 0)
    def _():
        hr_ref[...] = h0r_ref[...]
        hi_ref[...] = h0i_ref[...]

    u = pltpu.einshape("btd->(tb)d", u_ref[...])
    ub = u.astype(BF16)
    for j in range(S5_LB):
        uj = ub[:, j * LANES:(j + 1) * LANES]
        sl = slice(j * S5_SW, (j + 1) * S5_SW)
        xr = _dot(uj, bre_ref[j]).reshape(tc, bb, S5_SW)
        xi = _dot(uj, bim_ref[j]).reshape(tc, bb, S5_SW)
        lr = jnp.broadcast_to(lr_ref[:, sl], (bb, S5_SW))
        li = jnp.broadcast_to(li_ref[:, sl], (bb, S5_SW))
        hr, hi = hr_ref[:, sl], hi_ref[:, sl]
        hrs, his = [], []
        for t in range(tc):
            hr, hi = lr * hr - li * hi + xr[t], lr * hi + li * hr + xi[t]
            hrs.append(hr)
            his.append(hi)
        hr_ref[:, sl] = hr
        hi_ref[:, sl] = hi
        hcat = jnp.concatenate([jnp.concatenate(hrs, axis=0), jnp.concatenate(his, axis=0)], axis=1).astype(BF16)
        y_s[:, j * LANES:(j + 1) * LANES] = _dot(hcat, c_ref[j])
    z = jax.nn.gelu(y_s[...] + d_ref[...] * u)
    out = z * jax.nn.sigmoid(_dot(z.astype(BF16), gw_ref[...]) + gb_ref[...])
    y_ref[...] = pltpu.einshape("(tb)d->btd", _rms(out, gn_ref[...]), b=bb)


def _s5(u, h0r, h0i, sp, tc):
    bb, t, _ = u.shape
    const = lambda a: pl.BlockSpec(a.shape, lambda i: (0,) * a.ndim)
    args = (h0r, h0i, sp['lr'], sp['li'], sp['bre'], sp['bim'], sp['c'], sp['d'], sp['glu_w'], sp['glu_b'], sp['gn'])
    st = jax.ShapeDtypeStruct(h0r.shape, F32)
    blk = pl.BlockSpec((bb, tc, D_SSM), lambda i: (0, i, 0))
    return pl.pallas_call(
        functools.partial(_s5_body, tc=tc, bb=bb),
        grid=(t // tc,),
        in_specs=[blk] + [const(a) for a in args],
        out_specs=[blk, const(h0r), const(h0i)],
        out_shape=[jax.ShapeDtypeStruct((bb, t, D_SSM), F32), st, st],
        scratch_shapes=[pltpu.VMEM((tc * bb, D_SSM), F32)],
        compiler_params=_cparams(("arbitrary",)),
        name="s5",
    )(u, *args)


def _s5_params(log_dt, a_re, a_im, b_re, b_im, c_re, c_im, d, glu_w, glu_b, gn):
    dt = jnp.exp(log_dt)[:, None]
    mag = jnp.exp(a_re * dt)
    lr, li = mag * jnp.cos(a_im * dt), mag * jnp.sin(a_im * dt)
    den = a_re * a_re + a_im * a_im
    inv_r, inv_i = a_re / den, -a_im / den
    nr, ni = lr - 1.0, li
    fr, fi = nr * inv_r - ni * inv_i, nr * inv_i + ni * inv_r
    bbr = fr[..., None] * b_re - fi[..., None] * b_im
    bbi = fr[..., None] * b_im + fi[..., None] * b_re
    gl = LANES // SSM_CH
    eye = jnp.eye(gl, dtype=F32)

    def bmat(bb):
        x = bb.reshape(S5_LB, gl, SSM_STATE, SSM_CH)
        return jnp.einsum('jgpc,gh->jgchp', x, eye).reshape(S5_LB, LANES, S5_SW).astype(BF16)

    def cmat(cc):
        x = cc.reshape(S5_LB, gl, SSM_CH, SSM_STATE)
        return jnp.einsum('jgcp,gh->jgphc', x, eye).reshape(S5_LB, S5_SW, LANES)

    cm = jnp.concatenate([cmat(c_re), -cmat(c_im)], axis=1).astype(BF16)
    return dict(lr=lr.reshape(1, -1), li=li.reshape(1, -1), bre=bmat(bbr), bim=bmat(bbi), c=cm,
                d=d.reshape(1, -1), glu_w=glu_w.astype(BF16), glu_b=glu_b[None, :], gn=gn[None, :])


def _outproj_body(x_ref, a_ref, s_ref, wa_ref, ws_ref, gn_ref, o_ref, *, norm_attn):
    a = a_ref[...]
    if norm_attn:
        a = _rms(a, gn_ref[...])
    o_ref[...] = x_ref[...] + _dot(a.astype(BF16), wa_ref[...]) + _dot(s_ref[...].astype(BF16), ws_ref[...])


def _outproj(x, a, s, wa, ws, gn, norm_attn):
    m = x.shape[0]
    tm = min(m, 1024)
    row = lambda w: pl.BlockSpec((tm, w), lambda i: (i, 0))
    const = lambda a_: pl.BlockSpec(a_.shape, lambda i: (0, 0))
    return pl.pallas_call(
        functools.partial(_outproj_body, norm_attn=norm_attn),
        grid=(m // tm,),
        in_specs=[row(D_MODEL), row(D_ATTN), row(D_SSM), const(wa), const(ws), const(gn)],
        out_specs=row(D_MODEL),
        out_shape=jax.ShapeDtypeStruct((m, D_MODEL), F32),
        compiler_params=_cparams(("parallel",)),
        name="outproj",
    )(x, a, s, wa, ws, gn)


N_PAGES = PAST_LEN // PAGE_SIZE
CMP_CHUNK = 32
CMP_PITCH = CMP_STRIDE // 2


def _page_copy(pool_ref, buf_ref, sem_ref, pt_ref, b, i, slot):
    return pltpu.make_async_copy(pool_ref.at[pt_ref[b * N_PAGES + i]], buf_ref.at[slot, i], sem_ref.at[slot])


def _cmp_paged_body(pt_ref, pool_ref, w1_ref, b_ref, w2_ref, o_ref, buf_ref, rows_ref, p_ref, sem_ref):
    b = pl.program_id(0)
    slot = b % 2

    def start_all(bb, sl):
        def body(i, c):
            _page_copy(pool_ref, buf_ref, sem_ref, pt_ref, bb, i, sl).start()
            return c
        lax.fori_loop(0, N_PAGES, body, 0, unroll=8)

    @pl.when(b == 0)
    def _():
        start_all(0, 0)

    @pl.when(b + 1 < pl.num_programs(0))
    def _():
        start_all(b + 1, 1 - slot)

    def wait_one(i, c):
        _page_copy(pool_ref, buf_ref, sem_ref, pt_ref, b, i, slot).wait()
        return c
    lax.fori_loop(0, N_PAGES, wait_one, 0, unroll=8)

    hp = PAGE_SIZE // CMP_STRIDE
    hc = CMP_CHUNK * hp

    wp = PAGE_SIZE // 2

    def stage_a(c):
        for i in range(CMP_CHUNK):
            t = buf_ref[slot, c * CMP_CHUNK + i].astype(BF16).T
            rows_ref[c % 2, i * wp:(i + 1) * wp, :] = pltpu.bitcast(t, jnp.uint32)

    def stage_b(c):
        xs = []
        for j in range(CMP_PITCH):
            w = rows_ref[c % 2, pl.ds(j, hc, stride=CMP_PITCH), :]
            xs.append(pltpu.bitcast(w << 16, F32))
            xs.append(pltpu.bitcast(w & jnp.uint32(0xFFFF0000), F32))
        p = _dot(_per_head_halves(xs), w1_ref[...])
        for k in range(N_KV_HEADS):
            p_ref[k, c * hc:(c + 1) * hc, :] = p[k * hc:(k + 1) * hc]

    n_chunks = N_PAGES // CMP_CHUNK
    stage_a(0)
    for c in range(n_chunks):
        if c + 1 < n_chunks:
            stage_a(c + 1)
        stage_b(c)

    o_ref[0] = _compress_tail([p_ref[k] for k in range(N_KV_HEADS)], N_PAGES * hp, b_ref, w2_ref)


def _cmp_paged(page_table_flat, pool_t, w1, bias, w2, n_seq):
    n_half = PAST_LEN // CMP_STRIDE
    hc = CMP_CHUNK * PAGE_SIZE // CMP_STRIDE
    const = lambda a: pl.BlockSpec(a.shape, lambda i, pt: (0,) * a.ndim)
    return pl.pallas_call(
        _cmp_paged_body,
        grid_spec=pltpu.PrefetchScalarGridSpec(
            num_scalar_prefetch=1,
            grid=(n_seq,),
            in_specs=[pl.BlockSpec(memory_space=pl.ANY), const(w1), const(bias), const(w2)],
            out_specs=pl.BlockSpec((1, n_half, KV_W), lambda i, pt: (i, 0, 0)),
            scratch_shapes=[pltpu.VMEM((2, N_PAGES, PAGE_SIZE, KV_W), F32),
                            pltpu.VMEM((2, hc * CMP_PITCH, KV_W), jnp.uint32),
                            pltpu.VMEM((N_KV_HEADS, n_half, CMP_RATIO * CMP_HIDDEN), F32),
                            pltpu.SemaphoreType.DMA((2,))]),
        out_shape=jax.ShapeDtypeStruct((n_seq, n_half, KV_W), F32),
        compiler_params=_cparams(("arbitrary",)),
        name="cmp_paged",
    )(page_table_flat, pool_t, w1, bias, w2)


CMPATTN_SEQS = 16


def _cmpattn_body(q_ref, kc_ref, vc_ref, amat_ref, o_ref, idx_ref, *, ts, n_cmp, n_sel):
    rows = GQA * ts
    qpos = PAST_LEN + lax.broadcasted_iota(jnp.int32, (rows, 1), 0) % ts
    n_pad = kc_ref.shape[1]
    n_io = lax.broadcasted_iota(jnp.int32, (1, n_pad), 1)
    m = (n_io * CMP_STRIDE + (CMP_LEN - 1) <= qpos) & (n_io < n_cmp)
    n_sb = q_ref.shape[0]
    imps = []
    for b in range(n_sb):
        kc = kc_ref[b].astype(BF16)
        vc = vc_ref[b].astype(BF16)
        for k in range(N_KV_HEADS):
            s = _dot_nt((q_ref[b, k] * ATTN_SCALE).astype(BF16), kc)
            p = jnp.where(m, _softmax_rows(jnp.where(m, s, NEG_INF)), 0.0)
            o_ref[b, k] = _dot(p.astype(BF16), vc)
            imp = p[0:ts]
            for g in range(1, GQA):
                imp = imp + p[g * ts:(g + 1) * ts]
            imps.append(imp)
    imp_all = jnp.concatenate(imps, axis=0)
    n_rows = imp_all.shape[0]
    sel_w = amat_ref.shape[1]
    imp_sel = jnp.dot(imp_all, amat_ref[...], precision=lax.Precision.HIGHEST, preferred_element_type=F32)
    j = lax.broadcasted_iota(jnp.int32, (n_rows, sel_w), 1)
    jq = (PAST_LEN + lax.broadcasted_iota(jnp.int32, (n_rows, sel_w), 0) % ts) // SEL_BLOCK
    forced = (j == 0) | (j == jq) | (j == jq - 1)
    jf = j.astype(F32)
    lane_k = lax.broadcasted_iota(jnp.int32, (n_rows, LANES), 1)
    score = jnp.where(j > jq, -1.0, jnp.where(forced, FORCE_SCORE, imp_sel))
    score = jnp.where(j < n_sel, score, -jnp.inf)
    res = jnp.zeros((n_rows, LANES), F32)
    for i in range(N_SELECT):
        mx = jnp.max(score, axis=-1, keepdims=True)
        ix = jnp.min(jnp.where(score == mx, jf, 1e9), axis=-1, keepdims=True)
        res = jnp.where(lane_k == i, ix, res)
        score = jnp.where(jf == ix, -jnp.inf, score)
    idx_ref[...] = res.astype(jnp.int32).reshape(idx_ref.shape)


def _cmpattn(q_pad, kcmp, vcmp, ts):
    n_seq = q_pad.shape[0]
    n_cmp = (PAST_LEN + ts) // CMP_STRIDE - CMP_RATIO + 1
    n_sel = -(-(PAST_LEN + ts) // SEL_BLOCK)
    sel_w = -(-n_sel // LANES) * LANES
    per = SEL_BLOCK // CMP_STRIDE
    n = np.arange(kcmp.shape[1])[:, None]
    j = np.arange(sel_w)[None, :]
    amat = jnp.asarray(((n >= per * j - (CMP_RATIO - 1)) & (n <= per * j + per - 1) & (j < n_sel)).astype(np.float32))
    rows = GQA * ts
    sb = CMPATTN_SEQS
    return pl.pallas_call(
        functools.partial(_cmpattn_body, ts=ts, n_cmp=n_cmp, n_sel=n_sel),
        grid=(n_seq // sb,),
        in_specs=[pl.BlockSpec((sb, N_KV_HEADS, rows, LANES), lambda i: (i, 0, 0, 0)),
                  pl.BlockSpec((sb,) + kcmp.shape[1:], lambda i: (i, 0, 0)),
                  pl.BlockSpec((sb,) + vcmp.shape[1:], lambda i: (i, 0, 0)),
                  pl.BlockSpec(amat.shape, lambda i: (0, 0))],
        out_specs=[pl.BlockSpec((sb, N_KV_HEADS, rows, LANES), lambda i: (i, 0, 0, 0)),
                   pl.BlockSpec((sb, N_KV_HEADS * ts, LANES), lambda i: (i, 0, 0))],
        out_shape=[jax.ShapeDtypeStruct((n_seq, N_KV_HEADS, rows, LANES), F32),
                   jax.ShapeDtypeStruct((n_seq, N_KV_HEADS * ts, LANES), jnp.int32)],
        compiler_params=_cparams(("parallel",)),
        name="cmpattn_sample",
    )(q_pad, kcmp, vcmp, amat)


def _slc_copies(pools, bufs, sems, pt_ref, idx_ref, seq, r, slot, n_fetch, for_wait):
    if for_wait:
        pg = 0
    else:
        blk = idx_ref[seq * n_fetch + r]
        page = jnp.minimum(blk // (PAGE_SIZE // SEL_BLOCK), N_PAGES - 1)
        pg = pt_ref[seq * N_PAGES + page]
    return [pltpu.make_async_copy(pool.at[pg], buf.at[slot, r], sem.at[slot])
            for pool, buf, sem in zip(pools, bufs, sems)]


def _tail_body(pt_ref, idx_ref, kpool_ref, vpool_ref, q_ref, ksn_ref, vsn_ref, kwn_ref, vwn_ref, wk_ref, wv_ref,
               oc_ref, gt_ref, o_ref, wko_ref, wvo_ref, kbuf_ref, vbuf_ref, ksem_ref, vsem_ref, *, ts):
    step = pl.program_id(0)
    n_steps = pl.num_programs(0)
    slot = step % 2
    n_fetch = N_KV_HEADS * ts * N_SELECT

    pools, bufs, sems = (kpool_ref, vpool_ref), (kbuf_ref, vbuf_ref), (ksem_ref, vsem_ref)

    def start_all(st, sl):
        def body(r, c):
            for cp in _slc_copies(pools, bufs, sems, pt_ref, idx_ref, st, r, sl, n_fetch, False):
                cp.start()
            return c
        lax.fori_loop(0, n_fetch, body, 0, unroll=8)

    @pl.when(step == 0)
    def _():
        start_all(0, 0)

    @pl.when(step + 1 < n_steps)
    def _():
        start_all(step + 1, 1 - slot)

    def wait_one(r, c):
        for cp in _slc_copies(pools, bufs, sems, pt_ref, idx_ref, step, r, slot, n_fetch, True):
            cp.wait()
        return c
    lax.fori_loop(0, n_fetch, wait_one, 0, unroll=8)

    rows = GQA * ts
    row_t = lax.broadcasted_iota(jnp.int32, (rows, 1), 0) % ts
    qpos = PAST_LEN + row_t
    lane = lax.broadcasted_iota(jnp.int32, (1, LANES), 1)
    lane_half, lane_off = lane // SEL_BLOCK, lane % SEL_BLOCK
    pad = jnp.zeros((LANES - ksn_ref.shape[1], LANES), F32)
    new_pos = PAST_LEN + lane
    new_ok = lane < ts
    blk_new = PAST_LEN // SEL_BLOCK

    def new_rows(ref):
        return jnp.concatenate([ref[0], pad], axis=0).astype(BF16)

    ksn, vsn = new_rows(ksn_ref), new_rows(vsn_ref)
    kwn, vwn = new_rows(kwn_ref), new_rows(vwn_ref)
    wb = wk_ref.shape[2]
    wk, wv = wk_ref[0].astype(BF16), wv_ref[0].astype(BF16)
    kp = PAST_LEN - wb + lax.broadcasted_iota(jnp.int32, (1, wb), 1)
    ok_w = (kp >= 0) & (kp <= qpos) & (kp > qpos - WINDOW)
    ok_wn = new_ok & (new_pos <= qpos) & (new_pos > qpos - WINDOW)
    grp = 4

    def pages(buf_ref, r0):
        return jnp.concatenate([buf_ref[slot, r0 + u].astype(BF16) for u in range(grp)], axis=1)

    for k in range(N_KV_HEADS):
        q = (q_ref[0, k] * ATTN_SCALE).astype(BF16)

        s_new = _dot_nt(q, ksn)
        o_s = jnp.zeros((rows, LANES), F32)
        for t in range(ts):
            r0 = (k * ts + t) * N_SELECT
            pieces = []
            has_new = jnp.int32(0)
            for i0 in range(0, N_SELECT, grp):
                sc = _dot(q, pages(kbuf_ref, r0 + i0))
                for u in range(grp):
                    blk = idx_ref[step * n_fetch + r0 + i0 + u]
                    has_new = has_new | (blk == blk_new).astype(jnp.int32)
                    want_half = jnp.where(blk < blk_new, blk % (PAGE_SIZE // SEL_BLOCK), -1)
                    ok = (lane_half == want_half) & (blk * SEL_BLOCK + lane_off <= qpos)
                    pieces.append(jnp.where(ok, sc[:, u * LANES:(u + 1) * LANES], NEG_INF))
            ok_new = (lane < jnp.where(has_new > 0, ts, 0)) & (new_pos <= qpos)
            pieces.append(jnp.where(ok_new, s_new, NEG_INF))
            p = _softmax_rows(jnp.concatenate(pieces, axis=1)).astype(BF16)
            o_t = _dot(p[:, N_SELECT * LANES:], vsn)
            for i0 in range(0, N_SELECT, grp):
                o_t = o_t + _dot_nt(p[:, i0 * LANES:(i0 + grp) * LANES], pages(vbuf_ref, r0 + i0))
            o_s = jnp.where(row_t == t, o_t, o_s)

        s_w = jnp.where(ok_w, _dot(q, wk), NEG_INF)
        s_wn = jnp.where(ok_wn, _dot_nt(q, kwn), NEG_INF)
        p = _softmax_rows(jnp.concatenate([s_w, s_wn], axis=1)).astype(BF16)
        o_w = _dot_nt(p[:, :wb], wv) + _dot(p[:, wb:], vwn)

        gt = gt_ref[0, k]
        o_ref[0, k] = gt[:, 0:1] * oc_ref[0, k] + gt[:, 1:2] * o_s + gt[:, 2:3] * o_w

    lane_w = lax.broadcasted_iota(jnp.int32, (1, LANES), 1)

    def slide(w_ref, new_ref, out_ref):
        shifted = pltpu.roll(w_ref[0], wb - ts, 1)
        new_t = pltpu.roll(jnp.concatenate([new_ref[0], pad], axis=0).T, LANES - ts, 1)
        out_ref[0, :, :wb - LANES] = shifted[:, :wb - LANES]
        out_ref[0, :, wb - LANES:] = jnp.where(lane_w >= LANES - ts, new_t, shifted[:, wb - LANES:])

    slide(wk_ref, kwn_ref, wko_ref)
    slide(wv_ref, vwn_ref, wvo_ref)


def _tail(page_table_flat, idx_flat, kpool_t, vpool_t, q_pad, ksn, vsn, kwn, vwn, wk_t, wv_t, o_cmp, gates_pad, ts):
    n_seq = q_pad.shape[0]
    rows = GQA * ts
    n_fetch = N_KV_HEADS * ts * N_SELECT
    per_head = pl.BlockSpec((1, N_KV_HEADS, rows, LANES), lambda i, pt, ix: (i, 0, 0, 0))
    per_seq = lambda a: pl.BlockSpec((1,) + a.shape[1:], lambda i, pt, ix: (i, 0, 0))
    hbm = pl.BlockSpec(memory_space=pl.ANY)
    return pl.pallas_call(
        functools.partial(_tail_body, ts=ts),
        grid_spec=pltpu.PrefetchScalarGridSpec(
            num_scalar_prefetch=2,
            grid=(n_seq,),
            in_specs=[hbm, hbm, per_head, per_seq(ksn), per_seq(vsn), per_seq(kwn), per_seq(vwn),
                      per_seq(wk_t), per_seq(wv_t), per_head, per_head],
            out_specs=[per_head, per_seq(wk_t), per_seq(wv_t)],
            scratch_shapes=[pltpu.VMEM((2, n_fetch, PAGE_SIZE, KV_W), F32), pltpu.VMEM((2, n_fetch, PAGE_SIZE, KV_W), F32),
                            pltpu.SemaphoreType.DMA((2,)), pltpu.SemaphoreType.DMA((2,))]),
        out_shape=[jax.ShapeDtypeStruct((n_seq, N_KV_HEADS, rows, LANES), F32),
                   jax.ShapeDtypeStruct(wk_t.shape, F32), jax.ShapeDtypeStruct(wv_t.shape, F32)],
        compiler_params=_cparams(("arbitrary",)),
        name="slc_win_sample",
    )(page_table_flat, idx_flat, kpool_t, vpool_t, q_pad, ksn, vsn, kwn, vwn, wk_t, wv_t, o_cmp, gates_pad)


def _pad_heads(x, ts):
    n = x.shape[0] // ts
    w = x.shape[1] // N_HEADS
    return x.reshape(n, ts, N_KV_HEADS, GQA, w).transpose(0, 2, 3, 1, 4).reshape(n, N_KV_HEADS, GQA * ts, w)


def _q_on_kv_lanes(q, ts):
    x = _pad_heads(q, ts)
    z = jnp.zeros_like(x[:, 0])
    return jnp.stack([jnp.concatenate([x[:, 0], z], axis=-1), jnp.concatenate([z, x[:, 1]], axis=-1)], axis=1)


def kernel(x_prompt, x_sample, cache_cmp_k, cache_cmp_v, cache_slc_k, cache_slc_v, state_win_k, state_win_v, state_s5_re, state_s5_im, page_table, norm_ffn1, ffn1_gate, ffn1_up, ffn1_down, norm_mix, w_in, cmp_k_pos, cmp_k_w1, cmp_k_b1, cmp_k_w2, cmp_v_pos, cmp_v_w1, cmp_v_b1, cmp_v_w2, s5_log_dt, s5_a_re, s5_a_im, s5_b_re, s5_b_im, s5_c_re, s5_c_im, s5_d, s5_glu_w, s5_glu_b, norm_attn_out, norm_ssm_out, w_out, norm_ffn2, ffn2_gate, ffn2_up, ffn2_down, norm_final):
    depth = w_in.shape[0]
    assert depth == 1
    l = 0
    B, T, _ = x_prompt.shape
    BS, TS, _ = x_sample.shape
    assert TS < CMP_STRIDE and TS <= 8
    row = lambda a: a[None, :]

    w = w_in[l]
    cut = D_ATTN + 6 * KV_W
    w_p = jnp.concatenate([w[:, :cut], w[:, cut + 3 * N_HEADS:], w[:, cut:cut + 3 * N_HEADS],
                           jnp.zeros((D_MODEL, LANES - 3 * N_HEADS), F32)], axis=1).astype(BF16)
    f1 = (row(norm_ffn1[l]), ffn1_gate[l].astype(BF16), ffn1_up[l].astype(BF16), ffn1_down[l].astype(BF16))
    f2 = (row(norm_ffn2[l]), ffn2_gate[l].astype(BF16), ffn2_up[l].astype(BF16), ffn2_down[l].astype(BF16))
    gfin = row(norm_final)
    wo = w_out[l].astype(BF16)
    wo_a, wo_s = wo[:D_ATTN], wo[D_ATTN:]
    cwk = _cmp_weights(cmp_k_pos[l], cmp_k_w1[l], cmp_k_b1[l], cmp_k_w2[l])
    cwv = _cmp_weights(cmp_v_pos[l], cmp_v_w1[l], cmp_v_b1[l], cmp_v_w2[l])
    sp = _s5_params(s5_log_dt[l], s5_a_re[l], s5_a_im[l], s5_b_re[l], s5_b_im[l], s5_c_re[l], s5_c_im[l],
                    s5_d[l], s5_glu_w[l], s5_glu_b[l], norm_ssm_out[l])
    gn_attn = row(norm_attn_out[l])
    g_mix = row(norm_mix[l])

    xp = x_prompt.reshape(B * T, D_MODEL)
    x1 = _ffn(xp, *f1, gfin, False)
    tabs = _rope_tables(jnp.arange(T, dtype=jnp.int32))
    (q, qr, kc, vc, _, _, _, _, gt, u,
     kc_t, vc_t, ks_t, vs_t, kw_t, vw_t) = _proj(x1, g_mix, w_p, *tabs, seq_t=T)
    b3 = lambda a: a.reshape(B, T, a.shape[-1])
    kcmp, vcmp = _cmp_prompt(b3(kc), b3(vc), cwk, cwv)
    attn = _attn_prompt(b3(q), b3(qr), kcmp, vcmp, ks_t, vs_t, kw_t, vw_t, b3(gt), gn_attn)
    zeros = jnp.zeros((B, SSM_GROUPS * SSM_STATE), F32)
    ssm, p_re, p_im = _s5(b3(u), zeros, zeros, sp, S5_TC)
    ssm = ssm.reshape(B * T, D_SSM)
    x2 = _outproj(x1, attn.reshape(B * T, D_ATTN), ssm, wo_a, wo_s, gn_attn, False)
    y_prompt = _ffn(x2, *f2, gfin, True).reshape(B, T, D_MODEL)
    from_t = lambda a: a.reshape(a.shape[0], N_KV_HEADS, HEAD_DIM, a.shape[2]).transpose(0, 3, 1, 2)[None]
    wb = min(WINDOW, T)
    p_state = (from_t(kc_t), from_t(vc_t), from_t(ks_t), from_t(vs_t),
               from_t(kw_t[:, :, T - wb:]), from_t(vw_t[:, :, T - wb:]),
               p_re.reshape(1, B, SSM_GROUPS, SSM_STATE), p_im.reshape(1, B, SSM_GROUPS, SSM_STATE))

    xs = x_sample.reshape(BS * TS, D_MODEL)
    s1 = _ffn(xs, *f1, gfin, False)
    pos_s = PAST_LEN + jnp.arange(TS, dtype=jnp.int32)
    tabs_s = [jnp.tile(a, (BS, 1)) for a in _rope_tables(pos_s)]
    q, qr, kc, vc, ks, vs, kw, vw, gt, u = _proj(s1, g_mix, w_p, *tabs_s)
    s3 = lambda a: a.reshape(BS, TS, a.shape[-1])
    to_t = lambda a: a.transpose(0, 2, 3, 1).reshape(a.shape[0], KV_W, a.shape[1])
    pt_flat = page_table.reshape(-1)
    kcmp_s = _cmp_paged(pt_flat, to_t(cache_cmp_k[l]), *cwk, BS)
    vcmp_s = _cmp_paged(pt_flat, to_t(cache_cmp_v[l]), *cwv, BS)
    o_cmp_s, idx = _cmpattn(_q_on_kv_lanes(q, TS), kcmp_s, vcmp_s, TS)
    pad8 = lambda a: jnp.pad(s3(a), ((0, 0), (0, 8 - TS), (0, 0)))
    gates_s = _pad_heads(gt[:, :3 * N_HEADS], TS)
    gates_s = jnp.pad(gates_s, ((0, 0), (0, 0), (0, 0), (0, LANES - 3)))
    wk_t, wv_t = to_t(state_win_k[l]), to_t(state_win_v[l])
    comb, wk_new, wv_new = _tail(
        pt_flat, idx[:, :, :N_SELECT].reshape(-1), to_t(cache_slc_k[l]), to_t(cache_slc_v[l]),
        _q_on_kv_lanes(qr, TS), pad8(ks), pad8(vs), pad8(kw), pad8(vw), wk_t, wv_t, o_cmp_s, gates_s, TS)
    comb = comb.reshape(BS, N_KV_HEADS, GQA, TS, N_KV_HEADS, HEAD_DIM)
    comb = jnp.stack([comb[:, k, :, :, k] for k in range(N_KV_HEADS)], axis=1)
    attn_s = comb.transpose(0, 3, 1, 2, 4).reshape(BS * TS, D_ATTN)
    ssm_s, s_re, s_im = _s5(s3(u), state_s5_re[l].reshape(BS, -1), state_s5_im[l].reshape(BS, -1), sp, TS)
    ssm_s = ssm_s.reshape(BS * TS, D_SSM)
    s2 = _outproj(s1, attn_s, ssm_s, wo_a, wo_s, gn_attn, True)
    y_sample = _ffn(s2, *f2, gfin, True).reshape(BS, TS, D_MODEL)
    kv5 = lambda a, n: a.reshape(1, n, -1, N_KV_HEADS, HEAD_DIM)
    win_k, win_v = from_t(wk_new), from_t(wv_new)
    s_state = (kv5(kc, BS), kv5(vc, BS), kv5(ks, BS), kv5(vs, BS), win_k, win_v,
               s_re.reshape(1, BS, SSM_GROUPS, SSM_STATE), s_im.reshape(1, BS, SSM_GROUPS, SSM_STATE))

    return (y_prompt, y_sample) + p_state + s_state
```
